```python
import jax, jax.numpy as jnp
from jax import lax
import numpy as np

D_MODEL = 1024
BATCH = 8
SEQ = 8192
DEPTH = 4

DN_HEADS = 8
DN_DK = 128
DN_DV = 128
DN_CONV = 4
DN_CHUNK = 64
SW_Q_HEADS = 16
SW_KV_HEADS = 2
SW_HEAD_DIM = 64
SW_WINDOW = 128
SW_BLOCK = 128
ROPE_THETA = 500000.0
ROT_DIM = SW_HEAD_DIM // 4
D_FF = 4 * D_MODEL
EPS = 1e-6

DN_QK_W = DN_HEADS * DN_DK
DN_V_W = DN_HEADS * DN_DV
SW_Q_W = SW_Q_HEADS * SW_HEAD_DIM
SW_KV_W = SW_KV_HEADS * SW_HEAD_DIM
IN_SPLITS = [DN_QK_W, DN_QK_W, DN_V_W, DN_V_W, DN_HEADS, DN_HEADS,
             SW_Q_W, SW_KV_W, SW_KV_W, D_MODEL, D_MODEL]
D_IN = sum(IN_SPLITS)
IN_OFFSETS = np.cumsum(IN_SPLITS)[:-1].tolist()

kernel_name = 'hybrid_gdn_swa_sink_parallel_block'


def rmsnorm(x, g):
    xf = x.astype(jnp.float32)
    y = xf * lax.rsqrt(jnp.mean(xf * xf, axis=-1, keepdims=True) + EPS)
    return (y * g.astype(jnp.float32)).astype(x.dtype)


def l2norm(t):
    tf = t.astype(jnp.float32)
    return tf * lax.rsqrt(jnp.sum(tf * tf, axis=-1, keepdims=True) + EPS)


def causal_conv_silu(x, w):
    S = x.shape[1]
    K = w.shape[0]
    xp = jnp.pad(x, ((0, 0), (K - 1, 0), (0, 0)))
    y = sum(xp[:, j:j + S] * w[j] for j in range(K))
    return jax.nn.silu(y)


def gated_delta_rule(q, k, v, g, beta):
    B, S, H, dk = q.shape
    dv = v.shape[-1]
    C = DN_CHUNK
    N = S // C

    def chunks(t):
        return t.reshape(B, N, C, H, -1).transpose(0, 3, 1, 2, 4)

    q, k, v = chunks(q), chunks(k), chunks(v)
    g = g.reshape(B, N, C, H).transpose(0, 3, 1, 2)
    beta = beta.reshape(B, N, C, H).transpose(0, 3, 1, 2)
    g = jnp.cumsum(g, axis=-1)

    idx = jnp.arange(C)
    causal = idx[:, None] >= idx[None, :]
    strict = idx[:, None] > idx[None, :]
    diff = g[..., :, None] - g[..., None, :]
    decay = jnp.where(causal, jnp.exp(jnp.where(causal, diff, 0.0)), 0.0)

    kb = k * beta[..., None]
    L = jnp.where(strict, jnp.einsum('bhnid,bhnjd->bhnij', kb, k) * decay, 0.0)
    u = lax.linalg.triangular_solve(L, v * beta[..., None], left_side=True,
                                    lower=True, unit_diagonal=True)
    w = lax.linalg.triangular_solve(L, kb * jnp.exp(g)[..., None], left_side=True,
                                    lower=True, unit_diagonal=True)
    a_intra = jnp.einsum('bhnid,bhnjd->bhnij', q, k) * decay
    q_dec = q * jnp.exp(g)[..., None]
    g_last = g[..., -1]
    k_dec = k * jnp.exp(g_last[..., None] - g)[..., None]

    def to_front(t):
        return jnp.moveaxis(t, 2, 0)

    xs = (to_front(q_dec), to_front(k_dec), to_front(u), to_front(w),
          to_front(a_intra), jnp.moveaxis(g_last, 2, 0))

    def step(state, inp):
        qd, kd, u_c, w_c, a_c, gl = inp
        v_new = u_c - jnp.einsum('bhcd,bhde->bhce', w_c, state)
        o = (jnp.einsum('bhcd,bhde->bhce', qd, state)
             + jnp.einsum('bhij,bhje->bhie', a_c, v_new))
        state = state * jnp.exp(gl)[..., None, None] + jnp.einsum('bhcd,bhce->bhde', kd, v_new)
        return state, o

    state0 = jnp.zeros((B, H, dk, dv), jnp.float32)
    _, o = lax.scan(step, state0, xs)
    return o.transpose(1, 0, 3, 2, 4).reshape(B, S, H, dv)


def deltanet_branch(q_in, k_in, v_in, z, b_in, a_in, conv_w, a_log, dt_bias, norm_g):
    B, S, _ = q_in.shape
    qkv = causal_conv_silu(jnp.concatenate([q_in, k_in, v_in], axis=-1), conv_w)
    q, k, v = jnp.split(qkv, [DN_QK_W, 2 * DN_QK_W], axis=-1)
    q = l2norm(q.reshape(B, S, DN_HEADS, DN_DK)) * (DN_DK ** -0.5)
    k = l2norm(k.reshape(B, S, DN_HEADS, DN_DK))
    v = v.reshape(B, S, DN_HEADS, DN_DV).astype(jnp.float32)
    beta = jax.nn.sigmoid(b_in.astype(jnp.float32))
    g = -jnp.exp(a_log.astype(jnp.float32)) * jax.nn.softplus(
        a_in.astype(jnp.float32) + dt_bias.astype(jnp.float32))
    o = gated_delta_rule(q, k, v, g, beta)
    o = rmsnorm(o, norm_g) * jax.nn.silu(z.reshape(B, S, DN_HEADS, DN_DV).astype(jnp.float32))
    return o.reshape(B, S, DN_V_W).astype(q_in.dtype)


def partial_rope(x, positions):
    half = ROT_DIM // 2
    inv_freq = ROPE_THETA ** (-jnp.arange(half, dtype=jnp.float32) * (2.0 / ROT_DIM))
    ang = positions.astype(jnp.float32)[..., None] * inv_freq
    cos = jnp.cos(ang)[:, :, None, :]
    sin = jnp.sin(ang)[:, :, None, :]
    xr = x[..., :ROT_DIM].astype(jnp.float32)
    x1, x2 = xr[..., :half], xr[..., half:]
    rot = jnp.concatenate([x1 * cos - x2 * sin, x2 * cos + x1 * sin], axis=-1)
    return jnp.concatenate([rot.astype(x.dtype), x[..., ROT_DIM:]], axis=-1)


def swa_sink_branch(q_in, k_in, v_in, positions, sinks):
    B, S, _ = q_in.shape
    G = SW_Q_HEADS // SW_KV_HEADS
    nb = S // SW_BLOCK
    q = partial_rope(q_in.reshape(B, S, SW_Q_HEADS, SW_HEAD_DIM), positions)
    k = partial_rope(k_in.reshape(B, S, SW_KV_HEADS, SW_HEAD_DIM), positions)
    v = v_in.reshape(B, S, SW_KV_HEADS, SW_HEAD_DIM)

    qb = q.reshape(B, nb, SW_BLOCK, SW_KV_HEADS, G, SW_HEAD_DIM).astype(jnp.float32)

    def band(t):
        tp = jnp.pad(t, ((0, 0), (SW_BLOCK, 0), (0, 0), (0, 0)))
        tb = tp.reshape(B, nb + 1, SW_BLOCK, SW_KV_HEADS, SW_HEAD_DIM)
        return jnp.concatenate([tb[:, :-1], tb[:, 1:]], axis=2)

    kw = band(k).astype(jnp.float32)
    vw = band(v)
    scores = jnp.einsum('bnqhgd,bnkhd->bnhgqk', qb, kw) * (SW_HEAD_DIM ** -0.5)

    qi = jnp.arange(SW_BLOCK)[:, None] + SW_BLOCK
    ki = jnp.arange(2 * SW_BLOCK)[None, :]
    off = qi - ki
    in_band = (off >= 0) & (off < SW_WINDOW)
    blk = jnp.arange(nb)[:, None, None]
    valid = (blk * SW_BLOCK + ki[None] - SW_BLOCK) >= 0
    mask = (in_band[None] & valid)[None, :, None, None]
    scores = jnp.where(mask, scores, -jnp.inf)

    sink = sinks.astype(jnp.float32).reshape(SW_KV_HEADS, G)[None, None, :, :, None, None]
    m = jnp.maximum(jnp.max(scores, axis=-1, keepdims=True), sink)
    p = jnp.exp(scores - m)
    probs = p / (jnp.sum(p, axis=-1, keepdims=True) + jnp.exp(sink - m))
    o = jnp.einsum('bnhgqk,bnkhd->bnqhgd', probs.astype(vw.dtype), vw)
    return o.reshape(B, S, SW_Q_W)


def hybrid_layer(x, positions, pre_mix_g, w_in, dn_conv_w, dn_a_log, dn_dt_bias, dn_norm_g,
                 sw_sinks, w_up_dn, w_up_sw, w_o, post_mix_g, pre_mlp_g, w_ff1, w_ff2,
                 post_mlp_g):
    h = rmsnorm(x, pre_mix_g)
    proj = h @ w_in
    (dn_q, dn_k, dn_v, dn_z, dn_b, dn_a, sw_q, sw_k, sw_v,
     gate_a, gate_b) = jnp.split(proj, IN_OFFSETS, axis=-1)
    y_a = deltanet_branch(dn_q, dn_k, dn_v, dn_z, dn_b, dn_a, dn_conv_w, dn_a_log,
                          dn_dt_bias, dn_norm_g) @ w_up_dn
    y_b = swa_sink_branch(sw_q, sw_k, sw_v, positions, sw_sinks) @ w_up_sw
    mix = (jax.nn.sigmoid(gate_a) * y_a + jax.nn.sigmoid(gate_b) * y_b) @ w_o
    x = x + rmsnorm(mix, post_mix_g)

    h2 = rmsnorm(x, pre_mlp_g)
    ff = jnp.square(jax.nn.relu(h2 @ w_ff1)) @ w_ff2
    return x + rmsnorm(ff, post_mlp_g)


def _fwd_setup_inputs(seed: int = 0) -> dict:
    key = jax.random.key(seed)
    ks = jax.random.split(key, 20)
    f32 = jnp.float32

    def nrm(k, shape, scale):
        return jax.random.normal(k, shape, f32) * scale

    def gain(k, shape):
        return 1.0 + 0.02 * jax.random.normal(k, shape, f32)

    x = jax.random.normal(ks[0], (BATCH, SEQ, D_MODEL), f32)
    positions = jnp.broadcast_to(jnp.arange(SEQ, dtype=jnp.int32), (BATCH, SEQ))
    dt = jnp.exp(jax.random.uniform(ks[5], (DEPTH, DN_HEADS), f32,
                                    np.log(1e-3), np.log(1e-1)))
    return {
        'x': x,
        'positions': positions,
        'pre_mix_g': gain(ks[1], (DEPTH, D_MODEL)),
        'w_in': nrm(ks[2], (DEPTH, D_MODEL, D_IN), D_MODEL ** -0.5),
        'dn_conv_w': nrm(ks[3], (DEPTH, DN_CONV, 2 * DN_QK_W + DN_V_W), DN_CONV ** -0.5),
        'dn_a_log': jnp.log(jax.random.uniform(ks[4], (DEPTH, DN_HEADS), f32, 1.0, 16.0)),
        'dn_dt_bias': dt + jnp.log(-jnp.expm1(-dt)),
        'dn_norm_g': gain(ks[6], (DEPTH, DN_DV)),
        'sw_sinks': nrm(ks[7], (DEPTH, SW_Q_HEADS), 0.5),
        'w_up_dn': nrm(ks[8], (DEPTH, DN_V_W, D_MODEL), DN_V_W ** -0.5),
        'w_up_sw': nrm(ks[9], (DEPTH, SW_Q_W, D_MODEL), SW_Q_W ** -0.5),
        'w_o': nrm(ks[10], (DEPTH, D_MODEL, D_MODEL), D_MODEL ** -0.5),
        'post_mix_g': gain(ks[11], (DEPTH, D_MODEL)),
        'pre_mlp_g': gain(ks[12], (DEPTH, D_MODEL)),
        'w_ff1': nrm(ks[13], (DEPTH, D_MODEL, D_FF), D_MODEL ** -0.5),
        'w_ff2': nrm(ks[14], (DEPTH, D_FF, D_MODEL), D_FF ** -0.5),
        'post_mlp_g': gain(ks[15], (DEPTH, D_MODEL)),
    }


def _fwd_reference(x, positions, pre_mix_g, w_in, dn_conv_w, dn_a_log, dn_dt_bias, dn_norm_g,
              sw_sinks, w_up_dn, w_up_sw, w_o, post_mix_g, pre_mlp_g, w_ff1, w_ff2,
              post_mlp_g):
    for l in range(DEPTH):
        x = hybrid_layer(x, positions, pre_mix_g[l], w_in[l], dn_conv_w[l], dn_a_log[l],
                         dn_dt_bias[l], dn_norm_g[l], sw_sinks[l], w_up_dn[l], w_up_sw[l],
                         w_o[l], post_mix_g[l], pre_mlp_g[l], w_ff1[l], w_ff2[l],
                         post_mlp_g[l])
    return x


import jax as _jax
import jax.numpy as _jnp

TWIN_FORMAT = 'train_step'
FWD_PARAMS = ['x', 'positions', 'pre_mix_g', 'w_in', 'dn_conv_w', 'dn_a_log', 'dn_dt_bias', 'dn_norm_g', 'sw_sinks', 'w_up_dn', 'w_up_sw', 'w_o', 'post_mix_g', 'pre_mlp_g', 'w_ff1', 'w_ff2', 'post_mlp_g']
TWIN_WEIGHTS = ['pre_mix_g', 'w_in', 'dn_conv_w', 'dn_a_log', 'dn_dt_bias', 'dn_norm_g', 'sw_sinks', 'w_up_dn', 'w_up_sw', 'w_o', 'post_mix_g', 'pre_mlp_g', 'w_ff1', 'w_ff2', 'post_mlp_g']
TWIN_DIFF_INPUT = 'x'
TWIN_INPUTS = ['x', 'positions', 'pre_mix_g', 'w_in', 'dn_conv_w', 'dn_a_log', 'dn_dt_bias', 'dn_norm_g', 'sw_sinks', 'w_up_dn', 'w_up_sw', 'w_o', 'post_mix_g', 'pre_mlp_g', 'w_ff1', 'w_ff2', 'post_mlp_g', 'loss_target', 'm_pre_mix_g', 'm_w_in', 'm_dn_conv_w', 'm_dn_a_log', 'm_dn_dt_bias', 'm_dn_norm_g', 'm_sw_sinks', 'm_w_up_dn', 'm_w_up_sw', 'm_w_o', 'm_post_mix_g', 'm_pre_mlp_g', 'm_w_ff1', 'm_w_ff2', 'm_post_mlp_g', 'v_pre_mix_g', 'v_w_in', 'v_dn_conv_w', 'v_dn_a_log', 'v_dn_dt_bias', 'v_dn_norm_g', 'v_sw_sinks', 'v_w_up_dn', 'v_w_up_sw', 'v_w_o', 'v_post_mix_g', 'v_pre_mlp_g', 'v_w_ff1', 'v_w_ff2', 'v_post_mlp_g']
TWIN_OUTPUTS = ['loss', 'grad_x', 'grad_pre_mix_g', 'grad_w_in', 'grad_dn_conv_w', 'grad_dn_a_log', 'grad_dn_dt_bias', 'grad_dn_norm_g', 'grad_sw_sinks', 'grad_w_up_dn', 'grad_w_up_sw', 'grad_w_o', 'grad_post_mix_g', 'grad_pre_mlp_g', 'grad_w_ff1', 'grad_w_ff2', 'grad_post_mlp_g', 'delta_pre_mix_g', 'delta_w_in', 'delta_dn_conv_w', 'delta_dn_a_log', 'delta_dn_dt_bias', 'delta_dn_norm_g', 'delta_sw_sinks', 'delta_w_up_dn', 'delta_w_up_sw', 'delta_w_o', 'delta_post_mix_g', 'delta_pre_mlp_g', 'delta_w_ff1', 'delta_w_ff2', 'delta_post_mlp_g', 'new_m_pre_mix_g', 'new_m_w_in', 'new_m_dn_conv_w', 'new_m_dn_a_log', 'new_m_dn_dt_bias', 'new_m_dn_norm_g', 'new_m_sw_sinks', 'new_m_w_up_dn', 'new_m_w_up_sw', 'new_m_w_o', 'new_m_post_mix_g', 'new_m_pre_mlp_g', 'new_m_w_ff1', 'new_m_w_ff2', 'new_m_post_mlp_g', 'new_v_pre_mix_g', 'new_v_w_in', 'new_v_dn_conv_w', 'new_v_dn_a_log', 'new_v_dn_dt_bias', 'new_v_dn_norm_g', 'new_v_sw_sinks', 'new_v_w_up_dn', 'new_v_w_up_sw', 'new_v_w_o', 'new_v_post_mix_g', 'new_v_pre_mlp_g', 'new_v_w_ff1', 'new_v_w_ff2', 'new_v_post_mlp_g']
TWIN_LEAF_KINDS = {'loss': 'loss', 'grad_x': 'grad_x', 'grad_pre_mix_g': 'grad_w', 'grad_w_in': 'grad_w', 'grad_dn_conv_w': 'grad_w', 'grad_dn_a_log': 'grad_w', 'grad_dn_dt_bias': 'grad_w', 'grad_dn_norm_g': 'grad_w', 'grad_sw_sinks': 'grad_w', 'grad_w_up_dn': 'grad_w', 'grad_w_up_sw': 'grad_w', 'grad_w_o': 'grad_w', 'grad_post_mix_g': 'grad_w', 'grad_pre_mlp_g': 'grad_w', 'grad_w_ff1': 'grad_w', 'grad_w_ff2': 'grad_w', 'grad_post_mlp_g': 'grad_w', 'delta_pre_mix_g': 'delta_w', 'delta_w_in': 'delta_w', 'delta_dn_conv_w': 'delta_w', 'delta_dn_a_log': 'delta_w', 'delta_dn_dt_bias': 'delta_w', 'delta_dn_norm_g': 'delta_w', 'delta_sw_sinks': 'delta_w', 'delta_w_up_dn': 'delta_w', 'delta_w_up_sw': 'delta_w', 'delta_w_o': 'delta_w', 'delta_post_mix_g': 'delta_w', 'delta_pre_mlp_g': 'delta_w', 'delta_w_ff1': 'delta_w', 'delta_w_ff2': 'delta_w', 'delta_post_mlp_g': 'delta_w', 'new_m_pre_mix_g': 'new_m', 'new_m_w_in': 'new_m', 'new_m_dn_conv_w': 'new_m', 'new_m_dn_a_log': 'new_m', 'new_m_dn_dt_bias': 'new_m', 'new_m_dn_norm_g': 'new_m', 'new_m_sw_sinks': 'new_m', 'new_m_w_up_dn': 'new_m', 'new_m_w_up_sw': 'new_m', 'new_m_w_o': 'new_m', 'new_m_post_mix_g': 'new_m', 'new_m_pre_mlp_g': 'new_m', 'new_m_w_ff1': 'new_m', 'new_m_w_ff2': 'new_m', 'new_m_post_mlp_g': 'new_m', 'new_v_pre_mix_g': 'new_v', 'new_v_w_in': 'new_v', 'new_v_dn_conv_w': 'new_v', 'new_v_dn_a_log': 'new_v', 'new_v_dn_dt_bias': 'new_v', 'new_v_dn_norm_g': 'new_v', 'new_v_sw_sinks': 'new_v', 'new_v_w_up_dn': 'new_v', 'new_v_w_up_sw': 'new_v', 'new_v_w_o': 'new_v', 'new_v_post_mix_g': 'new_v', 'new_v_pre_mlp_g': 'new_v', 'new_v_w_ff1': 'new_v', 'new_v_w_ff2': 'new_v', 'new_v_post_mlp_g': 'new_v'}


def _forward(args):
    return _fwd_reference(*[args[k] for k in FWD_PARAMS])


def _output_shape():
    def fwd():
        inp = _fwd_setup_inputs(0)
        return _fwd_reference(*[inp[k] for k in FWD_PARAMS])
    out = _jax.eval_shape(fwd)
    return out.shape, out.dtype

N_MICROBATCH = 1
ADAM_LR = 0.001
ADAM_B1 = 0.9
ADAM_B2 = 0.999
ADAM_EPS = 1e-08
ADAM_WD = 0.01
ADAM_STEP = 10
PER_EXAMPLE_BATCH_AXIS = {'x': 0, 'positions': 0, 'loss_target': 0}
SHARED_INPUTS = []
_WEIGHT_DTYPES = {'pre_mix_g': _jnp.float32, 'w_in': _jnp.float32, 'dn_conv_w': _jnp.float32, 'dn_a_log': _jnp.float32, 'dn_dt_bias': _jnp.float32, 'dn_norm_g': _jnp.float32, 'sw_sinks': _jnp.float32, 'w_up_dn': _jnp.float32, 'w_up_sw': _jnp.float32, 'w_o': _jnp.float32, 'post_mix_g': _jnp.float32, 'pre_mlp_g': _jnp.float32, 'w_ff1': _jnp.float32, 'w_ff2': _jnp.float32, 'post_mlp_g': _jnp.float32}
MOMENT_SCALE = {'pre_mix_g': 3.704684e+01, 'w_in': 1.364924e+01, 'dn_conv_w': 9.612531e+00, 'dn_a_log': 2.301307e+01, 'dn_dt_bias': 2.186731e+01, 'dn_norm_g': 6.166287e+01, 'sw_sinks': 1.488846e+00, 'w_up_dn': 2.389382e+01, 'w_up_sw': 3.193776e+01, 'w_o': 3.870800e+01, 'post_mix_g': 7.718715e+01, 'pre_mlp_g': 1.972487e+01, 'w_ff1': 9.865685e+00, 'w_ff2': 4.827968e+01, 'post_mlp_g': 8.366531e+01}


def _to_microbatches(a, axis):
    t = _jnp.moveaxis(a, axis, 0)
    t = t.reshape((N_MICROBATCH, t.shape[0] // N_MICROBATCH) + t.shape[1:])
    return _jnp.moveaxis(t, 1, axis + 1)


def setup_inputs(seed: int = 0) -> dict:
    inp = _fwd_setup_inputs(seed)
    key = _jax.random.fold_in(_jax.random.key(seed), 7919)
    shape, _ = _output_shape()
    out = dict(inp)
    out["loss_target"] = _jax.random.normal(_jax.random.fold_in(key, 0), shape, _jnp.float32)
    for i, name in enumerate(TWIN_WEIGHTS):
        w = inp[name].astype(_jnp.float32)
        if MOMENT_SCALE is None:
            s = _jnp.sqrt(_jnp.mean(_jnp.square(w)) + 1e-30)
        else:
            s = MOMENT_SCALE[name]
        km, kv = _jax.random.split(_jax.random.fold_in(key, i + 1))
        out[name] = w
        out["m_" + name] = s * _jax.random.normal(km, w.shape, _jnp.float32)
        out["v_" + name] = (s * s) * _jax.random.uniform(kv, w.shape, _jnp.float32, 0.5, 1.5)
    if N_MICROBATCH > 1:
        for name, axis in PER_EXAMPLE_BATCH_AXIS.items():
            out[name] = _to_microbatches(out[name], axis)
    return {'x': out['x'], 'positions': out['positions'], 'pre_mix_g': out['pre_mix_g'], 'w_in': out['w_in'], 'dn_conv_w': out['dn_conv_w'], 'dn_a_log': out['dn_a_log'], 'dn_dt_bias': out['dn_dt_bias'], 'dn_norm_g': out['dn_norm_g'], 'sw_sinks': out['sw_sinks'], 'w_up_dn': out['w_up_dn'], 'w_up_sw': out['w_up_sw'], 'w_o': out['w_o'], 'post_mix_g': out['post_mix_g'], 'pre_mlp_g': out['pre_mlp_g'], 'w_ff1': out['w_ff1'], 'w_ff2': out['w_ff2'], 'post_mlp_g': out['post_mlp_g'], 'loss_target': out['loss_target'], 'm_pre_mix_g': out['m_pre_mix_g'], 'm_w_in': out['m_w_in'], 'm_dn_conv_w': out['m_dn_conv_w'], 'm_dn_a_log': out['m_dn_a_log'], 'm_dn_dt_bias': out['m_dn_dt_bias'], 'm_dn_norm_g': out['m_dn_norm_g'], 'm_sw_sinks': out['m_sw_sinks'], 'm_w_up_dn': out['m_w_up_dn'], 'm_w_up_sw': out['m_w_up_sw'], 'm_w_o': out['m_w_o'], 'm_post_mix_g': out['m_post_mix_g'], 'm_pre_mlp_g': out['m_pre_mlp_g'], 'm_w_ff1': out['m_w_ff1'], 'm_w_ff2': out['m_w_ff2'], 'm_post_mlp_g': out['m_post_mlp_g'], 'v_pre_mix_g': out['v_pre_mix_g'], 'v_w_in': out['v_w_in'], 'v_dn_conv_w': out['v_dn_conv_w'], 'v_dn_a_log': out['v_dn_a_log'], 'v_dn_dt_bias': out['v_dn_dt_bias'], 'v_dn_norm_g': out['v_dn_norm_g'], 'v_sw_sinks': out['v_sw_sinks'], 'v_w_up_dn': out['v_w_up_dn'], 'v_w_up_sw': out['v_w_up_sw'], 'v_w_o': out['v_w_o'], 'v_post_mix_g': out['v_post_mix_g'], 'v_pre_mlp_g': out['v_pre_mlp_g'], 'v_w_ff1': out['v_w_ff1'], 'v_w_ff2': out['v_w_ff2'], 'v_post_mlp_g': out['v_post_mlp_g']}


def _loss(weights, diff, rest, loss_target):
    with _jax.named_scope("forward"):
        args = {**rest, TWIN_DIFF_INPUT: diff, **{k: w.astype(_WEIGHT_DTYPES[k]) for k, w in weights.items()}}
        y = _forward(args)
    with _jax.named_scope("loss_head"):
        err = _jnp.square(y.astype(_jnp.float32) - loss_target)
        return 0.5 * _jnp.sum(_jnp.mean(err, axis=-1)) if err.ndim else 0.5 * err


def _adamw(w, g, m, v):
    m = ADAM_B1 * m + (1.0 - ADAM_B1) * g
    v = ADAM_B2 * v + (1.0 - ADAM_B2) * _jnp.square(g)
    m_hat = m / (1.0 - ADAM_B1 ** ADAM_STEP)
    v_hat = v / (1.0 - ADAM_B2 ** ADAM_STEP)
    delta = -ADAM_LR * (m_hat / (_jnp.sqrt(v_hat) + ADAM_EPS) + ADAM_WD * w)
    return delta, m, v


def reference(x, positions, pre_mix_g, w_in, dn_conv_w, dn_a_log, dn_dt_bias, dn_norm_g, sw_sinks, w_up_dn, w_up_sw, w_o, post_mix_g, pre_mlp_g, w_ff1, w_ff2, post_mlp_g, loss_target, m_pre_mix_g, m_w_in, m_dn_conv_w, m_dn_a_log, m_dn_dt_bias, m_dn_norm_g, m_sw_sinks, m_w_up_dn, m_w_up_sw, m_w_o, m_post_mix_g, m_pre_mlp_g, m_w_ff1, m_w_ff2, m_post_mlp_g, v_pre_mix_g, v_w_in, v_dn_conv_w, v_dn_a_log, v_dn_dt_bias, v_dn_norm_g, v_sw_sinks, v_w_up_dn, v_w_up_sw, v_w_o, v_post_mix_g, v_pre_mlp_g, v_w_ff1, v_w_ff2, v_post_mlp_g):
    given = dict(x=x, positions=positions, pre_mix_g=pre_mix_g, w_in=w_in, dn_conv_w=dn_conv_w, dn_a_log=dn_a_log, dn_dt_bias=dn_dt_bias, dn_norm_g=dn_norm_g, sw_sinks=sw_sinks, w_up_dn=w_up_dn, w_up_sw=w_up_sw, w_o=w_o, post_mix_g=post_mix_g, pre_mlp_g=pre_mlp_g, w_ff1=w_ff1, w_ff2=w_ff2, post_mlp_g=post_mlp_g, loss_target=loss_target, m_pre_mix_g=m_pre_mix_g, m_w_in=m_w_in, m_dn_conv_w=m_dn_conv_w, m_dn_a_log=m_dn_a_log, m_dn_dt_bias=m_dn_dt_bias, m_dn_norm_g=m_dn_norm_g, m_sw_sinks=m_sw_sinks, m_w_up_dn=m_w_up_dn, m_w_up_sw=m_w_up_sw, m_w_o=m_w_o, m_post_mix_g=m_post_mix_g, m_pre_mlp_g=m_pre_mlp_g, m_w_ff1=m_w_ff1, m_w_ff2=m_w_ff2, m_post_mlp_g=m_post_mlp_g, v_pre_mix_g=v_pre_mix_g, v_w_in=v_w_in, v_dn_conv_w=v_dn_conv_w, v_dn_a_log=v_dn_a_log, v_dn_dt_bias=v_dn_dt_bias, v_dn_norm_g=v_dn_norm_g, v_sw_sinks=v_sw_sinks, v_w_up_dn=v_w_up_dn, v_w_up_sw=v_w_up_sw, v_w_o=v_w_o, v_post_mix_g=v_post_mix_g, v_pre_mlp_g=v_pre_mlp_g, v_w_ff1=v_w_ff1, v_w_ff2=v_w_ff2, v_post_mlp_g=v_post_mlp_g)
    weights = {n: given[n] for n in TWIN_WEIGHTS}
    shared = {n: given[n] for n in SHARED_INPUTS}
    per_example = {n: given[n] for n in ['x', 'positions']}
    grad_fn = _jax.value_and_grad(_loss, argnums=(0, 1))

    def one_microbatch(ex, loss_target):
        ex = dict(ex)
        diff = ex.pop(TWIN_DIFF_INPUT)
        return grad_fn(weights, diff, {**shared, **ex}, loss_target)

    if N_MICROBATCH == 1:
        loss, (grad_w, grad_x) = one_microbatch(per_example, given["loss_target"])
    else:
        def body(carry, xs):
            loss_sum, grad_sum = carry
            l_k, (gw_k, gx_k) = one_microbatch(xs[0], xs[1])
            with _jax.named_scope("update"):
                return (loss_sum + l_k, _jax.tree.map(_jnp.add, grad_sum, gw_k)), gx_k

        init = (_jnp.zeros((), _jnp.float32), _jax.tree.map(_jnp.zeros_like, weights))
        (loss, grad_w), grad_x = _jax.lax.scan(body, init, (per_example, given["loss_target"]))
    with _jax.named_scope("update"):
        delta_w, new_m, new_v = {}, {}, {}
        for n in TWIN_WEIGHTS:
            delta_w[n], new_m[n], new_v[n] = _adamw(weights[n], grad_w[n], given["m_" + n], given["v_" + n])
    return (loss, grad_x, *[grad_w[n] for n in TWIN_WEIGHTS], *[delta_w[n] for n in TWIN_WEIGHTS],
            *[new_m[n] for n in TWIN_WEIGHTS], *[new_v[n] for n in TWIN_WEIGHTS])
```

```python
import functools
import math

import numpy as np
import jax
import jax.numpy as jnp
from jax import lax
from jax.experimental import pallas as pl
from jax.experimental.pallas import tpu as pltpu

f32 = jnp.float32
bf16 = jnp.bfloat16
HIGHEST = lax.Precision.HIGHEST

N_DEV = 8
D_MODEL = 1024
DN_HEADS = 8
DN_DK = 128
DN_CHUNK = 64
DN_CONV = 4
SW_Q_HEADS = 16
SW_KV_HEADS = 2
SW_HEAD_DIM = 64
SW_BLOCK = 128
ROPE_THETA = 500000.0
ROT_DIM = SW_HEAD_DIM // 4
D_FF = 4 * D_MODEL
EPS = 1e-6
LANES = 128
CONV_HALO = 8
NEG_BIG = -1e30

ADAM_LR = 0.001
ADAM_B1 = 0.9
ADAM_B2 = 0.999
ADAM_EPS = 1e-08
ADAM_WD = 0.01
ADAM_STEP = 10

PROJ_W = 7680
CB_Q, CB_K, CB_V, CB_Z, CB_SWQ, CB_GA, CB_GB = 0, 1, 2, 3, 4, 5, 6
CB_SWK, CB_SWV, CB_BA = 56, 57, 58

NN = ((1,), (0,))
NT = ((1,), (1,))
TN = ((0,), (0,))


def _mm(a, b, dims, hi=False):
    if hi:
        return lax.dot_general(a.astype(f32), b.astype(f32), (dims, ((), ())), precision=HIGHEST,
                               preferred_element_type=f32)
    return lax.dot_general(a.astype(bf16), b.astype(bf16), (dims, ((), ())), preferred_element_type=f32)


def _matmul(name, a, b, form, out_dtype, tm=512, tn=512, tk=1024):
    if form == "nn":
        (M, K), (_, N) = a.shape, b.shape
    elif form == "nt":
        (M, K), (N, _) = a.shape, b.shape
    else:
        (K, M), (_, N) = a.shape, b.shape
    tm, tn, tk = min(tm, M), min(tn, N), min(tk, K)
    assert M % tm == 0 and N % tn == 0 and K % tk == 0, (name, M, N, K, tm, tn, tk)
    nk = K // tk
    dims = {"nn": NN, "nt": NT, "tn": TN}[form]

    def body(a_ref, b_ref, o_ref, acc_ref):
        part = lax.dot_general(a_ref[...], b_ref[...], (dims, ((), ())), preferred_element_type=f32)
        if nk == 1:
            o_ref[...] = part.astype(out_dtype)
        else:
            k = pl.program_id(2)

            @pl.when(k == 0)
            def _():
                acc_ref[...] = part

            @pl.when(k > 0)
            def _():
                acc_ref[...] += part

            @pl.when(k == nk - 1)
            def _():
                o_ref[...] = acc_ref[...].astype(out_dtype)

    if form == "tn":
        a_spec = pl.BlockSpec((tk, tm), lambda i, j, k: (k, i))
    else:
        a_spec = pl.BlockSpec((tm, tk), lambda i, j, k: (i, k))
    if form == "nt":
        b_spec = pl.BlockSpec((tn, tk), lambda i, j, k: (j, k))
    else:
        b_spec = pl.BlockSpec((tk, tn), lambda i, j, k: (k, j))
    return pl.pallas_call(
        body, name=name,
        grid=(M // tm, N // tn, nk),
        in_specs=[a_spec, b_spec],
        out_specs=pl.BlockSpec((tm, tn), lambda i, j, k: (i, j)),
        out_shape=jax.ShapeDtypeStruct((M, N), out_dtype),
        scratch_shapes=[pltpu.VMEM((tm, tn) if nk > 1 else (8, 128), f32)],
        compiler_params=pltpu.CompilerParams(dimension_semantics=("parallel", "parallel", "arbitrary")),
    )(a, b)


def _tile_specs(ins, halo_ids, params, TM, HR, row_of):
    specs = [pl.BlockSpec((TM, w), lambda i, cb=cb: (row_of(i), cb)) for (_, w, cb) in ins]
    for h in halo_ids:
        _, w, cb = ins[h]
        specs.append(pl.BlockSpec((HR, w), lambda i, cb=cb: (jnp.maximum(row_of(i) * (TM // HR) - 1, 0), cb)))
    for p in params:
        specs.append(pl.BlockSpec(p.shape, lambda i, nd=p.ndim: (0,) * nd))
    return specs


def _tile_fwd(name, fn, T, TM, ins, params, outs, halo_ids=(), HR=CONV_HALO):
    TM = min(TM, T)
    n = T // TM
    ni, nh, npar = len(ins), len(halo_ids), len(params)

    def body(*refs):
        in_v = [r[...] for r in refs[:ni]]
        halo_v = [r[...] for r in refs[ni:ni + nh]]
        par_v = [r[...] for r in refs[ni + nh:ni + nh + npar]]
        o_refs = refs[ni + nh + npar:]
        first = pl.program_id(0) == 0
        vals = fn(first, in_v, halo_v, par_v)
        for o, val in zip(o_refs, vals):
            o[...] = val.astype(o.dtype)

    res = pl.pallas_call(
        body, name=name, grid=(n,),
        in_specs=_tile_specs(ins, halo_ids, params, TM, HR, lambda i: i),
        out_specs=[pl.BlockSpec((TM, w), lambda i: (i, 0)) for (w, _) in outs],
        out_shape=[jax.ShapeDtypeStruct((T, w), dt) for (w, dt) in outs],
        compiler_params=pltpu.CompilerParams(dimension_semantics=("arbitrary",)),
    )(*[a for (a, _, _) in ins], *[ins[h][0] for h in halo_ids], *params)
    return list(res)


def _tile_bwd(name, fn, T, TM, ins, params, cts, din, dpar, halo_ids=(), HR=CONV_HALO):
    TM = min(TM, T)
    n = T // TM
    ni, nh, npar, nc = len(ins), len(halo_ids), len(params), len(cts)
    din_ids = [j for (j, _) in din]
    dh_ids = [h for h in halo_ids if h in din_ids]
    nd, ndp, ndh = len(din), len(dpar), len(dh_ids)

    def body(*refs):
        in_v = [r[...] for r in refs[:ni]]
        halo_v = [r[...] for r in refs[ni:ni + nh]]
        par_v = [r[...] for r in refs[ni + nh:ni + nh + npar]]
        ct_v = [r[...].astype(f32) for r in refs[ni + nh + npar:ni + nh + npar + nc]]
        o_refs = refs[ni + nh + npar + nc:ni + nh + npar + nc + nd + ndp]
        carry_refs = refs[ni + nh + npar + nc + nd + ndp:]
        i = pl.program_id(0)
        first = i == n - 1

        def g(d_in, d_halo, d_par):
            full_in = list(in_v)
            for j, val in zip(din_ids, d_in):
                full_in[j] = val
            full_halo = list(halo_v)
            for h, val in zip(dh_ids, d_halo):
                full_halo[list(halo_ids).index(h)] = val
            full_par = list(par_v)
            for j, val in zip(dpar, d_par):
                full_par[j] = val
            return tuple(fn(first, full_in, full_halo, full_par))

        prim = ([in_v[j].astype(f32) for j in din_ids],
                [halo_v[list(halo_ids).index(h)].astype(f32) for h in dh_ids],
                [par_v[j] for j in dpar])
        _, vjp = jax.vjp(g, *prim)
        g_in, g_halo, g_par = vjp(tuple(ct_v))

        @pl.when(i == 0)
        def _():
            for c in carry_refs:
                c[...] = jnp.zeros_like(c)
            for o in o_refs[nd:]:
                o[...] = jnp.zeros_like(o)

        for slot, (j, _) in enumerate(din):
            val = g_in[slot]
            if j in dh_ids:
                c = carry_refs[dh_ids.index(j)]
                val = jnp.concatenate([val[:TM - HR], val[TM - HR:] + c[...]], axis=0) if TM > HR else val + c[...]
                c[...] = g_halo[dh_ids.index(j)]
            o_refs[slot][...] = val.astype(o_refs[slot].dtype)
        for slot in range(ndp):
            o_refs[nd + slot][...] += g_par[slot]

    rev = lambda i: n - 1 - i
    in_specs = _tile_specs(ins, halo_ids, params, TM, HR, rev)
    ct_specs = [pl.BlockSpec((TM, w), lambda i, cb=cb: (rev(i), cb)) for (_, w, cb) in cts]
    out_specs = [pl.BlockSpec((TM, ins[j][1]), lambda i: (rev(i), 0)) for j in din_ids]
    out_specs += [pl.BlockSpec(params[j].shape, lambda i, nd_=params[j].ndim: (0,) * nd_) for j in dpar]
    out_shape = [jax.ShapeDtypeStruct((T, ins[j][1]), dt) for (j, dt) in din]
    out_shape += [jax.ShapeDtypeStruct(params[j].shape, f32) for j in dpar]
    res = pl.pallas_call(
        body, name=name, grid=(n,),
        in_specs=in_specs + ct_specs,
        out_specs=out_specs,
        out_shape=out_shape,
        scratch_shapes=[pltpu.VMEM((HR, ins[h][1]), f32) for h in dh_ids],
        compiler_params=pltpu.CompilerParams(dimension_semantics=("arbitrary",)),
    )(*[a for (a, _, _) in ins], *[ins[h][0] for h in halo_ids], *params, *[a for (a, _, _) in cts])
    return list(res)


def _rms(x, g):
    return x * lax.rsqrt(jnp.mean(x * x, axis=-1, keepdims=True) + EPS) * g


def _fn_prenorm(first, ins, halos, params):
    (x,), (g,) = ins, params
    x = x.astype(f32)
    return [_rms(x, g), x]


def _fn_postmix(first, ins, halos, params):
    (x, mix), (g2, g3) = ins, params
    x1 = x + _rms(mix, g2)
    return [x1, _rms(x1, g3)]


def _fn_postmlp(first, ins, halos, params):
    (x1, ff), (g4,) = ins, params
    return [x1 + _rms(ff, g4)]


def _fn_rms_only(first, ins, halos, params):
    (ff,), (g4,) = ins, params
    return [_rms(ff, g4)]


def _fn_relu2(first, ins, halos, params):
    (h,) = ins
    return [jnp.square(jnp.maximum(h, 0.0))]


def _fn_merge(first, ins, halos, params):
    ga, gb, ya, yb = ins
    return [jax.nn.sigmoid(ga) * ya + jax.nn.sigmoid(gb) * yb]


def _make_fn_conv(norm_scale):
    def fn(first, ins, halos, params):
        (x,), (xp,), (w,) = ins, halos, params
        TM = x.shape[0]
        xp = jnp.where(first, 0.0, xp)
        xe = jnp.concatenate([xp, x], axis=0)
        off = CONV_HALO - (DN_CONV - 1)
        y = xe[off:off + TM] * w[0:1]
        for j in range(1, DN_CONV):
            y = y + xe[off + j:off + j + TM] * w[j:j + 1]
        y = jax.nn.silu(y)
        if norm_scale is None:
            return [y]
        outs = []
        for h in range(DN_HEADS):
            yh = y[:, DN_DK * h:DN_DK * (h + 1)]
            outs.append(yh * lax.rsqrt(jnp.sum(yh * yh, axis=-1, keepdims=True) + EPS) * norm_scale)
        return [jnp.concatenate(outs, axis=-1)]
    return fn


def _fn_gates(first, ins, halos, params):
    (ba,), (avec, dvec) = ins, params
    lane = lax.broadcasted_iota(jnp.int32, ba.shape, 1)
    beta = jax.nn.sigmoid(ba)
    g = -jnp.exp(avec) * jax.nn.softplus(ba + dvec)
    return [jnp.where(lane < DN_HEADS, beta, jnp.where(lane < 2 * DN_HEADS, g, 0.0))]


def _fn_dnpost(first, ins, halos, params):
    (o, z), (ng,) = ins, params
    outs = []
    for h in range(DN_HEADS):
        sl = slice(DN_DK * h, DN_DK * (h + 1))
        outs.append(_rms(o[:, sl], ng) * jax.nn.silu(z[:, sl]))
    return [jnp.concatenate(outs, axis=-1)]


def _fn_swa(first, ins, halos, params):
    q, k, v, cos, sin = ins
    kp, vp, cosp, sinp = halos
    sinks, rot, sel_a0, sel_b0, sel_a1, sel_b1 = params
    B = q.shape[0]

    def rope(x, c, s):
        return x * c + _mm(x, rot, NN, hi=True) * s

    kcat = jnp.concatenate([rope(kp, cosp, sinp), rope(k, cos, sin)], axis=0)
    vcat = jnp.concatenate([vp, v], axis=0)
    r = lax.broadcasted_iota(jnp.int32, (B, 2 * B), 0)
    c = lax.broadcasted_iota(jnp.int32, (B, 2 * B), 1)
    mask = (c > r) & (c <= r + B) & ((c >= B) | jnp.logical_not(first))
    group = SW_Q_HEADS // SW_KV_HEADS
    outs = []
    for hk in range(SW_KV_HEADS):
        sa, sb = (sel_a0, sel_b0) if hk == 0 else (sel_a1, sel_b1)
        kv = [(_mm(kcat, sa, NN), _mm(vcat, sa, NN)), (_mm(kcat, sb, NN), _mm(vcat, sb, NN))]
        for jp in range(group // 2):
            j = hk * (group // 2) + jp
            qp = rope(q[:, LANES * j:LANES * (j + 1)], cos, sin)
            o_pair = None
            for half in range(2):
                head = 2 * j + half
                kx, vx = kv[half]
                s = _mm(qp, kx, NT) * (SW_HEAD_DIM ** -0.5)
                s = jnp.where(mask, s, NEG_BIG)
                sink = sinks[:, head:head + 1]
                m = jnp.maximum(jnp.max(s, axis=-1, keepdims=True), sink)
                p = jnp.exp(s - m)
                probs = p / (jnp.sum(p, axis=-1, keepdims=True) + jnp.exp(sink - m))
                o = _mm(probs, vx, NN)
                o_pair = o if o_pair is None else o_pair + o
            outs.append(o_pair)
    return [jnp.concatenate(outs, axis=-1)]


def _fn_loss(first, ins, halos, params):
    y, tgt = ins
    return [y - tgt]


@jax.custom_vjp
def _inv_unit_lower(L):
    C = L.shape[0]
    ii = lax.broadcasted_iota(jnp.int32, (C, C), 0)
    jj = lax.broadcasted_iota(jnp.int32, (C, C), 1)

    def off(level):
        same_pair = jnp.right_shift(ii, level + 1) == jnp.right_shift(jj, level + 1)
        lower_left = (jnp.bitwise_and(jnp.right_shift(ii, level), 1) == 1) & (jnp.bitwise_and(jnp.right_shift(jj, level), 1) == 0)
        return jnp.where(same_pair & lower_left, L, 0.0)

    T_ = (ii == jj).astype(f32) - off(0)
    for level in range(1, int(math.log2(C))):
        T_ = T_ - _mm(_mm(T_, off(level), NN), T_, NN)
    return T_


def _inv_fwd(L):
    T_ = _inv_unit_lower(L)
    return T_, T_


def _inv_bwd(T_, dT):
    return (-_mm(_mm(T_, dT, TN), T_, NT),)


_inv_unit_lower.defvjp(_inv_fwd, _inv_bwd)


def _dn_chunk(q, k, v, gb, S):
    C = q.shape[0]
    ii = lax.broadcasted_iota(jnp.int32, (C, C), 0)
    jj = lax.broadcasted_iota(jnp.int32, (C, C), 1)
    causal, strict = ii >= jj, ii > jj
    gc_all = _mm(causal.astype(f32), gb, NN, hi=True)
    eye = (lax.broadcasted_iota(jnp.int32, (LANES, LANES), 0)
           == lax.broadcasted_iota(jnp.int32, (LANES, LANES), 1)).astype(f32)
    gc_t = _mm(eye, gc_all, NT, hi=True)
    outs, s_new = [], []
    for h in range(DN_HEADS):
        sl = slice(DN_DK * h, DN_DK * (h + 1))
        qh, kh, vh = q[:, sl], k[:, sl], v[:, sl]
        beta = gb[:, h:h + 1]
        gcol = gc_all[:, DN_HEADS + h:DN_HEADS + h + 1]
        grow = gc_t[DN_HEADS + h:DN_HEADS + h + 1, :]
        decay = jnp.where(causal, jnp.exp(jnp.where(causal, gcol - grow, 0.0)), 0.0)
        kb = kh * beta
        L = jnp.where(strict, _mm(kb, kh, NT) * decay, 0.0)
        tinv = _inv_unit_lower(L)
        eg = jnp.exp(gcol)
        u = _mm(tinv, vh * beta, NN)
        w = _mm(tinv, kb * eg, NN)
        a_intra = _mm(qh, kh, NT) * decay
        gl = gcol[C - 1:C, :]
        k_dec = kh * jnp.exp(gl - gcol)
        v_new = u - _mm(w, S[h], NN)
        outs.append(_mm(qh * eg, S[h], NN) + _mm(a_intra, v_new, NN))
        s_new.append(S[h] * jnp.exp(gl) + _mm(k_dec, v_new, TN))
    return jnp.concatenate(outs, axis=-1), tuple(s_new)


def _delta_fwd(qn, kn, vv, gb):
    T = qn.shape[0]
    C = DN_CHUNK
    n = T // C
    W = DN_HEADS * DN_DK

    def body(q_ref, k_ref, v_ref, gb_ref, o_ref, hist_ref, s_ref):
        @pl.when(pl.program_id(0) == 0)
        def _():
            s_ref[...] = jnp.zeros_like(s_ref)

        S = tuple(s_ref[h] for h in range(DN_HEADS))
        for h in range(DN_HEADS):
            hist_ref[0, h] = S[h]
        o, s_new = _dn_chunk(q_ref[...], k_ref[...], v_ref[...], gb_ref[...], S)
        o_ref[...] = o
        for h in range(DN_HEADS):
            s_ref[h] = s_new[h]

    row = pl.BlockSpec((C, W), lambda i: (i, 0))
    return pl.pallas_call(
        body, name="delta_fwd", grid=(n,),
        in_specs=[row, row, row, pl.BlockSpec((C, LANES), lambda i: (i, 0))],
        out_specs=[row, pl.BlockSpec((1, DN_HEADS, DN_DK, DN_DK), lambda i: (i, 0, 0, 0))],
        out_shape=[jax.ShapeDtypeStruct((T, W), f32), jax.ShapeDtypeStruct((n, DN_HEADS, DN_DK, DN_DK), f32)],
        scratch_shapes=[pltpu.VMEM((DN_HEADS, DN_DK, DN_DK), f32)],
        compiler_params=pltpu.CompilerParams(dimension_semantics=("arbitrary",)),
    )(qn, kn, vv, gb)


def _delta_bwd(qn, kn, vv, gb, hist, do):
    T = qn.shape[0]
    C = DN_CHUNK
    n = T // C
    W = DN_HEADS * DN_DK

    def body(q_ref, k_ref, v_ref, gb_ref, hist_ref, do_ref, dq_ref, dk_ref, dv_ref, dgb_ref, ds_ref):
        @pl.when(pl.program_id(0) == 0)
        def _():
            ds_ref[...] = jnp.zeros_like(ds_ref)

        S = tuple(hist_ref[0, h] for h in range(DN_HEADS))
        _, vjp = jax.vjp(_dn_chunk, q_ref[...], k_ref[...], v_ref[...], gb_ref[...], S)
        dS = tuple(ds_ref[h] for h in range(DN_HEADS))
        dq, dk, dv, dgb, dS_in = vjp((do_ref[...], dS))
        dq_ref[...] = dq
        dk_ref[...] = dk
        dv_ref[...] = dv
        dgb_ref[...] = dgb
        for h in range(DN_HEADS):
            ds_ref[h] = dS_in[h]

    row = pl.BlockSpec((C, W), lambda i: (n - 1 - i, 0))
    small = pl.BlockSpec((C, LANES), lambda i: (n - 1 - i, 0))
    return pl.pallas_call(
        body, name="delta_bwd", grid=(n,),
        in_specs=[row, row, row, small, pl.BlockSpec((1, DN_HEADS, DN_DK, DN_DK), lambda i: (n - 1 - i, 0, 0, 0)), row],
        out_specs=[row, row, row, small],
        out_shape=[jax.ShapeDtypeStruct((T, W), f32)] * 3 + [jax.ShapeDtypeStruct((T, LANES), f32)],
        scratch_shapes=[pltpu.VMEM((DN_HEADS, DN_DK, DN_DK), f32)],
        compiler_params=pltpu.CompilerParams(dimension_semantics=("arbitrary",)),
    )(qn, kn, vv, gb, hist, do)


TM_ROW = 256
W1 = D_MODEL


def _first_only(fn):
    return lambda *a: fn(*a)[:1]


def _swa_args(proj, cst):
    ins = [(proj, W1, CB_SWQ), (proj, LANES, CB_SWK), (proj, LANES, CB_SWV), (cst["cos"], LANES, 0), (cst["sin"], LANES, 0)]
    return ins, (1, 2, 3, 4)


def _layer_fwd(x, p, cst):
    T = x.shape[0]
    r = {"x": x}
    (h,) = _tile_fwd("prenorm", _first_only(_fn_prenorm), T, TM_ROW, [(x, W1, 0)], [p["g1"]], [(W1, bf16)])
    proj = _matmul("proj", h, p["w_in"], "nn", f32, tm=1024, tn=1536)
    conv = lambda nm, cb, scale: _tile_fwd(nm, _make_fn_conv(scale), T, TM_ROW, [(proj, W1, cb)],
                                            [p["conv_w"][:, W1 * cb:W1 * (cb + 1)]], [(W1, f32)], halo_ids=(0,))[0]
    qn = conv("conv_q", CB_Q, DN_DK ** -0.5)
    kn = conv("conv_k", CB_K, 1.0)
    vv = conv("conv_v", CB_V, None)
    (gbt,) = _tile_fwd("gates", _fn_gates, T, TM_ROW, [(proj, LANES, CB_BA)], [p["avec"], p["dvec"]], [(LANES, f32)])
    o, hist = _delta_fwd(qn, kn, vv, gbt)
    (dn_out,) = _tile_fwd("dnpost", _fn_dnpost, T, TM_ROW, [(o, W1, 0), (proj, W1, CB_Z)], [p["ng"]], [(W1, bf16)])
    sw_ins, sw_halo = _swa_args(proj, cst)
    sw_par = [p["sinks"], cst["rot"], cst["sel_a0"], cst["sel_b0"], cst["sel_a1"], cst["sel_b1"]]
    (sw_out,) = _tile_fwd("swa", _fn_swa, T, SW_BLOCK, sw_ins, sw_par, [(W1, bf16)], halo_ids=sw_halo, HR=SW_BLOCK)
    y_a = _matmul("up_dn", dn_out, p["w_up_dn"], "nn", f32, tm=1024, tn=1024)
    y_b = _matmul("up_sw", sw_out, p["w_up_sw"], "nn", f32, tm=1024, tn=1024)
    (gated,) = _tile_fwd("merge", _fn_merge, T, TM_ROW,
                         [(proj, W1, CB_GA), (proj, W1, CB_GB), (y_a, W1, 0), (y_b, W1, 0)], [], [(W1, bf16)])
    mix = _matmul("w_o", gated, p["w_o"], "nn", f32, tm=1024, tn=1024)
    x1, h2 = _tile_fwd("postmix", _fn_postmix, T, TM_ROW, [(x, W1, 0), (mix, W1, 0)], [p["g2"], p["g3"]],
                       [(W1, f32), (W1, bf16)])
    ffh = _matmul("ff1", h2, p["w_ff1"], "nn", f32, tm=1024, tn=1024)
    (act,) = _tile_fwd("relu2", _fn_relu2, T, TM_ROW, [(ffh, D_FF, 0)], [], [(D_FF, bf16)])
    ff = _matmul("ff2", act, p["w_ff2"], "nn", f32, tm=1024, tn=1024)
    (x2,) = _tile_fwd("postmlp", _fn_postmlp, T, TM_ROW, [(x1, W1, 0), (ff, W1, 0)], [p["g4"]], [(W1, f32)])
    r.update(h=h, proj=proj, qn=qn, kn=kn, vv=vv, gbt=gbt, o=o, hist=hist, dn_out=dn_out, sw_out=sw_out,
             y_a=y_a, y_b=y_b, gated=gated, mix=mix, h2=h2, ffh=ffh, act=act, ff=ff)
    return x2, r


def _layer_bwd(dx2, r, p, cst):
    T = dx2.shape[0]
    x, proj = r["x"], r["proj"]
    g = {}
    dff, g["g4"] = _tile_bwd("postmlp_b", _fn_rms_only, T, TM_ROW, [(r["ff"], W1, 0)], [p["g4"]], [(dx2, W1, 0)],
                             [(0, bf16)], [0])
    dact = _matmul("ff2_dx", dff, p["w_ff2"], "nt", f32, tm=1024, tn=1024)
    g["w_ff2"] = _matmul("ff2_dw", r["act"], dff, "tn", f32, tm=1024, tn=1024)
    (dffh,) = _tile_bwd("relu2_b", _fn_relu2, T, TM_ROW, [(r["ffh"], D_FF, 0)], [], [(dact, D_FF, 0)], [(0, bf16)], [])
    dh2 = _matmul("ff1_dx", dffh, p["w_ff1"], "nt", f32, tm=1024, tn=1024)
    g["w_ff1"] = _matmul("ff1_dw", r["h2"], dffh, "tn", f32, tm=1024, tn=1024)
    dx1, dmix, g["g2"], g["g3"] = _tile_bwd("postmix_b", _fn_postmix, T, TM_ROW, [(x, W1, 0), (r["mix"], W1, 0)],
                                            [p["g2"], p["g3"]], [(dx2, W1, 0), (dh2, W1, 0)], [(0, f32), (1, bf16)], [0, 1])
    dgated = _matmul("w_o_dx", dmix, p["w_o"], "nt", f32, tm=1024, tn=1024)
    g["w_o"] = _matmul("w_o_dw", r["gated"], dmix, "tn", f32, tm=1024, tn=1024)
    dga, dgb, dya, dyb = _tile_bwd("merge_b", _fn_merge, T, TM_ROW,
                                   [(proj, W1, CB_GA), (proj, W1, CB_GB), (r["y_a"], W1, 0), (r["y_b"], W1, 0)], [],
                                   [(dgated, W1, 0)], [(0, bf16), (1, bf16), (2, bf16), (3, bf16)], [])
    d_dn = _matmul("up_dn_dx", dya, p["w_up_dn"], "nt", f32, tm=1024, tn=1024)
    g["w_up_dn"] = _matmul("up_dn_dw", r["dn_out"], dya, "tn", f32, tm=1024, tn=1024)
    d_sw = _matmul("up_sw_dx", dyb, p["w_up_sw"], "nt", f32, tm=1024, tn=1024)
    g["w_up_sw"] = _matmul("up_sw_dw", r["sw_out"], dyb, "tn", f32, tm=1024, tn=1024)
    do, dz, g["ng"] = _tile_bwd("dnpost_b", _fn_dnpost, T, TM_ROW, [(r["o"], W1, 0), (proj, W1, CB_Z)], [p["ng"]],
                                [(d_dn, W1, 0)], [(0, f32), (1, bf16)], [0])
    dqn, dkn, dvv, dgbt = _delta_bwd(r["qn"], r["kn"], r["vv"], r["gbt"], r["hist"], do)
    conv_b = lambda nm, cb, scale, ct: _tile_bwd(nm, _make_fn_conv(scale), T, TM_ROW, [(proj, W1, cb)],
                                                 [p["conv_w"][:, W1 * cb:W1 * (cb + 1)]], [(ct, W1, 0)], [(0, bf16)], [0],
                                                 halo_ids=(0,))
    dq_in, dcw_q = conv_b("conv_q_b", CB_Q, DN_DK ** -0.5, dqn)
    dk_in, dcw_k = conv_b("conv_k_b", CB_K, 1.0, dkn)
    dv_in, dcw_v = conv_b("conv_v_b", CB_V, None, dvv)
    g["conv_w"] = jnp.concatenate([dcw_q, dcw_k, dcw_v], axis=-1)
    dba, g["avec"], g["dvec"] = _tile_bwd("gates_b", _fn_gates, T, TM_ROW, [(proj, LANES, CB_BA)], [p["avec"], p["dvec"]],
                                          [(dgbt, LANES, 0)], [(0, bf16)], [0, 1])
    sw_ins, sw_halo = _swa_args(proj, cst)
    sw_par = [p["sinks"], cst["rot"], cst["sel_a0"], cst["sel_b0"], cst["sel_a1"], cst["sel_b1"]]
    dswq, dswk, dswv, g["sinks"] = _tile_bwd("swa_b", _fn_swa, T, SW_BLOCK, sw_ins, sw_par, [(d_sw, W1, 0)],
                                             [(0, bf16), (1, bf16), (2, bf16)], [0], halo_ids=sw_halo, HR=SW_BLOCK)
    dproj = jnp.concatenate([dq_in, dk_in, dv_in, dz, dswq, dga, dgb, dswk, dswv, dba, jnp.zeros((T, LANES), bf16)], axis=-1)
    dh = _matmul("proj_dx", dproj, p["w_in"], "nt", f32, tm=1024, tn=1024, tk=1536)
    g["w_in"] = _matmul("proj_dw", r["h"], dproj, "tn", f32, tm=1024, tn=1536)
    dx, g["g1"] = _tile_bwd("prenorm_b", _fn_prenorm, T, TM_ROW, [(x, W1, 0)], [p["g1"]], [(dh, W1, 0), (dx1, W1, 0)],
                            [(0, f32)], [0])
    return dx, g


_OFF_BA, _OFF_SWQ, _OFF_SWK, _OFF_GA, _D_IN = 4096, 4112, 5136, 5392, 7440


def _proj_cols(w):
    pad = lambda n: jnp.zeros(w.shape[:-1] + (n,), w.dtype)
    return jnp.concatenate([w[..., :_OFF_BA], w[..., _OFF_SWQ:_OFF_SWK], w[..., _OFF_GA:_D_IN],
                            w[..., _OFF_SWK:_OFF_GA], w[..., _OFF_BA:_OFF_SWQ], pad(PROJ_W - _D_IN)], axis=-1)


def _proj_cols_inv(w):
    n_ba = _OFF_SWQ - _OFF_BA
    return jnp.concatenate([w[..., :4096], w[..., 7424:7424 + n_ba], w[..., 4096:5120], w[..., 7168:7424],
                            w[..., 5120:7168]], axis=-1)


def _lane_pad(v, at):
    return jnp.pad(v.astype(f32), (at, LANES - at - v.shape[0])).reshape(1, LANES)


def _layer_params(w):
    row = lambda v: v.reshape(1, -1).astype(f32)
    return dict(
        g1=row(w["pre_mix_g"]), g2=row(w["post_mix_g"]), g3=row(w["pre_mlp_g"]), g4=row(w["post_mlp_g"]),
        w_in=_proj_cols(w["w_in"]).astype(bf16), conv_w=w["dn_conv_w"].astype(f32),
        avec=_lane_pad(w["dn_a_log"], DN_HEADS), dvec=_lane_pad(w["dn_dt_bias"], DN_HEADS),
        ng=row(w["dn_norm_g"]), sinks=_lane_pad(w["sw_sinks"], 0),
        w_up_dn=w["w_up_dn"].astype(bf16), w_up_sw=w["w_up_sw"].astype(bf16), w_o=w["w_o"].astype(bf16),
        w_ff1=w["w_ff1"].astype(bf16), w_ff2=w["w_ff2"].astype(bf16))


def _layer_grads_ref_layout(g):
    return dict(
        pre_mix_g=g["g1"][0], post_mix_g=g["g2"][0], pre_mlp_g=g["g3"][0], post_mlp_g=g["g4"][0],
        w_in=_proj_cols_inv(g["w_in"]), dn_conv_w=g["conv_w"],
        dn_a_log=g["avec"][0, DN_HEADS:2 * DN_HEADS], dn_dt_bias=g["dvec"][0, DN_HEADS:2 * DN_HEADS],
        dn_norm_g=g["ng"][0], sw_sinks=g["sinks"][0, :SW_Q_HEADS],
        w_up_dn=g["w_up_dn"], w_up_sw=g["w_up_sw"], w_o=g["w_o"], w_ff1=g["w_ff1"], w_ff2=g["w_ff2"])


def _consts(positions):
    T = positions.shape[0]
    half = ROT_DIM // 2
    inv_freq = ROPE_THETA ** (-jnp.arange(half, dtype=f32) * (2.0 / ROT_DIM))
    ang = positions.astype(f32)[:, None] * inv_freq
    cos8, sin8 = jnp.cos(ang), jnp.sin(ang)
    rest = SW_HEAD_DIM - ROT_DIM
    c64 = jnp.concatenate([cos8, cos8, jnp.ones((T, rest), f32)], axis=-1)
    s64 = jnp.concatenate([sin8, sin8, jnp.zeros((T, rest), f32)], axis=-1)
    rot = np.zeros((LANES, LANES), np.float32)
    sel = np.zeros((2, 2, LANES, LANES), np.float32)
    for base in (0, SW_HEAD_DIM):
        for i in range(half):
            rot[base + half + i, base + i] = -1.0
            rot[base + i, base + half + i] = 1.0
    for hk in range(SW_KV_HEADS):
        for d in range(SW_HEAD_DIM):
            sel[hk, 0, SW_HEAD_DIM * hk + d, d] = 1.0
            sel[hk, 1, SW_HEAD_DIM * hk + d, SW_HEAD_DIM + d] = 1.0
    return dict(cos=jnp.concatenate([c64, c64], axis=-1), sin=jnp.concatenate([s64, s64], axis=-1),
                rot=jnp.asarray(rot), sel_a0=jnp.asarray(sel[0, 0]), sel_b0=jnp.asarray(sel[0, 1]),
                sel_a1=jnp.asarray(sel[1, 0]), sel_b1=jnp.asarray(sel[1, 1]))


def _loss(y, tgt):
    T, W = y.shape
    TM = min(TM_ROW, T)
    n = T // TM

    def body(y_ref, t_ref, dy_ref, acc_ref):
        @pl.when(pl.program_id(0) == 0)
        def _():
            acc_ref[...] = jnp.zeros_like(acc_ref)

        d = y_ref[...] - t_ref[...]
        dy_ref[...] = d * (1.0 / W)
        acc_ref[...] += jnp.sum(d * d, axis=0, keepdims=True)

    row = pl.BlockSpec((TM, W), lambda i: (i, 0))
    return pl.pallas_call(
        body, name="loss", grid=(n,), in_specs=[row, row],
        out_specs=[row, pl.BlockSpec((1, W), lambda i: (0, 0))],
        out_shape=[jax.ShapeDtypeStruct((T, W), f32), jax.ShapeDtypeStruct((1, W), f32)],
        compiler_params=pltpu.CompilerParams(dimension_semantics=("arbitrary",)),
    )(y, tgt)


def _exchange(name, arrs, scatter):
    n = len(arrs)
    outs_shape = [jax.ShapeDtypeStruct(((N_DEV,) + a.shape[1:]) if s else ((N_DEV,) + a.shape), a.dtype)
                  for a, s in zip(arrs, scatter)]

    def body(*refs):
        ins, outs = refs[:n], refs[n:2 * n]
        send_sems, recv_sems, loc_sems = refs[2 * n:]
        x, y, c = lax.axis_index("x"), lax.axis_index("y"), lax.axis_index("c")
        me = 4 * x + 2 * y + c

        def mine(a, dev):
            return ins[a].at[dev] if scatter[a] else ins[a]

        local = []
        for a in range(n):
            cp = pltpu.make_async_copy(mine(a, me), outs[a].at[me], loc_sems.at[a])
            cp.start()
            local.append(cp)
        remote = []
        for k in range(1, N_DEV):
            px = 1 - x if (k >> 2) & 1 else x
            py = 1 - y if (k >> 1) & 1 else y
            pc = 1 - c if k & 1 else c
            peer = 4 * px + 2 * py + pc
            for a in range(n):
                send = pltpu.make_async_remote_copy(
                    src_ref=mine(a, peer), dst_ref=outs[a].at[me],
                    send_sem=send_sems.at[a, k - 1], recv_sem=recv_sems.at[a, k - 1],
                    device_id=(px, py, pc), device_id_type=pl.DeviceIdType.MESH)
                send.start()
                recv = pltpu.make_async_remote_copy(
                    src_ref=mine(a, me), dst_ref=outs[a].at[peer],
                    send_sem=send_sems.at[a, k - 1], recv_sem=recv_sems.at[a, k - 1],
                    device_id=(px, py, pc), device_id_type=pl.DeviceIdType.MESH)
                remote.append((send, recv))
        for send, recv in remote:
            recv.wait_recv()
            send.wait_send()
        for cp in local:
            cp.wait()

    hbm = pl.BlockSpec(memory_space=pltpu.HBM)
    res = pl.pallas_call(
        body, name=name,
        in_specs=[hbm] * n, out_specs=[hbm] * n, out_shape=outs_shape,
        scratch_shapes=[pltpu.SemaphoreType.DMA((n, N_DEV - 1)), pltpu.SemaphoreType.DMA((n, N_DEV - 1)),
                        pltpu.SemaphoreType.DMA((n,))],
    )(*arrs)
    return list(res)


def _adamw(name, land, w, m, v, tr):
    R_, C_ = w.shape
    tr = min(tr, R_)
    assert R_ % tr == 0, (name, R_, tr)
    c1 = 1.0 - ADAM_B1 ** ADAM_STEP
    c2 = 1.0 - ADAM_B2 ** ADAM_STEP

    def body(l_ref, w_ref, m_ref, v_ref, g_ref, d_ref, mo_ref, vo_ref):
        g = l_ref[0].astype(f32)
        for s in range(1, N_DEV):
            g = g + l_ref[s].astype(f32)
        m_new = ADAM_B1 * m_ref[...] + (1.0 - ADAM_B1) * g
        v_new = ADAM_B2 * v_ref[...] + (1.0 - ADAM_B2) * jnp.square(g)
        m_hat = m_new / c1
        v_hat = v_new / c2
        g_ref[...] = g
        d_ref[...] = -ADAM_LR * (m_hat / (jnp.sqrt(v_hat) + ADAM_EPS) + ADAM_WD * w_ref[...])
        mo_ref[...] = m_new
        vo_ref[...] = v_new

    row = pl.BlockSpec((tr, C_), lambda i: (i, 0))
    return pl.pallas_call(
        body, name=name, grid=(R_ // tr,),
        in_specs=[pl.BlockSpec((N_DEV, tr, C_), lambda i: (0, i, 0)), row, row, row],
        out_specs=[row] * 4, out_shape=[jax.ShapeDtypeStruct((R_, C_), f32)] * 4,
        compiler_params=pltpu.CompilerParams(dimension_semantics=("arbitrary",)),
    )(land, w, m, v)


_BIG = ("w_in", "dn_conv_w", "w_up_dn", "w_up_sw", "w_o", "w_ff1", "w_ff2")
_COL_SHARDED = ("w_in", "dn_conv_w", "w_ff1")
_SMALL_ROWS = ("pre_mix_g", "post_mix_g", "pre_mlp_g", "post_mlp_g")
_SMALL_MISC = ("dn_a_log", "dn_dt_bias", "dn_norm_g", "sw_sinks")
_WEIGHTS = ("pre_mix_g", "w_in", "dn_conv_w", "dn_a_log", "dn_dt_bias", "dn_norm_g", "sw_sinks", "w_up_dn", "w_up_sw",
            "w_o", "post_mix_g", "pre_mlp_g", "w_ff1", "w_ff2", "post_mlp_g")
_SMALL_PACK_ROWS = 24


def _unshard(name, g):
    if name in _COL_SHARDED:
        g = jnp.moveaxis(g, 0, -2)
        return g.reshape(g.shape[:-2] + (g.shape[-2] * g.shape[-1],))
    g = jnp.moveaxis(g, 0, 1)
    return g.reshape((g.shape[0], g.shape[1] * g.shape[2]) + g.shape[3:])


def _shard_major(name, full):
    if name in _COL_SHARDED:
        s = full.reshape(full.shape[:-1] + (N_DEV, full.shape[-1] // N_DEV))
        return jnp.moveaxis(s, -2, 0)
    s = full.reshape((full.shape[0], N_DEV, full.shape[1] // N_DEV) + full.shape[2:])
    return jnp.moveaxis(s, 1, 0)


def _pack_small(d):
    rows = jnp.concatenate([d[n] for n in _SMALL_ROWS], axis=0)
    misc = jnp.concatenate([d[n].reshape(-1) for n in _SMALL_MISC])
    misc = jnp.pad(misc, (0, W1 - misc.shape[0])).reshape(1, W1)
    out = jnp.concatenate([rows, misc], axis=0)
    return jnp.pad(out, ((0, _SMALL_PACK_ROWS - out.shape[0]), (0, 0)))


def _unpack_small(a, like):
    out, L = {}, like[_SMALL_ROWS[0]].shape[0]
    for i, n in enumerate(_SMALL_ROWS):
        out[n] = a[L * i:L * (i + 1)]
    at, row = 0, a[L * len(_SMALL_ROWS)]
    for n in _SMALL_MISC:
        size = like[n].size
        out[n] = row[at:at + size].reshape(like[n].shape)
        at += size
    return out


def kernel(x, positions, pre_mix_g, w_in, dn_conv_w, dn_a_log, dn_dt_bias, dn_norm_g, sw_sinks, w_up_dn, w_up_sw, w_o, post_mix_g, pre_mlp_g, w_ff1, w_ff2, post_mlp_g, loss_target, m_pre_mix_g, m_w_in, m_dn_conv_w, m_dn_a_log, m_dn_dt_bias, m_dn_norm_g, m_sw_sinks, m_w_up_dn, m_w_up_sw, m_w_o, m_post_mix_g, m_pre_mlp_g, m_w_ff1, m_w_ff2, m_post_mlp_g, v_pre_mix_g, v_w_in, v_dn_conv_w, v_dn_a_log, v_dn_dt_bias, v_dn_norm_g, v_sw_sinks, v_w_up_dn, v_w_up_sw, v_w_o, v_post_mix_g, v_pre_mlp_g, v_w_ff1, v_w_ff2, v_post_mlp_g):
    w = dict(pre_mix_g=pre_mix_g, w_in=w_in, dn_conv_w=dn_conv_w, dn_a_log=dn_a_log, dn_dt_bias=dn_dt_bias,
             dn_norm_g=dn_norm_g, sw_sinks=sw_sinks, w_up_dn=w_up_dn, w_up_sw=w_up_sw, w_o=w_o, post_mix_g=post_mix_g,
             pre_mlp_g=pre_mlp_g, w_ff1=w_ff1, w_ff2=w_ff2, post_mlp_g=post_mlp_g)
    m = dict(pre_mix_g=m_pre_mix_g, w_in=m_w_in, dn_conv_w=m_dn_conv_w, dn_a_log=m_dn_a_log, dn_dt_bias=m_dn_dt_bias,
             dn_norm_g=m_dn_norm_g, sw_sinks=m_sw_sinks, w_up_dn=m_w_up_dn, w_up_sw=m_w_up_sw, w_o=m_w_o,
             post_mix_g=m_post_mix_g, pre_mlp_g=m_pre_mlp_g, w_ff1=m_w_ff1, w_ff2=m_w_ff2, post_mlp_g=m_post_mlp_g)
    v = dict(pre_mix_g=v_pre_mix_g, w_in=v_w_in, dn_conv_w=v_dn_conv_w, dn_a_log=v_dn_a_log, dn_dt_bias=v_dn_dt_bias,
             dn_norm_g=v_dn_norm_g, sw_sinks=v_sw_sinks, w_up_dn=v_w_up_dn, w_up_sw=v_w_up_sw, w_o=v_w_o,
             post_mix_g=v_post_mix_g, pre_mlp_g=v_pre_mlp_g, w_ff1=v_w_ff1, w_ff2=v_w_ff2, post_mlp_g=v_post_mlp_g)
    n_layers = pre_mix_g.shape[0]
    xs, pos, tgt = x[0], positions[0], loss_target[0]

    payload = [w[n] if n == "dn_conv_w" else w[n].astype(bf16) for n in _BIG]
    gathered = _exchange("allgather_weights", payload, [False] * len(_BIG))
    full = {n: _unshard(n, g) for n, g in zip(_BIG, gathered)}
    layers = []
    for l in range(n_layers):
        wl = {n: (full[n][l] if n in _BIG else w[n][l]) for n in _WEIGHTS}
        layers.append(_layer_params(wl))
    cst = _consts(pos)

    h, res = xs, []
    for l in range(n_layers):
        h, r = _layer_fwd(h, layers[l], cst)
        res.append(r)
    dy, sq = _loss(h, tgt)
    loss = lax.psum(0.5 / D_MODEL * jnp.sum(sq), ("x", "y", "c"))
    grads = [None] * n_layers
    for l in reversed(range(n_layers)):
        dy, g = _layer_bwd(dy, res[l], layers[l], cst)
        grads[l] = _layer_grads_ref_layout(g)
    grad_x = dy[None]
    gfull = {n: jnp.stack([grads[l][n] for l in range(n_layers)]) for n in _WEIGHTS}

    send = [_shard_major(n, gfull[n]).astype(bf16) for n in _BIG] + [_pack_small(gfull)]
    landed = _exchange("exchange_grads", send, [True] * len(_BIG) + [False])

    out_g, out_d, out_m, out_v = {}, {}, {}, {}
    for n, land in zip(_BIG, landed):
        shp = w[n].shape
        flat = lambda a: a.reshape(-1, shp[-1])
        R_ = flat(w[n]).shape[0]
        res4 = _adamw("adamw_" + n, land.reshape(N_DEV, R_, shp[-1]), flat(w[n]), flat(m[n]), flat(v[n]), tr=256)
        out_g[n], out_d[n], out_m[n], out_v[n] = [a.reshape(shp) for a in res4]
    small = _adamw("adamw_small", landed[-1], _pack_small(w), _pack_small(m), _pack_small(v), tr=_SMALL_PACK_ROWS)
    for dst, a in zip((out_g, out_d, out_m, out_v), small):
        dst.update(_unpack_small(a, w))
    return (loss, grad_x, *[out_g[n] for n in _WEIGHTS], *[out_d[n] for n in _WEIGHTS],
            *[out_m[n] for n in _WEIGHTS], *[out_v[n] for n in _WEIGHTS])
```

```python
import functools
import math

import numpy as np
import jax
import jax.numpy as jnp
from jax import lax
from jax.experimental import pallas as pl
from jax.experimental.pallas import tpu as pltpu

f32 = jnp.float32
bf16 = jnp.bfloat16
HIGHEST = lax.Precision.HIGHEST

N_DEV = 8
D_MODEL = 1024
DN_HEADS = 8
DN_DK = 128
DN_CHUNK = 64
DN_CONV = 4
SW_Q_HEADS = 16
SW_KV_HEADS = 2
SW_HEAD_DIM = 64
SW_BLOCK = 128
ROPE_THETA = 500000.0
ROT_DIM = SW_HEAD_DIM // 4
D_FF = 4 * D_MODEL
EPS = 1e-6
LANES = 128
CONV_HALO = 8
NEG_BIG = -1e30

ADAM_LR = 0.001
ADAM_B1 = 0.9
ADAM_B2 = 0.999
ADAM_EPS = 1e-08
ADAM_WD = 0.01
ADAM_STEP = 10

PROJ_W = 7680
CB_Q, CB_K, CB_V, CB_Z, CB_SWQ, CB_GA, CB_GB = 0, 1, 2, 3, 4, 5, 6
CB_SWK, CB_SWV, CB_BA = 56, 57, 58

NN = ((1,), (0,))
NT = ((1,), (1,))
TN = ((0,), (0,))


def _mm(a, b, dims, hi=False):
    if hi:
        return lax.dot_general(a.astype(f32), b.astype(f32), (dims, ((), ())), precision=HIGHEST,
                               preferred_element_type=f32)
    return lax.dot_general(a.astype(bf16), b.astype(bf16), (dims, ((), ())), preferred_element_type=f32)


def _matmul(name, a, b, form, out_dtype, tm=512, tn=512, tk=1024):
    if form == "nn":
        (M, K), (_, N) = a.shape, b.shape
    elif form == "nt":
        (M, K), (N, _) = a.shape, b.shape
    else:
        (K, M), (_, N) = a.shape, b.shape
    tm, tn, tk = min(tm, M), min(tn, N), min(tk, K)
    assert M % tm == 0 and N % tn == 0 and K % tk == 0, (name, M, N, K, tm, tn, tk)
    nk = K // tk
    dims = {"nn": NN, "nt": NT, "tn": TN}[form]

    def body(a_ref, b_ref, o_ref, acc_ref):
        part = lax.dot_general(a_ref[...], b_ref[...], (dims, ((), ())), preferred_element_type=f32)
        if nk == 1:
            o_ref[...] = part.astype(out_dtype)
        else:
            k = pl.program_id(2)

            @pl.when(k == 0)
            def _():
                acc_ref[...] = part

            @pl.when(k > 0)
            def _():
                acc_ref[...] += part

            @pl.when(k == nk - 1)
            def _():
                o_ref[...] = acc_ref[...].astype(out_dtype)

    if form == "tn":
        a_spec = pl.BlockSpec((tk, tm), lambda i, j, k: (k, i))
    else:
        a_spec = pl.BlockSpec((tm, tk), lambda i, j, k: (i, k))
    if form == "nt":
        b_spec = pl.BlockSpec((tn, tk), lambda i, j, k: (j, k))
    else:
        b_spec = pl.BlockSpec((tk, tn), lambda i, j, k: (k, j))
    return pl.pallas_call(
        body, name=name,
        grid=(M // tm, N // tn, nk),
        in_specs=[a_spec, b_spec],
        out_specs=pl.BlockSpec((tm, tn), lambda i, j, k: (i, j)),
        out_shape=jax.ShapeDtypeStruct((M, N), out_dtype),
        scratch_shapes=[pltpu.VMEM((tm, tn) if nk > 1 else (8, 128), f32)],
        compiler_params=pltpu.CompilerParams(dimension_semantics=("parallel", "parallel", "arbitrary")),
    )(a, b)


def _tile_specs(ins, halo_ids, params, TM, HR, row_of):
    specs = [pl.BlockSpec((TM, w), lambda i, cb=cb: (row_of(i), cb)) for (_, w, cb) in ins]
    for h in halo_ids:
        _, w, cb = ins[h]
        specs.append(pl.BlockSpec((HR, w), lambda i, cb=cb: (jnp.maximum(row_of(i) * (TM // HR) - 1, 0), cb)))
    for p in params:
        specs.append(pl.BlockSpec(p.shape, lambda i, nd=p.ndim: (0,) * nd))
    return specs


def _tile_fwd(name, fn, T, TM, ins, params, outs, halo_ids=(), HR=CONV_HALO):
    TM = min(TM, T)
    n = T // TM
    ni, nh, npar = len(ins), len(halo_ids), len(params)

    def body(*refs):
        in_v = [r[...] for r in refs[:ni]]
        halo_v = [r[...] for r in refs[ni:ni + nh]]
        par_v = [r[...] for r in refs[ni + nh:ni + nh + npar]]
        o_refs = refs[ni + nh + npar:]
        first = pl.program_id(0) == 0
        vals = fn(first, in_v, halo_v, par_v)
        for o, val in zip(o_refs, vals):
            o[...] = val.astype(o.dtype)

    res = pl.pallas_call(
        body, name=name, grid=(n,),
        in_specs=_tile_specs(ins, halo_ids, params, TM, HR, lambda i: i),
        out_specs=[pl.BlockSpec((TM, w), lambda i: (i, 0)) for (w, _) in outs],
        out_shape=[jax.ShapeDtypeStruct((T, w), dt) for (w, dt) in outs],
        compiler_params=pltpu.CompilerParams(dimension_semantics=("arbitrary",)),
    )(*[a for (a, _, _) in ins], *[ins[h][0] for h in halo_ids], *params)
    return list(res)


def _tile_bwd(name, fn, T, TM, ins, params, cts, din, dpar, halo_ids=(), HR=CONV_HALO):
    TM = min(TM, T)
    n = T // TM
    ni, nh, npar, nc = len(ins), len(halo_ids), len(params), len(cts)
    din_ids = [j for (j, _) in din]
    dh_ids = [h for h in halo_ids if h in din_ids]
    nd, ndp, ndh = len(din), len(dpar), len(dh_ids)

    def body(*refs):
        in_v = [r[...] for r in refs[:ni]]
        halo_v = [r[...] for r in refs[ni:ni + nh]]
        par_v = [r[...] for r in refs[ni + nh:ni + nh + npar]]
        ct_v = [r[...].astype(f32) for r in refs[ni + nh + npar:ni + nh + npar + nc]]
        o_refs = refs[ni + nh + npar + nc:ni + nh + npar + nc + nd + ndp]
        carry_refs = refs[ni + nh + npar + nc + nd + ndp:]
        i = pl.program_id(0)
        first = i == n - 1

        def g(d_in, d_halo, d_par):
            full_in = list(in_v)
            for j, val in zip(din_ids, d_in):
                full_in[j] = val
            full_halo = list(halo_v)
            for h, val in zip(dh_ids, d_halo):
                full_halo[list(halo_ids).index(h)] = val
            full_par = list(par_v)
            for j, val in zip(dpar, d_par):
                full_par[j] = val
            return tuple(fn(first, full_in, full_halo, full_par))

        prim = ([in_v[j].astype(f32) for j in din_ids],
                [halo_v[list(halo_ids).index(h)].astype(f32) for h in dh_ids],
                [par_v[j] for j in dpar])
        _, vjp = jax.vjp(g, *prim)
        g_in, g_halo, g_par = vjp(tuple(ct_v))

        @pl.when(i == 0)
        def _():
            for c in carry_refs:
                c[...] = jnp.zeros_like(c)
            for o in o_refs[nd:]:
                o[...] = jnp.zeros_like(o)

        for slot, (j, _) in enumerate(din):
            val = g_in[slot]
            if j in dh_ids:
                c = carry_refs[dh_ids.index(j)]
                val = jnp.concatenate([val[:TM - HR], val[TM - HR:] + c[...]], axis=0) if TM > HR else val + c[...]
                c[...] = g_halo[dh_ids.index(j)]
            o_refs[slot][...] = val.astype(o_refs[slot].dtype)
        for slot in range(ndp):
            o_refs[nd + slot][...] += g_par[slot]

    rev = lambda i: n - 1 - i
    in_specs = _tile_specs(ins, halo_ids, params, TM, HR, rev)
    ct_specs = [pl.BlockSpec((TM, w), lambda i, cb=cb: (rev(i), cb)) for (_, w, cb) in cts]
    out_specs = [pl.BlockSpec((TM, ins[j][1]), lambda i: (rev(i), 0)) for j in din_ids]
    out_specs += [pl.BlockSpec(params[j].shape, lambda i, nd_=params[j].ndim: (0,) * nd_) for j in dpar]
    out_shape = [jax.ShapeDtypeStruct((T, ins[j][1]), dt) for (j, dt) in din]
    out_shape += [jax.ShapeDtypeStruct(params[j].shape, f32) for j in dpar]
    res = pl.pallas_call(
        body, name=name, grid=(n,),
        in_specs=in_specs + ct_specs,
        out_specs=out_specs,
        out_shape=out_shape,
        scratch_shapes=[pltpu.VMEM((HR, ins[h][1]), f32) for h in dh_ids],
        compiler_params=pltpu.CompilerParams(dimension_semantics=("arbitrary",)),
    )(*[a for (a, _, _) in ins], *[ins[h][0] for h in halo_ids], *params, *[a for (a, _, _) in cts])
    return list(res)


def _rms(x, g):
    return x * lax.rsqrt(jnp.mean(x * x, axis=-1, keepdims=True) + EPS) * g


def _fn_prenorm(first, ins, halos, params):
    (x,), (g,) = ins, params
    x = x.astype(f32)
    return [_rms(x, g), x]


def _fn_postmix(first, ins, halos, params):
    (x, mix), (g2, g3) = ins, params
    x1 = x + _rms(mix, g2)
    return [x1, _rms(x1, g3)]


def _fn_postmlp(first, ins, halos, params):
    (x1, ff), (g4,) = ins, params
    return [x1 + _rms(ff, g4)]


def _fn_rms_only(first, ins, halos, params):
    (ff,), (g4,) = ins, params
    return [_rms(ff, g4)]


def _fn_relu2(first, ins, halos, params):
    (h,) = ins
    return [jnp.square(jnp.maximum(h, 0.0))]


def _fn_merge(first, ins, halos, params):
    ga, gb, ya, yb = ins
    return [jax.nn.sigmoid(ga) * ya + jax.nn.sigmoid(gb) * yb]


def _make_fn_conv(norm_scale):
    def fn(first, ins, halos, params):
        (x,), (xp,), (w,) = ins, halos, params
        TM = x.shape[0]
        xp = jnp.where(first, 0.0, xp)
        xe = jnp.concatenate([xp, x], axis=0)
        off = CONV_HALO - (DN_CONV - 1)
        y = xe[off:off + TM] * w[0:1]
        for j in range(1, DN_CONV):
            y = y + xe[off + j:off + j + TM] * w[j:j + 1]
        y = jax.nn.silu(y)
        if norm_scale is None:
            return [y]
        outs = []
        for h in range(DN_HEADS):
            yh = y[:, DN_DK * h:DN_DK * (h + 1)]
            outs.append(yh * lax.rsqrt(jnp.sum(yh * yh, axis=-1, keepdims=True) + EPS) * norm_scale)
        return [jnp.concatenate(outs, axis=-1)]
    return fn


def _fn_gates(first, ins, halos, params):
    (ba,), (avec, dvec) = ins, params
    lane = lax.broadcasted_iota(jnp.int32, ba.shape, 1)
    beta = jax.nn.sigmoid(ba)
    g = -jnp.exp(avec) * jax.nn.softplus(ba + dvec)
    return [jnp.where(lane < DN_HEADS, beta, jnp.where(lane < 2 * DN_HEADS, g, 0.0))]


def _fn_dnpost(first, ins, halos, params):
    (o, z), (ng,) = ins, params
    outs = []
    for h in range(DN_HEADS):
        sl = slice(DN_DK * h, DN_DK * (h + 1))
        outs.append(_rms(o[:, sl], ng) * jax.nn.silu(z[:, sl]))
    return [jnp.concatenate(outs, axis=-1)]


def _fn_swa(first, ins, halos, params):
    q, k, v, cos, sin = ins
    kp, vp, cosp, sinp = halos
    sinks, rot, sel_a0, sel_b0, sel_a1, sel_b1 = params
    B = q.shape[0]

    def rope(x, c, s):
        return x * c + _mm(x, rot, NN, hi=True) * s

    kcat = jnp.concatenate([rope(kp, cosp, sinp), rope(k, cos, sin)], axis=0)
    vcat = jnp.concatenate([vp, v], axis=0)
    r = lax.broadcasted_iota(jnp.int32, (B, 2 * B), 0)
    c = lax.broadcasted_iota(jnp.int32, (B, 2 * B), 1)
    mask = (c > r) & (c <= r + B) & ((c >= B) | jnp.logical_not(first))
    group = SW_Q_HEADS // SW_KV_HEADS
    heads = range(SW_Q_HEADS)
    sels = ((sel_a0, sel_b0), (sel_a1, sel_b1))
    kx = [[_mm(kcat, sels[hk][half], NN) for half in range(2)] for hk in range(SW_KV_HEADS)]
    vx = [[_mm(vcat, sels[hk][half], NN) for half in range(2)] for hk in range(SW_KV_HEADS)]
    qp = [rope(q[:, LANES * j:LANES * (j + 1)], cos, sin) for j in range(SW_Q_HEADS // 2)]
    s = [jnp.where(mask, _mm(qp[hd // 2], kx[hd // group][hd % 2], NT) * (SW_HEAD_DIM ** -0.5), NEG_BIG) for hd in heads]
    sink = [sinks[:, hd:hd + 1] for hd in heads]
    m = [jnp.maximum(jnp.max(s[hd], axis=-1, keepdims=True), sink[hd]) for hd in heads]
    p = [jnp.exp(s[hd] - m[hd]) for hd in heads]
    probs = [p[hd] / (jnp.sum(p[hd], axis=-1, keepdims=True) + jnp.exp(sink[hd] - m[hd])) for hd in heads]
    o = [_mm(probs[hd], vx[hd // group][hd % 2], NN) for hd in heads]
    return [jnp.concatenate([o[2 * j] + o[2 * j + 1] for j in range(SW_Q_HEADS // 2)], axis=-1)]


def _fn_loss(first, ins, halos, params):
    y, tgt = ins
    return [y - tgt]


@jax.custom_vjp
def _inv_unit_lower(Ls):
    C = Ls[0].shape[0]
    ii = lax.broadcasted_iota(jnp.int32, (C, C), 0)
    jj = lax.broadcasted_iota(jnp.int32, (C, C), 1)

    def off_mask(level):
        same_pair = jnp.right_shift(ii, level + 1) == jnp.right_shift(jj, level + 1)
        lower_left = (jnp.bitwise_and(jnp.right_shift(ii, level), 1) == 1) & (jnp.bitwise_and(jnp.right_shift(jj, level), 1) == 0)
        return same_pair & lower_left

    eye = (ii == jj).astype(f32)
    m0 = off_mask(0)
    Ts = [eye - jnp.where(m0, L, 0.0) for L in Ls]
    for level in range(1, int(math.log2(C))):
        mk = off_mask(level)
        left = [_mm(T_, jnp.where(mk, L, 0.0), NN) for T_, L in zip(Ts, Ls)]
        Ts = [T_ - _mm(a, T_, NN) for a, T_ in zip(left, Ts)]
    return tuple(Ts)


def _inv_fwd(Ls):
    Ts = _inv_unit_lower(Ls)
    return Ts, Ts


def _inv_bwd(Ts, dTs):
    left = [_mm(T_, dT, TN) for T_, dT in zip(Ts, dTs)]
    return (tuple(-_mm(a, T_, NT) for a, T_ in zip(left, Ts)),)


_inv_unit_lower.defvjp(_inv_fwd, _inv_bwd)


def _dn_chunk(q, k, v, gb, S):
    C = q.shape[0]
    H = range(DN_HEADS)
    ii = lax.broadcasted_iota(jnp.int32, (C, C), 0)
    jj = lax.broadcasted_iota(jnp.int32, (C, C), 1)
    causal, strict = ii >= jj, ii > jj
    gc_all = _mm(causal.astype(f32), gb, NN, hi=True)
    eye = (lax.broadcasted_iota(jnp.int32, (LANES, LANES), 0)
           == lax.broadcasted_iota(jnp.int32, (LANES, LANES), 1)).astype(f32)
    gc_t = _mm(eye, gc_all, NT, hi=True)
    sl = [slice(DN_DK * h, DN_DK * (h + 1)) for h in H]
    qs, ks, vs = [q[:, s] for s in sl], [k[:, s] for s in sl], [v[:, s] for s in sl]
    beta = [gb[:, h:h + 1] for h in H]
    gcol = [gc_all[:, DN_HEADS + h:DN_HEADS + h + 1] for h in H]
    grow = [gc_t[DN_HEADS + h:DN_HEADS + h + 1, :] for h in H]
    decay = [jnp.where(causal, jnp.exp(jnp.where(causal, gcol[h] - grow[h], 0.0)), 0.0) for h in H]
    kb = [ks[h] * beta[h] for h in H]
    kk = [_mm(kb[h], ks[h], NT) for h in H]
    qk = [_mm(qs[h], ks[h], NT) for h in H]
    tinv = _inv_unit_lower(tuple(jnp.where(strict, kk[h] * decay[h], 0.0) for h in H))
    eg = [jnp.exp(gcol[h]) for h in H]
    u = [_mm(tinv[h], vs[h] * beta[h], NN) for h in H]
    w = [_mm(tinv[h], kb[h] * eg[h], NN) for h in H]
    gl = [gcol[h][C - 1:C, :] for h in H]
    ws = [_mm(w[h], S[h], NN) for h in H]
    qS = [_mm(qs[h] * eg[h], S[h], NN) for h in H]
    v_new = [u[h] - ws[h] for h in H]
    av = [_mm(qk[h] * decay[h], v_new[h], NN) for h in H]
    kv = [_mm(ks[h] * jnp.exp(gl[h] - gcol[h]), v_new[h], TN) for h in H]
    o = jnp.concatenate([qS[h] + av[h] for h in H], axis=-1)
    return o, tuple(S[h] * jnp.exp(gl[h]) + kv[h] for h in H)


def _delta_fwd(qn, kn, vv, gb):
    T = qn.shape[0]
    C = DN_CHUNK
    n = T // C
    W = DN_HEADS * DN_DK

    def body(q_ref, k_ref, v_ref, gb_ref, o_ref, hist_ref, s_ref):
        @pl.when(pl.program_id(0) == 0)
        def _():
            s_ref[...] = jnp.zeros_like(s_ref)

        S = tuple(s_ref[h] for h in range(DN_HEADS))
        for h in range(DN_HEADS):
            hist_ref[0, h] = S[h]
        o, s_new = _dn_chunk(q_ref[...], k_ref[...], v_ref[...], gb_ref[...], S)
        o_ref[...] = o
        for h in range(DN_HEADS):
            s_ref[h] = s_new[h]

    row = pl.BlockSpec((C, W), lambda i: (i, 0))
    return pl.pallas_call(
        body, name="delta_fwd", grid=(n,),
        in_specs=[row, row, row, pl.BlockSpec((C, LANES), lambda i: (i, 0))],
        out_specs=[row, pl.BlockSpec((1, DN_HEADS, DN_DK, DN_DK), lambda i: (i, 0, 0, 0))],
        out_shape=[jax.ShapeDtypeStruct((T, W), f32), jax.ShapeDtypeStruct((n, DN_HEADS, DN_DK, DN_DK), f32)],
        scratch_shapes=[pltpu.VMEM((DN_HEADS, DN_DK, DN_DK), f32)],
        compiler_params=pltpu.CompilerParams(dimension_semantics=("arbitrary",)),
    )(qn, kn, vv, gb)


def _delta_bwd(qn, kn, vv, gb, hist, do):
    T = qn.shape[0]
    C = DN_CHUNK
    n = T // C
    W = DN_HEADS * DN_DK

    def body(q_ref, k_ref, v_ref, gb_ref, hist_ref, do_ref, dq_ref, dk_ref, dv_ref, dgb_ref, ds_ref):
        @pl.when(pl.program_id(0) == 0)
        def _():
            ds_ref[...] = jnp.zeros_like(ds_ref)

        S = tuple(hist_ref[0, h] for h in range(DN_HEADS))
        _, vjp = jax.vjp(_dn_chunk, q_ref[...], k_ref[...], v_ref[...], gb_ref[...], S)
        dS = tuple(ds_ref[h] for h in range(DN_HEADS))
        dq, dk, dv, dgb, dS_in = vjp((do_ref[...], dS))
        dq_ref[...] = dq
        dk_ref[...] = dk
        dv_ref[...] = dv
        dgb_ref[...] = dgb
        for h in range(DN_HEADS):
            ds_ref[h] = dS_in[h]

    row = pl.BlockSpec((C, W), lambda i: (n - 1 - i, 0))
    small = pl.BlockSpec((C, LANES), lambda i: (n - 1 - i, 0))
    return pl.pallas_call(
        body, name="delta_bwd", grid=(n,),
        in_specs=[row, row, row, small, pl.BlockSpec((1, DN_HEADS, DN_DK, DN_DK), lambda i: (n - 1 - i, 0, 0, 0)), row],
        out_specs=[row, row, row, small],
        out_shape=[jax.ShapeDtypeStruct((T, W), f32)] * 3 + [jax.ShapeDtypeStruct((T, LANES), f32)],
        scratch_shapes=[pltpu.VMEM((DN_HEADS, DN_DK, DN_DK), f32)],
        compiler_params=pltpu.CompilerParams(dimension_semantics=("arbitrary",)),
    )(qn, kn, vv, gb, hist, do)


TM_ROW = 256
W1 = D_MODEL


def _first_only(fn):
    return lambda *a: fn(*a)[:1]


def _swa_args(proj, cst):
    ins = [(proj, W1, CB_SWQ), (proj, LANES, CB_SWK), (proj, LANES, CB_SWV), (cst["cos"], LANES, 0), (cst["sin"], LANES, 0)]
    return ins, (1, 2, 3, 4)


def _layer_fwd(x, p, cst):
    T = x.shape[0]
    r = {"x": x}
    (h,) = _tile_fwd("prenorm", _first_only(_fn_prenorm), T, TM_ROW, [(x, W1, 0)], [p["g1"]], [(W1, bf16)])
    proj = _matmul("proj", h, p["w_in"], "nn", f32, tm=1024, tn=1536)
    conv = lambda nm, cb, scale: _tile_fwd(nm, _make_fn_conv(scale), T, TM_ROW, [(proj, W1, cb)],
                                            [p["conv_w"][:, W1 * cb:W1 * (cb + 1)]], [(W1, f32)], halo_ids=(0,))[0]
    qn = conv("conv_q", CB_Q, DN_DK ** -0.5)
    kn = conv("conv_k", CB_K, 1.0)
    vv = conv("conv_v", CB_V, None)
    (gbt,) = _tile_fwd("gates", _fn_gates, T, TM_ROW, [(proj, LANES, CB_BA)], [p["avec"], p["dvec"]], [(LANES, f32)])
    o, hist = _delta_fwd(qn, kn, vv, gbt)
    (dn_out,) = _tile_fwd("dnpost", _fn_dnpost, T, TM_ROW, [(o, W1, 0), (proj, W1, CB_Z)], [p["ng"]], [(W1, bf16)])
    sw_ins, sw_halo = _swa_args(proj, cst)
    sw_par = [p["sinks"], cst["rot"], cst["sel_a0"], cst["sel_b0"], cst["sel_a1"], cst["sel_b1"]]
    (sw_out,) = _tile_fwd("swa", _fn_swa, T, SW_BLOCK, sw_ins, sw_par, [(W1, bf16)], halo_ids=sw_halo, HR=SW_BLOCK)
    y_a = _matmul("up_dn", dn_out, p["w_up_dn"], "nn", f32, tm=1024, tn=1024)
    y_b = _matmul("up_sw", sw_out, p["w_up_sw"], "nn", f32, tm=1024, tn=1024)
    (gated,) = _tile_fwd("merge", _fn_merge, T, TM_ROW,
                         [(proj, W1, CB_GA), (proj, W1, CB_GB), (y_a, W1, 0), (y_b, W1, 0)], [], [(W1, bf16)])
    mix = _matmul("w_o", gated, p["w_o"], "nn", f32, tm=1024, tn=1024)
    x1, h2 = _tile_fwd("postmix", _fn_postmix, T, TM_ROW, [(x, W1, 0), (mix, W1, 0)], [p["g2"], p["g3"]],
                       [(W1, f32), (W1, bf16)])
    ffh = _matmul("ff1", h2, p["w_ff1"], "nn", f32, tm=1024, tn=1024)
    (act,) = _tile_fwd("relu2", _fn_relu2, T, TM_ROW, [(ffh, D_FF, 0)], [], [(D_FF, bf16)])
    ff = _matmul("ff2", act, p["w_ff2"], "nn", f32, tm=1024, tn=1024)
    (x2,) = _tile_fwd("postmlp", _fn_postmlp, T, TM_ROW, [(x1, W1, 0), (ff, W1, 0)], [p["g4"]], [(W1, f32)])
    r.update(h=h, proj=proj, qn=qn, kn=kn, vv=vv, gbt=gbt, o=o, hist=hist, dn_out=dn_out, sw_out=sw_out,
             y_a=y_a, y_b=y_b, gated=gated, mix=mix, h2=h2, ffh=ffh, act=act, ff=ff)
    return x2, r


def _layer_bwd(dx2, r, p, cst):
    T = dx2.shape[0]
    x, proj = r["x"], r["proj"]
    g = {}
    dff, g["g4"] = _tile_bwd("postmlp_b", _fn_rms_only, T, TM_ROW, [(r["ff"], W1, 0)], [p["g4"]], [(dx2, W1, 0)],
                             [(0, bf16)], [0])
    dact = _matmul("ff2_dx", dff, p["w_ff2"], "nt", f32, tm=1024, tn=1024)
    g["w_ff2"] = _matmul("ff2_dw", r["act"], dff, "tn", f32, tm=1024, tn=1024)
    (dffh,) = _tile_bwd("relu2_b", _fn_relu2, T, TM_ROW, [(r["ffh"], D_FF, 0)], [], [(dact, D_FF, 0)], [(0, bf16)], [])
    dh2 = _matmul("ff1_dx", dffh, p["w_ff1"], "nt", f32, tm=1024, tn=1024)
    g["w_ff1"] = _matmul("ff1_dw", r["h2"], dffh, "tn", f32, tm=1024, tn=1024)
    dx1, dmix, g["g2"], g["g3"] = _tile_bwd("postmix_b", _fn_postmix, T, TM_ROW, [(x, W1, 0), (r["mix"], W1, 0)],
                                            [p["g2"], p["g3"]], [(dx2, W1, 0), (dh2, W1, 0)], [(0, f32), (1, bf16)], [0, 1])
    dgated = _matmul("w_o_dx", dmix, p["w_o"], "nt", f32, tm=1024, tn=1024)
    g["w_o"] = _matmul("w_o_dw", r["gated"], dmix, "tn", f32, tm=1024, tn=1024)
    dga, dgb, dya, dyb = _tile_bwd("merge_b", _fn_merge, T, TM_ROW,
                                   [(proj, W1, CB_GA), (proj, W1, CB_GB), (r["y_a"], W1, 0), (r["y_b"], W1, 0)], [],
                                   [(dgated, W1, 0)], [(0, bf16), (1, bf16), (2, bf16), (3, bf16)], [])
    d_dn = _matmul("up_dn_dx", dya, p["w_up_dn"], "nt", f32, tm=1024, tn=1024)
    g["w_up_dn"] = _matmul("up_dn_dw", r["dn_out"], dya, "tn", f32, tm=1024, tn=1024)
    d_sw = _matmul("up_sw_dx", dyb, p["w_up_sw"], "nt", f32, tm=1024, tn=1024)
    g["w_up_sw"] = _matmul("up_sw_dw", r["sw_out"], dyb, "tn", f32, tm=1024, tn=1024)
    do, dz, g["ng"] = _tile_bwd("dnpost_b", _fn_dnpost, T, TM_ROW, [(r["o"], W1, 0), (proj, W1, CB_Z)], [p["ng"]],
                                [(d_dn, W1, 0)], [(0, f32), (1, bf16)], [0])
    dqn, dkn, dvv, dgbt = _delta_bwd(r["qn"], r["kn"], r["vv"], r["gbt"], r["hist"], do)
    conv_b = lambda nm, cb, scale, ct: _tile_bwd(nm, _make_fn_conv(scale), T, TM_ROW, [(proj, W1, cb)],
                                                 [p["conv_w"][:, W1 * cb:W1 * (cb + 1)]], [(ct, W1, 0)], [(0, bf16)], [0],
                                                 halo_ids=(0,))
    dq_in, dcw_q = conv_b("conv_q_b", CB_Q, DN_DK ** -0.5, dqn)
    dk_in, dcw_k = conv_b("conv_k_b", CB_K, 1.0, dkn)
    dv_in, dcw_v = conv_b("conv_v_b", CB_V, None, dvv)
    g["conv_w"] = jnp.concatenate([dcw_q, dcw_k, dcw_v], axis=-1)
    dba, g["avec"], g["dvec"] = _tile_bwd("gates_b", _fn_gates, T, TM_ROW, [(proj, LANES, CB_BA)], [p["avec"], p["dvec"]],
                                          [(dgbt, LANES, 0)], [(0, bf16)], [0, 1])
    sw_ins, sw_halo = _swa_args(proj, cst)
    sw_par = [p["sinks"], cst["rot"], cst["sel_a0"], cst["sel_b0"], cst["sel_a1"], cst["sel_b1"]]
    dswq, dswk, dswv, g["sinks"] = _tile_bwd("swa_b", _fn_swa, T, SW_BLOCK, sw_ins, sw_par, [(d_sw, W1, 0)],
                                             [(0, bf16), (1, bf16), (2, bf16)], [0], halo_ids=sw_halo, HR=SW_BLOCK)
    dproj = jnp.concatenate([dq_in, dk_in, dv_in, dz, dswq, dga, dgb, dswk, dswv, dba, jnp.zeros((T, LANES), bf16)], axis=-1)
    dh = _matmul("proj_dx", dproj, p["w_in"], "nt", f32, tm=1024, tn=1024, tk=1536)
    g["w_in"] = _matmul("proj_dw", r["h"], dproj, "tn", f32, tm=1024, tn=1536)
    dx, g["g1"] = _tile_bwd("prenorm_b", _fn_prenorm, T, TM_ROW, [(x, W1, 0)], [p["g1"]], [(dh, W1, 0), (dx1, W1, 0)],
                            [(0, f32)], [0])
    return dx, g


_OFF_BA, _OFF_SWQ, _OFF_SWK, _OFF_GA, _D_IN = 4096, 4112, 5136, 5392, 7440


def _proj_cols(w):
    pad = lambda n: jnp.zeros(w.shape[:-1] + (n,), w.dtype)
    return jnp.concatenate([w[..., :_OFF_BA], w[..., _OFF_SWQ:_OFF_SWK], w[..., _OFF_GA:_D_IN],
                            w[..., _OFF_SWK:_OFF_GA], w[..., _OFF_BA:_OFF_SWQ], pad(PROJ_W - _D_IN)], axis=-1)


def _proj_cols_inv(w):
    n_ba = _OFF_SWQ - _OFF_BA
    return jnp.concatenate([w[..., :4096], w[..., 7424:7424 + n_ba], w[..., 4096:5120], w[..., 7168:7424],
                            w[..., 5120:7168]], axis=-1)


def _lane_pad(v, at):
    return jnp.pad(v.astype(f32), (at, LANES - at - v.shape[0])).reshape(1, LANES)


def _layer_params(w):
    row = lambda v: v.reshape(1, -1).astype(f32)
    return dict(
        g1=row(w["pre_mix_g"]), g2=row(w["post_mix_g"]), g3=row(w["pre_mlp_g"]), g4=row(w["post_mlp_g"]),
        w_in=_proj_cols(w["w_in"]).astype(bf16), conv_w=w["dn_conv_w"].astype(f32),
        avec=_lane_pad(w["dn_a_log"], DN_HEADS), dvec=_lane_pad(w["dn_dt_bias"], DN_HEADS),
        ng=row(w["dn_norm_g"]), sinks=_lane_pad(w["sw_sinks"], 0),
        w_up_dn=w["w_up_dn"].astype(bf16), w_up_sw=w["w_up_sw"].astype(bf16), w_o=w["w_o"].astype(bf16),
        w_ff1=w["w_ff1"].astype(bf16), w_ff2=w["w_ff2"].astype(bf16))


def _layer_grads_ref_layout(g):
    return dict(
        pre_mix_g=g["g1"][0], post_mix_g=g["g2"][0], pre_mlp_g=g["g3"][0], post_mlp_g=g["g4"][0],
        w_in=_proj_cols_inv(g["w_in"]), dn_conv_w=g["conv_w"],
        dn_a_log=g["avec"][0, DN_HEADS:2 * DN_HEADS], dn_dt_bias=g["dvec"][0, DN_HEADS:2 * DN_HEADS],
        dn_norm_g=g["ng"][0], sw_sinks=g["sinks"][0, :SW_Q_HEADS],
        w_up_dn=g["w_up_dn"], w_up_sw=g["w_up_sw"], w_o=g["w_o"], w_ff1=g["w_ff1"], w_ff2=g["w_ff2"])


def _consts(positions):
    T = positions.shape[0]
    half = ROT_DIM // 2
    inv_freq = ROPE_THETA ** (-jnp.arange(half, dtype=f32) * (2.0 / ROT_DIM))
    ang = positions.astype(f32)[:, None] * inv_freq
    cos8, sin8 = jnp.cos(ang), jnp.sin(ang)
    rest = SW_HEAD_DIM - ROT_DIM
    c64 = jnp.concatenate([cos8, cos8, jnp.ones((T, rest), f32)], axis=-1)
    s64 = jnp.concatenate([sin8, sin8, jnp.zeros((T, rest), f32)], axis=-1)
    rot = np.zeros((LANES, LANES), np.float32)
    sel = np.zeros((2, 2, LANES, LANES), np.float32)
    for base in (0, SW_HEAD_DIM):
        for i in range(half):
            rot[base + half + i, base + i] = -1.0
            rot[base + i, base + half + i] = 1.0
    for hk in range(SW_KV_HEADS):
        for d in range(SW_HEAD_DIM):
            sel[hk, 0, SW_HEAD_DIM * hk + d, d] = 1.0
            sel[hk, 1, SW_HEAD_DIM * hk + d, SW_HEAD_DIM + d] = 1.0
    return dict(cos=jnp.concatenate([c64, c64], axis=-1), sin=jnp.concatenate([s64, s64], axis=-1),
                rot=jnp.asarray(rot), sel_a0=jnp.asarray(sel[0, 0]), sel_b0=jnp.asarray(sel[0, 1]),
                sel_a1=jnp.asarray(sel[1, 0]), sel_b1=jnp.asarray(sel[1, 1]))


def _loss(y, tgt):
    T, W = y.shape
    TM = min(TM_ROW, T)
    n = T // TM

    def body(y_ref, t_ref, dy_ref, acc_ref):
        @pl.when(pl.program_id(0) == 0)
        def _():
            acc_ref[...] = jnp.zeros_like(acc_ref)

        d = y_ref[...] - t_ref[...]
        dy_ref[...] = d * (1.0 / W)
        acc_ref[...] += jnp.sum(d * d, axis=0, keepdims=True)

    row = pl.BlockSpec((TM, W), lambda i: (i, 0))
    return pl.pallas_call(
        body, name="loss", grid=(n,), in_specs=[row, row],
        out_specs=[row, pl.BlockSpec((1, W), lambda i: (0, 0))],
        out_shape=[jax.ShapeDtypeStruct((T, W), f32), jax.ShapeDtypeStruct((1, W), f32)],
        compiler_params=pltpu.CompilerParams(dimension_semantics=("arbitrary",)),
    )(y, tgt)


def _exchange(name, arrs, scatter):
    n = len(arrs)
    outs_shape = [jax.ShapeDtypeStruct(((N_DEV,) + a.shape[1:]) if s else ((N_DEV,) + a.shape), a.dtype)
                  for a, s in zip(arrs, scatter)]

    def body(*refs):
        ins, outs = refs[:n], refs[n:2 * n]
        send_sems, recv_sems, loc_sems = refs[2 * n:]
        x, y, c = lax.axis_index("x"), lax.axis_index("y"), lax.axis_index("c")
        me = 4 * x + 2 * y + c

        def mine(a, dev):
            return ins[a].at[dev] if scatter[a] else ins[a]

        local = []
        for a in range(n):
            cp = pltpu.make_async_copy(mine(a, me), outs[a].at[me], loc_sems.at[a])
            cp.start()
            local.append(cp)
        remote = []
        for k in range(1, N_DEV):
            px = 1 - x if (k >> 2) & 1 else x
            py = 1 - y if (k >> 1) & 1 else y
            pc = 1 - c if k & 1 else c
            peer = 4 * px + 2 * py + pc
            for a in range(n):
                send = pltpu.make_async_remote_copy(
                    src_ref=mine(a, peer), dst_ref=outs[a].at[me],
                    send_sem=send_sems.at[a, k - 1], recv_sem=recv_sems.at[a, k - 1],
                    device_id=(px, py, pc), device_id_type=pl.DeviceIdType.MESH)
                send.start()
                recv = pltpu.make_async_remote_copy(
                    src_ref=mine(a, me), dst_ref=outs[a].at[peer],
                    send_sem=send_sems.at[a, k - 1], recv_sem=recv_sems.at[a, k - 1],
                    device_id=(px, py, pc), device_id_type=pl.DeviceIdType.MESH)
                remote.append((send, recv))
        for send, recv in remote:
            recv.wait_recv()
            send.wait_send()
        for cp in local:
            cp.wait()

    hbm = pl.BlockSpec(memory_space=pltpu.HBM)
    res = pl.pallas_call(
        body, name=name,
        in_specs=[hbm] * n, out_specs=[hbm] * n, out_shape=outs_shape,
        scratch_shapes=[pltpu.SemaphoreType.DMA((n, N_DEV - 1)), pltpu.SemaphoreType.DMA((n, N_DEV - 1)),
                        pltpu.SemaphoreType.DMA((n,))],
    )(*arrs)
    return list(res)


def _adamw(name, land, w, m, v, tr):
    R_, C_ = w.shape
    tr = min(tr, R_)
    assert R_ % tr == 0, (name, R_, tr)
    c1 = 1.0 - ADAM_B1 ** ADAM_STEP
    c2 = 1.0 - ADAM_B2 ** ADAM_STEP

    def body(l_ref, w_ref, m_ref, v_ref, g_ref, d_ref, mo_ref, vo_ref):
        g = l_ref[0].astype(f32)
        for s in range(1, N_DEV):
            g = g + l_ref[s].astype(f32)
        m_new = ADAM_B1 * m_ref[...] + (1.0 - ADAM_B1) * g
        v_new = ADAM_B2 * v_ref[...] + (1.0 - ADAM_B2) * jnp.square(g)
        m_hat = m_new / c1
        v_hat = v_new / c2
        g_ref[...] = g
        d_ref[...] = -ADAM_LR * (m_hat / (jnp.sqrt(v_hat) + ADAM_EPS) + ADAM_WD * w_ref[...])
        mo_ref[...] = m_new
        vo_ref[...] = v_new

    row = pl.BlockSpec((tr, C_), lambda i: (i, 0))
    return pl.pallas_call(
        body, name=name, grid=(R_ // tr,),
        in_specs=[pl.BlockSpec((N_DEV, tr, C_), lambda i: (0, i, 0)), row, row, row],
        out_specs=[row] * 4, out_shape=[jax.ShapeDtypeStruct((R_, C_), f32)] * 4,
        compiler_params=pltpu.CompilerParams(dimension_semantics=("arbitrary",)),
    )(land, w, m, v)


_BIG = ("w_in", "dn_conv_w", "w_up_dn", "w_up_sw", "w_o", "w_ff1", "w_ff2")
_COL_SHARDED = ("w_in", "dn_conv_w", "w_ff1")
_SMALL_ROWS = ("pre_mix_g", "post_mix_g", "pre_mlp_g", "post_mlp_g")
_SMALL_MISC = ("dn_a_log", "dn_dt_bias", "dn_norm_g", "sw_sinks")
_WEIGHTS = ("pre_mix_g", "w_in", "dn_conv_w", "dn_a_log", "dn_dt_bias", "dn_norm_g", "sw_sinks", "w_up_dn", "w_up_sw",
            "w_o", "post_mix_g", "pre_mlp_g", "w_ff1", "w_ff2", "post_mlp_g")
_SMALL_PACK_ROWS = 24


def _unshard(name, g):
    if name in _COL_SHARDED:
        g = jnp.moveaxis(g, 0, -2)
        return g.reshape(g.shape[:-2] + (g.shape[-2] * g.shape[-1],))
    g = jnp.moveaxis(g, 0, 1)
    return g.reshape((g.shape[0], g.shape[1] * g.shape[2]) + g.shape[3:])


def _shard_major(name, full):
    if name in _COL_SHARDED:
        s = full.reshape(full.shape[:-1] + (N_DEV, full.shape[-1] // N_DEV))
        return jnp.moveaxis(s, -2, 0)
    s = full.reshape((full.shape[0], N_DEV, full.shape[1] // N_DEV) + full.shape[2:])
    return jnp.moveaxis(s, 1, 0)


def _pack_small(d):
    rows = jnp.concatenate([d[n] for n in _SMALL_ROWS], axis=0)
    misc = jnp.concatenate([d[n].reshape(-1) for n in _SMALL_MISC])
    misc = jnp.pad(misc, (0, W1 - misc.shape[0])).reshape(1, W1)
    out = jnp.concatenate([rows, misc], axis=0)
    return jnp.pad(out, ((0, _SMALL_PACK_ROWS - out.shape[0]), (0, 0)))


def _unpack_small(a, like):
    out, L = {}, like[_SMALL_ROWS[0]].shape[0]
    for i, n in enumerate(_SMALL_ROWS):
        out[n] = a[L * i:L * (i + 1)]
    at, row = 0, a[L * len(_SMALL_ROWS)]
    for n in _SMALL_MISC:
        size = like[n].size
        out[n] = row[at:at + size].reshape(like[n].shape)
        at += size
    return out


def kernel(x, positions, pre_mix_g, w_in, dn_conv_w, dn_a_log, dn_dt_bias, dn_norm_g, sw_sinks, w_up_dn, w_up_sw, w_o, post_mix_g, pre_mlp_g, w_ff1, w_ff2, post_mlp_g, loss_target, m_pre_mix_g, m_w_in, m_dn_conv_w, m_dn_a_log, m_dn_dt_bias, m_dn_norm_g, m_sw_sinks, m_w_up_dn, m_w_up_sw, m_w_o, m_post_mix_g, m_pre_mlp_g, m_w_ff1, m_w_ff2, m_post_mlp_g, v_pre_mix_g, v_w_in, v_dn_conv_w, v_dn_a_log, v_dn_dt_bias, v_dn_norm_g, v_sw_sinks, v_w_up_dn, v_w_up_sw, v_w_o, v_post_mix_g, v_pre_mlp_g, v_w_ff1, v_w_ff2, v_post_mlp_g):
    w = dict(pre_mix_g=pre_mix_g, w_in=w_in, dn_conv_w=dn_conv_w, dn_a_log=dn_a_log, dn_dt_bias=dn_dt_bias,
             dn_norm_g=dn_norm_g, sw_sinks=sw_sinks, w_up_dn=w_up_dn, w_up_sw=w_up_sw, w_o=w_o, post_mix_g=post_mix_g,
             pre_mlp_g=pre_mlp_g, w_ff1=w_ff1, w_ff2=w_ff2, post_mlp_g=post_mlp_g)
    m = dict(pre_mix_g=m_pre_mix_g, w_in=m_w_in, dn_conv_w=m_dn_conv_w, dn_a_log=m_dn_a_log, dn_dt_bias=m_dn_dt_bias,
             dn_norm_g=m_dn_norm_g, sw_sinks=m_sw_sinks, w_up_dn=m_w_up_dn, w_up_sw=m_w_up_sw, w_o=m_w_o,
             post_mix_g=m_post_mix_g, pre_mlp_g=m_pre_mlp_g, w_ff1=m_w_ff1, w_ff2=m_w_ff2, post_mlp_g=m_post_mlp_g)
    v = dict(pre_mix_g=v_pre_mix_g, w_in=v_w_in, dn_conv_w=v_dn_conv_w, dn_a_log=v_dn_a_log, dn_dt_bias=v_dn_dt_bias,
             dn_norm_g=v_dn_norm_g, sw_sinks=v_sw_sinks, w_up_dn=v_w_up_dn, w_up_sw=v_w_up_sw, w_o=v_w_o,
             post_mix_g=v_post_mix_g, pre_mlp_g=v_pre_mlp_g, w_ff1=v_w_ff1, w_ff2=v_w_ff2, post_mlp_g=v_post_mlp_g)
    n_layers = pre_mix_g.shape[0]
    xs, pos, tgt = x[0], positions[0], loss_target[0]

    payload = [w[n] if n == "dn_conv_w" else w[n].astype(bf16) for n in _BIG]
    gathered = _exchange("allgather_weights", payload, [False] * len(_BIG))
    full = {n: _unshard(n, g) for n, g in zip(_BIG, gathered)}
    layers = []
    for l in range(n_layers):
        wl = {n: (full[n][l] if n in _BIG else w[n][l]) for n in _WEIGHTS}
        layers.append(_layer_params(wl))
    cst = _consts(pos)

    h, res = xs, []
    for l in range(n_layers):
        h, r = _layer_fwd(h, layers[l], cst)
        res.append(r)
    dy, sq = _loss(h, tgt)
    loss = lax.psum(0.5 / D_MODEL * jnp.sum(sq), ("x", "y", "c"))
    grads = [None] * n_layers
    for l in reversed(range(n_layers)):
        dy, g = _layer_bwd(dy, res[l], layers[l], cst)
        grads[l] = _layer_grads_ref_layout(g)
    grad_x = dy[None]
    gfull = {n: jnp.stack([grads[l][n] for l in range(n_layers)]) for n in _WEIGHTS}

    send = [_shard_major(n, gfull[n]).astype(bf16) for n in _BIG] + [_pack_small(gfull)]
    landed = _exchange("exchange_grads", send, [True] * len(_BIG) + [False])

    out_g, out_d, out_m, out_v = {}, {}, {}, {}
    for n, land in zip(_BIG, landed):
        shp = w[n].shape
        flat = lambda a: a.reshape(-1, shp[-1])
        R_ = flat(w[n]).shape[0]
        res4 = _adamw("adamw_" + n, land.reshape(N_DEV, R_, shp[-1]), flat(w[n]), flat(m[n]), flat(v[n]), tr=256)
        out_g[n], out_d[n], out_m[n], out_v[n] = [a.reshape(shp) for a in res4]
    small = _adamw("adamw_small", landed[-1], _pack_small(w), _pack_small(m), _pack_small(v), tr=_SMALL_PACK_ROWS)
    for dst, a in zip((out_g, out_d, out_m, out_v), small):
        dst.update(_unpack_small(a, w))
    return (loss, grad_x, *[out_g[n] for n in _WEIGHTS], *[out_d[n] for n in _WEIGHTS],
            *[out_m[n] for n in _WEIGHTS], *[out_v[n] for n in _WEIGHTS])
```

```python
import functools
import math

import numpy as np
import jax
import jax.numpy as jnp
from jax import lax
from jax.experimental import pallas as pl
from jax.experimental.pallas import tpu as pltpu

f32 = jnp.float32
bf16 = jnp.bfloat16
HIGHEST = lax.Precision.HIGHEST

N_DEV = 8
D_MODEL = 1024
DN_HEADS = 8
DN_DK = 128
DN_CHUNK = 64
DN_CONV = 4
SW_Q_HEADS = 16
SW_KV_HEADS = 2
SW_HEAD_DIM = 64
SW_BLOCK = 128
ROPE_THETA = 500000.0
ROT_DIM = SW_HEAD_DIM // 4
D_FF = 4 * D_MODEL
EPS = 1e-6
LANES = 128
CONV_HALO = 8
NEG_BIG = -1e30

ADAM_LR = 0.001
ADAM_B1 = 0.9
ADAM_B2 = 0.999
ADAM_EPS = 1e-08
ADAM_WD = 0.01
ADAM_STEP = 10

PROJ_W = 7680
CB_Q, CB_K, CB_V, CB_Z, CB_SWQ, CB_GA, CB_GB = 0, 1, 2, 3, 4, 5, 6
CB_SWK, CB_SWV, CB_BA = 56, 57, 58

NN = ((1,), (0,))
NT = ((1,), (1,))
TN = ((0,), (0,))


def _mm(a, b, dims, hi=False):
    if hi:
        return lax.dot_general(a.astype(f32), b.astype(f32), (dims, ((), ())), precision=HIGHEST,
                               preferred_element_type=f32)
    return lax.dot_general(a.astype(bf16), b.astype(bf16), (dims, ((), ())), preferred_element_type=f32)


def _matmul(name, a, b, form, out_dtype, tm=512, tn=512, tk=1024):
    if form == "nn":
        (M, K), (_, N) = a.shape, b.shape
    elif form == "nt":
        (M, K), (N, _) = a.shape, b.shape
    else:
        (K, M), (_, N) = a.shape, b.shape
    tm, tn, tk = min(tm, M), min(tn, N), min(tk, K)
    assert M % tm == 0 and N % tn == 0 and K % tk == 0, (name, M, N, K, tm, tn, tk)
    nk = K // tk
    dims = {"nn": NN, "nt": NT, "tn": TN}[form]

    def body(a_ref, b_ref, o_ref, acc_ref):
        part = lax.dot_general(a_ref[...], b_ref[...], (dims, ((), ())), preferred_element_type=f32)
        if nk == 1:
            o_ref[...] = part.astype(out_dtype)
        else:
            k = pl.program_id(2)

            @pl.when(k == 0)
            def _():
                acc_ref[...] = part

            @pl.when(k > 0)
            def _():
                acc_ref[...] += part

            @pl.when(k == nk - 1)
            def _():
                o_ref[...] = acc_ref[...].astype(out_dtype)

    if form == "tn":
        a_spec = pl.BlockSpec((tk, tm), lambda i, j, k: (k, i))
    else:
        a_spec = pl.BlockSpec((tm, tk), lambda i, j, k: (i, k))
    if form == "nt":
        b_spec = pl.BlockSpec((tn, tk), lambda i, j, k: (j, k))
    else:
        b_spec = pl.BlockSpec((tk, tn), lambda i, j, k: (k, j))
    return pl.pallas_call(
        body, name=name,
        grid=(M // tm, N // tn, nk),
        in_specs=[a_spec, b_spec],
        out_specs=pl.BlockSpec((tm, tn), lambda i, j, k: (i, j)),
        out_shape=jax.ShapeDtypeStruct((M, N), out_dtype),
        scratch_shapes=[pltpu.VMEM((tm, tn) if nk > 1 else (8, 128), f32)],
        compiler_params=pltpu.CompilerParams(dimension_semantics=("parallel", "parallel", "arbitrary")),
    )(a, b)


def _tile_specs(ins, halo_ids, params, TM, HR, row_of):
    specs = [pl.BlockSpec((TM, w), lambda i, cb=cb: (row_of(i), cb)) for (_, w, cb) in ins]
    for h in halo_ids:
        _, w, cb = ins[h]
        specs.append(pl.BlockSpec((HR, w), lambda i, cb=cb: (jnp.maximum(row_of(i) * (TM // HR) - 1, 0), cb)))
    for p in params:
        specs.append(pl.BlockSpec(p.shape, lambda i, nd=p.ndim: (0,) * nd))
    return specs


def _tile_fwd(name, fn, T, TM, ins, params, outs, halo_ids=(), HR=CONV_HALO):
    TM = min(TM, T)
    n = T // TM
    ni, nh, npar = len(ins), len(halo_ids), len(params)

    def body(*refs):
        in_v = [r[...] for r in refs[:ni]]
        halo_v = [r[...] for r in refs[ni:ni + nh]]
        par_v = [r[...] for r in refs[ni + nh:ni + nh + npar]]
        o_refs = refs[ni + nh + npar:]
        first = pl.program_id(0) == 0
        vals = fn(first, in_v, halo_v, par_v)
        for o, val in zip(o_refs, vals):
            o[...] = val.astype(o.dtype)

    res = pl.pallas_call(
        body, name=name, grid=(n,),
        in_specs=_tile_specs(ins, halo_ids, params, TM, HR, lambda i: i),
        out_specs=[pl.BlockSpec((TM, w), lambda i: (i, 0)) for (w, _) in outs],
        out_shape=[jax.ShapeDtypeStruct((T, w), dt) for (w, dt) in outs],
        compiler_params=pltpu.CompilerParams(dimension_semantics=("arbitrary",)),
    )(*[a for (a, _, _) in ins], *[ins[h][0] for h in halo_ids], *params)
    return list(res)


def _tile_bwd(name, fn, T, TM, ins, params, cts, din, dpar, halo_ids=(), HR=CONV_HALO):
    TM = min(TM, T)
    n = T // TM
    ni, nh, npar, nc = len(ins), len(halo_ids), len(params), len(cts)
    din_ids = [j for (j, _) in din]
    dh_ids = [h for h in halo_ids if h in din_ids]
    nd, ndp, ndh = len(din), len(dpar), len(dh_ids)

    def body(*refs):
        in_v = [r[...] for r in refs[:ni]]
        halo_v = [r[...] for r in refs[ni:ni + nh]]
        par_v = [r[...] for r in refs[ni + nh:ni + nh + npar]]
        ct_v = [r[...].astype(f32) for r in refs[ni + nh + npar:ni + nh + npar + nc]]
        o_refs = refs[ni + nh + npar + nc:ni + nh + npar + nc + nd + ndp]
        carry_refs = refs[ni + nh + npar + nc + nd + ndp:]
        i = pl.program_id(0)
        first = i == n - 1

        def g(d_in, d_halo, d_par):
            full_in = list(in_v)
            for j, val in zip(din_ids, d_in):
                full_in[j] = val
            full_halo = list(halo_v)
            for h, val in zip(dh_ids, d_halo):
                full_halo[list(halo_ids).index(h)] = val
            full_par = list(par_v)
            for j, val in zip(dpar, d_par):
                full_par[j] = val
            return tuple(fn(first, full_in, full_halo, full_par))

        prim = ([in_v[j].astype(f32) for j in din_ids],
                [halo_v[list(halo_ids).index(h)].astype(f32) for h in dh_ids],
                [par_v[j] for j in dpar])
        _, vjp = jax.vjp(g, *prim)
        g_in, g_halo, g_par = vjp(tuple(ct_v))

        @pl.when(i == 0)
        def _():
            for c in carry_refs:
                c[...] = jnp.zeros_like(c)
            for o in o_refs[nd:]:
                o[...] = jnp.zeros_like(o)

        for slot, (j, _) in enumerate(din):
            val = g_in[slot]
            if j in dh_ids:
                c = carry_refs[dh_ids.index(j)]
                val = jnp.concatenate([val[:TM - HR], val[TM - HR:] + c[...]], axis=0) if TM > HR else val + c[...]
                c[...] = g_halo[dh_ids.index(j)]
            o_refs[slot][...] = val.astype(o_refs[slot].dtype)
        for slot in range(ndp):
            o_refs[nd + slot][...] += g_par[slot]

    rev = lambda i: n - 1 - i
    in_specs = _tile_specs(ins, halo_ids, params, TM, HR, rev)
    ct_specs = [pl.BlockSpec((TM, w), lambda i, cb=cb: (rev(i), cb)) for (_, w, cb) in cts]
    out_specs = [pl.BlockSpec((TM, ins[j][1]), lambda i: (rev(i), 0)) for j in din_ids]
    out_specs += [pl.BlockSpec(params[j].shape, lambda i, nd_=params[j].ndim: (0,) * nd_) for j in dpar]
    out_shape = [jax.ShapeDtypeStruct((T, ins[j][1]), dt) for (j, dt) in din]
    out_shape += [jax.ShapeDtypeStruct(params[j].shape, f32) for j in dpar]
    res = pl.pallas_call(
        body, name=name, grid=(n,),
        in_specs=in_specs + ct_specs,
        out_specs=out_specs,
        out_shape=out_shape,
        scratch_shapes=[pltpu.VMEM((HR, ins[h][1]), f32) for h in dh_ids],
        compiler_params=pltpu.CompilerParams(dimension_semantics=("arbitrary",)),
    )(*[a for (a, _, _) in ins], *[ins[h][0] for h in halo_ids], *params, *[a for (a, _, _) in cts])
    return list(res)


def _rms(x, g):
    return x * lax.rsqrt(jnp.mean(x * x, axis=-1, keepdims=True) + EPS) * g


def _fn_prenorm(first, ins, halos, params):
    (x,), (g,) = ins, params
    x = x.astype(f32)
    return [_rms(x, g), x]


def _fn_postmix(first, ins, halos, params):
    (x, mix), (g2, g3) = ins, params
    x1 = x + _rms(mix, g2)
    return [x1, _rms(x1, g3)]


def _fn_postmlp(first, ins, halos, params):
    (x1, ff), (g4,) = ins, params
    return [x1 + _rms(ff, g4)]


def _fn_rms_only(first, ins, halos, params):
    (ff,), (g4,) = ins, params
    return [_rms(ff, g4)]


def _fn_relu2(first, ins, halos, params):
    (h,) = ins
    return [jnp.square(jnp.maximum(h, 0.0))]


def _fn_merge(first, ins, halos, params):
    ga, gb, ya, yb = ins
    return [jax.nn.sigmoid(ga) * ya + jax.nn.sigmoid(gb) * yb]


def _make_fn_conv(norm_scale):
    def fn(first, ins, halos, params):
        (x,), (xp,), (w,) = ins, halos, params
        TM = x.shape[0]
        xp = jnp.where(first, 0.0, xp)
        xe = jnp.concatenate([xp, x], axis=0)
        off = CONV_HALO - (DN_CONV - 1)
        y = xe[off:off + TM] * w[0:1]
        for j in range(1, DN_CONV):
            y = y + xe[off + j:off + j + TM] * w[j:j + 1]
        y = jax.nn.silu(y)
        if norm_scale is None:
            return [y]
        outs = []
        for h in range(DN_HEADS):
            yh = y[:, DN_DK * h:DN_DK * (h + 1)]
            outs.append(yh * lax.rsqrt(jnp.sum(yh * yh, axis=-1, keepdims=True) + EPS) * norm_scale)
        return [jnp.concatenate(outs, axis=-1)]
    return fn


def _fn_gates(first, ins, halos, params):
    (ba,), (avec, dvec) = ins, params
    lane = lax.broadcasted_iota(jnp.int32, ba.shape, 1)
    beta = jax.nn.sigmoid(ba)
    g = -jnp.exp(avec) * jax.nn.softplus(ba + dvec)
    return [jnp.where(lane < DN_HEADS, beta, jnp.where(lane < 2 * DN_HEADS, g, 0.0))]


def _fn_dnpost(first, ins, halos, params):
    (o, z), (ng,) = ins, params
    outs = []
    for h in range(DN_HEADS):
        sl = slice(DN_DK * h, DN_DK * (h + 1))
        outs.append(_rms(o[:, sl], ng) * jax.nn.silu(z[:, sl]))
    return [jnp.concatenate(outs, axis=-1)]


def _fn_swa(first, ins, halos, params):
    q, k, v, cos, sin = ins
    kp, vp, cosp, sinp = halos
    sinks, rot, sel_a0, sel_b0, sel_a1, sel_b1 = params
    B = q.shape[0]

    def rope(x, c, s):
        return x * c + _mm(x, rot, NN, hi=True) * s

    kcat = jnp.concatenate([rope(kp, cosp, sinp), rope(k, cos, sin)], axis=0)
    vcat = jnp.concatenate([vp, v], axis=0)
    r = lax.broadcasted_iota(jnp.int32, (B, 2 * B), 0)
    c = lax.broadcasted_iota(jnp.int32, (B, 2 * B), 1)
    mask = (c > r) & (c <= r + B) & ((c >= B) | jnp.logical_not(first))
    group = SW_Q_HEADS // SW_KV_HEADS
    sels = ((sel_a0, sel_b0), (sel_a1, sel_b1))
    outs = []
    for hk in range(SW_KV_HEADS):
        heads = range(group)
        kx = [_mm(kcat, sels[hk][half], NN) for half in range(2)]
        vx = [_mm(vcat, sels[hk][half], NN) for half in range(2)]
        qp = [rope(q[:, LANES * j:LANES * (j + 1)], cos, sin) for j in range(hk * group // 2, (hk + 1) * group // 2)]
        s = [jnp.where(mask, _mm(qp[g // 2], kx[g % 2], NT) * (SW_HEAD_DIM ** -0.5), NEG_BIG) for g in heads]
        sink = [sinks[:, hk * group + g:hk * group + g + 1] for g in heads]
        m = [jnp.maximum(jnp.max(s[g], axis=-1, keepdims=True), sink[g]) for g in heads]
        p = [jnp.exp(s[g] - m[g]) for g in heads]
        probs = [p[g] / (jnp.sum(p[g], axis=-1, keepdims=True) + jnp.exp(sink[g] - m[g])) for g in heads]
        o = [_mm(probs[g], vx[g % 2], NN) for g in heads]
        outs += [o[2 * j] + o[2 * j + 1] for j in range(group // 2)]
    return [jnp.concatenate(outs, axis=-1)]


def _fn_add(first, ins, halos, params):
    a, b = ins
    return [a.astype(f32) + b.astype(f32)]


@jax.custom_vjp
def _inv_unit_lower(Ls):
    C = Ls[0].shape[0]
    ii = lax.broadcasted_iota(jnp.int32, (C, C), 0)
    jj = lax.broadcasted_iota(jnp.int32, (C, C), 1)

    def off_mask(level):
        same_pair = jnp.right_shift(ii, level + 1) == jnp.right_shift(jj, level + 1)
        lower_left = (jnp.bitwise_and(jnp.right_shift(ii, level), 1) == 1) & (jnp.bitwise_and(jnp.right_shift(jj, level), 1) == 0)
        return same_pair & lower_left

    eye = (ii == jj).astype(f32)
    m0 = off_mask(0)
    Ts = [eye - jnp.where(m0, L, 0.0) for L in Ls]
    for level in range(1, int(math.log2(C))):
        mk = off_mask(level)
        left = [_mm(T_, jnp.where(mk, L, 0.0), NN) for T_, L in zip(Ts, Ls)]
        Ts = [T_ - _mm(a, T_, NN) for a, T_ in zip(left, Ts)]
    return tuple(Ts)


def _inv_fwd(Ls):
    Ts = _inv_unit_lower(Ls)
    return Ts, Ts


def _inv_bwd(Ts, dTs):
    left = [_mm(T_, dT, TN) for T_, dT in zip(Ts, dTs)]
    return (tuple(-_mm(a, T_, NT) for a, T_ in zip(left, Ts)),)


_inv_unit_lower.defvjp(_inv_fwd, _inv_bwd)


def _dn_chunk(q, k, v, gb, S):
    C = q.shape[0]
    H = range(DN_HEADS)
    ii = lax.broadcasted_iota(jnp.int32, (C, C), 0)
    jj = lax.broadcasted_iota(jnp.int32, (C, C), 1)
    causal, strict = ii >= jj, ii > jj
    gc_all = _mm(causal.astype(f32), gb, NN, hi=True)
    eye = (lax.broadcasted_iota(jnp.int32, (LANES, LANES), 0)
           == lax.broadcasted_iota(jnp.int32, (LANES, LANES), 1)).astype(f32)
    gc_t = _mm(eye, gc_all, NT, hi=True)
    sl = [slice(DN_DK * h, DN_DK * (h + 1)) for h in H]
    qs, ks, vs = [q[:, s] for s in sl], [k[:, s] for s in sl], [v[:, s] for s in sl]
    beta = [gb[:, h:h + 1] for h in H]
    gcol = [gc_all[:, DN_HEADS + h:DN_HEADS + h + 1] for h in H]
    grow = [gc_t[DN_HEADS + h:DN_HEADS + h + 1, :] for h in H]
    decay = [jnp.where(causal, jnp.exp(jnp.where(causal, gcol[h] - grow[h], 0.0)), 0.0) for h in H]
    kb = [ks[h] * beta[h] for h in H]
    kk = [_mm(kb[h], ks[h], NT) for h in H]
    qk = [_mm(qs[h], ks[h], NT) for h in H]
    tinv = _inv_unit_lower(tuple(jnp.where(strict, kk[h] * decay[h], 0.0) for h in H))
    eg = [jnp.exp(gcol[h]) for h in H]
    u = [_mm(tinv[h], vs[h] * beta[h], NN) for h in H]
    w = [_mm(tinv[h], kb[h] * eg[h], NN) for h in H]
    gl = [gcol[h][C - 1:C, :] for h in H]
    ws = [_mm(w[h], S[h], NN) for h in H]
    qS = [_mm(qs[h] * eg[h], S[h], NN) for h in H]
    v_new = [u[h] - ws[h] for h in H]
    av = [_mm(qk[h] * decay[h], v_new[h], NN) for h in H]
    kv = [_mm(ks[h] * jnp.exp(gl[h] - gcol[h]), v_new[h], TN) for h in H]
    o = jnp.concatenate([qS[h] + av[h] for h in H], axis=-1)
    return o, tuple(S[h] * jnp.exp(gl[h]) + kv[h] for h in H)


def _delta_fwd(qn, kn, vv, gb):
    T = qn.shape[0]
    C = DN_CHUNK
    n = T // C
    W = DN_HEADS * DN_DK

    def body(q_ref, k_ref, v_ref, gb_ref, o_ref, hist_ref, s_ref):
        @pl.when(pl.program_id(0) == 0)
        def _():
            s_ref[...] = jnp.zeros_like(s_ref)

        S = tuple(s_ref[h] for h in range(DN_HEADS))
        for h in range(DN_HEADS):
            hist_ref[0, h] = S[h]
        o, s_new = _dn_chunk(q_ref[...], k_ref[...], v_ref[...], gb_ref[...], S)
        o_ref[...] = o
        for h in range(DN_HEADS):
            s_ref[h] = s_new[h]

    row = pl.BlockSpec((C, W), lambda i: (i, 0))
    return pl.pallas_call(
        body, name="delta_fwd", grid=(n,),
        in_specs=[row, row, row, pl.BlockSpec((C, LANES), lambda i: (i, 0))],
        out_specs=[row, pl.BlockSpec((1, DN_HEADS, DN_DK, DN_DK), lambda i: (i, 0, 0, 0))],
        out_shape=[jax.ShapeDtypeStruct((T, W), f32), jax.ShapeDtypeStruct((n, DN_HEADS, DN_DK, DN_DK), f32)],
        scratch_shapes=[pltpu.VMEM((DN_HEADS, DN_DK, DN_DK), f32)],
        compiler_params=pltpu.CompilerParams(dimension_semantics=("arbitrary",)),
    )(qn, kn, vv, gb)


def _delta_bwd(qn, kn, vv, gb, hist, do):
    T = qn.shape[0]
    C = DN_CHUNK
    n = T // C
    W = DN_HEADS * DN_DK

    def body(q_ref, k_ref, v_ref, gb_ref, hist_ref, do_ref, dq_ref, dk_ref, dv_ref, dgb_ref, ds_ref):
        @pl.when(pl.program_id(0) == 0)
        def _():
            ds_ref[...] = jnp.zeros_like(ds_ref)

        S = tuple(hist_ref[0, h] for h in range(DN_HEADS))
        _, vjp = jax.vjp(_dn_chunk, q_ref[...], k_ref[...], v_ref[...], gb_ref[...], S)
        dS = tuple(ds_ref[h] for h in range(DN_HEADS))
        dq, dk, dv, dgb, dS_in = vjp((do_ref[...], dS))
        dq_ref[...] = dq
        dk_ref[...] = dk
        dv_ref[...] = dv
        dgb_ref[...] = dgb
        for h in range(DN_HEADS):
            ds_ref[h] = dS_in[h]

    row = pl.BlockSpec((C, W), lambda i: (n - 1 - i, 0))
    small = pl.BlockSpec((C, LANES), lambda i: (n - 1 - i, 0))
    return pl.pallas_call(
        body, name="delta_bwd", grid=(n,),
        in_specs=[row, row, row, small, pl.BlockSpec((1, DN_HEADS, DN_DK, DN_DK), lambda i: (n - 1 - i, 0, 0, 0)), row],
        out_specs=[row, row, row, small],
        out_shape=[jax.ShapeDtypeStruct((T, W), f32)] * 3 + [jax.ShapeDtypeStruct((T, LANES), f32)],
        scratch_shapes=[pltpu.VMEM((DN_HEADS, DN_DK, DN_DK), f32)],
        compiler_params=pltpu.CompilerParams(dimension_semantics=("arbitrary",)),
    )(qn, kn, vv, gb, hist, do)


TM_ROW = 256
W1 = D_MODEL


def _first_only(fn):
    return lambda *a: fn(*a)[:1]


def _swa_args(proj, cst):
    ins = [(proj, W1, CB_SWQ), (proj, LANES, CB_SWK), (proj, LANES, CB_SWV), (cst["cos"], LANES, 0), (cst["sin"], LANES, 0)]
    return ins, (1, 2, 3, 4)


def _layer_fwd(x, p, cst):
    T = x.shape[0]
    r = {"x": x}
    (h,) = _tile_fwd("prenorm", _first_only(_fn_prenorm), T, TM_ROW, [(x, W1, 0)], [p["g1"]], [(W1, bf16)])
    proj = _matmul("proj", h, p["w_in"], "nn", f32, tm=1024, tn=1536)
    conv = lambda nm, cb, scale: _tile_fwd(nm, _make_fn_conv(scale), T, TM_ROW, [(proj, W1, cb)],
                                            [p["conv_w"][:, W1 * cb:W1 * (cb + 1)]], [(W1, f32)], halo_ids=(0,))[0]
    qn = conv("conv_q", CB_Q, DN_DK ** -0.5)
    kn = conv("conv_k", CB_K, 1.0)
    vv = conv("conv_v", CB_V, None)
    (gbt,) = _tile_fwd("gates", _fn_gates, T, TM_ROW, [(proj, LANES, CB_BA)], [p["avec"], p["dvec"]], [(LANES, f32)])
    o, hist = _delta_fwd(qn, kn, vv, gbt)
    (dn_out,) = _tile_fwd("dnpost", _fn_dnpost, T, TM_ROW, [(o, W1, 0), (proj, W1, CB_Z)], [p["ng"]], [(W1, bf16)])
    sw_ins, sw_halo = _swa_args(proj, cst)
    sw_par = [p["sinks"], cst["rot"], cst["sel_a0"], cst["sel_b0"], cst["sel_a1"], cst["sel_b1"]]
    (sw_out,) = _tile_fwd("swa", _fn_swa, T, SW_BLOCK, sw_ins, sw_par, [(W1, bf16)], halo_ids=sw_halo, HR=SW_BLOCK)
    y_a = _matmul("up_dn", dn_out, p["w_up_dn"], "nn", f32, tm=1024, tn=1024)
    y_b = _matmul("up_sw", sw_out, p["w_up_sw"], "nn", f32, tm=1024, tn=1024)
    (gated,) = _tile_fwd("merge", _fn_merge, T, TM_ROW,
                         [(proj, W1, CB_GA), (proj, W1, CB_GB), (y_a, W1, 0), (y_b, W1, 0)], [], [(W1, bf16)])
    mix = _matmul("w_o", gated, p["w_o"], "nn", f32, tm=1024, tn=1024)
    x1, h2 = _tile_fwd("postmix", _fn_postmix, T, TM_ROW, [(x, W1, 0), (mix, W1, 0)], [p["g2"], p["g3"]],
                       [(W1, f32), (W1, bf16)])
    ffh = _matmul("ff1", h2, p["w_ff1"], "nn", f32, tm=1024, tn=1024)
    (act,) = _tile_fwd("relu2", _fn_relu2, T, TM_ROW, [(ffh, D_FF, 0)], [], [(D_FF, bf16)])
    ff = _matmul("ff2", act, p["w_ff2"], "nn", f32, tm=1024, tn=1024)
    (x2,) = _tile_fwd("postmlp", _fn_postmlp, T, TM_ROW, [(x1, W1, 0), (ff, W1, 0)], [p["g4"]], [(W1, f32)])
    r.update(h=h, proj=proj, qn=qn, kn=kn, vv=vv, gbt=gbt, o=o, hist=hist, dn_out=dn_out, sw_out=sw_out,
             y_a=y_a, y_b=y_b, gated=gated, mix=mix, h2=h2, ffh=ffh, act=act, ff=ff)
    return x2, r


def _layer_bwd(dx2, r, p, cst):
    T = dx2.shape[0]
    x, proj = r["x"], r["proj"]
    g = {}
    dff, g["g4"] = _tile_bwd("postmlp_b", _fn_rms_only, T, TM_ROW, [(r["ff"], W1, 0)], [p["g4"]], [(dx2, W1, 0)],
                             [(0, bf16)], [0])
    dact = _matmul("ff2_dx", dff, p["w_ff2"], "nt", f32, tm=1024, tn=1024)
    g["w_ff2"] = _matmul("ff2_dw", r["act"], dff, "tn", f32, tm=1024, tn=1024)
    (dffh,) = _tile_bwd("relu2_b", _fn_relu2, T, TM_ROW, [(r["ffh"], D_FF, 0)], [], [(dact, D_FF, 0)], [(0, bf16)], [])
    dh2 = _matmul("ff1_dx", dffh, p["w_ff1"], "nt", f32, tm=1024, tn=1024)
    g["w_ff1"] = _matmul("ff1_dw", r["h2"], dffh, "tn", f32, tm=1024, tn=1024)
    dx1, dmix, g["g2"], g["g3"] = _tile_bwd("postmix_b", _fn_postmix, T, TM_ROW, [(x, W1, 0), (r["mix"], W1, 0)],
                                            [p["g2"], p["g3"]], [(dx2, W1, 0), (dh2, W1, 0)], [(0, f32), (1, bf16)], [0, 1])
    dgated = _matmul("w_o_dx", dmix, p["w_o"], "nt", f32, tm=1024, tn=1024)
    g["w_o"] = _matmul("w_o_dw", r["gated"], dmix, "tn", f32, tm=1024, tn=1024)
    dga, dgb, dya, dyb = _tile_bwd("merge_b", _fn_merge, T, TM_ROW,
                                   [(proj, W1, CB_GA), (proj, W1, CB_GB), (r["y_a"], W1, 0), (r["y_b"], W1, 0)], [],
                                   [(dgated, W1, 0)], [(0, bf16), (1, bf16), (2, bf16), (3, bf16)], [])
    d_dn = _matmul("up_dn_dx", dya, p["w_up_dn"], "nt", f32, tm=1024, tn=1024)
    g["w_up_dn"] = _matmul("up_dn_dw", r["dn_out"], dya, "tn", f32, tm=1024, tn=1024)
    d_sw = _matmul("up_sw_dx", dyb, p["w_up_sw"], "nt", f32, tm=1024, tn=1024)
    g["w_up_sw"] = _matmul("up_sw_dw", r["sw_out"], dyb, "tn", f32, tm=1024, tn=1024)
    do, dz, g["ng"] = _tile_bwd("dnpost_b", _fn_dnpost, T, TM_ROW, [(r["o"], W1, 0), (proj, W1, CB_Z)], [p["ng"]],
                                [(d_dn, W1, 0)], [(0, f32), (1, bf16)], [0])
    dqn, dkn, dvv, dgbt = _delta_bwd(r["qn"], r["kn"], r["vv"], r["gbt"], r["hist"], do)
    conv_b = lambda nm, cb, scale, ct: _tile_bwd(nm, _make_fn_conv(scale), T, TM_ROW, [(proj, W1, cb)],
                                                 [p["conv_w"][:, W1 * cb:W1 * (cb + 1)]], [(ct, W1, 0)], [(0, bf16)], [0],
                                                 halo_ids=(0,))
    dq_in, dcw_q = conv_b("conv_q_b", CB_Q, DN_DK ** -0.5, dqn)
    dk_in, dcw_k = conv_b("conv_k_b", CB_K, 1.0, dkn)
    dv_in, dcw_v = conv_b("conv_v_b", CB_V, None, dvv)
    g["conv_w"] = jnp.concatenate([dcw_q, dcw_k, dcw_v], axis=-1)
    dba, g["avec"], g["dvec"] = _tile_bwd("gates_b", _fn_gates, T, TM_ROW, [(proj, LANES, CB_BA)], [p["avec"], p["dvec"]],
                                          [(dgbt, LANES, 0)], [(0, bf16)], [0, 1])
    sw_ins, sw_halo = _swa_args(proj, cst)
    sw_par = [p["sinks"], cst["rot"], cst["sel_a0"], cst["sel_b0"], cst["sel_a1"], cst["sel_b1"]]
    dswq, dswk, dswv, g["sinks"] = _tile_bwd("swa_b", _fn_swa, T, SW_BLOCK, sw_ins, sw_par, [(d_sw, W1, 0)],
                                             [(0, bf16), (1, bf16), (2, bf16)], [0], halo_ids=sw_halo, HR=SW_BLOCK)
    dproj = jnp.concatenate([dq_in, dk_in, dv_in, dz, dswq, dga, dgb, dswk, dswv, dba, jnp.zeros((T, LANES), bf16)], axis=-1)
    dh = _matmul("proj_dx", dproj, p["w_in"], "nt", f32, tm=1024, tn=1024, tk=1536)
    g["w_in"] = _matmul("proj_dw", r["h"], dproj, "tn", f32, tm=1024, tn=1536)
    dx, g["g1"] = _tile_bwd("prenorm_b", _fn_prenorm, T, TM_ROW, [(x, W1, 0)], [p["g1"]], [(dh, W1, 0), (dx1, W1, 0)],
                            [(0, f32)], [0])
    return dx, g


_OFF_BA, _OFF_SWQ, _OFF_SWK, _OFF_GA, _D_IN = 4096, 4112, 5136, 5392, 7440


def _proj_cols(w):
    pad = lambda n: jnp.zeros(w.shape[:-1] + (n,), w.dtype)
    return jnp.concatenate([w[..., :_OFF_BA], w[..., _OFF_SWQ:_OFF_SWK], w[..., _OFF_GA:_D_IN],
                            w[..., _OFF_SWK:_OFF_GA], w[..., _OFF_BA:_OFF_SWQ], pad(PROJ_W - _D_IN)], axis=-1)


def _proj_cols_inv(w):
    n_ba = _OFF_SWQ - _OFF_BA
    return jnp.concatenate([w[..., :4096], w[..., 7424:7424 + n_ba], w[..., 4096:5120], w[..., 7168:7424],
                            w[..., 5120:7168]], axis=-1)


def _lane_pad(v, at):
    return jnp.pad(v.astype(f32), (at, LANES - at - v.shape[0])).reshape(1, LANES)


def _layer_params(w):
    row = lambda v: v.reshape(1, -1).astype(f32)
    return dict(
        g1=row(w["pre_mix_g"]), g2=row(w["post_mix_g"]), g3=row(w["pre_mlp_g"]), g4=row(w["post_mlp_g"]),
        w_in=_proj_cols(w["w_in"]).astype(bf16), conv_w=w["dn_conv_w"].astype(f32),
        avec=_lane_pad(w["dn_a_log"], DN_HEADS), dvec=_lane_pad(w["dn_dt_bias"], DN_HEADS),
        ng=row(w["dn_norm_g"]), sinks=_lane_pad(w["sw_sinks"], 0),
        w_up_dn=w["w_up_dn"].astype(bf16), w_up_sw=w["w_up_sw"].astype(bf16), w_o=w["w_o"].astype(bf16),
        w_ff1=w["w_ff1"].astype(bf16), w_ff2=w["w_ff2"].astype(bf16))


def _layer_grads_ref_layout(g):
    return dict(
        pre_mix_g=g["g1"][0], post_mix_g=g["g2"][0], pre_mlp_g=g["g3"][0], post_mlp_g=g["g4"][0],
        w_in=_proj_cols_inv(g["w_in"]), dn_conv_w=g["conv_w"],
        dn_a_log=g["avec"][0, DN_HEADS:2 * DN_HEADS], dn_dt_bias=g["dvec"][0, DN_HEADS:2 * DN_HEADS],
        dn_norm_g=g["ng"][0], sw_sinks=g["sinks"][0, :SW_Q_HEADS],
        w_up_dn=g["w_up_dn"], w_up_sw=g["w_up_sw"], w_o=g["w_o"], w_ff1=g["w_ff1"], w_ff2=g["w_ff2"])


def _consts(positions):
    T = positions.shape[0]
    half = ROT_DIM // 2
    inv_freq = ROPE_THETA ** (-jnp.arange(half, dtype=f32) * (2.0 / ROT_DIM))
    ang = positions.astype(f32)[:, None] * inv_freq
    cos8, sin8 = jnp.cos(ang), jnp.sin(ang)
    rest = SW_HEAD_DIM - ROT_DIM
    c64 = jnp.concatenate([cos8, cos8, jnp.ones((T, rest), f32)], axis=-1)
    s64 = jnp.concatenate([sin8, sin8, jnp.zeros((T, rest), f32)], axis=-1)
    rot = np.zeros((LANES, LANES), np.float32)
    sel = np.zeros((2, 2, LANES, LANES), np.float32)
    for base in (0, SW_HEAD_DIM):
        for i in range(half):
            rot[base + half + i, base + i] = -1.0
            rot[base + i, base + half + i] = 1.0
    for hk in range(SW_KV_HEADS):
        for d in range(SW_HEAD_DIM):
            sel[hk, 0, SW_HEAD_DIM * hk + d, d] = 1.0
            sel[hk, 1, SW_HEAD_DIM * hk + d, SW_HEAD_DIM + d] = 1.0
    return dict(cos=jnp.concatenate([c64, c64], axis=-1), sin=jnp.concatenate([s64, s64], axis=-1),
                rot=jnp.asarray(rot), sel_a0=jnp.asarray(sel[0, 0]), sel_b0=jnp.asarray(sel[0, 1]),
                sel_a1=jnp.asarray(sel[1, 0]), sel_b1=jnp.asarray(sel[1, 1]))


def _loss(y, tgt):
    T, W = y.shape
    TM = min(TM_ROW, T)
    n = T // TM

    def body(y_ref, t_ref, dy_ref, acc_ref):
        @pl.when(pl.program_id(0) == 0)
        def _():
            acc_ref[...] = jnp.zeros_like(acc_ref)

        d = y_ref[...] - t_ref[...]
        dy_ref[...] = d * (1.0 / W)
        acc_ref[...] += jnp.sum(d * d, axis=0, keepdims=True)

    row = pl.BlockSpec((TM, W), lambda i: (i, 0))
    return pl.pallas_call(
        body, name="loss", grid=(n,), in_specs=[row, row],
        out_specs=[row, pl.BlockSpec((1, W), lambda i: (0, 0))],
        out_shape=[jax.ShapeDtypeStruct((T, W), f32), jax.ShapeDtypeStruct((1, W), f32)],
        compiler_params=pltpu.CompilerParams(dimension_semantics=("arbitrary",)),
    )(y, tgt)


N_CHIP = 4
N_SEM = N_DEV - 1


def _exchange(name, arrs, modes):
    n = len(arrs)
    lead = {"gather": N_DEV, "gather2": N_DEV, "to_sibling": N_CHIP, "to_chips": N_CHIP}
    outs_shape = [jax.ShapeDtypeStruct((lead[md],) + (a.shape if md.startswith("gather") else a.shape[1:]), a.dtype)
                  for a, md in zip(arrs, modes)]

    def body(*refs):
        ins, outs = refs[:n], refs[n:2 * n]
        send_sems, recv_sems, loc_sems = refs[2 * n:]
        x, y, c = lax.axis_index("x"), lax.axis_index("y"), lax.axis_index("c")
        me, chip, sib = 4 * x + 2 * y + c, 2 * x + y, (x, y, 1 - c)
        flips = [(1 - x, y), (x, 1 - y), (1 - x, 1 - y)]

        def rcopy(a, k, src, dst, dev):
            return pltpu.make_async_remote_copy(src_ref=src, dst_ref=dst, send_sem=send_sems.at[a, k],
                                                recv_sem=recv_sems.at[a, k], device_id=dev,
                                                device_id_type=pl.DeviceIdType.MESH)

        sends, recvs, local, passes = [], [], [], []
        for a, md in enumerate(modes):
            src_all, out = ins[a], outs[a]
            if md in ("gather", "gather2"):
                local.append(pltpu.make_async_copy(src_all, out.at[me], loc_sems.at[a]))
            if md == "to_chips":
                local.append(pltpu.make_async_copy(src_all.at[chip], out.at[chip], loc_sems.at[a]))
            if md == "gather":
                for k in range(1, N_DEV):
                    px = 1 - x if (k >> 2) & 1 else x
                    py = 1 - y if (k >> 1) & 1 else y
                    pc = 1 - c if k & 1 else c
                    sends.append(rcopy(a, k - 1, src_all, out.at[me], (px, py, pc)))
                    recvs.append(rcopy(a, k - 1, src_all, out.at[4 * px + 2 * py + pc], (px, py, pc)))
            elif md == "gather2":
                sends.append(rcopy(a, 0, src_all, out.at[me], sib))
                recvs.append(rcopy(a, 0, src_all, out.at[4 * x + 2 * y + 1 - c], sib))
                for j, (px, py) in enumerate(flips):
                    sends.append(rcopy(a, 1 + j, src_all, out.at[me], (px, py, c)))
                    theirs = out.at[4 * px + 2 * py + c]
                    arrive = rcopy(a, 1 + j, src_all, theirs, (px, py, c))
                    passes.append((arrive, rcopy(a, 4 + j, theirs, theirs, sib)))
                    recvs.append(rcopy(a, 4 + j, src_all, out.at[4 * px + 2 * py + 1 - c], sib))
            elif md == "to_sibling":
                for j in range(N_CHIP):
                    sends.append(rcopy(a, j, src_all.at[2 * j + 1 - c], out.at[j], sib))
                    recvs.append(rcopy(a, j, src_all.at[2 * j + c], out.at[j], sib))
            elif md == "to_chips":
                for j, (px, py) in enumerate(flips):
                    sends.append(rcopy(a, j, src_all.at[2 * px + py], out.at[chip], (px, py, c)))
                    recvs.append(rcopy(a, j, src_all.at[chip], out.at[2 * px + py], (px, py, c)))
        for cp in local + sends:
            cp.start()
        for arrive, onward in passes:
            arrive.wait_recv()
            onward.start()
        for cp in recvs:
            cp.wait_recv()
        for cp in sends:
            cp.wait_send()
        for _, onward in passes:
            onward.wait_send()
        for cp in local:
            cp.wait()

    hbm = pl.BlockSpec(memory_space=pltpu.HBM)
    res = pl.pallas_call(
        body, name=name,
        in_specs=[hbm] * n, out_specs=[hbm] * n, out_shape=outs_shape,
        scratch_shapes=[pltpu.SemaphoreType.DMA((n, N_SEM)), pltpu.SemaphoreType.DMA((n, N_SEM)),
                        pltpu.SemaphoreType.DMA((n,))],
    )(*arrs)
    return list(res)


def _adamw(name, land, w, m, v, tr):
    R_, C_ = w.shape
    n_slots = land.shape[0]
    tr = min(tr, R_)
    assert R_ % tr == 0, (name, R_, tr)
    c1 = 1.0 - ADAM_B1 ** ADAM_STEP
    c2 = 1.0 - ADAM_B2 ** ADAM_STEP

    def body(l_ref, w_ref, m_ref, v_ref, g_ref, d_ref, mo_ref, vo_ref):
        g = l_ref[0].astype(f32)
        for s in range(1, n_slots):
            g = g + l_ref[s].astype(f32)
        m_new = ADAM_B1 * m_ref[...] + (1.0 - ADAM_B1) * g
        v_new = ADAM_B2 * v_ref[...] + (1.0 - ADAM_B2) * jnp.square(g)
        m_hat = m_new / c1
        v_hat = v_new / c2
        g_ref[...] = g
        d_ref[...] = -ADAM_LR * (m_hat / (jnp.sqrt(v_hat) + ADAM_EPS) + ADAM_WD * w_ref[...])
        mo_ref[...] = m_new
        vo_ref[...] = v_new

    row = pl.BlockSpec((tr, C_), lambda i: (i, 0))
    return pl.pallas_call(
        body, name=name, grid=(R_ // tr,),
        in_specs=[pl.BlockSpec((n_slots, tr, C_), lambda i: (0, i, 0)), row, row, row],
        out_specs=[row] * 4, out_shape=[jax.ShapeDtypeStruct((R_, C_), f32)] * 4,
        compiler_params=pltpu.CompilerParams(dimension_semantics=("arbitrary",)),
    )(land, w, m, v)


_BIG = ("w_in", "dn_conv_w", "w_up_dn", "w_up_sw", "w_o", "w_ff1", "w_ff2")
_COL_SHARDED = ("w_in", "dn_conv_w", "w_ff1")
_SMALL_ROWS = ("pre_mix_g", "post_mix_g", "pre_mlp_g", "post_mlp_g")
_SMALL_MISC = ("dn_a_log", "dn_dt_bias", "dn_norm_g", "sw_sinks")
_WEIGHTS = ("pre_mix_g", "w_in", "dn_conv_w", "dn_a_log", "dn_dt_bias", "dn_norm_g", "sw_sinks", "w_up_dn", "w_up_sw",
            "w_o", "post_mix_g", "pre_mlp_g", "w_ff1", "w_ff2", "post_mlp_g")
_SMALL_PACK_ROWS = 24


def _unshard(name, g):
    if name in _COL_SHARDED:
        g = jnp.moveaxis(g, 0, -2)
        return g.reshape(g.shape[:-2] + (g.shape[-2] * g.shape[-1],))
    g = jnp.moveaxis(g, 0, 1)
    return g.reshape((g.shape[0], g.shape[1] * g.shape[2]) + g.shape[3:])


def _shard_major(name, full):
    if name in _COL_SHARDED:
        s = full.reshape(full.shape[:-1] + (N_DEV, full.shape[-1] // N_DEV))
        return jnp.moveaxis(s, -2, 0)
    s = full.reshape((full.shape[0], N_DEV, full.shape[1] // N_DEV) + full.shape[2:])
    return jnp.moveaxis(s, 1, 0)


def _pack_small(d):
    rows = jnp.concatenate([d[n] for n in _SMALL_ROWS], axis=0)
    misc = jnp.concatenate([d[n].reshape(-1) for n in _SMALL_MISC])
    misc = jnp.pad(misc, (0, W1 - misc.shape[0])).reshape(1, W1)
    out = jnp.concatenate([rows, misc], axis=0)
    return jnp.pad(out, ((0, _SMALL_PACK_ROWS - out.shape[0]), (0, 0)))


def _unpack_small(a, like):
    out, L = {}, like[_SMALL_ROWS[0]].shape[0]
    for i, n in enumerate(_SMALL_ROWS):
        out[n] = a[L * i:L * (i + 1)]
    at, row = 0, a[L * len(_SMALL_ROWS)]
    for n in _SMALL_MISC:
        size = like[n].size
        out[n] = row[at:at + size].reshape(like[n].shape)
        at += size
    return out


def kernel(x, positions, pre_mix_g, w_in, dn_conv_w, dn_a_log, dn_dt_bias, dn_norm_g, sw_sinks, w_up_dn, w_up_sw, w_o, post_mix_g, pre_mlp_g, w_ff1, w_ff2, post_mlp_g, loss_target, m_pre_mix_g, m_w_in, m_dn_conv_w, m_dn_a_log, m_dn_dt_bias, m_dn_norm_g, m_sw_sinks, m_w_up_dn, m_w_up_sw, m_w_o, m_post_mix_g, m_pre_mlp_g, m_w_ff1, m_w_ff2, m_post_mlp_g, v_pre_mix_g, v_w_in, v_dn_conv_w, v_dn_a_log, v_dn_dt_bias, v_dn_norm_g, v_sw_sinks, v_w_up_dn, v_w_up_sw, v_w_o, v_post_mix_g, v_pre_mlp_g, v_w_ff1, v_w_ff2, v_post_mlp_g):
    w = dict(pre_mix_g=pre_mix_g, w_in=w_in, dn_conv_w=dn_conv_w, dn_a_log=dn_a_log, dn_dt_bias=dn_dt_bias,
             dn_norm_g=dn_norm_g, sw_sinks=sw_sinks, w_up_dn=w_up_dn, w_up_sw=w_up_sw, w_o=w_o, post_mix_g=post_mix_g,
             pre_mlp_g=pre_mlp_g, w_ff1=w_ff1, w_ff2=w_ff2, post_mlp_g=post_mlp_g)
    m = dict(pre_mix_g=m_pre_mix_g, w_in=m_w_in, dn_conv_w=m_dn_conv_w, dn_a_log=m_dn_a_log, dn_dt_bias=m_dn_dt_bias,
             dn_norm_g=m_dn_norm_g, sw_sinks=m_sw_sinks, w_up_dn=m_w_up_dn, w_up_sw=m_w_up_sw, w_o=m_w_o,
             post_mix_g=m_post_mix_g, pre_mlp_g=m_pre_mlp_g, w_ff1=m_w_ff1, w_ff2=m_w_ff2, post_mlp_g=m_post_mlp_g)
    v = dict(pre_mix_g=v_pre_mix_g, w_in=v_w_in, dn_conv_w=v_dn_conv_w, dn_a_log=v_dn_a_log, dn_dt_bias=v_dn_dt_bias,
             dn_norm_g=v_dn_norm_g, sw_sinks=v_sw_sinks, w_up_dn=v_w_up_dn, w_up_sw=v_w_up_sw, w_o=v_w_o,
             post_mix_g=v_post_mix_g, pre_mlp_g=v_pre_mlp_g, w_ff1=v_w_ff1, w_ff2=v_w_ff2, post_mlp_g=v_post_mlp_g)
    n_layers = pre_mix_g.shape[0]
    xs, pos, tgt = x[0], positions[0], loss_target[0]

    payload = [w[n] if n == "dn_conv_w" else w[n].astype(bf16) for n in _BIG]
    gathered = _exchange("allgather_weights", payload, ["gather2"] * len(_BIG))
    full = {n: _unshard(n, g) for n, g in zip(_BIG, gathered)}
    layers = []
    for l in range(n_layers):
        wl = {n: (full[n][l] if n in _BIG else w[n][l]) for n in _WEIGHTS}
        layers.append(_layer_params(wl))
    cst = _consts(pos)

    h, res = xs, []
    for l in range(n_layers):
        h, r = _layer_fwd(h, layers[l], cst)
        res.append(r)
    dy, sq = _loss(h, tgt)
    loss = lax.psum(0.5 / D_MODEL * jnp.sum(sq), ("x", "y", "c"))
    grads = [None] * n_layers
    for l in reversed(range(n_layers)):
        dy, g = _layer_bwd(dy, res[l], layers[l], cst)
        grads[l] = _layer_grads_ref_layout(g)
    grad_x = dy[None]
    gfull = {n: jnp.stack([grads[l][n] for l in range(n_layers)]) for n in _WEIGHTS}

    nb = len(_BIG)
    send = [_shard_major(n, gfull[n]).astype(bf16) for n in _BIG]
    stage_a = _exchange("grads_to_sibling", send + [_pack_small(gfull)], ["to_sibling"] * nb + ["gather"])
    core = lax.axis_index("c")
    summed = []
    for n, mine, theirs in zip(_BIG, send, stage_a[:nb]):
        own = lax.dynamic_index_in_dim(mine.reshape((N_CHIP, 2) + mine.shape[1:]), core, axis=1, keepdims=False)
        cols = own.shape[-1]
        rows = own.size // cols
        (s,) = _tile_fwd("add_" + n, _fn_add, rows, 512, [(own.reshape(rows, cols), cols, 0), (theirs.reshape(rows, cols), cols, 0)],
                         [], [(cols, bf16)])
        summed.append(s.reshape(own.shape))
    landed = _exchange("grads_to_chips", summed, ["to_chips"] * nb)

    out_g, out_d, out_m, out_v = {}, {}, {}, {}
    for n, land in zip(_BIG, landed):
        shp = w[n].shape
        flat = lambda a: a.reshape(-1, shp[-1])
        R_ = flat(w[n]).shape[0]
        res4 = _adamw("adamw_" + n, land.reshape(N_CHIP, R_, shp[-1]), flat(w[n]), flat(m[n]), flat(v[n]), tr=256)
        out_g[n], out_d[n], out_m[n], out_v[n] = [a.reshape(shp) for a in res4]
    small = _adamw("adamw_small", stage_a[nb], _pack_small(w), _pack_small(m), _pack_small(v), tr=_SMALL_PACK_ROWS)
    for dst, a in zip((out_g, out_d, out_m, out_v), small):
        dst.update(_unpack_small(a, w))
    return (loss, grad_x, *[out_g[n] for n in _WEIGHTS], *[out_d[n] for n in _WEIGHTS],
            *[out_m[n] for n in _WEIGHTS], *[out_v[n] for n in _WEIGHTS])
```

```python
import math

import numpy as np
import jax
import jax.numpy as jnp
from jax import lax
from jax.experimental import pallas as pl
from jax.experimental.pallas import tpu as pltpu

f32 = jnp.float32
bf16 = jnp.bfloat16
HIGHEST = lax.Precision.HIGHEST

N_DEV = 8
D_MODEL = 1024
DN_HEADS = 8
DN_DK = 128
DN_CHUNK = 64
DN_CONV = 4
SW_Q_HEADS = 16
SW_KV_HEADS = 2
SW_HEAD_DIM = 64
SW_BLOCK = 128
ROPE_THETA = 500000.0
ROT_DIM = SW_HEAD_DIM // 4
D_FF = 4 * D_MODEL
EPS = 1e-6
LANES = 128
CONV_HALO = 8
NEG_BIG = -1e30

ADAM_LR = 0.001
ADAM_B1 = 0.9
ADAM_B2 = 0.999
ADAM_EPS = 1e-08
ADAM_WD = 0.01
ADAM_STEP = 10

PROJ_W = 7680
CB_Q, CB_K, CB_V, CB_Z, CB_SWQ, CB_GA, CB_GB = 0, 1, 2, 3, 4, 5, 6
CB_SWK, CB_SWV, CB_BA = 56, 57, 58

NN = ((1,), (0,))
NT = ((1,), (1,))
TN = ((0,), (0,))


def _mm(a, b, dims, hi=False):
    if hi:
        return lax.dot_general(a.astype(f32), b.astype(f32), (dims, ((), ())), precision=HIGHEST,
                               preferred_element_type=f32)
    return lax.dot_general(a.astype(bf16), b.astype(bf16), (dims, ((), ())), preferred_element_type=f32)


def _matmul(name, a, b, form, out_dtype, tm=512, tn=512, tk=1024, extra=(), epilogue=None):
    if form == "nn":
        (M, K), (_, N) = a.shape, b.shape
    elif form == "nt":
        (M, K), (N, _) = a.shape, b.shape
    else:
        (K, M), (_, N) = a.shape, b.shape
    tm, tn, tk = min(tm, M), min(tn, N), min(tk, K)
    assert M % tm == 0 and N % tn == 0 and K % tk == 0, (name, M, N, K, tm, tn, tk)
    nk = K // tk
    dims = {"nn": NN, "nt": NT, "tn": TN}[form]
    out_dtypes = [out_dtype] if epilogue is None else list(out_dtype)
    ne, no = len(extra), len(out_dtypes)

    def body(a_ref, b_ref, *rest):
        e_refs, o_refs, acc_ref = rest[:ne], rest[ne:ne + no], rest[ne + no]

        def finish(acc):
            vals = [acc] if epilogue is None else epilogue(acc, [e[...] for e in e_refs])
            for o, val in zip(o_refs, vals):
                o[...] = val.astype(o.dtype)

        part = lax.dot_general(a_ref[...], b_ref[...], (dims, ((), ())), preferred_element_type=f32)
        if nk == 1:
            finish(part)
        else:
            k = pl.program_id(2)

            @pl.when(k == 0)
            def _():
                acc_ref[...] = part

            @pl.when(k > 0)
            def _():
                acc_ref[...] += part

            @pl.when(k == nk - 1)
            def _():
                finish(acc_ref[...])

    if form == "tn":
        a_spec = pl.BlockSpec((tk, tm), lambda i, j, k: (k, i))
    else:
        a_spec = pl.BlockSpec((tm, tk), lambda i, j, k: (i, k))
    if form == "nt":
        b_spec = pl.BlockSpec((tn, tk), lambda i, j, k: (j, k))
    else:
        b_spec = pl.BlockSpec((tk, tn), lambda i, j, k: (k, j))
    tile = pl.BlockSpec((tm, tn), lambda i, j, k: (i, j))
    res = pl.pallas_call(
        body, name=name,
        grid=(M // tm, N // tn, nk),
        in_specs=[a_spec, b_spec] + [tile] * ne,
        out_specs=[tile] * no,
        out_shape=[jax.ShapeDtypeStruct((M, N), dt) for dt in out_dtypes],
        scratch_shapes=[pltpu.VMEM((tm, tn) if nk > 1 else (8, 128), f32)],
        compiler_params=pltpu.CompilerParams(dimension_semantics=("parallel", "parallel", "arbitrary")),
    )(a, b, *extra)
    return res[0] if epilogue is None else list(res)


def _tile_specs(ins, halo_ids, params, TM, HR, row_of):
    specs = [pl.BlockSpec((TM, w), lambda i, cb=cb: (row_of(i), cb)) for (_, w, cb) in ins]
    for h in halo_ids:
        _, w, cb = ins[h]
        specs.append(pl.BlockSpec((HR, w), lambda i, cb=cb: (jnp.maximum(row_of(i) * (TM // HR) - 1, 0), cb)))
    for p in params:
        specs.append(pl.BlockSpec(p.shape, lambda i, nd=p.ndim: (0,) * nd))
    return specs


def _tile_fwd(name, fn, T, TM, ins, params, outs, halo_ids=(), HR=CONV_HALO):
    TM = min(TM, T)
    n = T // TM
    ni, nh, npar = len(ins), len(halo_ids), len(params)

    def body(*refs):
        in_v = [r[...] for r in refs[:ni]]
        halo_v = [r[...] for r in refs[ni:ni + nh]]
        par_v = [r[...] for r in refs[ni + nh:ni + nh + npar]]
        o_refs = refs[ni + nh + npar:]
        first = pl.program_id(0) == 0
        vals = fn(first, in_v, halo_v, par_v)
        for o, val in zip(o_refs, vals):
            o[...] = val.astype(o.dtype)

    res = pl.pallas_call(
        body, name=name, grid=(n,),
        in_specs=_tile_specs(ins, halo_ids, params, TM, HR, lambda i: i),
        out_specs=[pl.BlockSpec((TM, w), lambda i: (i, 0)) for (w, _) in outs],
        out_shape=[jax.ShapeDtypeStruct((T, w), dt) for (w, dt) in outs],
        compiler_params=pltpu.CompilerParams(dimension_semantics=("arbitrary",)),
    )(*[a for (a, _, _) in ins], *[ins[h][0] for h in halo_ids], *params)
    return list(res)


def _tile_bwd(name, fn, T, TM, ins, params, cts, din, dpar, halo_ids=(), HR=CONV_HALO):
    TM = min(TM, T)
    n = T // TM
    ni, nh, npar, nc = len(ins), len(halo_ids), len(params), len(cts)
    din_ids = [j for (j, _) in din]
    dh_ids = [h for h in halo_ids if h in din_ids]
    nd, ndp, ndh = len(din), len(dpar), len(dh_ids)

    def body(*refs):
        in_v = [r[...] for r in refs[:ni]]
        halo_v = [r[...] for r in refs[ni:ni + nh]]
        par_v = [r[...] for r in refs[ni + nh:ni + nh + npar]]
        ct_v = [r[...].astype(f32) for r in refs[ni + nh + npar:ni + nh + npar + nc]]
        o_refs = refs[ni + nh + npar + nc:ni + nh + npar + nc + nd + ndp]
        carry_refs = refs[ni + nh + npar + nc + nd + ndp:]
        i = pl.program_id(0)
        first = i == n - 1

        def g(d_in, d_halo, d_par):
            full_in = list(in_v)
            for j, val in zip(din_ids, d_in):
                full_in[j] = val
            full_halo = list(halo_v)
            for h, val in zip(dh_ids, d_halo):
                full_halo[list(halo_ids).index(h)] = val
            full_par = list(par_v)
            for j, val in zip(dpar, d_par):
                full_par[j] = val
            return tuple(fn(first, full_in, full_halo, full_par))

        prim = ([in_v[j].astype(f32) for j in din_ids],
                [halo_v[list(halo_ids).index(h)].astype(f32) for h in dh_ids],
                [par_v[j] for j in dpar])
        _, vjp = jax.vjp(g, *prim)
        g_in, g_halo, g_par = vjp(tuple(ct_v))

        @pl.when(i == 0)
        def _():
            for c in carry_refs:
                c[...] = jnp.zeros_like(c)
            for o in o_refs[nd:]:
                o[...] = jnp.zeros_like(o)

        for slot, (j, _) in enumerate(din):
            val = g_in[slot]
            if j in dh_ids:
                c = carry_refs[dh_ids.index(j)]
                val = jnp.concatenate([val[:TM - HR], val[TM - HR:] + c[...]], axis=0) if TM > HR else val + c[...]
                c[...] = g_halo[dh_ids.index(j)]
            o_refs[slot][...] = val.astype(o_refs[slot].dtype)
        for slot in range(ndp):
            o_refs[nd + slot][...] += g_par[slot]

    rev = lambda i: n - 1 - i
    in_specs = _tile_specs(ins, halo_ids, params, TM, HR, rev)
    ct_specs = [pl.BlockSpec((TM, w), lambda i, cb=cb: (rev(i), cb)) for (_, w, cb) in cts]
    out_specs = [pl.BlockSpec((TM, ins[j][1]), lambda i: (rev(i), 0)) for j in din_ids]
    out_specs += [pl.BlockSpec(params[j].shape, lambda i, nd_=params[j].ndim: (0,) * nd_) for j in dpar]
    out_shape = [jax.ShapeDtypeStruct((T, ins[j][1]), dt) for (j, dt) in din]
    out_shape += [jax.ShapeDtypeStruct(params[j].shape, f32) for j in dpar]
    res = pl.pallas_call(
        body, name=name, grid=(n,),
        in_specs=in_specs + ct_specs,
        out_specs=out_specs,
        out_shape=out_shape,
        scratch_shapes=[pltpu.VMEM((HR, ins[h][1]), f32) for h in dh_ids],
        compiler_params=pltpu.CompilerParams(dimension_semantics=("arbitrary",)),
    )(*[a for (a, _, _) in ins], *[ins[h][0] for h in halo_ids], *params, *[a for (a, _, _) in cts])
    return list(res)


def _rms(x, g):
    return x * lax.rsqrt(jnp.mean(x * x, axis=-1, keepdims=True) + EPS) * g


def _fn_prenorm(first, ins, halos, params):
    (x,), (g,) = ins, params
    x = x.astype(f32)
    return [_rms(x, g), x]


def _fn_postmix(first, ins, halos, params):
    (x, mix), (g2, g3) = ins, params
    x1 = x + _rms(mix, g2)
    return [x1, _rms(x1, g3)]


def _fn_postmlp(first, ins, halos, params):
    (x1, ff), (g4,) = ins, params
    return [x1 + _rms(ff, g4)]


def _fn_rms_only(first, ins, halos, params):
    (ff,), (g4,) = ins, params
    return [_rms(ff, g4)]


def _fn_merge(first, ins, halos, params):
    ga, gb, ya, yb = ins
    return [jax.nn.sigmoid(ga) * ya + jax.nn.sigmoid(gb) * yb]


def _make_fn_conv(norm_scale):
    def fn(first, ins, halos, params):
        (x,), (xp,), (w,) = ins, halos, params
        TM = x.shape[0]
        xp = jnp.where(first, 0.0, xp)
        xe = jnp.concatenate([xp, x], axis=0)
        off = CONV_HALO - (DN_CONV - 1)
        y = xe[off:off + TM] * w[0:1]
        for j in range(1, DN_CONV):
            y = y + xe[off + j:off + j + TM] * w[j:j + 1]
        y = jax.nn.silu(y)
        if norm_scale is None:
            return [y]
        outs = []
        for h in range(DN_HEADS):
            yh = y[:, DN_DK * h:DN_DK * (h + 1)]
            outs.append(yh * lax.rsqrt(jnp.sum(yh * yh, axis=-1, keepdims=True) + EPS) * norm_scale)
        return [jnp.concatenate(outs, axis=-1)]
    return fn


def _fn_gates(first, ins, halos, params):
    (ba,), (avec, dvec) = ins, params
    lane = lax.broadcasted_iota(jnp.int32, ba.shape, 1)
    beta = jax.nn.sigmoid(ba)
    g = -jnp.exp(avec) * jax.nn.softplus(ba + dvec)
    return [jnp.where(lane < DN_HEADS, beta, jnp.where(lane < 2 * DN_HEADS, g, 0.0))]


def _fn_dnpost(first, ins, halos, params):
    (o, z), (ng,) = ins, params
    outs = []
    for h in range(DN_HEADS):
        sl = slice(DN_DK * h, DN_DK * (h + 1))
        outs.append(_rms(o[:, sl], ng) * jax.nn.silu(z[:, sl]))
    return [jnp.concatenate(outs, axis=-1)]


def _roll_lanes(x, shift):
    return pltpu.roll(x, shift, 1)


_lane_roll = jax.custom_vjp(_roll_lanes, nondiff_argnums=(1,))
_lane_roll.defvjp(lambda x, shift: (_roll_lanes(x, shift), None),
                  lambda shift, _, ct: (_roll_lanes(ct, LANES - shift),))


def _fn_swa(first, ins, halos, params):
    q, k, v, cos, sin = ins
    kp, vp, cosp, sinp = halos
    sinks, sel_a0, sel_b0, sel_a1, sel_b1 = params
    B = q.shape[0]
    half = ROT_DIM // 2
    in_head = jnp.bitwise_and(lax.broadcasted_iota(jnp.int32, (1, LANES), 1), SW_HEAD_DIM - 1)

    def rope(x, c, s):
        return (x * c + _lane_roll(x, LANES - half) * jnp.where(in_head < half, -s, 0.0)
                + _lane_roll(x, half) * jnp.where(in_head >= half, s, 0.0))

    kcat = jnp.concatenate([rope(kp, cosp, sinp), rope(k, cos, sin)], axis=0)
    vcat = jnp.concatenate([vp, v], axis=0)
    r = lax.broadcasted_iota(jnp.int32, (B, 2 * B), 0)
    c = lax.broadcasted_iota(jnp.int32, (B, 2 * B), 1)
    mask = (c > r) & (c <= r + B) & ((c >= B) | jnp.logical_not(first))
    group = SW_Q_HEADS // SW_KV_HEADS
    sels = ((sel_a0, sel_b0), (sel_a1, sel_b1))
    outs = []
    for hk in range(SW_KV_HEADS):
        heads = range(group)
        kx = [_mm(kcat, sels[hk][half], NN) for half in range(2)]
        vx = [_mm(vcat, sels[hk][half], NN) for half in range(2)]
        qp = [rope(q[:, LANES * j:LANES * (j + 1)], cos, sin) * (SW_HEAD_DIM ** -0.5)
              for j in range(hk * group // 2, (hk + 1) * group // 2)]
        s = [jnp.where(mask, _mm(qp[g // 2], kx[g % 2], NT), NEG_BIG) for g in heads]
        sink = [sinks[:, hk * group + g:hk * group + g + 1] for g in heads]
        m = [jnp.maximum(jnp.max(s[g], axis=-1, keepdims=True), sink[g]) for g in heads]
        p = [jnp.exp(s[g] - m[g]) for g in heads]
        inv = [1.0 / (jnp.sum(p[g], axis=-1, keepdims=True) + jnp.exp(sink[g] - m[g])) for g in heads]
        o = [_mm(p[g] * inv[g], vx[g % 2], NN) for g in heads]
        outs += [o[2 * j] + o[2 * j + 1] for j in range(group // 2)]
    return [jnp.concatenate(outs, axis=-1)]


@jax.custom_vjp
def _inv_unit_lower(Ls):
    C = Ls[0].shape[0]
    ii = lax.broadcasted_iota(jnp.int32, (C, C), 0)
    jj = lax.broadcasted_iota(jnp.int32, (C, C), 1)

    def off_mask(level):
        same_pair = jnp.right_shift(ii, level + 1) == jnp.right_shift(jj, level + 1)
        lower_left = (jnp.bitwise_and(jnp.right_shift(ii, level), 1) == 1) & (jnp.bitwise_and(jnp.right_shift(jj, level), 1) == 0)
        return same_pair & lower_left

    eye = (ii == jj).astype(f32)
    m0 = off_mask(0)
    Ts = [eye - jnp.where(m0, L, 0.0) for L in Ls]
    for level in range(1, int(math.log2(C))):
        mk = off_mask(level)
        left = [_mm(T_, jnp.where(mk, L, 0.0), NN) for T_, L in zip(Ts, Ls)]
        Ts = [T_ - _mm(a, T_, NN) for a, T_ in zip(left, Ts)]
    return tuple(Ts)


def _inv_fwd(Ls):
    Ts = _inv_unit_lower(Ls)
    return Ts, Ts


def _inv_bwd(Ts, dTs):
    left = [_mm(T_, dT, TN) for T_, dT in zip(Ts, dTs)]
    return (tuple(-_mm(a, T_, NT) for a, T_ in zip(left, Ts)),)


_inv_unit_lower.defvjp(_inv_fwd, _inv_bwd)


def _mm_01(a, b, dims):
    hi = b.astype(bf16)
    r1 = b - hi.astype(f32)
    mid = r1.astype(bf16)
    lo = (r1 - mid.astype(f32)).astype(bf16)
    a16 = a.astype(bf16)
    dot = lambda part: lax.dot_general(a16, part, (dims, ((), ())), preferred_element_type=f32)
    return dot(hi) + dot(mid) + dot(lo)


def _eye(n):
    return lax.broadcasted_iota(jnp.int32, (n, n), 0) == lax.broadcasted_iota(jnp.int32, (n, n), 1)


def _lower(n):
    return lax.broadcasted_iota(jnp.int32, (n, n), 0) >= lax.broadcasted_iota(jnp.int32, (n, n), 1)


@jax.custom_vjp
def _transpose(x):
    return _mm_01(_eye(x.shape[1]), x, NT)


_transpose.defvjp(lambda x: (_transpose(x), None), lambda _, ct: (_transpose(ct),))


@jax.custom_vjp
def _cumsum_rows(x):
    return _mm_01(_lower(x.shape[0]), x, NN)


_cumsum_rows.defvjp(lambda x: (_cumsum_rows(x), None), lambda _, ct: (_mm_01(_lower(ct.shape[0]), ct, TN),))


def _dn_chunk(q, k, v, gb, S):
    C = q.shape[0]
    H = range(DN_HEADS)
    ii = lax.broadcasted_iota(jnp.int32, (C, C), 0)
    jj = lax.broadcasted_iota(jnp.int32, (C, C), 1)
    causal, strict = ii >= jj, ii > jj
    gc_all = _cumsum_rows(gb)
    gc_t = _transpose(gc_all)
    sl = [slice(DN_DK * h, DN_DK * (h + 1)) for h in H]
    qs, ks, vs = [q[:, s] for s in sl], [k[:, s] for s in sl], [v[:, s] for s in sl]
    beta = [gb[:, h:h + 1] for h in H]
    gcol = [gc_all[:, DN_HEADS + h:DN_HEADS + h + 1] for h in H]
    grow = [gc_t[DN_HEADS + h:DN_HEADS + h + 1, :] for h in H]
    decay = [jnp.where(causal, jnp.exp(jnp.where(causal, gcol[h] - grow[h], 0.0)), 0.0) for h in H]
    kb = [ks[h] * beta[h] for h in H]
    kk = [_mm(kb[h], ks[h], NT) for h in H]
    qk = [_mm(qs[h], ks[h], NT) for h in H]
    tinv = _inv_unit_lower(tuple(jnp.where(strict, kk[h] * decay[h], 0.0) for h in H))
    eg = [jnp.exp(gcol[h]) for h in H]
    u = [_mm(tinv[h], vs[h] * beta[h], NN) for h in H]
    w = [_mm(tinv[h], kb[h] * eg[h], NN) for h in H]
    gl = [gcol[h][C - 1:C, :] for h in H]
    ws = [_mm(w[h], S[h], NN) for h in H]
    qS = [_mm(qs[h] * eg[h], S[h], NN) for h in H]
    v_new = [u[h] - ws[h] for h in H]
    av = [_mm(qk[h] * decay[h], v_new[h], NN) for h in H]
    kv = [_mm(ks[h] * jnp.exp(gl[h] - gcol[h]), v_new[h], TN) for h in H]
    o = jnp.concatenate([qS[h] + av[h] for h in H], axis=-1)
    return o, tuple(S[h] * jnp.exp(gl[h]) + kv[h] for h in H)


def _delta_fwd(qn, kn, vv, gb):
    T = qn.shape[0]
    C = DN_CHUNK
    n = T // C
    W = DN_HEADS * DN_DK

    def body(q_ref, k_ref, v_ref, gb_ref, o_ref, hist_ref, s_ref):
        @pl.when(pl.program_id(0) == 0)
        def _():
            s_ref[...] = jnp.zeros_like(s_ref)

        S = tuple(s_ref[h] for h in range(DN_HEADS))
        for h in range(DN_HEADS):
            hist_ref[0, h] = S[h]
        o, s_new = _dn_chunk(q_ref[...], k_ref[...], v_ref[...], gb_ref[...], S)
        o_ref[...] = o
        for h in range(DN_HEADS):
            s_ref[h] = s_new[h]

    row = pl.BlockSpec((C, W), lambda i: (i, 0))
    return pl.pallas_call(
        body, name="delta_fwd", grid=(n,),
        in_specs=[row, row, row, pl.BlockSpec((C, LANES), lambda i: (i, 0))],
        out_specs=[row, pl.BlockSpec((1, DN_HEADS, DN_DK, DN_DK), lambda i: (i, 0, 0, 0))],
        out_shape=[jax.ShapeDtypeStruct((T, W), f32), jax.ShapeDtypeStruct((n, DN_HEADS, DN_DK, DN_DK), f32)],
        scratch_shapes=[pltpu.VMEM((DN_HEADS, DN_DK, DN_DK), f32)],
        compiler_params=pltpu.CompilerParams(dimension_semantics=("arbitrary",)),
    )(qn, kn, vv, gb)


def _delta_bwd(qn, kn, vv, gb, hist, do):
    T = qn.shape[0]
    C = DN_CHUNK
    n = T // C
    W = DN_HEADS * DN_DK

    def body(q_ref, k_ref, v_ref, gb_ref, hist_ref, do_ref, dq_ref, dk_ref, dv_ref, dgb_ref, ds_ref):
        @pl.when(pl.program_id(0) == 0)
        def _():
            ds_ref[...] = jnp.zeros_like(ds_ref)

        S = tuple(hist_ref[0, h] for h in range(DN_HEADS))
        _, vjp = jax.vjp(_dn_chunk, q_ref[...], k_ref[...], v_ref[...], gb_ref[...], S)
        dS = tuple(ds_ref[h] for h in range(DN_HEADS))
        dq, dk, dv, dgb, dS_in = vjp((do_ref[...], dS))
        dq_ref[...] = dq
        dk_ref[...] = dk
        dv_ref[...] = dv
        dgb_ref[...] = dgb
        for h in range(DN_HEADS):
            ds_ref[h] = dS_in[h]

    row = pl.BlockSpec((C, W), lambda i: (n - 1 - i, 0))
    small = pl.BlockSpec((C, LANES), lambda i: (n - 1 - i, 0))
    return pl.pallas_call(
        body, name="delta_bwd", grid=(n,),
        in_specs=[row, row, row, small, pl.BlockSpec((1, DN_HEADS, DN_DK, DN_DK), lambda i: (n - 1 - i, 0, 0, 0)), row],
        out_specs=[row, row, row, small],
        out_shape=[jax.ShapeDtypeStruct((T, W), f32)] * 3 + [jax.ShapeDtypeStruct((T, LANES), f32)],
        scratch_shapes=[pltpu.VMEM((DN_HEADS, DN_DK, DN_DK), f32)],
        compiler_params=pltpu.CompilerParams(dimension_semantics=("arbitrary",)),
    )(qn, kn, vv, gb, hist, do)


TM_ROW = 256
W1 = D_MODEL


def _first_only(fn):
    return lambda *a: fn(*a)[:1]


def _swa_args(proj, cst):
    ins = [(proj, W1, CB_SWQ), (proj, LANES, CB_SWK), (proj, LANES, CB_SWV), (cst["cos"], LANES, 0), (cst["sin"], LANES, 0)]
    return ins, (1, 2, 3, 4)


def _layer_fwd(x, p, cst):
    T = x.shape[0]
    r = {"x": x}
    (h,) = _tile_fwd("prenorm", _first_only(_fn_prenorm), T, TM_ROW, [(x, W1, 0)], [p["g1"]], [(W1, bf16)])
    proj = _matmul("proj", h, p["w_in"], "nn", f32, tm=1024, tn=1536)
    conv = lambda nm, cb, scale: _tile_fwd(nm, _make_fn_conv(scale), T, TM_ROW, [(proj, W1, cb)],
                                            [p["conv_w"][:, W1 * cb:W1 * (cb + 1)]], [(W1, f32)], halo_ids=(0,))[0]
    qn = conv("conv_q", CB_Q, DN_DK ** -0.5)
    kn = conv("conv_k", CB_K, 1.0)
    vv = conv("conv_v", CB_V, None)
    (gbt,) = _tile_fwd("gates", _fn_gates, T, TM_ROW, [(proj, LANES, CB_BA)], [p["avec"], p["dvec"]], [(LANES, f32)])
    o, hist = _delta_fwd(qn, kn, vv, gbt)
    (dn_out,) = _tile_fwd("dnpost", _fn_dnpost, T, TM_ROW, [(o, W1, 0), (proj, W1, CB_Z)], [p["ng"]], [(W1, bf16)])
    sw_ins, sw_halo = _swa_args(proj, cst)
    sw_par = [p["sinks"], cst["sel_a0"], cst["sel_b0"], cst["sel_a1"], cst["sel_b1"]]
    (sw_out,) = _tile_fwd("swa", _fn_swa, T, SW_BLOCK, sw_ins, sw_par, [(W1, bf16)], halo_ids=sw_halo, HR=SW_BLOCK)
    y_a = _matmul("up_dn", dn_out, p["w_up_dn"], "nn", f32, tm=1024, tn=1024)
    y_b = _matmul("up_sw", sw_out, p["w_up_sw"], "nn", f32, tm=1024, tn=1024)
    (gated,) = _tile_fwd("merge", _fn_merge, T, TM_ROW,
                         [(proj, W1, CB_GA), (proj, W1, CB_GB), (y_a, W1, 0), (y_b, W1, 0)], [], [(W1, bf16)])
    mix = _matmul("w_o", gated, p["w_o"], "nn", f32, tm=1024, tn=1024)
    x1, h2 = _tile_fwd("postmix", _fn_postmix, T, TM_ROW, [(x, W1, 0), (mix, W1, 0)], [p["g2"], p["g3"]],
                       [(W1, f32), (W1, bf16)])
    ffh, act = _matmul("ff1", h2, p["w_ff1"], "nn", [f32, bf16], tm=1024, tn=1024,
                       epilogue=lambda acc, ex: [acc, jnp.square(jnp.maximum(acc, 0.0))])
    ff = _matmul("ff2", act, p["w_ff2"], "nn", f32, tm=1024, tn=1024)
    (x2,) = _tile_fwd("postmlp", _fn_postmlp, T, TM_ROW, [(x1, W1, 0), (ff, W1, 0)], [p["g4"]], [(W1, f32)])
    r.update(h=h, proj=proj, qn=qn, kn=kn, vv=vv, gbt=gbt, o=o, hist=hist, dn_out=dn_out, sw_out=sw_out,
             y_a=y_a, y_b=y_b, gated=gated, mix=mix, h2=h2, ffh=ffh, act=act, ff=ff)
    return x2, r


def _layer_bwd(dx2, r, p, cst):
    T = dx2.shape[0]
    x, proj = r["x"], r["proj"]
    g = {}
    dff, g["g4"] = _tile_bwd("postmlp_b", _fn_rms_only, T, TM_ROW, [(r["ff"], W1, 0)], [p["g4"]], [(dx2, W1, 0)],
                             [(0, bf16)], [0])
    (dffh,) = _matmul("ff2_dx", dff, p["w_ff2"], "nt", [bf16], tm=1024, tn=1024, extra=[r["ffh"]],
                      epilogue=lambda acc, ex: [acc * (2.0 * jnp.maximum(ex[0], 0.0))])
    g["w_ff2"] = _matmul("ff2_dw", r["act"], dff, "tn", f32, tm=1024, tn=1024)
    dh2 = _matmul("ff1_dx", dffh, p["w_ff1"], "nt", f32, tm=1024, tn=1024)
    g["w_ff1"] = _matmul("ff1_dw", r["h2"], dffh, "tn", f32, tm=1024, tn=1024)
    dx1, dmix, g["g2"], g["g3"] = _tile_bwd("postmix_b", _fn_postmix, T, TM_ROW, [(x, W1, 0), (r["mix"], W1, 0)],
                                            [p["g2"], p["g3"]], [(dx2, W1, 0), (dh2, W1, 0)], [(0, f32), (1, bf16)], [0, 1])
    dgated = _matmul("w_o_dx", dmix, p["w_o"], "nt", f32, tm=1024, tn=1024)
    g["w_o"] = _matmul("w_o_dw", r["gated"], dmix, "tn", f32, tm=1024, tn=1024)
    dga, dgb, dya, dyb = _tile_bwd("merge_b", _fn_merge, T, TM_ROW,
                                   [(proj, W1, CB_GA), (proj, W1, CB_GB), (r["y_a"], W1, 0), (r["y_b"], W1, 0)], [],
                                   [(dgated, W1, 0)], [(0, bf16), (1, bf16), (2, bf16), (3, bf16)], [])
    d_dn = _matmul("up_dn_dx", dya, p["w_up_dn"], "nt", f32, tm=1024, tn=1024)
    g["w_up_dn"] = _matmul("up_dn_dw", r["dn_out"], dya, "tn", f32, tm=1024, tn=1024)
    d_sw = _matmul("up_sw_dx", dyb, p["w_up_sw"], "nt", f32, tm=1024, tn=1024)
    g["w_up_sw"] = _matmul("up_sw_dw", r["sw_out"], dyb, "tn", f32, tm=1024, tn=1024)
    do, dz, g["ng"] = _tile_bwd("dnpost_b", _fn_dnpost, T, TM_ROW, [(r["o"], W1, 0), (proj, W1, CB_Z)], [p["ng"]],
                                [(d_dn, W1, 0)], [(0, f32), (1, bf16)], [0])
    dqn, dkn, dvv, dgbt = _delta_bwd(r["qn"], r["kn"], r["vv"], r["gbt"], r["hist"], do)
    conv_b = lambda nm, cb, scale, ct: _tile_bwd(nm, _make_fn_conv(scale), T, TM_ROW, [(proj, W1, cb)],
                                                 [p["conv_w"][:, W1 * cb:W1 * (cb + 1)]], [(ct, W1, 0)], [(0, bf16)], [0],
                                                 halo_ids=(0,))
    dq_in, dcw_q = conv_b("conv_q_b", CB_Q, DN_DK ** -0.5, dqn)
    dk_in, dcw_k = conv_b("conv_k_b", CB_K, 1.0, dkn)
    dv_in, dcw_v = conv_b("conv_v_b", CB_V, None, dvv)
    g["conv_w"] = jnp.concatenate([dcw_q, dcw_k, dcw_v], axis=-1)
    dba, g["avec"], g["dvec"] = _tile_bwd("gates_b", _fn_gates, T, TM_ROW, [(proj, LANES, CB_BA)], [p["avec"], p["dvec"]],
                                          [(dgbt, LANES, 0)], [(0, bf16)], [0, 1])
    sw_ins, sw_halo = _swa_args(proj, cst)
    sw_par = [p["sinks"], cst["sel_a0"], cst["sel_b0"], cst["sel_a1"], cst["sel_b1"]]
    dswq, dswk, dswv, g["sinks"] = _tile_bwd("swa_b", _fn_swa, T, SW_BLOCK, sw_ins, sw_par, [(d_sw, W1, 0)],
                                             [(0, bf16), (1, bf16), (2, bf16)], [0], halo_ids=sw_halo, HR=SW_BLOCK)
    dproj = jnp.concatenate([dq_in, dk_in, dv_in, dz, dswq, dga, dgb, dswk, dswv, dba, jnp.zeros((T, LANES), bf16)], axis=-1)
    dh = _matmul("proj_dx", dproj, p["w_in"], "nt", f32, tm=1024, tn=1024, tk=1536)
    g["w_in"] = _matmul("proj_dw", r["h"], dproj, "tn", f32, tm=1024, tn=1536)
    dx, g["g1"] = _tile_bwd("prenorm_b", _fn_prenorm, T, TM_ROW, [(x, W1, 0)], [p["g1"]], [(dh, W1, 0), (dx1, W1, 0)],
                            [(0, f32)], [0])
    return dx, g


_OFF_BA, _OFF_SWQ, _OFF_SWK, _OFF_GA, _D_IN = 4096, 4112, 5136, 5392, 7440


def _proj_cols(w):
    pad = lambda n: jnp.zeros(w.shape[:-1] + (n,), w.dtype)
    return jnp.concatenate([w[..., :_OFF_BA], w[..., _OFF_SWQ:_OFF_SWK], w[..., _OFF_GA:_D_IN],
                            w[..., _OFF_SWK:_OFF_GA], w[..., _OFF_BA:_OFF_SWQ], pad(PROJ_W - _D_IN)], axis=-1)


def _proj_cols_inv(w):
    n_ba = _OFF_SWQ - _OFF_BA
    return jnp.concatenate([w[..., :4096], w[..., 7424:7424 + n_ba], w[..., 4096:5120], w[..., 7168:7424],
                            w[..., 5120:7168]], axis=-1)


def _lane_pad(v, at):
    return jnp.pad(v.astype(f32), (at, LANES - at - v.shape[0])).reshape(1, LANES)


def _layer_params(w):
    row = lambda v: v.reshape(1, -1).astype(f32)
    return dict(
        g1=row(w["pre_mix_g"]), g2=row(w["post_mix_g"]), g3=row(w["pre_mlp_g"]), g4=row(w["post_mlp_g"]),
        w_in=_proj_cols(w["w_in"]).astype(bf16), conv_w=w["dn_conv_w"].astype(f32),
        avec=_lane_pad(w["dn_a_log"], DN_HEADS), dvec=_lane_pad(w["dn_dt_bias"], DN_HEADS),
        ng=row(w["dn_norm_g"]), sinks=_lane_pad(w["sw_sinks"], 0),
        w_up_dn=w["w_up_dn"].astype(bf16), w_up_sw=w["w_up_sw"].astype(bf16), w_o=w["w_o"].astype(bf16),
        w_ff1=w["w_ff1"].astype(bf16), w_ff2=w["w_ff2"].astype(bf16))


def _layer_grads_ref_layout(g):
    return dict(
        pre_mix_g=g["g1"][0], post_mix_g=g["g2"][0], pre_mlp_g=g["g3"][0], post_mlp_g=g["g4"][0],
        w_in=_proj_cols_inv(g["w_in"]), dn_conv_w=g["conv_w"],
        dn_a_log=g["avec"][0, DN_HEADS:2 * DN_HEADS], dn_dt_bias=g["dvec"][0, DN_HEADS:2 * DN_HEADS],
        dn_norm_g=g["ng"][0], sw_sinks=g["sinks"][0, :SW_Q_HEADS],
        w_up_dn=g["w_up_dn"], w_up_sw=g["w_up_sw"], w_o=g["w_o"], w_ff1=g["w_ff1"], w_ff2=g["w_ff2"])


def _consts(positions):
    T = positions.shape[0]
    half = ROT_DIM // 2
    inv_freq = ROPE_THETA ** (-jnp.arange(half, dtype=f32) * (2.0 / ROT_DIM))
    ang = positions.astype(f32)[:, None] * inv_freq
    cos8, sin8 = jnp.cos(ang), jnp.sin(ang)
    rest = SW_HEAD_DIM - ROT_DIM
    c64 = jnp.concatenate([cos8, cos8, jnp.ones((T, rest), f32)], axis=-1)
    s64 = jnp.concatenate([sin8, sin8, jnp.zeros((T, rest), f32)], axis=-1)
    sel = np.zeros((2, 2, LANES, LANES), np.float32)
    for hk in range(SW_KV_HEADS):
        for d in range(SW_HEAD_DIM):
            sel[hk, 0, SW_HEAD_DIM * hk + d, d] = 1.0
            sel[hk, 1, SW_HEAD_DIM * hk + d, SW_HEAD_DIM + d] = 1.0
    return dict(cos=jnp.concatenate([c64, c64], axis=-1), sin=jnp.concatenate([s64, s64], axis=-1),
                sel_a0=jnp.asarray(sel[0, 0]), sel_b0=jnp.asarray(sel[0, 1]),
                sel_a1=jnp.asarray(sel[1, 0]), sel_b1=jnp.asarray(sel[1, 1]))


def _loss(y, tgt):
    T, W = y.shape
    TM = min(TM_ROW, T)
    n = T // TM

    def body(y_ref, t_ref, dy_ref, acc_ref):
        @pl.when(pl.program_id(0) == 0)
        def _():
            acc_ref[...] = jnp.zeros_like(acc_ref)

        d = y_ref[...] - t_ref[...]
        dy_ref[...] = d * (1.0 / W)
        acc_ref[...] += jnp.sum(d * d, axis=0, keepdims=True)

    row = pl.BlockSpec((TM, W), lambda i: (i, 0))
    return pl.pallas_call(
        body, name="loss", grid=(n,), in_specs=[row, row],
        out_specs=[row, pl.BlockSpec((1, W), lambda i: (0, 0))],
        out_shape=[jax.ShapeDtypeStruct((T, W), f32), jax.ShapeDtypeStruct((1, W), f32)],
        compiler_params=pltpu.CompilerParams(dimension_semantics=("arbitrary",)),
    )(y, tgt)


N_CHIP = 4
N_SEM = N_DEV - 1


def _exchange(name, arrs, modes):
    n = len(arrs)
    lead = {"gather": N_DEV, "gather2": N_DEV, "to_sibling": N_CHIP, "to_chips": N_CHIP}
    outs_shape = [jax.ShapeDtypeStruct((lead[md],) + (a.shape if md.startswith("gather") else a.shape[1:]), a.dtype)
                  for a, md in zip(arrs, modes)]

    def body(*refs):
        ins, outs = refs[:n], refs[n:2 * n]
        send_sems, recv_sems, loc_sems = refs[2 * n:]
        x, y, c = lax.axis_index("x"), lax.axis_index("y"), lax.axis_index("c")
        me, chip, sib = 4 * x + 2 * y + c, 2 * x + y, (x, y, 1 - c)
        flips = [(1 - x, y), (x, 1 - y), (1 - x, 1 - y)]

        def rcopy(a, k, src, dst, dev):
            return pltpu.make_async_remote_copy(src_ref=src, dst_ref=dst, send_sem=send_sems.at[a, k],
                                                recv_sem=recv_sems.at[a, k], device_id=dev,
                                                device_id_type=pl.DeviceIdType.MESH)

        sends, recvs, local, passes = [], [], [], []
        for a, md in enumerate(modes):
            src_all, out = ins[a], outs[a]
            if md in ("gather", "gather2"):
                local.append(pltpu.make_async_copy(src_all, out.at[me], loc_sems.at[a]))
            if md == "to_chips":
                local.append(pltpu.make_async_copy(src_all.at[chip], out.at[chip], loc_sems.at[a]))
            if md == "gather":
                for k in range(1, N_DEV):
                    px = 1 - x if (k >> 2) & 1 else x
                    py = 1 - y if (k >> 1) & 1 else y
                    pc = 1 - c if k & 1 else c
                    sends.append(rcopy(a, k - 1, src_all, out.at[me], (px, py, pc)))
                    recvs.append(rcopy(a, k - 1, src_all, out.at[4 * px + 2 * py + pc], (px, py, pc)))
            elif md == "gather2":
                sends.append(rcopy(a, 0, src_all, out.at[me], sib))
                recvs.append(rcopy(a, 0, src_all, out.at[4 * x + 2 * y + 1 - c], sib))
                for j, (px, py) in enumerate(flips):
                    sends.append(rcopy(a, 1 + j, src_all, out.at[me], (px, py, c)))
                    theirs = out.at[4 * px + 2 * py + c]
                    arrive = rcopy(a, 1 + j, src_all, theirs, (px, py, c))
                    passes.append((arrive, rcopy(a, 4 + j, theirs, theirs, sib)))
                    recvs.append(rcopy(a, 4 + j, src_all, out.at[4 * px + 2 * py + 1 - c], sib))
            elif md == "to_sibling":
                for j in range(N_CHIP):
                    sends.append(rcopy(a, j, src_all.at[2 * j + 1 - c], out.at[j], sib))
                    recvs.append(rcopy(a, j, src_all.at[2 * j + c], out.at[j], sib))
            elif md == "to_chips":
                for j, (px, py) in enumerate(flips):
                    sends.append(rcopy(a, j, src_all.at[2 * px + py], out.at[chip], (px, py, c)))
                    recvs.append(rcopy(a, j, src_all.at[chip], out.at[2 * px + py], (px, py, c)))
        for cp in local + sends:
            cp.start()
        for arrive, onward in passes:
            arrive.wait_recv()
            onward.start()
        for cp in recvs:
            cp.wait_recv()
        for cp in sends:
            cp.wait_send()
        for _, onward in passes:
            onward.wait_send()
        for cp in local:
            cp.wait()

    hbm = pl.BlockSpec(memory_space=pltpu.HBM)
    res = pl.pallas_call(
        body, name=name,
        in_specs=[hbm] * n, out_specs=[hbm] * n, out_shape=outs_shape,
        scratch_shapes=[pltpu.SemaphoreType.DMA((n, N_SEM)), pltpu.SemaphoreType.DMA((n, N_SEM)),
                        pltpu.SemaphoreType.DMA((n,))],
    )(*arrs)
    return list(res)


def _add_bf16(name, a, b, tr):
    L_, R_, C_ = a.shape
    tr = min(tr, R_)
    assert R_ % tr == 0, (name, R_, tr)

    def body(a_ref, b_ref, o_ref):
        o_ref[...] = (a_ref[...].astype(f32) + b_ref[...].astype(f32)).astype(bf16)

    blk = pl.BlockSpec((1, tr, C_), lambda l, i: (l, i, 0))
    return pl.pallas_call(
        body, name=name, grid=(L_, R_ // tr), in_specs=[blk, blk], out_specs=blk,
        out_shape=jax.ShapeDtypeStruct(a.shape, bf16),
        compiler_params=pltpu.CompilerParams(dimension_semantics=("arbitrary", "arbitrary")),
    )(a, b)


def _adamw(name, land, w, m, v, tr):
    L_, R_, C_ = w.shape
    n_slots = land.shape[0]
    tr = min(tr, R_)
    assert R_ % tr == 0, (name, R_, tr)
    c1 = 1.0 - ADAM_B1 ** ADAM_STEP
    c2 = 1.0 - ADAM_B2 ** ADAM_STEP

    def body(l_ref, w_ref, m_ref, v_ref, g_ref, d_ref, mo_ref, vo_ref):
        g = l_ref[0].astype(f32)
        for s in range(1, n_slots):
            g = g + l_ref[s].astype(f32)
        m_new = ADAM_B1 * m_ref[...] + (1.0 - ADAM_B1) * g
        v_new = ADAM_B2 * v_ref[...] + (1.0 - ADAM_B2) * jnp.square(g)
        m_hat = m_new / c1
        v_hat = v_new / c2
        g_ref[...] = g
        d_ref[...] = -ADAM_LR * (m_hat / (jnp.sqrt(v_hat) + ADAM_EPS) + ADAM_WD * w_ref[...])
        mo_ref[...] = m_new
        vo_ref[...] = v_new

    row = pl.BlockSpec((1, tr, C_), lambda l, i: (l, i, 0))
    return pl.pallas_call(
        body, name=name, grid=(L_, R_ // tr),
        in_specs=[pl.BlockSpec((n_slots, 1, tr, C_), lambda l, i: (0, l, i, 0)), row, row, row],
        out_specs=[row] * 4, out_shape=[jax.ShapeDtypeStruct((L_, R_, C_), f32)] * 4,
        compiler_params=pltpu.CompilerParams(dimension_semantics=("arbitrary", "arbitrary")),
    )(land, w, m, v)


_BIG = ("w_in", "dn_conv_w", "w_up_dn", "w_up_sw", "w_o", "w_ff1", "w_ff2")
_COL_SHARDED = ("w_in", "dn_conv_w", "w_ff1")
_SMALL_ROWS = ("pre_mix_g", "post_mix_g", "pre_mlp_g", "post_mlp_g")
_SMALL_MISC = ("dn_a_log", "dn_dt_bias", "dn_norm_g", "sw_sinks")
_WEIGHTS = ("pre_mix_g", "w_in", "dn_conv_w", "dn_a_log", "dn_dt_bias", "dn_norm_g", "sw_sinks", "w_up_dn", "w_up_sw",
            "w_o", "post_mix_g", "pre_mlp_g", "w_ff1", "w_ff2", "post_mlp_g")
_SMALL_PACK_ROWS = 24


def _unshard(name, g):
    if name in _COL_SHARDED:
        g = jnp.moveaxis(g, 0, -2)
        return g.reshape(g.shape[:-2] + (g.shape[-2] * g.shape[-1],))
    g = jnp.moveaxis(g, 0, 1)
    return g.reshape((g.shape[0], g.shape[1] * g.shape[2]) + g.shape[3:])


def _shard_major(name, full):
    if name in _COL_SHARDED:
        s = full.reshape(full.shape[:-1] + (N_DEV, full.shape[-1] // N_DEV))
        return jnp.moveaxis(s, -2, 0)
    s = full.reshape((full.shape[0], N_DEV, full.shape[1] // N_DEV) + full.shape[2:])
    return jnp.moveaxis(s, 1, 0)


def _pack_small(d):
    rows = jnp.concatenate([d[n] for n in _SMALL_ROWS], axis=0)
    misc = jnp.concatenate([d[n].reshape(-1) for n in _SMALL_MISC])
    misc = jnp.pad(misc, (0, W1 - misc.shape[0])).reshape(1, W1)
    out = jnp.concatenate([rows, misc], axis=0)
    return jnp.pad(out, ((0, _SMALL_PACK_ROWS - out.shape[0]), (0, 0)))


def _unpack_small(a, like):
    out, L = {}, like[_SMALL_ROWS[0]].shape[0]
    for i, n in enumerate(_SMALL_ROWS):
        out[n] = a[L * i:L * (i + 1)]
    at, row = 0, a[L * len(_SMALL_ROWS)]
    for n in _SMALL_MISC:
        size = like[n].size
        out[n] = row[at:at + size].reshape(like[n].shape)
        at += size
    return out


def kernel(x, positions, pre_mix_g, w_in, dn_conv_w, dn_a_log, dn_dt_bias, dn_norm_g, sw_sinks, w_up_dn, w_up_sw, w_o, post_mix_g, pre_mlp_g, w_ff1, w_ff2, post_mlp_g, loss_target, m_pre_mix_g, m_w_in, m_dn_conv_w, m_dn_a_log, m_dn_dt_bias, m_dn_norm_g, m_sw_sinks, m_w_up_dn, m_w_up_sw, m_w_o, m_post_mix_g, m_pre_mlp_g, m_w_ff1, m_w_ff2, m_post_mlp_g, v_pre_mix_g, v_w_in, v_dn_conv_w, v_dn_a_log, v_dn_dt_bias, v_dn_norm_g, v_sw_sinks, v_w_up_dn, v_w_up_sw, v_w_o, v_post_mix_g, v_pre_mlp_g, v_w_ff1, v_w_ff2, v_post_mlp_g):
    w = dict(pre_mix_g=pre_mix_g, w_in=w_in, dn_conv_w=dn_conv_w, dn_a_log=dn_a_log, dn_dt_bias=dn_dt_bias,
             dn_norm_g=dn_norm_g, sw_sinks=sw_sinks, w_up_dn=w_up_dn, w_up_sw=w_up_sw, w_o=w_o, post_mix_g=post_mix_g,
             pre_mlp_g=pre_mlp_g, w_ff1=w_ff1, w_ff2=w_ff2, post_mlp_g=post_mlp_g)
    m = dict(pre_mix_g=m_pre_mix_g, w_in=m_w_in, dn_conv_w=m_dn_conv_w, dn_a_log=m_dn_a_log, dn_dt_bias=m_dn_dt_bias,
             dn_norm_g=m_dn_norm_g, sw_sinks=m_sw_sinks, w_up_dn=m_w_up_dn, w_up_sw=m_w_up_sw, w_o=m_w_o,
             post_mix_g=m_post_mix_g, pre_mlp_g=m_pre_mlp_g, w_ff1=m_w_ff1, w_ff2=m_w_ff2, post_mlp_g=m_post_mlp_g)
    v = dict(pre_mix_g=v_pre_mix_g, w_in=v_w_in, dn_conv_w=v_dn_conv_w, dn_a_log=v_dn_a_log, dn_dt_bias=v_dn_dt_bias,
             dn_norm_g=v_dn_norm_g, sw_sinks=v_sw_sinks, w_up_dn=v_w_up_dn, w_up_sw=v_w_up_sw, w_o=v_w_o,
             post_mix_g=v_post_mix_g, pre_mlp_g=v_pre_mlp_g, w_ff1=v_w_ff1, w_ff2=v_w_ff2, post_mlp_g=v_post_mlp_g)
    n_layers = pre_mix_g.shape[0]
    xs, pos, tgt = x[0], positions[0], loss_target[0]

    payload = [w[n] if n == "dn_conv_w" else w[n].astype(bf16) for n in _BIG]
    gathered = _exchange("allgather_weights", payload, ["gather2"] * len(_BIG))
    full = {n: _unshard(n, g) for n, g in zip(_BIG, gathered)}
    layers = []
    for l in range(n_layers):
        wl = {n: (full[n][l] if n in _BIG else w[n][l]) for n in _WEIGHTS}
        layers.append(_layer_params(wl))
    cst = _consts(pos)

    h, res = xs, []
    for l in range(n_layers):
        h, r = _layer_fwd(h, layers[l], cst)
        res.append(r)
    dy, sq = _loss(h, tgt)
    loss = lax.psum(0.5 / D_MODEL * jnp.sum(sq), ("x", "y", "c"))
    grads = [None] * n_layers
    for l in reversed(range(n_layers)):
        dy, g = _layer_bwd(dy, res[l], layers[l], cst)
        grads[l] = _layer_grads_ref_layout(g)
    grad_x = dy[None]
    gfull = {n: jnp.stack([grads[l][n] for l in range(n_layers)]) for n in _WEIGHTS}

    nb = len(_BIG)
    send = [_shard_major(n, gfull[n]).astype(bf16) for n in _BIG]
    stage_a = _exchange("grads_to_sibling", send + [_pack_small(gfull)], ["to_sibling"] * nb + ["gather"])
    core = lax.axis_index("c")
    summed = []
    for n, mine, theirs in zip(_BIG, send, stage_a[:nb]):
        own = lax.dynamic_index_in_dim(mine.reshape((N_CHIP, 2) + mine.shape[1:]), core, axis=1, keepdims=False)
        lead = (-1,) + own.shape[-2:]
        summed.append(_add_bf16("add_" + n, own.reshape(lead), theirs.reshape(lead), tr=512).reshape(own.shape))
    landed = _exchange("grads_to_chips", summed, ["to_chips"] * nb)

    out_g, out_d, out_m, out_v = {}, {}, {}, {}
    for n, land in zip(_BIG, landed):
        out_g[n], out_d[n], out_m[n], out_v[n] = _adamw("adamw_" + n, land, w[n], m[n], v[n], tr=256)
    small = _adamw("adamw_small", stage_a[nb][:, None], _pack_small(w)[None], _pack_small(m)[None], _pack_small(v)[None],
                   tr=_SMALL_PACK_ROWS)
    for dst, a in zip((out_g, out_d, out_m, out_v), small):
        dst.update(_unpack_small(a[0], w))
    return (loss, grad_x, *[out_g[n] for n in _WEIGHTS], *[out_d[n] for n in _WEIGHTS],
            *[out_m[n] for n in _WEIGHTS], *[out_v[n] for n in _WEIGHTS])
```

```python
import math

import numpy as np
import jax
import jax.numpy as jnp
from jax import lax
from jax.experimental import pallas as pl
from jax.experimental.pallas import tpu as pltpu

f32 = jnp.float32
bf16 = jnp.bfloat16
HIGHEST = lax.Precision.HIGHEST

N_DEV = 8
D_MODEL = 1024
DN_HEADS = 8
DN_DK = 128
DN_CHUNK = 128
DN_CONV = 4
SW_Q_HEADS = 16
SW_KV_HEADS = 2
SW_HEAD_DIM = 64
SW_BLOCK = 128
ROPE_THETA = 500000.0
ROT_DIM = SW_HEAD_DIM // 4
D_FF = 4 * D_MODEL
EPS = 1e-6
LANES = 128
CONV_HALO = 8
NEG_BIG = -1e30

ADAM_LR = 0.001
ADAM_B1 = 0.9
ADAM_B2 = 0.999
ADAM_EPS = 1e-08
ADAM_WD = 0.01
ADAM_STEP = 10

PROJ_W = 7680
CB_Q, CB_K, CB_V, CB_Z, CB_SWQ, CB_GA, CB_GB = 0, 1, 2, 3, 4, 5, 6
CB_SWK, CB_SWV, CB_BA = 56, 57, 58

NN = ((1,), (0,))
NT = ((1,), (1,))
TN = ((0,), (0,))


def _mm(a, b, dims, hi=False):
    if hi:
        return lax.dot_general(a.astype(f32), b.astype(f32), (dims, ((), ())), precision=HIGHEST,
                               preferred_element_type=f32)
    return lax.dot_general(a.astype(bf16), b.astype(bf16), (dims, ((), ())), preferred_element_type=f32)


def _matmul(name, a, b, form, out_dtype, tm=512, tn=512, tk=1024, extra=(), epilogue=None):
    if form == "nn":
        (M, K), (_, N) = a.shape, b.shape
    elif form == "nt":
        (M, K), (N, _) = a.shape, b.shape
    else:
        (K, M), (_, N) = a.shape, b.shape
    tm, tn, tk = min(tm, M), min(tn, N), min(tk, K)
    assert M % tm == 0 and N % tn == 0 and K % tk == 0, (name, M, N, K, tm, tn, tk)
    nk = K // tk
    dims = {"nn": NN, "nt": NT, "tn": TN}[form]
    out_dtypes = [out_dtype] if epilogue is None else list(out_dtype)
    ne, no = len(extra), len(out_dtypes)

    def body(a_ref, b_ref, *rest):
        e_refs, o_refs, acc_ref = rest[:ne], rest[ne:ne + no], rest[ne + no]

        def finish(acc):
            vals = [acc] if epilogue is None else epilogue(acc, [e[...] for e in e_refs])
            for o, val in zip(o_refs, vals):
                o[...] = val.astype(o.dtype)

        part = lax.dot_general(a_ref[...], b_ref[...], (dims, ((), ())), preferred_element_type=f32)
        if nk == 1:
            finish(part)
        else:
            k = pl.program_id(2)

            @pl.when(k == 0)
            def _():
                acc_ref[...] = part

            @pl.when(k > 0)
            def _():
                acc_ref[...] += part

            @pl.when(k == nk - 1)
            def _():
                finish(acc_ref[...])

    if form == "tn":
        a_spec = pl.BlockSpec((tk, tm), lambda i, j, k: (k, i))
    else:
        a_spec = pl.BlockSpec((tm, tk), lambda i, j, k: (i, k))
    if form == "nt":
        b_spec = pl.BlockSpec((tn, tk), lambda i, j, k: (j, k))
    else:
        b_spec = pl.BlockSpec((tk, tn), lambda i, j, k: (k, j))
    tile = pl.BlockSpec((tm, tn), lambda i, j, k: (i, j))
    res = pl.pallas_call(
        body, name=name,
        grid=(M // tm, N // tn, nk),
        in_specs=[a_spec, b_spec] + [tile] * ne,
        out_specs=[tile] * no,
        out_shape=[jax.ShapeDtypeStruct((M, N), dt) for dt in out_dtypes],
        scratch_shapes=[pltpu.VMEM((tm, tn) if nk > 1 else (8, 128), f32)],
        compiler_params=pltpu.CompilerParams(dimension_semantics=("parallel", "parallel", "arbitrary")),
    )(a, b, *extra)
    return res[0] if epilogue is None else list(res)


def _tile_specs(ins, halo_ids, params, TM, HR, row_of):
    specs = [pl.BlockSpec((TM, w), lambda i, cb=cb: (row_of(i), cb)) for (_, w, cb) in ins]
    for h in halo_ids:
        _, w, cb = ins[h]
        specs.append(pl.BlockSpec((HR, w), lambda i, cb=cb: (jnp.maximum(row_of(i) * (TM // HR) - 1, 0), cb)))
    for p in params:
        specs.append(pl.BlockSpec(p.shape, lambda i, nd=p.ndim: (0,) * nd))
    return specs


def _tile_fwd(name, fn, T, TM, ins, params, outs, halo_ids=(), HR=CONV_HALO):
    TM = min(TM, T)
    n = T // TM
    ni, nh, npar = len(ins), len(halo_ids), len(params)

    def body(*refs):
        in_v = [r[...] for r in refs[:ni]]
        halo_v = [r[...] for r in refs[ni:ni + nh]]
        par_v = [r[...] for r in refs[ni + nh:ni + nh + npar]]
        o_refs = refs[ni + nh + npar:]
        first = pl.program_id(0) == 0
        vals = fn(first, in_v, halo_v, par_v)
        for o, val in zip(o_refs, vals):
            o[...] = val.astype(o.dtype)

    res = pl.pallas_call(
        body, name=name, grid=(n,),
        in_specs=_tile_specs(ins, halo_ids, params, TM, HR, lambda i: i),
        out_specs=[pl.BlockSpec((TM, w), lambda i: (i, 0)) for (w, _) in outs],
        out_shape=[jax.ShapeDtypeStruct((T, w), dt) for (w, dt) in outs],
        compiler_params=pltpu.CompilerParams(dimension_semantics=("arbitrary",)),
    )(*[a for (a, _, _) in ins], *[ins[h][0] for h in halo_ids], *params)
    return list(res)


def _tile_bwd(name, fn, T, TM, ins, params, cts, din, dpar, halo_ids=(), HR=CONV_HALO):
    TM = min(TM, T)
    n = T // TM
    ni, nh, npar, nc = len(ins), len(halo_ids), len(params), len(cts)
    din_ids = [j for (j, _) in din]
    dh_ids = [h for h in halo_ids if h in din_ids]
    nd, ndp, ndh = len(din), len(dpar), len(dh_ids)

    def body(*refs):
        in_v = [r[...] for r in refs[:ni]]
        halo_v = [r[...] for r in refs[ni:ni + nh]]
        par_v = [r[...] for r in refs[ni + nh:ni + nh + npar]]
        ct_v = [r[...].astype(f32) for r in refs[ni + nh + npar:ni + nh + npar + nc]]
        o_refs = refs[ni + nh + npar + nc:ni + nh + npar + nc + nd + ndp]
        carry_refs = refs[ni + nh + npar + nc + nd + ndp:]
        i = pl.program_id(0)
        first = i == n - 1

        def g(d_in, d_halo, d_par):
            full_in = list(in_v)
            for j, val in zip(din_ids, d_in):
                full_in[j] = val
            full_halo = list(halo_v)
            for h, val in zip(dh_ids, d_halo):
                full_halo[list(halo_ids).index(h)] = val
            full_par = list(par_v)
            for j, val in zip(dpar, d_par):
                full_par[j] = val
            return tuple(fn(first, full_in, full_halo, full_par))

        prim = ([in_v[j].astype(f32) for j in din_ids],
                [halo_v[list(halo_ids).index(h)].astype(f32) for h in dh_ids],
                [par_v[j] for j in dpar])
        _, vjp = jax.vjp(g, *prim)
        g_in, g_halo, g_par = vjp(tuple(ct_v))

        @pl.when(i == 0)
        def _():
            for c in carry_refs:
                c[...] = jnp.zeros_like(c)
            for o in o_refs[nd:]:
                o[...] = jnp.zeros_like(o)

        for slot, (j, _) in enumerate(din):
            val = g_in[slot]
            if j in dh_ids:
                c = carry_refs[dh_ids.index(j)]
                val = jnp.concatenate([val[:TM - HR], val[TM - HR:] + c[...]], axis=0) if TM > HR else val + c[...]
                c[...] = g_halo[dh_ids.index(j)]
            o_refs[slot][...] = val.astype(o_refs[slot].dtype)
        for slot in range(ndp):
            o_refs[nd + slot][...] += g_par[slot]

    rev = lambda i: n - 1 - i
    in_specs = _tile_specs(ins, halo_ids, params, TM, HR, rev)
    ct_specs = [pl.BlockSpec((TM, w), lambda i, cb=cb: (rev(i), cb)) for (_, w, cb) in cts]
    out_specs = [pl.BlockSpec((TM, ins[j][1]), lambda i: (rev(i), 0)) for j in din_ids]
    out_specs += [pl.BlockSpec(params[j].shape, lambda i, nd_=params[j].ndim: (0,) * nd_) for j in dpar]
    out_shape = [jax.ShapeDtypeStruct((T, ins[j][1]), dt) for (j, dt) in din]
    out_shape += [jax.ShapeDtypeStruct(params[j].shape, f32) for j in dpar]
    res = pl.pallas_call(
        body, name=name, grid=(n,),
        in_specs=in_specs + ct_specs,
        out_specs=out_specs,
        out_shape=out_shape,
        scratch_shapes=[pltpu.VMEM((HR, ins[h][1]), f32) for h in dh_ids],
        compiler_params=pltpu.CompilerParams(dimension_semantics=("arbitrary",)),
    )(*[a for (a, _, _) in ins], *[ins[h][0] for h in halo_ids], *params, *[a for (a, _, _) in cts])
    return list(res)


def _rms(x, g):
    return x * lax.rsqrt(jnp.mean(x * x, axis=-1, keepdims=True) + EPS) * g


def _fn_prenorm(first, ins, halos, params):
    (x,), (g,) = ins, params
    x = x.astype(f32)
    return [_rms(x, g), x]


def _fn_postmix(first, ins, halos, params):
    (x, mix), (g2, g3) = ins, params
    x1 = x + _rms(mix, g2)
    return [x1, _rms(x1, g3)]


def _fn_postmlp(first, ins, halos, params):
    (x1, ff), (g4,) = ins, params
    return [x1 + _rms(ff, g4)]


def _fn_rms_only(first, ins, halos, params):
    (ff,), (g4,) = ins, params
    return [_rms(ff, g4)]


def _fn_merge(first, ins, halos, params):
    ga, gb, ya, yb = ins
    return [jax.nn.sigmoid(ga) * ya + jax.nn.sigmoid(gb) * yb]


def _make_fn_conv(norm_scale):
    def fn(first, ins, halos, params):
        (x,), (xp,), (w,) = ins, halos, params
        TM = x.shape[0]
        xp = jnp.where(first, 0.0, xp)
        xe = jnp.concatenate([xp, x], axis=0)
        off = CONV_HALO - (DN_CONV - 1)
        y = xe[off:off + TM] * w[0:1]
        for j in range(1, DN_CONV):
            y = y + xe[off + j:off + j + TM] * w[j:j + 1]
        y = jax.nn.silu(y)
        if norm_scale is None:
            return [y]
        outs = []
        for h in range(DN_HEADS):
            yh = y[:, DN_DK * h:DN_DK * (h + 1)]
            outs.append(yh * lax.rsqrt(jnp.sum(yh * yh, axis=-1, keepdims=True) + EPS) * norm_scale)
        return [jnp.concatenate(outs, axis=-1)]
    return fn


def _fn_gates(first, ins, halos, params):
    (ba,), (avec, dvec) = ins, params
    lane = lax.broadcasted_iota(jnp.int32, ba.shape, 1)
    beta = jax.nn.sigmoid(ba)
    g = -jnp.exp(avec) * jax.nn.softplus(ba + dvec)
    return [jnp.where(lane < DN_HEADS, beta, jnp.where(lane < 2 * DN_HEADS, g, 0.0))]


def _fn_dnpost(first, ins, halos, params):
    (o, z), (ng,) = ins, params
    outs = []
    for h in range(DN_HEADS):
        sl = slice(DN_DK * h, DN_DK * (h + 1))
        outs.append(_rms(o[:, sl], ng) * jax.nn.silu(z[:, sl]))
    return [jnp.concatenate(outs, axis=-1)]


def _roll_lanes(x, shift):
    return pltpu.roll(x, shift, 1)


_lane_roll = jax.custom_vjp(_roll_lanes, nondiff_argnums=(1,))
_lane_roll.defvjp(lambda x, shift: (_roll_lanes(x, shift), None),
                  lambda shift, _, ct: (_roll_lanes(ct, LANES - shift),))


def _fn_swa(first, ins, halos, params):
    q, k, v, cos, sin = ins
    kp, vp, cosp, sinp = halos
    sinks, sel_a0, sel_b0, sel_a1, sel_b1 = params
    B = q.shape[0]
    half = ROT_DIM // 2
    in_head = jnp.bitwise_and(lax.broadcasted_iota(jnp.int32, (1, LANES), 1), SW_HEAD_DIM - 1)

    def rope(x, c, s):
        return (x * c + _lane_roll(x, LANES - half) * jnp.where(in_head < half, -s, 0.0)
                + _lane_roll(x, half) * jnp.where(in_head >= half, s, 0.0))

    kcat = jnp.concatenate([rope(kp, cosp, sinp), rope(k, cos, sin)], axis=0)
    vcat = jnp.concatenate([vp, v], axis=0)
    r = lax.broadcasted_iota(jnp.int32, (B, 2 * B), 0)
    c = lax.broadcasted_iota(jnp.int32, (B, 2 * B), 1)
    mask = (c > r) & (c <= r + B) & ((c >= B) | jnp.logical_not(first))
    group = SW_Q_HEADS // SW_KV_HEADS
    sels = ((sel_a0, sel_b0), (sel_a1, sel_b1))
    outs = []
    for hk in range(SW_KV_HEADS):
        heads = range(group)
        kx = [_mm(kcat, sels[hk][half], NN) for half in range(2)]
        vx = [_mm(vcat, sels[hk][half], NN) for half in range(2)]
        qp = [rope(q[:, LANES * j:LANES * (j + 1)], cos, sin) * (SW_HEAD_DIM ** -0.5)
              for j in range(hk * group // 2, (hk + 1) * group // 2)]
        s = [jnp.where(mask, _mm(qp[g // 2], kx[g % 2], NT), NEG_BIG) for g in heads]
        sink = [sinks[:, hk * group + g:hk * group + g + 1] for g in heads]
        m = [jnp.maximum(jnp.max(s[g], axis=-1, keepdims=True), sink[g]) for g in heads]
        p = [jnp.exp(s[g] - m[g]) for g in heads]
        inv = [1.0 / (jnp.sum(p[g], axis=-1, keepdims=True) + jnp.exp(sink[g] - m[g])) for g in heads]
        o = [_mm(p[g] * inv[g], vx[g % 2], NN) for g in heads]
        outs += [o[2 * j] + o[2 * j + 1] for j in range(group // 2)]
    return [jnp.concatenate(outs, axis=-1)]


@jax.custom_vjp
def _inv_unit_lower(Ls):
    C = Ls[0].shape[0]
    ii = lax.broadcasted_iota(jnp.int32, (C, C), 0)
    jj = lax.broadcasted_iota(jnp.int32, (C, C), 1)

    def off_mask(level):
        same_pair = jnp.right_shift(ii, level + 1) == jnp.right_shift(jj, level + 1)
        lower_left = (jnp.bitwise_and(jnp.right_shift(ii, level), 1) == 1) & (jnp.bitwise_and(jnp.right_shift(jj, level), 1) == 0)
        return same_pair & lower_left

    eye = (ii == jj).astype(f32)
    m0 = off_mask(0)
    Ts = [eye - jnp.where(m0, L, 0.0) for L in Ls]
    for level in range(1, int(math.log2(C))):
        mk = off_mask(level)
        left = [_mm(T_, jnp.where(mk, L, 0.0), NN) for T_, L in zip(Ts, Ls)]
        Ts = [T_ - _mm(a, T_, NN) for a, T_ in zip(left, Ts)]
    return tuple(Ts)


def _inv_fwd(Ls):
    Ts = _inv_unit_lower(Ls)
    return Ts, Ts


def _inv_bwd(Ts, dTs):
    left = [_mm(T_, dT, TN) for T_, dT in zip(Ts, dTs)]
    return (tuple(-_mm(a, T_, NT) for a, T_ in zip(left, Ts)),)


_inv_unit_lower.defvjp(_inv_fwd, _inv_bwd)


@jax.custom_vjp
def _inv_known(Ls, Ts):
    return Ts


_inv_known.defvjp(lambda Ls, Ts: (Ts, Ts),
                  lambda Ts, dTs: (_inv_bwd(Ts, dTs)[0], tuple(jnp.zeros_like(t) for t in Ts)))


def _mm_01(a, b, dims):
    hi = b.astype(bf16)
    r1 = b - hi.astype(f32)
    mid = r1.astype(bf16)
    lo = (r1 - mid.astype(f32)).astype(bf16)
    a16 = a.astype(bf16)
    dot = lambda part: lax.dot_general(a16, part, (dims, ((), ())), preferred_element_type=f32)
    return dot(hi) + dot(mid) + dot(lo)


def _eye(n):
    return lax.broadcasted_iota(jnp.int32, (n, n), 0) == lax.broadcasted_iota(jnp.int32, (n, n), 1)


def _lower(n):
    return lax.broadcasted_iota(jnp.int32, (n, n), 0) >= lax.broadcasted_iota(jnp.int32, (n, n), 1)


@jax.custom_vjp
def _transpose(x):
    return _mm_01(_eye(x.shape[1]), x, NT)


_transpose.defvjp(lambda x: (_transpose(x), None), lambda _, ct: (_transpose(ct),))


@jax.custom_vjp
def _cumsum_rows(x):
    return _mm_01(_lower(x.shape[0]), x, NN)


_cumsum_rows.defvjp(lambda x: (_cumsum_rows(x), None), lambda _, ct: (_mm_01(_lower(ct.shape[0]), ct, TN),))


def _dn_chunk(q, k, v, gb, S, tinv_known=None):
    C = q.shape[0]
    H = range(DN_HEADS)
    ii = lax.broadcasted_iota(jnp.int32, (C, C), 0)
    jj = lax.broadcasted_iota(jnp.int32, (C, C), 1)
    causal, strict = ii >= jj, ii > jj
    gc_all = _cumsum_rows(gb)
    gc_t = _transpose(gc_all)
    sl = [slice(DN_DK * h, DN_DK * (h + 1)) for h in H]
    qs, ks, vs = [q[:, s] for s in sl], [k[:, s] for s in sl], [v[:, s] for s in sl]
    beta = [gb[:, h:h + 1] for h in H]
    gcol = [gc_all[:, DN_HEADS + h:DN_HEADS + h + 1] for h in H]
    grow = [gc_t[DN_HEADS + h:DN_HEADS + h + 1, :] for h in H]
    decay = [jnp.where(causal, jnp.exp(jnp.where(causal, gcol[h] - grow[h], 0.0)), 0.0) for h in H]
    kb = [ks[h] * beta[h] for h in H]
    kk = [_mm(kb[h], ks[h], NT) for h in H]
    qk = [_mm(qs[h], ks[h], NT) for h in H]
    Ls = tuple(jnp.where(strict, kk[h] * decay[h], 0.0) for h in H)
    tinv = _inv_unit_lower(Ls) if tinv_known is None else _inv_known(Ls, tinv_known)
    eg = [jnp.exp(gcol[h]) for h in H]
    u = [_mm(tinv[h], vs[h] * beta[h], NN) for h in H]
    w = [_mm(tinv[h], kb[h] * eg[h], NN) for h in H]
    gl = [gcol[h][C - 1:C, :] for h in H]
    ws = [_mm(w[h], S[h], NN) for h in H]
    qS = [_mm(qs[h] * eg[h], S[h], NN) for h in H]
    v_new = [u[h] - ws[h] for h in H]
    av = [_mm(qk[h] * decay[h], v_new[h], NN) for h in H]
    kv = [_mm(ks[h] * jnp.exp(gl[h] - gcol[h]), v_new[h], TN) for h in H]
    o = jnp.concatenate([qS[h] + av[h] for h in H], axis=-1)
    return o, tuple(S[h] * jnp.exp(gl[h]) + kv[h] for h in H), tinv


def _delta_fwd(qn, kn, vv, gb):
    T = qn.shape[0]
    C = min(DN_CHUNK, T)
    n = T // C
    W = DN_HEADS * DN_DK

    def body(q_ref, k_ref, v_ref, gb_ref, o_ref, hist_ref, tinv_ref, s_ref):
        @pl.when(pl.program_id(0) == 0)
        def _():
            s_ref[...] = jnp.zeros_like(s_ref)

        S = tuple(s_ref[h] for h in range(DN_HEADS))
        for h in range(DN_HEADS):
            hist_ref[0, h] = S[h]
        o, s_new, tinv = _dn_chunk(q_ref[...], k_ref[...], v_ref[...], gb_ref[...], S)
        o_ref[...] = o
        for h in range(DN_HEADS):
            s_ref[h] = s_new[h]
            tinv_ref[0, h] = tinv[h]

    row = pl.BlockSpec((C, W), lambda i: (i, 0))
    return pl.pallas_call(
        body, name="delta_fwd", grid=(n,),
        in_specs=[row, row, row, pl.BlockSpec((C, LANES), lambda i: (i, 0))],
        out_specs=[row, pl.BlockSpec((1, DN_HEADS, DN_DK, DN_DK), lambda i: (i, 0, 0, 0)),
                   pl.BlockSpec((1, DN_HEADS, C, C), lambda i: (i, 0, 0, 0))],
        out_shape=[jax.ShapeDtypeStruct((T, W), f32), jax.ShapeDtypeStruct((n, DN_HEADS, DN_DK, DN_DK), f32),
                   jax.ShapeDtypeStruct((n, DN_HEADS, C, C), f32)],
        scratch_shapes=[pltpu.VMEM((DN_HEADS, DN_DK, DN_DK), f32)],
        compiler_params=pltpu.CompilerParams(dimension_semantics=("arbitrary",)),
    )(qn, kn, vv, gb)


def _delta_bwd(qn, kn, vv, gb, hist, tinv, do):
    T = qn.shape[0]
    C = min(DN_CHUNK, T)
    n = T // C
    W = DN_HEADS * DN_DK

    def body(q_ref, k_ref, v_ref, gb_ref, hist_ref, tinv_ref, do_ref, dq_ref, dk_ref, dv_ref, dgb_ref, ds_ref):
        @pl.when(pl.program_id(0) == 0)
        def _():
            ds_ref[...] = jnp.zeros_like(ds_ref)

        S = tuple(hist_ref[0, h] for h in range(DN_HEADS))
        known = tuple(tinv_ref[0, h] for h in range(DN_HEADS))
        chunk = lambda q, k, v, g, s: _dn_chunk(q, k, v, g, s, tinv_known=known)[:2]
        _, vjp = jax.vjp(chunk, q_ref[...], k_ref[...], v_ref[...], gb_ref[...], S)
        dS = tuple(ds_ref[h] for h in range(DN_HEADS))
        dq, dk, dv, dgb, dS_in = vjp((do_ref[...], dS))
        dq_ref[...] = dq
        dk_ref[...] = dk
        dv_ref[...] = dv
        dgb_ref[...] = dgb
        for h in range(DN_HEADS):
            ds_ref[h] = dS_in[h]

    row = pl.BlockSpec((C, W), lambda i: (n - 1 - i, 0))
    small = pl.BlockSpec((C, LANES), lambda i: (n - 1 - i, 0))
    return pl.pallas_call(
        body, name="delta_bwd", grid=(n,),
        in_specs=[row, row, row, small, pl.BlockSpec((1, DN_HEADS, DN_DK, DN_DK), lambda i: (n - 1 - i, 0, 0, 0)),
                  pl.BlockSpec((1, DN_HEADS, C, C), lambda i: (n - 1 - i, 0, 0, 0)), row],
        out_specs=[row, row, row, small],
        out_shape=[jax.ShapeDtypeStruct((T, W), f32)] * 3 + [jax.ShapeDtypeStruct((T, LANES), f32)],
        scratch_shapes=[pltpu.VMEM((DN_HEADS, DN_DK, DN_DK), f32)],
        compiler_params=pltpu.CompilerParams(dimension_semantics=("arbitrary",)),
    )(qn, kn, vv, gb, hist, tinv, do)


TM_ROW = 256
W1 = D_MODEL


def _first_only(fn):
    return lambda *a: fn(*a)[:1]


def _swa_args(proj, cst):
    ins = [(proj, W1, CB_SWQ), (proj, LANES, CB_SWK), (proj, LANES, CB_SWV), (cst["cos"], LANES, 0), (cst["sin"], LANES, 0)]
    return ins, (1, 2, 3, 4)


def _layer_fwd(x, p, cst):
    T = x.shape[0]
    r = {"x": x}
    (h,) = _tile_fwd("prenorm", _first_only(_fn_prenorm), T, TM_ROW, [(x, W1, 0)], [p["g1"]], [(W1, bf16)])
    proj = _matmul("proj", h, p["w_in"], "nn", f32, tm=1024, tn=1536)
    conv = lambda nm, cb, scale: _tile_fwd(nm, _make_fn_conv(scale), T, TM_ROW, [(proj, W1, cb)],
                                            [p["conv_w"][:, W1 * cb:W1 * (cb + 1)]], [(W1, f32)], halo_ids=(0,))[0]
    qn = conv("conv_q", CB_Q, DN_DK ** -0.5)
    kn = conv("conv_k", CB_K, 1.0)
    vv = conv("conv_v", CB_V, None)
    (gbt,) = _tile_fwd("gates", _fn_gates, T, TM_ROW, [(proj, LANES, CB_BA)], [p["avec"], p["dvec"]], [(LANES, f32)])
    o, hist, tinv = _delta_fwd(qn, kn, vv, gbt)
    (dn_out,) = _tile_fwd("dnpost", _fn_dnpost, T, TM_ROW, [(o, W1, 0), (proj, W1, CB_Z)], [p["ng"]], [(W1, bf16)])
    sw_ins, sw_halo = _swa_args(proj, cst)
    sw_par = [p["sinks"], cst["sel_a0"], cst["sel_b0"], cst["sel_a1"], cst["sel_b1"]]
    (sw_out,) = _tile_fwd("swa", _fn_swa, T, SW_BLOCK, sw_ins, sw_par, [(W1, bf16)], halo_ids=sw_halo, HR=SW_BLOCK)
    y_a = _matmul("up_dn", dn_out, p["w_up_dn"], "nn", f32, tm=1024, tn=1024)
    y_b = _matmul("up_sw", sw_out, p["w_up_sw"], "nn", f32, tm=1024, tn=1024)
    (gated,) = _tile_fwd("merge", _fn_merge, T, TM_ROW,
                         [(proj, W1, CB_GA), (proj, W1, CB_GB), (y_a, W1, 0), (y_b, W1, 0)], [], [(W1, bf16)])
    mix = _matmul("w_o", gated, p["w_o"], "nn", f32, tm=1024, tn=1024)
    x1, h2 = _tile_fwd("postmix", _fn_postmix, T, TM_ROW, [(x, W1, 0), (mix, W1, 0)], [p["g2"], p["g3"]],
                       [(W1, f32), (W1, bf16)])
    ffh, act = _matmul("ff1", h2, p["w_ff1"], "nn", [f32, bf16], tm=1024, tn=1024,
                       epilogue=lambda acc, ex: [acc, jnp.square(jnp.maximum(acc, 0.0))])
    ff = _matmul("ff2", act, p["w_ff2"], "nn", f32, tm=1024, tn=1024)
    (x2,) = _tile_fwd("postmlp", _fn_postmlp, T, TM_ROW, [(x1, W1, 0), (ff, W1, 0)], [p["g4"]], [(W1, f32)])
    r.update(h=h, proj=proj, qn=qn, kn=kn, vv=vv, gbt=gbt, o=o, hist=hist, tinv=tinv, dn_out=dn_out, sw_out=sw_out,
             y_a=y_a, y_b=y_b, gated=gated, mix=mix, h2=h2, ffh=ffh, act=act, ff=ff)
    return x2, r


def _layer_bwd(dx2, r, p, cst):
    T = dx2.shape[0]
    x, proj = r["x"], r["proj"]
    g = {}
    dff, g["g4"] = _tile_bwd("postmlp_b", _fn_rms_only, T, TM_ROW, [(r["ff"], W1, 0)], [p["g4"]], [(dx2, W1, 0)],
                             [(0, bf16)], [0])
    (dffh,) = _matmul("ff2_dx", dff, p["w_ff2"], "nt", [bf16], tm=1024, tn=1024, extra=[r["ffh"]],
                      epilogue=lambda acc, ex: [acc * (2.0 * jnp.maximum(ex[0], 0.0))])
    g["w_ff2"] = _matmul("ff2_dw", r["act"], dff, "tn", bf16, tm=1024, tn=1024)
    dh2 = _matmul("ff1_dx", dffh, p["w_ff1"], "nt", f32, tm=1024, tn=1024)
    g["w_ff1"] = _matmul("ff1_dw", r["h2"], dffh, "tn", bf16, tm=1024, tn=1024)
    dx1, dmix, g["g2"], g["g3"] = _tile_bwd("postmix_b", _fn_postmix, T, TM_ROW, [(x, W1, 0), (r["mix"], W1, 0)],
                                            [p["g2"], p["g3"]], [(dx2, W1, 0), (dh2, W1, 0)], [(0, f32), (1, bf16)], [0, 1])
    dgated = _matmul("w_o_dx", dmix, p["w_o"], "nt", f32, tm=1024, tn=1024)
    g["w_o"] = _matmul("w_o_dw", r["gated"], dmix, "tn", bf16, tm=1024, tn=1024)
    dga, dgb, dya, dyb = _tile_bwd("merge_b", _fn_merge, T, TM_ROW,
                                   [(proj, W1, CB_GA), (proj, W1, CB_GB), (r["y_a"], W1, 0), (r["y_b"], W1, 0)], [],
                                   [(dgated, W1, 0)], [(0, bf16), (1, bf16), (2, bf16), (3, bf16)], [])
    d_dn = _matmul("up_dn_dx", dya, p["w_up_dn"], "nt", f32, tm=1024, tn=1024)
    g["w_up_dn"] = _matmul("up_dn_dw", r["dn_out"], dya, "tn", bf16, tm=1024, tn=1024)
    d_sw = _matmul("up_sw_dx", dyb, p["w_up_sw"], "nt", f32, tm=1024, tn=1024)
    g["w_up_sw"] = _matmul("up_sw_dw", r["sw_out"], dyb, "tn", bf16, tm=1024, tn=1024)
    do, dz, g["ng"] = _tile_bwd("dnpost_b", _fn_dnpost, T, TM_ROW, [(r["o"], W1, 0), (proj, W1, CB_Z)], [p["ng"]],
                                [(d_dn, W1, 0)], [(0, f32), (1, bf16)], [0])
    dqn, dkn, dvv, dgbt = _delta_bwd(r["qn"], r["kn"], r["vv"], r["gbt"], r["hist"], r["tinv"], do)
    conv_b = lambda nm, cb, scale, ct: _tile_bwd(nm, _make_fn_conv(scale), T, TM_ROW, [(proj, W1, cb)],
                                                 [p["conv_w"][:, W1 * cb:W1 * (cb + 1)]], [(ct, W1, 0)], [(0, bf16)], [0],
                                                 halo_ids=(0,))
    dq_in, dcw_q = conv_b("conv_q_b", CB_Q, DN_DK ** -0.5, dqn)
    dk_in, dcw_k = conv_b("conv_k_b", CB_K, 1.0, dkn)
    dv_in, dcw_v = conv_b("conv_v_b", CB_V, None, dvv)
    g["conv_w"] = jnp.concatenate([dcw_q, dcw_k, dcw_v], axis=-1)
    dba, g["avec"], g["dvec"] = _tile_bwd("gates_b", _fn_gates, T, TM_ROW, [(proj, LANES, CB_BA)], [p["avec"], p["dvec"]],
                                          [(dgbt, LANES, 0)], [(0, bf16)], [0, 1])
    sw_ins, sw_halo = _swa_args(proj, cst)
    sw_par = [p["sinks"], cst["sel_a0"], cst["sel_b0"], cst["sel_a1"], cst["sel_b1"]]
    dswq, dswk, dswv, g["sinks"] = _tile_bwd("swa_b", _fn_swa, T, SW_BLOCK, sw_ins, sw_par, [(d_sw, W1, 0)],
                                             [(0, bf16), (1, bf16), (2, bf16)], [0], halo_ids=sw_halo, HR=SW_BLOCK)
    dproj = jnp.concatenate([dq_in, dk_in, dv_in, dz, dswq, dga, dgb, dswk, dswv, dba, jnp.zeros((T, LANES), bf16)], axis=-1)
    dh = _matmul("proj_dx", dproj, p["w_in"], "nt", f32, tm=1024, tn=1024, tk=1536)
    g["w_in"] = _matmul("proj_dw", r["h"], dproj, "tn", bf16, tm=1024, tn=1536)
    dx, g["g1"] = _tile_bwd("prenorm_b", _fn_prenorm, T, TM_ROW, [(x, W1, 0)], [p["g1"]], [(dh, W1, 0), (dx1, W1, 0)],
                            [(0, f32)], [0])
    return dx, g


_OFF_BA, _OFF_SWQ, _OFF_SWK, _OFF_GA, _D_IN = 4096, 4112, 5136, 5392, 7440


def _proj_cols(w):
    pad = lambda n: jnp.zeros(w.shape[:-1] + (n,), w.dtype)
    return jnp.concatenate([w[..., :_OFF_BA], w[..., _OFF_SWQ:_OFF_SWK], w[..., _OFF_GA:_D_IN],
                            w[..., _OFF_SWK:_OFF_GA], w[..., _OFF_BA:_OFF_SWQ], pad(PROJ_W - _D_IN)], axis=-1)


def _proj_cols_inv(w):
    n_ba = _OFF_SWQ - _OFF_BA
    return jnp.concatenate([w[..., :4096], w[..., 7424:7424 + n_ba], w[..., 4096:5120], w[..., 7168:7424],
                            w[..., 5120:7168]], axis=-1)


def _lane_pad(v, at):
    return jnp.pad(v.astype(f32), (at, LANES - at - v.shape[0])).reshape(1, LANES)


def _layer_params(w):
    row = lambda v: v.reshape(1, -1).astype(f32)
    return dict(
        g1=row(w["pre_mix_g"]), g2=row(w["post_mix_g"]), g3=row(w["pre_mlp_g"]), g4=row(w["post_mlp_g"]),
        w_in=_proj_cols(w["w_in"]).astype(bf16), conv_w=w["dn_conv_w"].astype(f32),
        avec=_lane_pad(w["dn_a_log"], DN_HEADS), dvec=_lane_pad(w["dn_dt_bias"], DN_HEADS),
        ng=row(w["dn_norm_g"]), sinks=_lane_pad(w["sw_sinks"], 0),
        w_up_dn=w["w_up_dn"].astype(bf16), w_up_sw=w["w_up_sw"].astype(bf16), w_o=w["w_o"].astype(bf16),
        w_ff1=w["w_ff1"].astype(bf16), w_ff2=w["w_ff2"].astype(bf16))


def _layer_grads_ref_layout(g):
    return dict(
        pre_mix_g=g["g1"][0], post_mix_g=g["g2"][0], pre_mlp_g=g["g3"][0], post_mlp_g=g["g4"][0],
        w_in=_proj_cols_inv(g["w_in"]), dn_conv_w=g["conv_w"],
        dn_a_log=g["avec"][0, DN_HEADS:2 * DN_HEADS], dn_dt_bias=g["dvec"][0, DN_HEADS:2 * DN_HEADS],
        dn_norm_g=g["ng"][0], sw_sinks=g["sinks"][0, :SW_Q_HEADS],
        w_up_dn=g["w_up_dn"], w_up_sw=g["w_up_sw"], w_o=g["w_o"], w_ff1=g["w_ff1"], w_ff2=g["w_ff2"])


def _consts(positions):
    T = positions.shape[0]
    half = ROT_DIM // 2
    inv_freq = ROPE_THETA ** (-jnp.arange(half, dtype=f32) * (2.0 / ROT_DIM))
    ang = positions.astype(f32)[:, None] * inv_freq
    cos8, sin8 = jnp.cos(ang), jnp.sin(ang)
    rest = SW_HEAD_DIM - ROT_DIM
    c64 = jnp.concatenate([cos8, cos8, jnp.ones((T, rest), f32)], axis=-1)
    s64 = jnp.concatenate([sin8, sin8, jnp.zeros((T, rest), f32)], axis=-1)
    sel = np.zeros((2, 2, LANES, LANES), np.float32)
    for hk in range(SW_KV_HEADS):
        for d in range(SW_HEAD_DIM):
            sel[hk, 0, SW_HEAD_DIM * hk + d, d] = 1.0
            sel[hk, 1, SW_HEAD_DIM * hk + d, SW_HEAD_DIM + d] = 1.0
    return dict(cos=jnp.concatenate([c64, c64], axis=-1), sin=jnp.concatenate([s64, s64], axis=-1),
                sel_a0=jnp.asarray(sel[0, 0]), sel_b0=jnp.asarray(sel[0, 1]),
                sel_a1=jnp.asarray(sel[1, 0]), sel_b1=jnp.asarray(sel[1, 1]))


def _loss(y, tgt):
    T, W = y.shape
    TM = min(TM_ROW, T)
    n = T // TM

    def body(y_ref, t_ref, dy_ref, acc_ref):
        @pl.when(pl.program_id(0) == 0)
        def _():
            acc_ref[...] = jnp.zeros_like(acc_ref)

        d = y_ref[...] - t_ref[...]
        dy_ref[...] = d * (1.0 / W)
        acc_ref[...] += jnp.sum(d * d, axis=0, keepdims=True)

    row = pl.BlockSpec((TM, W), lambda i: (i, 0))
    return pl.pallas_call(
        body, name="loss", grid=(n,), in_specs=[row, row],
        out_specs=[row, pl.BlockSpec((1, W), lambda i: (0, 0))],
        out_shape=[jax.ShapeDtypeStruct((T, W), f32), jax.ShapeDtypeStruct((1, W), f32)],
        compiler_params=pltpu.CompilerParams(dimension_semantics=("arbitrary",)),
    )(y, tgt)


N_CHIP = 4
N_SEM = N_DEV - 1


def _exchange(name, arrs, modes):
    n = len(arrs)
    lead = {"gather": N_DEV, "gather2": N_DEV, "to_sibling": N_CHIP, "to_chips": N_CHIP}
    outs_shape = [jax.ShapeDtypeStruct((lead[md],) + (a.shape if md.startswith("gather") else a.shape[1:]), a.dtype)
                  for a, md in zip(arrs, modes)]

    def body(*refs):
        ins, outs = refs[:n], refs[n:2 * n]
        send_sems, recv_sems, loc_sems = refs[2 * n:]
        x, y, c = lax.axis_index("x"), lax.axis_index("y"), lax.axis_index("c")
        me, chip, sib = 4 * x + 2 * y + c, 2 * x + y, (x, y, 1 - c)
        flips = [(1 - x, y), (x, 1 - y), (1 - x, 1 - y)]

        def rcopy(a, k, src, dst, dev):
            return pltpu.make_async_remote_copy(src_ref=src, dst_ref=dst, send_sem=send_sems.at[a, k],
                                                recv_sem=recv_sems.at[a, k], device_id=dev,
                                                device_id_type=pl.DeviceIdType.MESH)

        sends, recvs, local, passes = [], [], [], []
        for a, md in enumerate(modes):
            src_all, out = ins[a], outs[a]
            if md in ("gather", "gather2"):
                local.append(pltpu.make_async_copy(src_all, out.at[me], loc_sems.at[a]))
            if md == "to_chips":
                local.append(pltpu.make_async_copy(src_all.at[chip], out.at[chip], loc_sems.at[a]))
            if md == "gather":
                for k in range(1, N_DEV):
                    px = 1 - x if (k >> 2) & 1 else x
                    py = 1 - y if (k >> 1) & 1 else y
                    pc = 1 - c if k & 1 else c
                    sends.append(rcopy(a, k - 1, src_all, out.at[me], (px, py, pc)))
                    recvs.append(rcopy(a, k - 1, src_all, out.at[4 * px + 2 * py + pc], (px, py, pc)))
            elif md == "gather2":
                sends.append(rcopy(a, 0, src_all, out.at[me], sib))
                recvs.append(rcopy(a, 0, src_all, out.at[4 * x + 2 * y + 1 - c], sib))
                for j, (px, py) in enumerate(flips):
                    sends.append(rcopy(a, 1 + j, src_all, out.at[me], (px, py, c)))
                    theirs = out.at[4 * px + 2 * py + c]
                    arrive = rcopy(a, 1 + j, src_all, theirs, (px, py, c))
                    passes.append((arrive, rcopy(a, 4 + j, theirs, theirs, sib)))
                    recvs.append(rcopy(a, 4 + j, src_all, out.at[4 * px + 2 * py + 1 - c], sib))
            elif md == "to_sibling":
                for j in range(N_CHIP):
                    sends.append(rcopy(a, j, src_all.at[2 * j + 1 - c], out.at[j], sib))
                    recvs.append(rcopy(a, j, src_all.at[2 * j + c], out.at[j], sib))
            elif md == "to_chips":
                for j, (px, py) in enumerate(flips):
                    sends.append(rcopy(a, j, src_all.at[2 * px + py], out.at[chip], (px, py, c)))
                    recvs.append(rcopy(a, j, src_all.at[chip], out.at[2 * px + py], (px, py, c)))
        for cp in local + sends:
            cp.start()
        for arrive, onward in passes:
            arrive.wait_recv()
            onward.start()
        for cp in recvs:
            cp.wait_recv()
        for cp in sends:
            cp.wait_send()
        for _, onward in passes:
            onward.wait_send()
        for cp in local:
            cp.wait()

    hbm = pl.BlockSpec(memory_space=pltpu.HBM)
    res = pl.pallas_call(
        body, name=name,
        in_specs=[hbm] * n, out_specs=[hbm] * n, out_shape=outs_shape,
        scratch_shapes=[pltpu.SemaphoreType.DMA((n, N_SEM)), pltpu.SemaphoreType.DMA((n, N_SEM)),
                        pltpu.SemaphoreType.DMA((n,))],
    )(*arrs)
    return list(res)


def _add_bf16(name, a, b, tr):
    L_, R_, C_ = a.shape
    tr = min(tr, R_)
    assert R_ % tr == 0, (name, R_, tr)

    def body(a_ref, b_ref, o_ref):
        o_ref[...] = (a_ref[...].astype(f32) + b_ref[...].astype(f32)).astype(bf16)

    blk = pl.BlockSpec((1, tr, C_), lambda l, i: (l, i, 0))
    return pl.pallas_call(
        body, name=name, grid=(L_, R_ // tr), in_specs=[blk, blk], out_specs=blk,
        out_shape=jax.ShapeDtypeStruct(a.shape, bf16),
        compiler_params=pltpu.CompilerParams(dimension_semantics=("arbitrary", "arbitrary")),
    )(a, b)


def _adamw(name, land, w, m, v, tr):
    L_, R_, C_ = w.shape
    n_slots = land.shape[0]
    tr = min(tr, R_)
    assert R_ % tr == 0, (name, R_, tr)
    c1 = 1.0 - ADAM_B1 ** ADAM_STEP
    c2 = 1.0 - ADAM_B2 ** ADAM_STEP

    def body(l_ref, w_ref, m_ref, v_ref, g_ref, d_ref, mo_ref, vo_ref):
        g = l_ref[0].astype(f32)
        for s in range(1, n_slots):
            g = g + l_ref[s].astype(f32)
        m_new = ADAM_B1 * m_ref[...] + (1.0 - ADAM_B1) * g
        v_new = ADAM_B2 * v_ref[...] + (1.0 - ADAM_B2) * jnp.square(g)
        m_hat = m_new / c1
        v_hat = v_new / c2
        g_ref[...] = g
        d_ref[...] = -ADAM_LR * (m_hat / (jnp.sqrt(v_hat) + ADAM_EPS) + ADAM_WD * w_ref[...])
        mo_ref[...] = m_new
        vo_ref[...] = v_new

    row = pl.BlockSpec((1, tr, C_), lambda l, i: (l, i, 0))
    return pl.pallas_call(
        body, name=name, grid=(L_, R_ // tr),
        in_specs=[pl.BlockSpec((n_slots, 1, tr, C_), lambda l, i: (0, l, i, 0)), row, row, row],
        out_specs=[row] * 4, out_shape=[jax.ShapeDtypeStruct((L_, R_, C_), f32)] * 4,
        compiler_params=pltpu.CompilerParams(dimension_semantics=("arbitrary", "arbitrary")),
    )(land, w, m, v)


_BIG = ("w_in", "dn_conv_w", "w_up_dn", "w_up_sw", "w_o", "w_ff1", "w_ff2")
_COL_SHARDED = ("w_in", "dn_conv_w", "w_ff1")
_SMALL_ROWS = ("pre_mix_g", "post_mix_g", "pre_mlp_g", "post_mlp_g")
_SMALL_MISC = ("dn_a_log", "dn_dt_bias", "dn_norm_g", "sw_sinks")
_WEIGHTS = ("pre_mix_g", "w_in", "dn_conv_w", "dn_a_log", "dn_dt_bias", "dn_norm_g", "sw_sinks", "w_up_dn", "w_up_sw",
            "w_o", "post_mix_g", "pre_mlp_g", "w_ff1", "w_ff2", "post_mlp_g")
_SMALL_PACK_ROWS = 24


def _unshard(name, g):
    if name in _COL_SHARDED:
        g = jnp.moveaxis(g, 0, -2)
        return g.reshape(g.shape[:-2] + (g.shape[-2] * g.shape[-1],))
    g = jnp.moveaxis(g, 0, 1)
    return g.reshape((g.shape[0], g.shape[1] * g.shape[2]) + g.shape[3:])


def _shard_major(name, full):
    if name in _COL_SHARDED:
        s = full.reshape(full.shape[:-1] + (N_DEV, full.shape[-1] // N_DEV))
        return jnp.moveaxis(s, -2, 0)
    s = full.reshape((full.shape[0], N_DEV, full.shape[1] // N_DEV) + full.shape[2:])
    return jnp.moveaxis(s, 1, 0)


def _pack_small(d):
    rows = jnp.concatenate([d[n] for n in _SMALL_ROWS], axis=0)
    misc = jnp.concatenate([d[n].reshape(-1) for n in _SMALL_MISC])
    misc = jnp.pad(misc, (0, W1 - misc.shape[0])).reshape(1, W1)
    out = jnp.concatenate([rows, misc], axis=0)
    return jnp.pad(out, ((0, _SMALL_PACK_ROWS - out.shape[0]), (0, 0)))


def _unpack_small(a, like):
    out, L = {}, like[_SMALL_ROWS[0]].shape[0]
    for i, n in enumerate(_SMALL_ROWS):
        out[n] = a[L * i:L * (i + 1)]
    at, row = 0, a[L * len(_SMALL_ROWS)]
    for n in _SMALL_MISC:
        size = like[n].size
        out[n] = row[at:at + size].reshape(like[n].shape)
        at += size
    return out


def kernel(x, positions, pre_mix_g, w_in, dn_conv_w, dn_a_log, dn_dt_bias, dn_norm_g, sw_sinks, w_up_dn, w_up_sw, w_o, post_mix_g, pre_mlp_g, w_ff1, w_ff2, post_mlp_g, loss_target, m_pre_mix_g, m_w_in, m_dn_conv_w, m_dn_a_log, m_dn_dt_bias, m_dn_norm_g, m_sw_sinks, m_w_up_dn, m_w_up_sw, m_w_o, m_post_mix_g, m_pre_mlp_g, m_w_ff1, m_w_ff2, m_post_mlp_g, v_pre_mix_g, v_w_in, v_dn_conv_w, v_dn_a_log, v_dn_dt_bias, v_dn_norm_g, v_sw_sinks, v_w_up_dn, v_w_up_sw, v_w_o, v_post_mix_g, v_pre_mlp_g, v_w_ff1, v_w_ff2, v_post_mlp_g):
    w = dict(pre_mix_g=pre_mix_g, w_in=w_in, dn_conv_w=dn_conv_w, dn_a_log=dn_a_log, dn_dt_bias=dn_dt_bias,
             dn_norm_g=dn_norm_g, sw_sinks=sw_sinks, w_up_dn=w_up_dn, w_up_sw=w_up_sw, w_o=w_o, post_mix_g=post_mix_g,
             pre_mlp_g=pre_mlp_g, w_ff1=w_ff1, w_ff2=w_ff2, post_mlp_g=post_mlp_g)
    m = dict(pre_mix_g=m_pre_mix_g, w_in=m_w_in, dn_conv_w=m_dn_conv_w, dn_a_log=m_dn_a_log, dn_dt_bias=m_dn_dt_bias,
             dn_norm_g=m_dn_norm_g, sw_sinks=m_sw_sinks, w_up_dn=m_w_up_dn, w_up_sw=m_w_up_sw, w_o=m_w_o,
             post_mix_g=m_post_mix_g, pre_mlp_g=m_pre_mlp_g, w_ff1=m_w_ff1, w_ff2=m_w_ff2, post_mlp_g=m_post_mlp_g)
    v = dict(pre_mix_g=v_pre_mix_g, w_in=v_w_in, dn_conv_w=v_dn_conv_w, dn_a_log=v_dn_a_log, dn_dt_bias=v_dn_dt_bias,
             dn_norm_g=v_dn_norm_g, sw_sinks=v_sw_sinks, w_up_dn=v_w_up_dn, w_up_sw=v_w_up_sw, w_o=v_w_o,
             post_mix_g=v_post_mix_g, pre_mlp_g=v_pre_mlp_g, w_ff1=v_w_ff1, w_ff2=v_w_ff2, post_mlp_g=v_post_mlp_g)
    n_layers = pre_mix_g.shape[0]
    xs, pos, tgt = x[0], positions[0], loss_target[0]

    payload = [w[n] if n == "dn_conv_w" else w[n].astype(bf16) for n in _BIG]
    gathered = _exchange("allgather_weights", payload, ["gather2"] * len(_BIG))
    full = {n: _unshard(n, g) for n, g in zip(_BIG, gathered)}
    layers = []
    for l in range(n_layers):
        wl = {n: (full[n][l] if n in _BIG else w[n][l]) for n in _WEIGHTS}
        layers.append(_layer_params(wl))
    cst = _consts(pos)

    h, res = xs, []
    for l in range(n_layers):
        h, r = _layer_fwd(h, layers[l], cst)
        res.append(r)
    dy, sq = _loss(h, tgt)
    loss = lax.psum(0.5 / D_MODEL * jnp.sum(sq), ("x", "y", "c"))
    grads = [None] * n_layers
    for l in reversed(range(n_layers)):
        dy, g = _layer_bwd(dy, res[l], layers[l], cst)
        grads[l] = _layer_grads_ref_layout(g)
    grad_x = dy[None]
    gfull = {n: jnp.stack([grads[l][n] for l in range(n_layers)]) for n in _WEIGHTS}

    nb = len(_BIG)
    send = [_shard_major(n, gfull[n]).astype(bf16) for n in _BIG]
    stage_a = _exchange("grads_to_sibling", send + [_pack_small(gfull)], ["to_sibling"] * nb + ["gather"])
    core = lax.axis_index("c")
    summed = []
    for n, mine, theirs in zip(_BIG, send, stage_a[:nb]):
        own = lax.dynamic_index_in_dim(mine.reshape((N_CHIP, 2) + mine.shape[1:]), core, axis=1, keepdims=False)
        lead = (-1,) + own.shape[-2:]
        summed.append(_add_bf16("add_" + n, own.reshape(lead), theirs.reshape(lead), tr=512).reshape(own.shape))
    landed = _exchange("grads_to_chips", summed, ["to_chips"] * nb)

    out_g, out_d, out_m, out_v = {}, {}, {}, {}
    for n, land in zip(_BIG, landed):
        out_g[n], out_d[n], out_m[n], out_v[n] = _adamw("adamw_" + n, land, w[n], m[n], v[n], tr=256)
    small = _adamw("adamw_small", stage_a[nb][:, None], _pack_small(w)[None], _pack_small(m)[None], _pack_small(v)[None],
                   tr=_SMALL_PACK_ROWS)
    for dst, a in zip((out_g, out_d, out_m, out_v), small):
        dst.update(_unpack_small(a[0], w))
    return (loss, grad_x, *[out_g[n] for n in _WEIGHTS], *[out_d[n] for n in _WEIGHTS],
            *[out_m[n] for n in _WEIGHTS], *[out_v[n] for n in _WEIGHTS])
```

```python
import math

import numpy as np
import jax
import jax.numpy as jnp
from jax import lax
from jax.experimental import pallas as pl
from jax.experimental.pallas import tpu as pltpu

f32 = jnp.float32
bf16 = jnp.bfloat16
HIGHEST = lax.Precision.HIGHEST

N_DEV = 8
D_MODEL = 1024
DN_HEADS = 8
DN_DK = 128
DN_CHUNK = 128
DN_CONV = 4
SW_Q_HEADS = 16
SW_KV_HEADS = 2
SW_HEAD_DIM = 64
SW_BLOCK = 128
ROPE_THETA = 500000.0
ROT_DIM = SW_HEAD_DIM // 4
D_FF = 4 * D_MODEL
EPS = 1e-6
LANES = 128
CONV_HALO = 8
NEG_BIG = -1e30

ADAM_LR = 0.001
ADAM_B1 = 0.9
ADAM_B2 = 0.999
ADAM_EPS = 1e-08
ADAM_WD = 0.01
ADAM_STEP = 10

PROJ_W = 7680
CB_Q, CB_K, CB_V, CB_Z, CB_SWQ, CB_GA, CB_GB = 0, 1, 2, 3, 4, 5, 6
CB_SWK, CB_SWV, CB_BA = 56, 57, 58

NN = ((1,), (0,))
NT = ((1,), (1,))
TN = ((0,), (0,))


def _mm(a, b, dims, hi=False):
    if hi:
        return lax.dot_general(a.astype(f32), b.astype(f32), (dims, ((), ())), precision=HIGHEST,
                               preferred_element_type=f32)
    return lax.dot_general(a.astype(bf16), b.astype(bf16), (dims, ((), ())), preferred_element_type=f32)


def _matmul(name, a, b, form, out_dtype, tm=512, tn=512, tk=1024, extra=(), epilogue=None):
    if form == "nn":
        (M, K), (_, N) = a.shape, b.shape
    elif form == "nt":
        (M, K), (N, _) = a.shape, b.shape
    else:
        (K, M), (_, N) = a.shape, b.shape
    tm, tn, tk = min(tm, M), min(tn, N), min(tk, K)
    assert M % tm == 0 and N % tn == 0 and K % tk == 0, (name, M, N, K, tm, tn, tk)
    nk = K // tk
    dims = {"nn": NN, "nt": NT, "tn": TN}[form]
    out_dtypes = [out_dtype] if epilogue is None else list(out_dtype)
    ne, no = len(extra), len(out_dtypes)

    def body(a_ref, b_ref, *rest):
        e_refs, o_refs, acc_ref = rest[:ne], rest[ne:ne + no], rest[ne + no]

        def finish(acc):
            vals = [acc] if epilogue is None else epilogue(acc, [e[...] for e in e_refs])
            for o, val in zip(o_refs, vals):
                o[...] = val.astype(o.dtype)

        part = lax.dot_general(a_ref[...], b_ref[...], (dims, ((), ())), preferred_element_type=f32)
        if nk == 1:
            finish(part)
        else:
            k = pl.program_id(2)

            @pl.when(k == 0)
            def _():
                acc_ref[...] = part

            @pl.when(k > 0)
            def _():
                acc_ref[...] += part

            @pl.when(k == nk - 1)
            def _():
                finish(acc_ref[...])

    if form == "tn":
        a_spec = pl.BlockSpec((tk, tm), lambda i, j, k: (k, i))
    else:
        a_spec = pl.BlockSpec((tm, tk), lambda i, j, k: (i, k))
    if form == "nt":
        b_spec = pl.BlockSpec((tn, tk), lambda i, j, k: (j, k))
    else:
        b_spec = pl.BlockSpec((tk, tn), lambda i, j, k: (k, j))
    tile = pl.BlockSpec((tm, tn), lambda i, j, k: (i, j))
    res = pl.pallas_call(
        body, name=name,
        grid=(M // tm, N // tn, nk),
        in_specs=[a_spec, b_spec] + [tile] * ne,
        out_specs=[tile] * no,
        out_shape=[jax.ShapeDtypeStruct((M, N), dt) for dt in out_dtypes],
        scratch_shapes=[pltpu.VMEM((tm, tn) if nk > 1 else (8, 128), f32)],
        compiler_params=pltpu.CompilerParams(dimension_semantics=("parallel", "parallel", "arbitrary")),
    )(a, b, *extra)
    return res[0] if epilogue is None else list(res)


def _tile_specs(ins, halo_ids, params, TM, HR, row_of):
    specs = [pl.BlockSpec((TM, w), lambda i, cb=cb: (row_of(i), cb)) for (_, w, cb) in ins]
    for h in halo_ids:
        _, w, cb = ins[h]
        specs.append(pl.BlockSpec((HR, w), lambda i, cb=cb: (jnp.maximum(row_of(i) * (TM // HR) - 1, 0), cb)))
    for p in params:
        specs.append(pl.BlockSpec(p.shape, lambda i, nd=p.ndim: (0,) * nd))
    return specs


def _tile_fwd(name, fn, T, TM, ins, params, outs, halo_ids=(), HR=CONV_HALO):
    TM = min(TM, T)
    n = T // TM
    ni, nh, npar = len(ins), len(halo_ids), len(params)

    def body(*refs):
        in_v = [r[...] for r in refs[:ni]]
        halo_v = [r[...] for r in refs[ni:ni + nh]]
        par_v = [r[...] for r in refs[ni + nh:ni + nh + npar]]
        o_refs = refs[ni + nh + npar:]
        first = pl.program_id(0) == 0
        vals = fn(first, in_v, halo_v, par_v)
        for o, val in zip(o_refs, vals):
            o[...] = val.astype(o.dtype)

    res = pl.pallas_call(
        body, name=name, grid=(n,),
        in_specs=_tile_specs(ins, halo_ids, params, TM, HR, lambda i: i),
        out_specs=[pl.BlockSpec((TM, w), lambda i: (i, 0)) for (w, _) in outs],
        out_shape=[jax.ShapeDtypeStruct((T, w), dt) for (w, dt) in outs],
        compiler_params=pltpu.CompilerParams(dimension_semantics=("arbitrary",)),
    )(*[a for (a, _, _) in ins], *[ins[h][0] for h in halo_ids], *params)
    return list(res)


def _tile_bwd(name, fn, T, TM, ins, params, cts, din, dpar, halo_ids=(), HR=CONV_HALO):
    TM = min(TM, T)
    n = T // TM
    ni, nh, npar, nc = len(ins), len(halo_ids), len(params), len(cts)
    din_ids = [j for (j, _) in din]
    dh_ids = [h for h in halo_ids if h in din_ids]
    nd, ndp, ndh = len(din), len(dpar), len(dh_ids)

    def body(*refs):
        in_v = [r[...] for r in refs[:ni]]
        halo_v = [r[...] for r in refs[ni:ni + nh]]
        par_v = [r[...] for r in refs[ni + nh:ni + nh + npar]]
        ct_v = [r[...].astype(f32) for r in refs[ni + nh + npar:ni + nh + npar + nc]]
        o_refs = refs[ni + nh + npar + nc:ni + nh + npar + nc + nd + ndp]
        carry_refs = refs[ni + nh + npar + nc + nd + ndp:]
        i = pl.program_id(0)
        first = i == n - 1

        def g(d_in, d_halo, d_par):
            full_in = list(in_v)
            for j, val in zip(din_ids, d_in):
                full_in[j] = val
            full_halo = list(halo_v)
            for h, val in zip(dh_ids, d_halo):
                full_halo[list(halo_ids).index(h)] = val
            full_par = list(par_v)
            for j, val in zip(dpar, d_par):
                full_par[j] = val
            return tuple(fn(first, full_in, full_halo, full_par))

        prim = ([in_v[j].astype(f32) for j in din_ids],
                [halo_v[list(halo_ids).index(h)].astype(f32) for h in dh_ids],
                [par_v[j] for j in dpar])
        _, vjp = jax.vjp(g, *prim)
        g_in, g_halo, g_par = vjp(tuple(ct_v))

        @pl.when(i == 0)
        def _():
            for c in carry_refs:
                c[...] = jnp.zeros_like(c)
            for o in o_refs[nd:]:
                o[...] = jnp.zeros_like(o)

        for slot, (j, _) in enumerate(din):
            val = g_in[slot]
            if j in dh_ids:
                c = carry_refs[dh_ids.index(j)]
                val = jnp.concatenate([val[:TM - HR], val[TM - HR:] + c[...]], axis=0) if TM > HR else val + c[...]
                c[...] = g_halo[dh_ids.index(j)]
            o_refs[slot][...] = val.astype(o_refs[slot].dtype)
        for slot in range(ndp):
            o_refs[nd + slot][...] += g_par[slot]

    rev = lambda i: n - 1 - i
    in_specs = _tile_specs(ins, halo_ids, params, TM, HR, rev)
    ct_specs = [pl.BlockSpec((TM, w), lambda i, cb=cb: (rev(i), cb)) for (_, w, cb) in cts]
    out_specs = [pl.BlockSpec((TM, ins[j][1]), lambda i: (rev(i), 0)) for j in din_ids]
    out_specs += [pl.BlockSpec(params[j].shape, lambda i, nd_=params[j].ndim: (0,) * nd_) for j in dpar]
    out_shape = [jax.ShapeDtypeStruct((T, ins[j][1]), dt) for (j, dt) in din]
    out_shape += [jax.ShapeDtypeStruct(params[j].shape, f32) for j in dpar]
    res = pl.pallas_call(
        body, name=name, grid=(n,),
        in_specs=in_specs + ct_specs,
        out_specs=out_specs,
        out_shape=out_shape,
        scratch_shapes=[pltpu.VMEM((HR, ins[h][1]), f32) for h in dh_ids],
        compiler_params=pltpu.CompilerParams(dimension_semantics=("arbitrary",)),
    )(*[a for (a, _, _) in ins], *[ins[h][0] for h in halo_ids], *params, *[a for (a, _, _) in cts])
    return list(res)


def _rms(x, g):
    return x * lax.rsqrt(jnp.mean(x * x, axis=-1, keepdims=True) + EPS) * g


def _fn_prenorm(first, ins, halos, params):
    (x,), (g,) = ins, params
    x = x.astype(f32)
    return [_rms(x, g), x]


def _fn_postmix(first, ins, halos, params):
    (x, mix), (g2, g3) = ins, params
    x1 = x + _rms(mix, g2)
    return [x1, _rms(x1, g3)]


def _fn_postmlp(first, ins, halos, params):
    (x1, ff), (g4,) = ins, params
    return [x1 + _rms(ff, g4)]


def _fn_rms_only(first, ins, halos, params):
    (ff,), (g4,) = ins, params
    return [_rms(ff, g4)]


def _fn_merge(first, ins, halos, params):
    ga, gb, ya, yb = ins
    return [jax.nn.sigmoid(ga) * ya + jax.nn.sigmoid(gb) * yb]


def _make_fn_conv(norm_scale):
    def fn(first, ins, halos, params):
        (x,), (xp,), (w,) = ins, halos, params
        TM = x.shape[0]
        xp = jnp.where(first, 0.0, xp)
        xe = jnp.concatenate([xp, x], axis=0)
        off = CONV_HALO - (DN_CONV - 1)
        y = xe[off:off + TM] * w[0:1]
        for j in range(1, DN_CONV):
            y = y + xe[off + j:off + j + TM] * w[j:j + 1]
        y = jax.nn.silu(y)
        if norm_scale is None:
            return [y]
        outs = []
        for h in range(DN_HEADS):
            yh = y[:, DN_DK * h:DN_DK * (h + 1)]
            outs.append(yh * lax.rsqrt(jnp.sum(yh * yh, axis=-1, keepdims=True) + EPS) * norm_scale)
        return [jnp.concatenate(outs, axis=-1)]
    return fn


def _fn_gates(first, ins, halos, params):
    (ba,), (avec, dvec) = ins, params
    lane = lax.broadcasted_iota(jnp.int32, ba.shape, 1)
    beta = jax.nn.sigmoid(ba)
    g = -jnp.exp(avec) * jax.nn.softplus(ba + dvec)
    return [jnp.where(lane < DN_HEADS, beta, jnp.where(lane < 2 * DN_HEADS, g, 0.0))]


def _fn_dnpost(first, ins, halos, params):
    (o, z), (ng,) = ins, params
    outs = []
    for h in range(DN_HEADS):
        sl = slice(DN_DK * h, DN_DK * (h + 1))
        outs.append(_rms(o[:, sl], ng) * jax.nn.silu(z[:, sl]))
    return [jnp.concatenate(outs, axis=-1)]


def _roll_lanes(x, shift):
    return pltpu.roll(x, shift, 1)


_lane_roll = jax.custom_vjp(_roll_lanes, nondiff_argnums=(1,))
_lane_roll.defvjp(lambda x, shift: (_roll_lanes(x, shift), None),
                  lambda shift, _, ct: (_roll_lanes(ct, LANES - shift),))


def _attn_probs(qp, kx, sink, mask):
    heads = range(len(sink))
    s = [jnp.where(mask, _mm(qp[g // 2], kx[g % 2], NT), NEG_BIG) for g in heads]
    m = [jnp.maximum(jnp.max(s[g], axis=-1, keepdims=True), sink[g]) for g in heads]
    p = [jnp.exp(s[g] - m[g]) for g in heads]
    ps = [jnp.exp(sink[g] - m[g]) for g in heads]
    inv = [1.0 / (jnp.sum(p[g], axis=-1, keepdims=True) + ps[g]) for g in heads]
    return [p[g] * inv[g] for g in heads], [ps[g] * inv[g] for g in heads]


@jax.custom_vjp
def _attn_group(qp, kx, vx, sink, mask):
    probs, _ = _attn_probs(qp, kx, sink, mask)
    return tuple(_mm(probs[g], vx[g % 2], NN) for g in range(len(sink)))


def _attn_group_fwd(qp, kx, vx, sink, mask):
    o = _attn_group(qp, kx, vx, sink, mask)
    return o, (qp, kx, vx, sink, mask, o)


def _attn_group_bwd(res, do):
    qp, kx, vx, sink, mask, o = res
    heads = range(len(sink))
    probs, p_sink = _attn_probs(qp, kx, sink, mask)
    d_probs = [_mm(do[g], vx[g % 2], NT) for g in heads]
    dot = [jnp.sum(do[g] * o[g], axis=-1, keepdims=True) for g in heads]
    ds = [probs[g] * (d_probs[g] - dot[g]) for g in heads]
    d_qp = tuple(_mm(ds[2 * j], kx[0], NN) + _mm(ds[2 * j + 1], kx[1], NN) for j in range(len(qp)))
    rows = lambda xs: jnp.concatenate(xs, axis=0)
    d_kx = tuple(_mm(rows([ds[g] for g in heads if g % 2 == c]), rows([qp[g // 2] for g in heads if g % 2 == c]), TN)
                 for c in range(2))
    d_vx = tuple(_mm(rows([probs[g] for g in heads if g % 2 == c]), rows([do[g] for g in heads if g % 2 == c]), TN)
                 for c in range(2))
    d_sink = tuple(-jnp.sum(p_sink[g] * dot[g], axis=0, keepdims=True) for g in heads)
    return d_qp, d_kx, d_vx, d_sink, None


_attn_group.defvjp(_attn_group_fwd, _attn_group_bwd)


def _fn_swa(first, ins, halos, params):
    q, k, v, cos, sin = ins
    kp, vp, cosp, sinp = halos
    sinks, sel_a0, sel_b0, sel_a1, sel_b1 = params
    B = q.shape[0]
    half = ROT_DIM // 2
    in_head = jnp.bitwise_and(lax.broadcasted_iota(jnp.int32, (1, LANES), 1), SW_HEAD_DIM - 1)

    def rope(x, c, s):
        return (x * c + _lane_roll(x, LANES - half) * jnp.where(in_head < half, -s, 0.0)
                + _lane_roll(x, half) * jnp.where(in_head >= half, s, 0.0))

    kcat = jnp.concatenate([rope(kp, cosp, sinp), rope(k, cos, sin)], axis=0)
    vcat = jnp.concatenate([vp, v], axis=0)
    r = lax.broadcasted_iota(jnp.int32, (B, 2 * B), 0)
    c = lax.broadcasted_iota(jnp.int32, (B, 2 * B), 1)
    mask = (c > r) & (c <= r + B) & ((c >= B) | jnp.logical_not(first))
    group = SW_Q_HEADS // SW_KV_HEADS
    sels = ((sel_a0, sel_b0), (sel_a1, sel_b1))
    outs = []
    for hk in range(SW_KV_HEADS):
        heads = range(group)
        kx = [_mm(kcat, sels[hk][half], NN) for half in range(2)]
        vx = [_mm(vcat, sels[hk][half], NN) for half in range(2)]
        qp = [rope(q[:, LANES * j:LANES * (j + 1)], cos, sin) * (SW_HEAD_DIM ** -0.5)
              for j in range(hk * group // 2, (hk + 1) * group // 2)]
        sink = tuple(sinks[:, hk * group + g:hk * group + g + 1] for g in heads)
        o = _attn_group(tuple(qp), tuple(kx), tuple(vx), sink, mask)
        outs += [o[2 * j] + o[2 * j + 1] for j in range(group // 2)]
    return [jnp.concatenate(outs, axis=-1)]


@jax.custom_vjp
def _inv_unit_lower(Ls):
    C = Ls[0].shape[0]
    ii = lax.broadcasted_iota(jnp.int32, (C, C), 0)
    jj = lax.broadcasted_iota(jnp.int32, (C, C), 1)

    def off_mask(level):
        same_pair = jnp.right_shift(ii, level + 1) == jnp.right_shift(jj, level + 1)
        lower_left = (jnp.bitwise_and(jnp.right_shift(ii, level), 1) == 1) & (jnp.bitwise_and(jnp.right_shift(jj, level), 1) == 0)
        return same_pair & lower_left

    eye = (ii == jj).astype(f32)
    m0 = off_mask(0)
    Ts = [eye - jnp.where(m0, L, 0.0) for L in Ls]
    for level in range(1, int(math.log2(C))):
        mk = off_mask(level)
        left = [_mm(T_, jnp.where(mk, L, 0.0), NN) for T_, L in zip(Ts, Ls)]
        Ts = [T_ - _mm(a, T_, NN) for a, T_ in zip(left, Ts)]
    return tuple(Ts)


def _inv_fwd(Ls):
    Ts = _inv_unit_lower(Ls)
    return Ts, Ts


def _inv_bwd(Ts, dTs):
    left = [_mm(T_, dT, TN) for T_, dT in zip(Ts, dTs)]
    return (tuple(-_mm(a, T_, NT) for a, T_ in zip(left, Ts)),)


_inv_unit_lower.defvjp(_inv_fwd, _inv_bwd)


@jax.custom_vjp
def _inv_known(Ls, Ts):
    return Ts


_inv_known.defvjp(lambda Ls, Ts: (Ts, Ts),
                  lambda Ts, dTs: (_inv_bwd(Ts, dTs)[0], tuple(jnp.zeros_like(t) for t in Ts)))


def _mm_01(a, b, dims):
    hi = b.astype(bf16)
    r1 = b - hi.astype(f32)
    mid = r1.astype(bf16)
    lo = (r1 - mid.astype(f32)).astype(bf16)
    a16 = a.astype(bf16)
    dot = lambda part: lax.dot_general(a16, part, (dims, ((), ())), preferred_element_type=f32)
    return dot(hi) + dot(mid) + dot(lo)


def _eye(n):
    return lax.broadcasted_iota(jnp.int32, (n, n), 0) == lax.broadcasted_iota(jnp.int32, (n, n), 1)


def _lower(n):
    return lax.broadcasted_iota(jnp.int32, (n, n), 0) >= lax.broadcasted_iota(jnp.int32, (n, n), 1)


@jax.custom_vjp
def _transpose(x):
    return _mm_01(_eye(x.shape[1]), x, NT)


_transpose.defvjp(lambda x: (_transpose(x), None), lambda _, ct: (_transpose(ct),))


@jax.custom_vjp
def _cumsum_rows(x):
    return _mm_01(_lower(x.shape[0]), x, NN)


_cumsum_rows.defvjp(lambda x: (_cumsum_rows(x), None), lambda _, ct: (_mm_01(_lower(ct.shape[0]), ct, TN),))


def _dn_chunk(q, k, v, gb, S, tinv_known=None):
    C = q.shape[0]
    H = range(DN_HEADS)
    ii = lax.broadcasted_iota(jnp.int32, (C, C), 0)
    jj = lax.broadcasted_iota(jnp.int32, (C, C), 1)
    causal, strict = ii >= jj, ii > jj
    gc_all = _cumsum_rows(gb)
    gc_t = _transpose(gc_all)
    sl = [slice(DN_DK * h, DN_DK * (h + 1)) for h in H]
    qs, ks, vs = [q[:, s] for s in sl], [k[:, s] for s in sl], [v[:, s] for s in sl]
    beta = [gb[:, h:h + 1] for h in H]
    gcol = [gc_all[:, DN_HEADS + h:DN_HEADS + h + 1] for h in H]
    grow = [gc_t[DN_HEADS + h:DN_HEADS + h + 1, :] for h in H]
    decay = [jnp.where(causal, jnp.exp(jnp.where(causal, gcol[h] - grow[h], 0.0)), 0.0) for h in H]
    kb = [ks[h] * beta[h] for h in H]
    kk = [_mm(kb[h], ks[h], NT) for h in H]
    qk = [_mm(qs[h], ks[h], NT) for h in H]
    Ls = tuple(jnp.where(strict, kk[h] * decay[h], 0.0) for h in H)
    tinv = _inv_unit_lower(Ls) if tinv_known is None else _inv_known(Ls, tinv_known)
    eg = [jnp.exp(gcol[h]) for h in H]
    u = [_mm(tinv[h], vs[h] * beta[h], NN) for h in H]
    w = [_mm(tinv[h], kb[h] * eg[h], NN) for h in H]
    gl = [gcol[h][C - 1:C, :] for h in H]
    ws = [_mm(w[h], S[h], NN) for h in H]
    qS = [_mm(qs[h] * eg[h], S[h], NN) for h in H]
    v_new = [u[h] - ws[h] for h in H]
    av = [_mm(qk[h] * decay[h], v_new[h], NN) for h in H]
    kv = [_mm(ks[h] * jnp.exp(gl[h] - gcol[h]), v_new[h], TN) for h in H]
    o = jnp.concatenate([qS[h] + av[h] for h in H], axis=-1)
    return o, tuple(S[h] * jnp.exp(gl[h]) + kv[h] for h in H), tinv


_DN_W = DN_HEADS * DN_DK


def _dn_inputs(first, xq, xk, xv, hq, hk, hv, ba, cw, avec, dvec):
    conv = lambda scale, x, h, j: _make_fn_conv(scale)(first, [x], [h], [cw[:, _DN_W * j:_DN_W * (j + 1)]])[0]
    return (conv(DN_DK ** -0.5, xq, hq, 0), conv(1.0, xk, hk, 1), conv(None, xv, hv, 2),
            _fn_gates(first, [ba], [], [avec, dvec])[0])


def _delta_specs(C, row_of):
    cols = (CB_Q, CB_K, CB_V)
    specs = [pl.BlockSpec((C, _DN_W), lambda i, cb=cb: (row_of(i), cb)) for cb in cols]
    specs += [pl.BlockSpec((CONV_HALO, _DN_W), lambda i, cb=cb: (jnp.maximum(row_of(i) * (C // CONV_HALO) - 1, 0), cb))
              for cb in cols]
    specs.append(pl.BlockSpec((C, LANES), lambda i: (row_of(i), CB_BA)))
    return specs


def _whole(a):
    return pl.BlockSpec(a.shape, lambda i, nd=a.ndim: (0,) * nd)


def _delta_fwd(proj, cw, avec, dvec):
    T = proj.shape[0]
    C = min(DN_CHUNK, T)
    n = T // C

    def body(xq, xk, xv, hq, hk, hv, ba, cw_ref, a_ref, d_ref, o_ref, hist_ref, tinv_ref,
             q_ref, k_ref, v_ref, gb_ref, s_ref):
        first = pl.program_id(0) == 0

        @pl.when(first)
        def _():
            s_ref[...] = jnp.zeros_like(s_ref)

        S = tuple(s_ref[h] for h in range(DN_HEADS))
        for h in range(DN_HEADS):
            hist_ref[0, h] = S[h]
        q, k, v, gb = _dn_inputs(first, xq[...], xk[...], xv[...], hq[...], hk[...], hv[...], ba[...],
                                 cw_ref[...], a_ref[...], d_ref[...])
        q_ref[...], k_ref[...], v_ref[...], gb_ref[...] = q, k, v, gb
        o, s_new, tinv = _dn_chunk(q, k, v, gb, S)
        o_ref[...] = o
        for h in range(DN_HEADS):
            s_ref[h] = s_new[h]
            tinv_ref[0, h] = tinv[h]

    row = pl.BlockSpec((C, _DN_W), lambda i: (i, 0))
    return pl.pallas_call(
        body, name="delta_fwd", grid=(n,),
        in_specs=_delta_specs(C, lambda i: i) + [_whole(cw), _whole(avec), _whole(dvec)],
        out_specs=[row, pl.BlockSpec((1, DN_HEADS, DN_DK, DN_DK), lambda i: (i, 0, 0, 0)),
                   pl.BlockSpec((1, DN_HEADS, C, C), lambda i: (i, 0, 0, 0)),
                   row, row, row, pl.BlockSpec((C, LANES), lambda i: (i, 0))],
        out_shape=[jax.ShapeDtypeStruct((T, _DN_W), f32), jax.ShapeDtypeStruct((n, DN_HEADS, DN_DK, DN_DK), f32),
                   jax.ShapeDtypeStruct((n, DN_HEADS, C, C), f32)]
                  + [jax.ShapeDtypeStruct((T, _DN_W), f32)] * 3 + [jax.ShapeDtypeStruct((T, LANES), f32)],
        scratch_shapes=[pltpu.VMEM((DN_HEADS, DN_DK, DN_DK), f32)],
        compiler_params=pltpu.CompilerParams(dimension_semantics=("arbitrary",)),
    )(proj, proj, proj, proj, proj, proj, proj, cw, avec, dvec)


def _delta_bwd(qn, kn, vv, gb, hist, tinv, do):
    T = qn.shape[0]
    C = min(DN_CHUNK, T)
    n = T // C

    def body(q_ref, k_ref, v_ref, gb_ref, hist_ref, tinv_ref, do_ref, dq_ref, dk_ref, dv_ref, dgb_ref, ds_ref):
        @pl.when(pl.program_id(0) == 0)
        def _():
            ds_ref[...] = jnp.zeros_like(ds_ref)

        S = tuple(hist_ref[0, h] for h in range(DN_HEADS))
        known = tuple(tinv_ref[0, h] for h in range(DN_HEADS))
        chunk = lambda q, k, v, g, s: _dn_chunk(q, k, v, g, s, tinv_known=known)[:2]
        _, vjp = jax.vjp(chunk, q_ref[...], k_ref[...], v_ref[...], gb_ref[...], S)
        dS = tuple(ds_ref[h] for h in range(DN_HEADS))
        dq, dk, dv, dgb, dS_in = vjp((do_ref[...], dS))
        dq_ref[...] = dq
        dk_ref[...] = dk
        dv_ref[...] = dv
        dgb_ref[...] = dgb
        for h in range(DN_HEADS):
            ds_ref[h] = dS_in[h]

    row = pl.BlockSpec((C, _DN_W), lambda i: (n - 1 - i, 0))
    small = pl.BlockSpec((C, LANES), lambda i: (n - 1 - i, 0))
    return pl.pallas_call(
        body, name="delta_bwd", grid=(n,),
        in_specs=[row, row, row, small, pl.BlockSpec((1, DN_HEADS, DN_DK, DN_DK), lambda i: (n - 1 - i, 0, 0, 0)),
                  pl.BlockSpec((1, DN_HEADS, C, C), lambda i: (n - 1 - i, 0, 0, 0)), row],
        out_specs=[row, row, row, small],
        out_shape=[jax.ShapeDtypeStruct((T, _DN_W), f32)] * 3 + [jax.ShapeDtypeStruct((T, LANES), f32)],
        scratch_shapes=[pltpu.VMEM((DN_HEADS, DN_DK, DN_DK), f32)],
        compiler_params=pltpu.CompilerParams(dimension_semantics=("arbitrary",)),
    )(qn, kn, vv, gb, hist, tinv, do)


TM_ROW = 256
W1 = D_MODEL


def _first_only(fn):
    return lambda *a: fn(*a)[:1]


def _swa_args(proj, cst):
    ins = [(proj, W1, CB_SWQ), (proj, LANES, CB_SWK), (proj, LANES, CB_SWV), (cst["cos"], LANES, 0), (cst["sin"], LANES, 0)]
    return ins, (1, 2, 3, 4)


def _layer_fwd(x, p, cst):
    T = x.shape[0]
    r = {"x": x}
    (h,) = _tile_fwd("prenorm", _first_only(_fn_prenorm), T, TM_ROW, [(x, W1, 0)], [p["g1"]], [(W1, bf16)])
    proj = _matmul("proj", h, p["w_in"], "nn", f32, tm=1024, tn=1536)
    o, hist, tinv, qn, kn, vv, gbt = _delta_fwd(proj, p["conv_w"], p["avec"], p["dvec"])
    (dn_out,) = _tile_fwd("dnpost", _fn_dnpost, T, TM_ROW, [(o, W1, 0), (proj, W1, CB_Z)], [p["ng"]], [(W1, bf16)])
    sw_ins, sw_halo = _swa_args(proj, cst)
    sw_par = [p["sinks"], cst["sel_a0"], cst["sel_b0"], cst["sel_a1"], cst["sel_b1"]]
    (sw_out,) = _tile_fwd("swa", _fn_swa, T, SW_BLOCK, sw_ins, sw_par, [(W1, bf16)], halo_ids=sw_halo, HR=SW_BLOCK)
    y_a = _matmul("up_dn", dn_out, p["w_up_dn"], "nn", f32, tm=1024, tn=1024)
    y_b = _matmul("up_sw", sw_out, p["w_up_sw"], "nn", f32, tm=1024, tn=1024)
    (gated,) = _tile_fwd("merge", _fn_merge, T, TM_ROW,
                         [(proj, W1, CB_GA), (proj, W1, CB_GB), (y_a, W1, 0), (y_b, W1, 0)], [], [(W1, bf16)])
    mix = _matmul("w_o", gated, p["w_o"], "nn", f32, tm=1024, tn=1024)
    x1, h2 = _tile_fwd("postmix", _fn_postmix, T, TM_ROW, [(x, W1, 0), (mix, W1, 0)], [p["g2"], p["g3"]],
                       [(W1, f32), (W1, bf16)])
    ffh, act = _matmul("ff1", h2, p["w_ff1"], "nn", [f32, bf16], tm=1024, tn=1024,
                       epilogue=lambda acc, ex: [acc, jnp.square(jnp.maximum(acc, 0.0))])
    ff = _matmul("ff2", act, p["w_ff2"], "nn", f32, tm=1024, tn=1024)
    (x2,) = _tile_fwd("postmlp", _fn_postmlp, T, TM_ROW, [(x1, W1, 0), (ff, W1, 0)], [p["g4"]], [(W1, f32)])
    r.update(h=h, proj=proj, qn=qn, kn=kn, vv=vv, gbt=gbt, o=o, hist=hist, tinv=tinv, dn_out=dn_out, sw_out=sw_out,
             y_a=y_a, y_b=y_b, gated=gated, mix=mix, h2=h2, ffh=ffh, act=act, ff=ff)
    return x2, r


def _layer_bwd(dx2, r, p, cst):
    T = dx2.shape[0]
    x, proj = r["x"], r["proj"]
    g = {}
    dff, g["g4"] = _tile_bwd("postmlp_b", _fn_rms_only, T, TM_ROW, [(r["ff"], W1, 0)], [p["g4"]], [(dx2, W1, 0)],
                             [(0, bf16)], [0])
    (dffh,) = _matmul("ff2_dx", dff, p["w_ff2"], "nt", [bf16], tm=1024, tn=1024, extra=[r["ffh"]],
                      epilogue=lambda acc, ex: [acc * (2.0 * jnp.maximum(ex[0], 0.0))])
    g["w_ff2"] = _matmul("ff2_dw", r["act"], dff, "tn", bf16, tm=1024, tn=1024)
    dh2 = _matmul("ff1_dx", dffh, p["w_ff1"], "nt", f32, tm=1024, tn=1024)
    g["w_ff1"] = _matmul("ff1_dw", r["h2"], dffh, "tn", bf16, tm=1024, tn=1024)
    dx1, dmix, g["g2"], g["g3"] = _tile_bwd("postmix_b", _fn_postmix, T, TM_ROW, [(x, W1, 0), (r["mix"], W1, 0)],
                                            [p["g2"], p["g3"]], [(dx2, W1, 0), (dh2, W1, 0)], [(0, f32), (1, bf16)], [0, 1])
    dgated = _matmul("w_o_dx", dmix, p["w_o"], "nt", f32, tm=1024, tn=1024)
    g["w_o"] = _matmul("w_o_dw", r["gated"], dmix, "tn", bf16, tm=1024, tn=1024)
    dga, dgb, dya, dyb = _tile_bwd("merge_b", _fn_merge, T, TM_ROW,
                                   [(proj, W1, CB_GA), (proj, W1, CB_GB), (r["y_a"], W1, 0), (r["y_b"], W1, 0)], [],
                                   [(dgated, W1, 0)], [(0, bf16), (1, bf16), (2, bf16), (3, bf16)], [])
    d_dn = _matmul("up_dn_dx", dya, p["w_up_dn"], "nt", f32, tm=1024, tn=1024)
    g["w_up_dn"] = _matmul("up_dn_dw", r["dn_out"], dya, "tn", bf16, tm=1024, tn=1024)
    d_sw = _matmul("up_sw_dx", dyb, p["w_up_sw"], "nt", f32, tm=1024, tn=1024)
    g["w_up_sw"] = _matmul("up_sw_dw", r["sw_out"], dyb, "tn", bf16, tm=1024, tn=1024)
    do, dz, g["ng"] = _tile_bwd("dnpost_b", _fn_dnpost, T, TM_ROW, [(r["o"], W1, 0), (proj, W1, CB_Z)], [p["ng"]],
                                [(d_dn, W1, 0)], [(0, f32), (1, bf16)], [0])
    dqn, dkn, dvv, dgbt = _delta_bwd(r["qn"], r["kn"], r["vv"], r["gbt"], r["hist"], r["tinv"], do)
    conv_b = lambda nm, cb, scale, ct: _tile_bwd(nm, _make_fn_conv(scale), T, TM_ROW, [(proj, W1, cb)],
                                                 [p["conv_w"][:, W1 * cb:W1 * (cb + 1)]], [(ct, W1, 0)], [(0, bf16)], [0],
                                                 halo_ids=(0,))
    dq_in, dcw_q = conv_b("conv_q_b", CB_Q, DN_DK ** -0.5, dqn)
    dk_in, dcw_k = conv_b("conv_k_b", CB_K, 1.0, dkn)
    dv_in, dcw_v = conv_b("conv_v_b", CB_V, None, dvv)
    g["conv_w"] = jnp.concatenate([dcw_q, dcw_k, dcw_v], axis=-1)
    dba, g["avec"], g["dvec"] = _tile_bwd("gates_b", _fn_gates, T, TM_ROW, [(proj, LANES, CB_BA)], [p["avec"], p["dvec"]],
                                          [(dgbt, LANES, 0)], [(0, bf16)], [0, 1])
    sw_ins, sw_halo = _swa_args(proj, cst)
    sw_par = [p["sinks"], cst["sel_a0"], cst["sel_b0"], cst["sel_a1"], cst["sel_b1"]]
    dswq, dswk, dswv, g["sinks"] = _tile_bwd("swa_b", _fn_swa, T, SW_BLOCK, sw_ins, sw_par, [(d_sw, W1, 0)],
                                             [(0, bf16), (1, bf16), (2, bf16)], [0], halo_ids=sw_halo, HR=SW_BLOCK)
    dproj = jnp.concatenate([dq_in, dk_in, dv_in, dz, dswq, dga, dgb, dswk, dswv, dba, jnp.zeros((T, LANES), bf16)], axis=-1)
    dh = _matmul("proj_dx", dproj, p["w_in"], "nt", f32, tm=1024, tn=1024, tk=1536)
    g["w_in"] = _matmul("proj_dw", r["h"], dproj, "tn", bf16, tm=1024, tn=1536)
    dx, g["g1"] = _tile_bwd("prenorm_b", _fn_prenorm, T, TM_ROW, [(x, W1, 0)], [p["g1"]], [(dh, W1, 0), (dx1, W1, 0)],
                            [(0, f32)], [0])
    return dx, g


_OFF_BA, _OFF_SWQ, _OFF_SWK, _OFF_GA, _D_IN = 4096, 4112, 5136, 5392, 7440


def _proj_cols(w):
    pad = lambda n: jnp.zeros(w.shape[:-1] + (n,), w.dtype)
    return jnp.concatenate([w[..., :_OFF_BA], w[..., _OFF_SWQ:_OFF_SWK], w[..., _OFF_GA:_D_IN],
                            w[..., _OFF_SWK:_OFF_GA], w[..., _OFF_BA:_OFF_SWQ], pad(PROJ_W - _D_IN)], axis=-1)


def _proj_cols_inv(w):
    n_ba = _OFF_SWQ - _OFF_BA
    return jnp.concatenate([w[..., :4096], w[..., 7424:7424 + n_ba], w[..., 4096:5120], w[..., 7168:7424],
                            w[..., 5120:7168]], axis=-1)


def _lane_pad(v, at):
    return jnp.pad(v.astype(f32), (at, LANES - at - v.shape[0])).reshape(1, LANES)


def _layer_params(w):
    row = lambda v: v.reshape(1, -1).astype(f32)
    return dict(
        g1=row(w["pre_mix_g"]), g2=row(w["post_mix_g"]), g3=row(w["pre_mlp_g"]), g4=row(w["post_mlp_g"]),
        w_in=_proj_cols(w["w_in"]).astype(bf16), conv_w=w["dn_conv_w"].astype(f32),
        avec=_lane_pad(w["dn_a_log"], DN_HEADS), dvec=_lane_pad(w["dn_dt_bias"], DN_HEADS),
        ng=row(w["dn_norm_g"]), sinks=_lane_pad(w["sw_sinks"], 0),
        w_up_dn=w["w_up_dn"].astype(bf16), w_up_sw=w["w_up_sw"].astype(bf16), w_o=w["w_o"].astype(bf16),
        w_ff1=w["w_ff1"].astype(bf16), w_ff2=w["w_ff2"].astype(bf16))


def _layer_grads_ref_layout(g):
    return dict(
        pre_mix_g=g["g1"][0], post_mix_g=g["g2"][0], pre_mlp_g=g["g3"][0], post_mlp_g=g["g4"][0],
        w_in=_proj_cols_inv(g["w_in"]), dn_conv_w=g["conv_w"],
        dn_a_log=g["avec"][0, DN_HEADS:2 * DN_HEADS], dn_dt_bias=g["dvec"][0, DN_HEADS:2 * DN_HEADS],
        dn_norm_g=g["ng"][0], sw_sinks=g["sinks"][0, :SW_Q_HEADS],
        w_up_dn=g["w_up_dn"], w_up_sw=g["w_up_sw"], w_o=g["w_o"], w_ff1=g["w_ff1"], w_ff2=g["w_ff2"])


def _consts(positions):
    T = positions.shape[0]
    half = ROT_DIM // 2
    inv_freq = ROPE_THETA ** (-jnp.arange(half, dtype=f32) * (2.0 / ROT_DIM))
    ang = positions.astype(f32)[:, None] * inv_freq
    cos8, sin8 = jnp.cos(ang), jnp.sin(ang)
    rest = SW_HEAD_DIM - ROT_DIM
    c64 = jnp.concatenate([cos8, cos8, jnp.ones((T, rest), f32)], axis=-1)
    s64 = jnp.concatenate([sin8, sin8, jnp.zeros((T, rest), f32)], axis=-1)
    sel = np.zeros((2, 2, LANES, LANES), np.float32)
    for hk in range(SW_KV_HEADS):
        for d in range(SW_HEAD_DIM):
            sel[hk, 0, SW_HEAD_DIM * hk + d, d] = 1.0
            sel[hk, 1, SW_HEAD_DIM * hk + d, SW_HEAD_DIM + d] = 1.0
    return dict(cos=jnp.concatenate([c64, c64], axis=-1), sin=jnp.concatenate([s64, s64], axis=-1),
                sel_a0=jnp.asarray(sel[0, 0]), sel_b0=jnp.asarray(sel[0, 1]),
                sel_a1=jnp.asarray(sel[1, 0]), sel_b1=jnp.asarray(sel[1, 1]))


def _loss(y, tgt):
    T, W = y.shape
    TM = min(TM_ROW, T)
    n = T // TM

    def body(y_ref, t_ref, dy_ref, acc_ref):
        @pl.when(pl.program_id(0) == 0)
        def _():
            acc_ref[...] = jnp.zeros_like(acc_ref)

        d = y_ref[...] - t_ref[...]
        dy_ref[...] = d * (1.0 / W)
        acc_ref[...] += jnp.sum(d * d, axis=0, keepdims=True)

    row = pl.BlockSpec((TM, W), lambda i: (i, 0))
    return pl.pallas_call(
        body, name="loss", grid=(n,), in_specs=[row, row],
        out_specs=[row, pl.BlockSpec((1, W), lambda i: (0, 0))],
        out_shape=[jax.ShapeDtypeStruct((T, W), f32), jax.ShapeDtypeStruct((1, W), f32)],
        compiler_params=pltpu.CompilerParams(dimension_semantics=("arbitrary",)),
    )(y, tgt)


N_CHIP = 4
N_SEM = N_DEV - 1


def _exchange(name, arrs, modes):
    n = len(arrs)
    lead = {"gather": N_DEV, "gather2": N_DEV, "to_sibling": N_CHIP, "to_chips": N_CHIP}
    outs_shape = [jax.ShapeDtypeStruct((lead[md],) + (a.shape if md.startswith("gather") else a.shape[1:]), a.dtype)
                  for a, md in zip(arrs, modes)]

    def body(*refs):
        ins, outs = refs[:n], refs[n:2 * n]
        send_sems, recv_sems, loc_sems = refs[2 * n:]
        x, y, c = lax.axis_index("x"), lax.axis_index("y"), lax.axis_index("c")
        me, chip, sib = 4 * x + 2 * y + c, 2 * x + y, (x, y, 1 - c)
        flips = [(1 - x, y), (x, 1 - y), (1 - x, 1 - y)]

        def rcopy(a, k, src, dst, dev):
            return pltpu.make_async_remote_copy(src_ref=src, dst_ref=dst, send_sem=send_sems.at[a, k],
                                                recv_sem=recv_sems.at[a, k], device_id=dev,
                                                device_id_type=pl.DeviceIdType.MESH)

        sends, recvs, local, passes = [], [], [], []
        for a, md in enumerate(modes):
            src_all, out = ins[a], outs[a]
            if md in ("gather", "gather2"):
                local.append(pltpu.make_async_copy(src_all, out.at[me], loc_sems.at[a]))
            if md == "to_chips":
                local.append(pltpu.make_async_copy(src_all.at[chip], out.at[chip], loc_sems.at[a]))
            if md == "gather":
                for k in range(1, N_DEV):
                    px = 1 - x if (k >> 2) & 1 else x
                    py = 1 - y if (k >> 1) & 1 else y
                    pc = 1 - c if k & 1 else c
                    sends.append(rcopy(a, k - 1, src_all, out.at[me], (px, py, pc)))
                    recvs.append(rcopy(a, k - 1, src_all, out.at[4 * px + 2 * py + pc], (px, py, pc)))
            elif md == "gather2":
                sends.append(rcopy(a, 0, src_all, out.at[me], sib))
                recvs.append(rcopy(a, 0, src_all, out.at[4 * x + 2 * y + 1 - c], sib))
                for j, (px, py) in enumerate(flips):
                    sends.append(rcopy(a, 1 + j, src_all, out.at[me], (px, py, c)))
                    theirs = out.at[4 * px + 2 * py + c]
                    arrive = rcopy(a, 1 + j, src_all, theirs, (px, py, c))
                    passes.append((arrive, rcopy(a, 4 + j, theirs, theirs, sib)))
                    recvs.append(rcopy(a, 4 + j, src_all, out.at[4 * px + 2 * py + 1 - c], sib))
            elif md == "to_sibling":
                for j in range(N_CHIP):
                    sends.append(rcopy(a, j, src_all.at[2 * j + 1 - c], out.at[j], sib))
                    recvs.append(rcopy(a, j, src_all.at[2 * j + c], out.at[j], sib))
            elif md == "to_chips":
                for j, (px, py) in enumerate(flips):
                    sends.append(rcopy(a, j, src_all.at[2 * px + py], out.at[chip], (px, py, c)))
                    recvs.append(rcopy(a, j, src_all.at[chip], out.at[2 * px + py], (px, py, c)))
        for cp in local + sends:
            cp.start()
        for arrive, onward in passes:
            arrive.wait_recv()
            onward.start()
        for cp in recvs:
            cp.wait_recv()
        for cp in sends:
            cp.wait_send()
        for _, onward in passes:
            onward.wait_send()
        for cp in local:
            cp.wait()

    hbm = pl.BlockSpec(memory_space=pltpu.HBM)
    res = pl.pallas_call(
        body, name=name,
        in_specs=[hbm] * n, out_specs=[hbm] * n, out_shape=outs_shape,
        scratch_shapes=[pltpu.SemaphoreType.DMA((n, N_SEM)), pltpu.SemaphoreType.DMA((n, N_SEM)),
                        pltpu.SemaphoreType.DMA((n,))],
    )(*arrs)
    return list(res)


def _add_bf16(name, a, b, tr):
    L_, R_, C_ = a.shape
    tr = min(tr, R_)
    assert R_ % tr == 0, (name, R_, tr)

    def body(a_ref, b_ref, o_ref):
        o_ref[...] = (a_ref[...].astype(f32) + b_ref[...].astype(f32)).astype(bf16)

    blk = pl.BlockSpec((1, tr, C_), lambda l, i: (l, i, 0))
    return pl.pallas_call(
        body, name=name, grid=(L_, R_ // tr), in_specs=[blk, blk], out_specs=blk,
        out_shape=jax.ShapeDtypeStruct(a.shape, bf16),
        compiler_params=pltpu.CompilerParams(dimension_semantics=("arbitrary", "arbitrary")),
    )(a, b)


def _adamw(name, land, w, m, v, tr):
    L_, R_, C_ = w.shape
    n_slots = land.shape[0]
    tr = min(tr, R_)
    assert R_ % tr == 0, (name, R_, tr)
    c1 = 1.0 - ADAM_B1 ** ADAM_STEP
    c2 = 1.0 - ADAM_B2 ** ADAM_STEP

    def body(l_ref, w_ref, m_ref, v_ref, g_ref, d_ref, mo_ref, vo_ref):
        g = l_ref[0].astype(f32)
        for s in range(1, n_slots):
            g = g + l_ref[s].astype(f32)
        m_new = ADAM_B1 * m_ref[...] + (1.0 - ADAM_B1) * g
        v_new = ADAM_B2 * v_ref[...] + (1.0 - ADAM_B2) * jnp.square(g)
        m_hat = m_new / c1
        v_hat = v_new / c2
        g_ref[...] = g
        d_ref[...] = -ADAM_LR * (m_hat / (jnp.sqrt(v_hat) + ADAM_EPS) + ADAM_WD * w_ref[...])
        mo_ref[...] = m_new
        vo_ref[...] = v_new

    row = pl.BlockSpec((1, tr, C_), lambda l, i: (l, i, 0))
    return pl.pallas_call(
        body, name=name, grid=(L_, R_ // tr),
        in_specs=[pl.BlockSpec((n_slots, 1, tr, C_), lambda l, i: (0, l, i, 0)), row, row, row],
        out_specs=[row] * 4, out_shape=[jax.ShapeDtypeStruct((L_, R_, C_), f32)] * 4,
        compiler_params=pltpu.CompilerParams(dimension_semantics=("arbitrary", "arbitrary")),
    )(land, w, m, v)


_BIG = ("w_in", "dn_conv_w", "w_up_dn", "w_up_sw", "w_o", "w_ff1", "w_ff2")
_COL_SHARDED = ("w_in", "dn_conv_w", "w_ff1")
_SMALL_ROWS = ("pre_mix_g", "post_mix_g", "pre_mlp_g", "post_mlp_g")
_SMALL_MISC = ("dn_a_log", "dn_dt_bias", "dn_norm_g", "sw_sinks")
_WEIGHTS = ("pre_mix_g", "w_in", "dn_conv_w", "dn_a_log", "dn_dt_bias", "dn_norm_g", "sw_sinks", "w_up_dn", "w_up_sw",
            "w_o", "post_mix_g", "pre_mlp_g", "w_ff1", "w_ff2", "post_mlp_g")
_SMALL_PACK_ROWS = 24


def _unshard(name, g):
    if name in _COL_SHARDED:
        g = jnp.moveaxis(g, 0, -2)
        return g.reshape(g.shape[:-2] + (g.shape[-2] * g.shape[-1],))
    g = jnp.moveaxis(g, 0, 1)
    return g.reshape((g.shape[0], g.shape[1] * g.shape[2]) + g.shape[3:])


def _shard_major(name, full):
    if name in _COL_SHARDED:
        s = full.reshape(full.shape[:-1] + (N_DEV, full.shape[-1] // N_DEV))
        return jnp.moveaxis(s, -2, 0)
    s = full.reshape((full.shape[0], N_DEV, full.shape[1] // N_DEV) + full.shape[2:])
    return jnp.moveaxis(s, 1, 0)


def _pack_small(d):
    rows = jnp.concatenate([d[n] for n in _SMALL_ROWS], axis=0)
    misc = jnp.concatenate([d[n].reshape(-1) for n in _SMALL_MISC])
    misc = jnp.pad(misc, (0, W1 - misc.shape[0])).reshape(1, W1)
    out = jnp.concatenate([rows, misc], axis=0)
    return jnp.pad(out, ((0, _SMALL_PACK_ROWS - out.shape[0]), (0, 0)))


def _unpack_small(a, like):
    out, L = {}, like[_SMALL_ROWS[0]].shape[0]
    for i, n in enumerate(_SMALL_ROWS):
        out[n] = a[L * i:L * (i + 1)]
    at, row = 0, a[L * len(_SMALL_ROWS)]
    for n in _SMALL_MISC:
        size = like[n].size
        out[n] = row[at:at + size].reshape(like[n].shape)
        at += size
    return out


def kernel(x, positions, pre_mix_g, w_in, dn_conv_w, dn_a_log, dn_dt_bias, dn_norm_g, sw_sinks, w_up_dn, w_up_sw, w_o, post_mix_g, pre_mlp_g, w_ff1, w_ff2, post_mlp_g, loss_target, m_pre_mix_g, m_w_in, m_dn_conv_w, m_dn_a_log, m_dn_dt_bias, m_dn_norm_g, m_sw_sinks, m_w_up_dn, m_w_up_sw, m_w_o, m_post_mix_g, m_pre_mlp_g, m_w_ff1, m_w_ff2, m_post_mlp_g, v_pre_mix_g, v_w_in, v_dn_conv_w, v_dn_a_log, v_dn_dt_bias, v_dn_norm_g, v_sw_sinks, v_w_up_dn, v_w_up_sw, v_w_o, v_post_mix_g, v_pre_mlp_g, v_w_ff1, v_w_ff2, v_post_mlp_g):
    w = dict(pre_mix_g=pre_mix_g, w_in=w_in, dn_conv_w=dn_conv_w, dn_a_log=dn_a_log, dn_dt_bias=dn_dt_bias,
             dn_norm_g=dn_norm_g, sw_sinks=sw_sinks, w_up_dn=w_up_dn, w_up_sw=w_up_sw, w_o=w_o, post_mix_g=post_mix_g,
             pre_mlp_g=pre_mlp_g, w_ff1=w_ff1, w_ff2=w_ff2, post_mlp_g=post_mlp_g)
    m = dict(pre_mix_g=m_pre_mix_g, w_in=m_w_in, dn_conv_w=m_dn_conv_w, dn_a_log=m_dn_a_log, dn_dt_bias=m_dn_dt_bias,
             dn_norm_g=m_dn_norm_g, sw_sinks=m_sw_sinks, w_up_dn=m_w_up_dn, w_up_sw=m_w_up_sw, w_o=m_w_o,
             post_mix_g=m_post_mix_g, pre_mlp_g=m_pre_mlp_g, w_ff1=m_w_ff1, w_ff2=m_w_ff2, post_mlp_g=m_post_mlp_g)
    v = dict(pre_mix_g=v_pre_mix_g, w_in=v_w_in, dn_conv_w=v_dn_conv_w, dn_a_log=v_dn_a_log, dn_dt_bias=v_dn_dt_bias,
             dn_norm_g=v_dn_norm_g, sw_sinks=v_sw_sinks, w_up_dn=v_w_up_dn, w_up_sw=v_w_up_sw, w_o=v_w_o,
             post_mix_g=v_post_mix_g, pre_mlp_g=v_pre_mlp_g, w_ff1=v_w_ff1, w_ff2=v_w_ff2, post_mlp_g=v_post_mlp_g)
    n_layers = pre_mix_g.shape[0]
    xs, pos, tgt = x[0], positions[0], loss_target[0]

    payload = [w[n] if n == "dn_conv_w" else w[n].astype(bf16) for n in _BIG]
    gathered = _exchange("allgather_weights", payload, ["gather2"] * len(_BIG))
    full = {n: _unshard(n, g) for n, g in zip(_BIG, gathered)}
    layers = []
    for l in range(n_layers):
        wl = {n: (full[n][l] if n in _BIG else w[n][l]) for n in _WEIGHTS}
        layers.append(_layer_params(wl))
    cst = _consts(pos)

    h, res = xs, []
    for l in range(n_layers):
        h, r = _layer_fwd(h, layers[l], cst)
        res.append(r)
    dy, sq = _loss(h, tgt)
    loss = lax.psum(0.5 / D_MODEL * jnp.sum(sq), ("x", "y", "c"))
    grads = [None] * n_layers
    for l in reversed(range(n_layers)):
        dy, g = _layer_bwd(dy, res[l], layers[l], cst)
        grads[l] = _layer_grads_ref_layout(g)
    grad_x = dy[None]
    gfull = {n: jnp.stack([grads[l][n] for l in range(n_layers)]) for n in _WEIGHTS}

    nb = len(_BIG)
    send = [_shard_major(n, gfull[n]).astype(bf16) for n in _BIG]
    stage_a = _exchange("grads_to_sibling", send + [_pack_small(gfull)], ["to_sibling"] * nb + ["gather"])
    core = lax.axis_index("c")
    summed = []
    for n, mine, theirs in zip(_BIG, send, stage_a[:nb]):
        own = lax.dynamic_index_in_dim(mine.reshape((N_CHIP, 2) + mine.shape[1:]), core, axis=1, keepdims=False)
        lead = (-1,) + own.shape[-2:]
        summed.append(_add_bf16("add_" + n, own.reshape(lead), theirs.reshape(lead), tr=512).reshape(own.shape))
    landed = _exchange("grads_to_chips", summed, ["to_chips"] * nb)

    out_g, out_d, out_m, out_v = {}, {}, {}, {}
    for n, land in zip(_BIG, landed):
        out_g[n], out_d[n], out_m[n], out_v[n] = _adamw("adamw_" + n, land, w[n], m[n], v[n], tr=256)
    small = _adamw("adamw_small", stage_a[nb][:, None], _pack_small(w)[None], _pack_small(m)[None], _pack_small(v)[None],
                   tr=_SMALL_PACK_ROWS)
    for dst, a in zip((out_g, out_d, out_m, out_v), small):
        dst.update(_unpack_small(a[0], w))
    return (loss, grad_x, *[out_g[n] for n in _WEIGHTS], *[out_d[n] for n in _WEIGHTS],
            *[out_m[n] for n in _WEIGHTS], *[out_v[n] for n in _WEIGHTS])
```

```python
import math

import numpy as np
import jax
import jax.numpy as jnp
from jax import lax
from jax.experimental import pallas as pl
from jax.experimental.pallas import tpu as pltpu

f32 = jnp.float32
bf16 = jnp.bfloat16
HIGHEST = lax.Precision.HIGHEST

N_DEV = 8
D_MODEL = 1024
DN_HEADS = 8
DN_DK = 128
DN_CHUNK = 128
DN_CONV = 4
SW_Q_HEADS = 16
SW_KV_HEADS = 2
SW_HEAD_DIM = 64
SW_BLOCK = 128
ROPE_THETA = 500000.0
ROT_DIM = SW_HEAD_DIM // 4
D_FF = 4 * D_MODEL
EPS = 1e-6
LANES = 128
CONV_HALO = 8
NEG_BIG = -1e30

ADAM_LR = 0.001
ADAM_B1 = 0.9
ADAM_B2 = 0.999
ADAM_EPS = 1e-08
ADAM_WD = 0.01
ADAM_STEP = 10

PROJ_W = 7680
CB_Q, CB_K, CB_V, CB_Z, CB_SWQ, CB_GA, CB_GB = 0, 1, 2, 3, 4, 5, 6
CB_SWK, CB_SWV, CB_BA = 56, 57, 58

NN = ((1,), (0,))
NT = ((1,), (1,))
TN = ((0,), (0,))


def _mm(a, b, dims, hi=False):
    if hi:
        return lax.dot_general(a.astype(f32), b.astype(f32), (dims, ((), ())), precision=HIGHEST,
                               preferred_element_type=f32)
    return lax.dot_general(a.astype(bf16), b.astype(bf16), (dims, ((), ())), preferred_element_type=f32)


def _matmul(name, a, b, form, out_dtype, tm=512, tn=512, tk=1024, extra=(), epilogue=None, into=None):
    if form == "nn":
        (M, K), (_, N) = a.shape, b.shape
    elif form == "nt":
        (M, K), (N, _) = a.shape, b.shape
    else:
        (K, M), (_, N) = a.shape, b.shape
    tm, tn, tk = min(tm, M), min(tn, N), min(tk, K)
    assert M % tm == 0 and N % tn == 0 and K % tk == 0, (name, M, N, K, tm, tn, tk)
    nk = K // tk
    dims = {"nn": NN, "nt": NT, "tn": TN}[form]
    out_dtypes = [out_dtype] if epilogue is None else list(out_dtype)
    ne, no = len(extra), len(out_dtypes)
    nb = 0 if into is None else 1

    def body(a_ref, b_ref, *rest):
        e_refs, o_refs, acc_ref = rest[:ne], rest[ne + nb:ne + nb + no], rest[ne + nb + no]

        def finish(acc):
            vals = [acc] if epilogue is None else epilogue(acc, [e[...] for e in e_refs])
            for o, val in zip(o_refs, vals):
                o[...] = val.astype(o.dtype)

        part = lax.dot_general(a_ref[...], b_ref[...], (dims, ((), ())), preferred_element_type=f32)
        if nk == 1:
            finish(part)
        else:
            k = pl.program_id(2)

            @pl.when(k == 0)
            def _():
                acc_ref[...] = part

            @pl.when(k > 0)
            def _():
                acc_ref[...] += part

            @pl.when(k == nk - 1)
            def _():
                finish(acc_ref[...])

    if form == "tn":
        a_spec = pl.BlockSpec((tk, tm), lambda i, j, k: (k, i))
    else:
        a_spec = pl.BlockSpec((tm, tk), lambda i, j, k: (i, k))
    if form == "nt":
        b_spec = pl.BlockSpec((tn, tk), lambda i, j, k: (j, k))
    else:
        b_spec = pl.BlockSpec((tk, tn), lambda i, j, k: (k, j))
    tile = pl.BlockSpec((tm, tn), lambda i, j, k: (i, j))
    in_specs, out_specs = [a_spec, b_spec] + [tile] * ne, [tile] * no
    out_shape = [jax.ShapeDtypeStruct((M, N), dt) for dt in out_dtypes]
    operands, aliases = (a, b, *extra), {}
    if into is not None:
        buf, slab = into
        assert epilogue is None and buf.shape[1:] == (M, N) and buf.dtype == out_dtype, (name, buf.shape, buf.dtype)
        in_specs.append(pl.BlockSpec(memory_space=pl.ANY))
        out_specs = [pl.BlockSpec((None, tm, tn), lambda i, j, k: (slab, i, j))]
        out_shape = [jax.ShapeDtypeStruct(buf.shape, buf.dtype)]
        operands, aliases = operands + (buf,), {2 + ne: 0}
    res = pl.pallas_call(
        body, name=name,
        grid=(M // tm, N // tn, nk),
        in_specs=in_specs, out_specs=out_specs, out_shape=out_shape,
        input_output_aliases=aliases,
        scratch_shapes=[pltpu.VMEM((tm, tn) if nk > 1 else (8, 128), f32)],
        compiler_params=pltpu.CompilerParams(dimension_semantics=("parallel", "parallel", "arbitrary")),
    )(*operands)
    return res[0] if epilogue is None else list(res)


def _tile_specs(ins, halo_ids, params, TM, HR, row_of):
    specs = [pl.BlockSpec((TM, w), lambda i, cb=cb: (row_of(i), cb)) for (_, w, cb) in ins]
    for h in halo_ids:
        _, w, cb = ins[h]
        specs.append(pl.BlockSpec((HR, w), lambda i, cb=cb: (jnp.maximum(row_of(i) * (TM // HR) - 1, 0), cb)))
    for p in params:
        specs.append(pl.BlockSpec(p.shape, lambda i, nd=p.ndim: (0,) * nd))
    return specs


def _tile_fwd(name, fn, T, TM, ins, params, outs, halo_ids=(), HR=CONV_HALO):
    TM = min(TM, T)
    n = T // TM
    ni, nh, npar = len(ins), len(halo_ids), len(params)

    def body(*refs):
        in_v = [r[...] for r in refs[:ni]]
        halo_v = [r[...] for r in refs[ni:ni + nh]]
        par_v = [r[...] for r in refs[ni + nh:ni + nh + npar]]
        o_refs = refs[ni + nh + npar:]
        first = pl.program_id(0) == 0
        vals = fn(first, in_v, halo_v, par_v)
        for o, val in zip(o_refs, vals):
            o[...] = val.astype(o.dtype)

    res = pl.pallas_call(
        body, name=name, grid=(n,),
        in_specs=_tile_specs(ins, halo_ids, params, TM, HR, lambda i: i),
        out_specs=[pl.BlockSpec((TM, w), lambda i: (i, 0)) for (w, _) in outs],
        out_shape=[jax.ShapeDtypeStruct((T, w), dt) for (w, dt) in outs],
        compiler_params=pltpu.CompilerParams(dimension_semantics=("arbitrary",)),
    )(*[a for (a, _, _) in ins], *[ins[h][0] for h in halo_ids], *params)
    return list(res)


def _tile_bwd(name, fn, T, TM, ins, params, cts, din, dpar, halo_ids=(), HR=CONV_HALO):
    TM = min(TM, T)
    n = T // TM
    ni, nh, npar, nc = len(ins), len(halo_ids), len(params), len(cts)
    din_ids = [j for (j, _) in din]
    dh_ids = [h for h in halo_ids if h in din_ids]
    nd, ndp, ndh = len(din), len(dpar), len(dh_ids)

    def body(*refs):
        in_v = [r[...] for r in refs[:ni]]
        halo_v = [r[...] for r in refs[ni:ni + nh]]
        par_v = [r[...] for r in refs[ni + nh:ni + nh + npar]]
        ct_v = [r[...].astype(f32) for r in refs[ni + nh + npar:ni + nh + npar + nc]]
        o_refs = refs[ni + nh + npar + nc:ni + nh + npar + nc + nd + ndp]
        carry_refs = refs[ni + nh + npar + nc + nd + ndp:]
        i = pl.program_id(0)
        first = i == n - 1

        def g(d_in, d_halo, d_par):
            full_in = list(in_v)
            for j, val in zip(din_ids, d_in):
                full_in[j] = val
            full_halo = list(halo_v)
            for h, val in zip(dh_ids, d_halo):
                full_halo[list(halo_ids).index(h)] = val
            full_par = list(par_v)
            for j, val in zip(dpar, d_par):
                full_par[j] = val
            return tuple(fn(first, full_in, full_halo, full_par))

        prim = ([in_v[j].astype(f32) for j in din_ids],
                [halo_v[list(halo_ids).index(h)].astype(f32) for h in dh_ids],
                [par_v[j] for j in dpar])
        _, vjp = jax.vjp(g, *prim)
        g_in, g_halo, g_par = vjp(tuple(ct_v))

        @pl.when(i == 0)
        def _():
            for c in carry_refs:
                c[...] = jnp.zeros_like(c)
            for o in o_refs[nd:]:
                o[...] = jnp.zeros_like(o)

        for slot, (j, _) in enumerate(din):
            val = g_in[slot]
            if j in dh_ids:
                c = carry_refs[dh_ids.index(j)]
                val = jnp.concatenate([val[:TM - HR], val[TM - HR:] + c[...]], axis=0) if TM > HR else val + c[...]
                c[...] = g_halo[dh_ids.index(j)]
            o_refs[slot][...] = val.astype(o_refs[slot].dtype)
        for slot in range(ndp):
            o_refs[nd + slot][...] += g_par[slot]

    rev = lambda i: n - 1 - i
    in_specs = _tile_specs(ins, halo_ids, params, TM, HR, rev)
    ct_specs = [pl.BlockSpec((TM, w), lambda i, cb=cb: (rev(i), cb)) for (_, w, cb) in cts]
    out_specs = [pl.BlockSpec((TM, ins[j][1]), lambda i: (rev(i), 0)) for j in din_ids]
    out_specs += [pl.BlockSpec(params[j].shape, lambda i, nd_=params[j].ndim: (0,) * nd_) for j in dpar]
    out_shape = [jax.ShapeDtypeStruct((T, ins[j][1]), dt) for (j, dt) in din]
    out_shape += [jax.ShapeDtypeStruct(params[j].shape, f32) for j in dpar]
    res = pl.pallas_call(
        body, name=name, grid=(n,),
        in_specs=in_specs + ct_specs,
        out_specs=out_specs,
        out_shape=out_shape,
        scratch_shapes=[pltpu.VMEM((HR, ins[h][1]), f32) for h in dh_ids],
        compiler_params=pltpu.CompilerParams(dimension_semantics=("arbitrary",)),
    )(*[a for (a, _, _) in ins], *[ins[h][0] for h in halo_ids], *params, *[a for (a, _, _) in cts])
    return list(res)


def _rms(x, g):
    return x * lax.rsqrt(jnp.mean(x * x, axis=-1, keepdims=True) + EPS) * g


def _fn_prenorm(first, ins, halos, params):
    (x,), (g,) = ins, params
    x = x.astype(f32)
    return [_rms(x, g), x]


def _fn_postmix(first, ins, halos, params):
    (x, mix), (g2, g3) = ins, params
    x1 = x + _rms(mix, g2)
    return [x1, _rms(x1, g3)]


def _fn_postmlp(first, ins, halos, params):
    (x1, ff), (g4,) = ins, params
    return [x1 + _rms(ff, g4)]


def _fn_rms_only(first, ins, halos, params):
    (ff,), (g4,) = ins, params
    return [_rms(ff, g4)]


def _fn_merge(first, ins, halos, params):
    ga, gb, ya, yb = ins
    return [jax.nn.sigmoid(ga) * ya + jax.nn.sigmoid(gb) * yb]


def _roll_rows(x, shift):
    return pltpu.roll(x, shift, 0)


_row_roll = jax.custom_vjp(_roll_rows, nondiff_argnums=(1,))
_row_roll.defvjp(lambda x, shift: (_roll_rows(x, shift), None),
                 lambda shift, _, ct: (_roll_rows(ct, ct.shape[0] - shift),))


def _make_fn_conv(norm_scale):
    def fn(first, ins, halos, params):
        (x,), (xp,), (w,) = ins, halos, params
        xp = jnp.where(first, 0.0, xp)
        outs = []
        for h in range(DN_HEADS):
            sl = slice(DN_DK * h, DN_DK * (h + 1))
            xe, wh = jnp.concatenate([xp[:, sl], x[:, sl]], axis=0), w[:, sl]
            y = xe[CONV_HALO:] * wh[DN_CONV - 1:DN_CONV]
            for j in range(DN_CONV - 1):
                y = y + _row_roll(xe, DN_CONV - 1 - j)[CONV_HALO:] * wh[j:j + 1]
            y = jax.nn.silu(y)
            if norm_scale is not None:
                y = y * lax.rsqrt(jnp.sum(y * y, axis=-1, keepdims=True) + EPS) * norm_scale
            outs.append(y)
        return [jnp.concatenate(outs, axis=-1)]
    return fn


def _fn_gates(first, ins, halos, params):
    (ba,), (avec, dvec) = ins, params
    lane = lax.broadcasted_iota(jnp.int32, ba.shape, 1)
    beta = jax.nn.sigmoid(ba)
    g = -jnp.exp(avec) * jax.nn.softplus(ba + dvec)
    return [jnp.where(lane < DN_HEADS, beta, jnp.where(lane < 2 * DN_HEADS, g, 0.0))]


def _fn_dnpost(first, ins, halos, params):
    (o, z), (ng,) = ins, params
    outs = []
    for h in range(DN_HEADS):
        sl = slice(DN_DK * h, DN_DK * (h + 1))
        outs.append(_rms(o[:, sl], ng) * jax.nn.silu(z[:, sl]))
    return [jnp.concatenate(outs, axis=-1)]


def _roll_lanes(x, shift):
    return pltpu.roll(x, shift, 1)


_lane_roll = jax.custom_vjp(_roll_lanes, nondiff_argnums=(1,))
_lane_roll.defvjp(lambda x, shift: (_roll_lanes(x, shift), None),
                  lambda shift, _, ct: (_roll_lanes(ct, LANES - shift),))


def _attn_probs(qp, kx, sink, mask):
    heads = range(len(sink))
    s = [jnp.where(mask, _mm(qp[g // 2], kx[g % 2], NT), NEG_BIG) for g in heads]
    m = [jnp.maximum(jnp.max(s[g], axis=-1, keepdims=True), sink[g]) for g in heads]
    p = [jnp.exp(s[g] - m[g]) for g in heads]
    ps = [jnp.exp(sink[g] - m[g]) for g in heads]
    inv = [1.0 / (jnp.sum(p[g], axis=-1, keepdims=True) + ps[g]) for g in heads]
    return [p[g] * inv[g] for g in heads], [ps[g] * inv[g] for g in heads]


@jax.custom_vjp
def _attn_group(qp, kx, vx, sink, mask):
    probs, _ = _attn_probs(qp, kx, sink, mask)
    return tuple(_mm(probs[g], vx[g % 2], NN) for g in range(len(sink)))


def _attn_group_fwd(qp, kx, vx, sink, mask):
    o = _attn_group(qp, kx, vx, sink, mask)
    return o, (qp, kx, vx, sink, mask, o)


def _attn_group_bwd(res, do):
    qp, kx, vx, sink, mask, o = res
    heads = range(len(sink))
    probs, p_sink = _attn_probs(qp, kx, sink, mask)
    d_probs = [_mm(do[g], vx[g % 2], NT) for g in heads]
    dot = [jnp.sum(do[g] * o[g], axis=-1, keepdims=True) for g in heads]
    ds = [probs[g] * (d_probs[g] - dot[g]) for g in heads]
    d_qp = tuple(_mm(ds[2 * j], kx[0], NN) + _mm(ds[2 * j + 1], kx[1], NN) for j in range(len(qp)))
    rows = lambda xs: jnp.concatenate(xs, axis=0)
    d_kx = tuple(_mm(rows([ds[g] for g in heads if g % 2 == c]), rows([qp[g // 2] for g in heads if g % 2 == c]), TN)
                 for c in range(2))
    d_vx = tuple(_mm(rows([probs[g] for g in heads if g % 2 == c]), rows([do[g] for g in heads if g % 2 == c]), TN)
                 for c in range(2))
    d_sink = tuple(-jnp.sum(p_sink[g] * dot[g], axis=0, keepdims=True) for g in heads)
    return d_qp, d_kx, d_vx, d_sink, None


_attn_group.defvjp(_attn_group_fwd, _attn_group_bwd)


def _fn_swa(first, ins, halos, params):
    q, k, v, cos, sin = ins
    kp, vp, cosp, sinp = halos
    sinks, sel_a0, sel_b0, sel_a1, sel_b1 = params
    B = q.shape[0]
    half = ROT_DIM // 2
    in_head = jnp.bitwise_and(lax.broadcasted_iota(jnp.int32, (1, LANES), 1), SW_HEAD_DIM - 1)

    def rope(x, c, s):
        return (x * c + _lane_roll(x, LANES - half) * jnp.where(in_head < half, -s, 0.0)
                + _lane_roll(x, half) * jnp.where(in_head >= half, s, 0.0))

    kcat = jnp.concatenate([rope(kp, cosp, sinp), rope(k, cos, sin)], axis=0)
    vcat = jnp.concatenate([vp, v], axis=0)
    r = lax.broadcasted_iota(jnp.int32, (B, 2 * B), 0)
    c = lax.broadcasted_iota(jnp.int32, (B, 2 * B), 1)
    mask = (c > r) & (c <= r + B) & ((c >= B) | jnp.logical_not(first))
    group = SW_Q_HEADS // SW_KV_HEADS
    sels = ((sel_a0, sel_b0), (sel_a1, sel_b1))
    outs = []
    for hk in range(SW_KV_HEADS):
        heads = range(group)
        kx = [_mm(kcat, sels[hk][half], NN) for half in range(2)]
        vx = [_mm(vcat, sels[hk][half], NN) for half in range(2)]
        qp = [rope(q[:, LANES * j:LANES * (j + 1)], cos, sin) * (SW_HEAD_DIM ** -0.5)
              for j in range(hk * group // 2, (hk + 1) * group // 2)]
        sink = tuple(sinks[:, hk * group + g:hk * group + g + 1] for g in heads)
        o = _attn_group(tuple(qp), tuple(kx), tuple(vx), sink, mask)
        outs += [o[2 * j] + o[2 * j + 1] for j in range(group // 2)]
    return [jnp.concatenate(outs, axis=-1)]


@jax.custom_vjp
def _inv_unit_lower(Ls):
    C = Ls[0].shape[0]
    ii = lax.broadcasted_iota(jnp.int32, (C, C), 0)
    jj = lax.broadcasted_iota(jnp.int32, (C, C), 1)

    def off_mask(level):
        same_pair = jnp.right_shift(ii, level + 1) == jnp.right_shift(jj, level + 1)
        lower_left = (jnp.bitwise_and(jnp.right_shift(ii, level), 1) == 1) & (jnp.bitwise_and(jnp.right_shift(jj, level), 1) == 0)
        return same_pair & lower_left

    eye = (ii == jj).astype(f32)
    m0 = off_mask(0)
    Ts = [eye - jnp.where(m0, L, 0.0) for L in Ls]
    for level in range(1, int(math.log2(C))):
        mk = off_mask(level)
        left = [_mm(T_, jnp.where(mk, L, 0.0), NN) for T_, L in zip(Ts, Ls)]
        Ts = [T_ - _mm(a, T_, NN) for a, T_ in zip(left, Ts)]
    return tuple(Ts)


def _inv_fwd(Ls):
    Ts = _inv_unit_lower(Ls)
    return Ts, Ts


def _inv_bwd(Ts, dTs):
    left = [_mm(T_, dT, TN) for T_, dT in zip(Ts, dTs)]
    return (tuple(-_mm(a, T_, NT) for a, T_ in zip(left, Ts)),)


_inv_unit_lower.defvjp(_inv_fwd, _inv_bwd)


@jax.custom_vjp
def _inv_known(Ls, Ts):
    return Ts


_inv_known.defvjp(lambda Ls, Ts: (Ts, Ts),
                  lambda Ts, dTs: (_inv_bwd(Ts, dTs)[0], tuple(jnp.zeros_like(t) for t in Ts)))


def _mm_01(a, b, dims):
    hi = b.astype(bf16)
    r1 = b - hi.astype(f32)
    mid = r1.astype(bf16)
    lo = (r1 - mid.astype(f32)).astype(bf16)
    a16 = a.astype(bf16)
    dot = lambda part: lax.dot_general(a16, part, (dims, ((), ())), preferred_element_type=f32)
    return dot(hi) + dot(mid) + dot(lo)


def _eye(n):
    return lax.broadcasted_iota(jnp.int32, (n, n), 0) == lax.broadcasted_iota(jnp.int32, (n, n), 1)


def _lower(n):
    return lax.broadcasted_iota(jnp.int32, (n, n), 0) >= lax.broadcasted_iota(jnp.int32, (n, n), 1)


@jax.custom_vjp
def _transpose(x):
    return _mm_01(_eye(x.shape[1]), x, NT)


_transpose.defvjp(lambda x: (_transpose(x), None), lambda _, ct: (_transpose(ct),))


@jax.custom_vjp
def _cumsum_rows(x):
    return _mm_01(_lower(x.shape[0]), x, NN)


_cumsum_rows.defvjp(lambda x: (_cumsum_rows(x), None), lambda _, ct: (_mm_01(_lower(ct.shape[0]), ct, TN),))


def _dn_chunk(q, k, v, gb, S, tinv_known=None):
    C = q.shape[0]
    H = range(DN_HEADS)
    ii = lax.broadcasted_iota(jnp.int32, (C, C), 0)
    jj = lax.broadcasted_iota(jnp.int32, (C, C), 1)
    causal, strict = ii >= jj, ii > jj
    gc_all = _cumsum_rows(gb)
    gc_t = _transpose(gc_all)
    sl = [slice(DN_DK * h, DN_DK * (h + 1)) for h in H]
    qs, ks, vs = [q[:, s] for s in sl], [k[:, s] for s in sl], [v[:, s] for s in sl]
    beta = [gb[:, h:h + 1] for h in H]
    gcol = [gc_all[:, DN_HEADS + h:DN_HEADS + h + 1] for h in H]
    grow = [gc_t[DN_HEADS + h:DN_HEADS + h + 1, :] for h in H]
    decay = [jnp.where(causal, jnp.exp(jnp.where(causal, gcol[h] - grow[h], 0.0)), 0.0) for h in H]
    kb = [ks[h] * beta[h] for h in H]
    kk = [_mm(kb[h], ks[h], NT) for h in H]
    qk = [_mm(qs[h], ks[h], NT) for h in H]
    Ls = tuple(jnp.where(strict, kk[h] * decay[h], 0.0) for h in H)
    tinv = _inv_unit_lower(Ls) if tinv_known is None else _inv_known(Ls, tinv_known)
    eg = [jnp.exp(gcol[h]) for h in H]
    u = [_mm(tinv[h], vs[h] * beta[h], NN) for h in H]
    w = [_mm(tinv[h], kb[h] * eg[h], NN) for h in H]
    gl = [gcol[h][C - 1:C, :] for h in H]
    ws = [_mm(w[h], S[h], NN) for h in H]
    qS = [_mm(qs[h] * eg[h], S[h], NN) for h in H]
    v_new = [u[h] - ws[h] for h in H]
    av = [_mm(qk[h] * decay[h], v_new[h], NN) for h in H]
    kv = [_mm(ks[h] * jnp.exp(gl[h] - gcol[h]), v_new[h], TN) for h in H]
    o = jnp.concatenate([qS[h] + av[h] for h in H], axis=-1)
    return o, tuple(S[h] * jnp.exp(gl[h]) + kv[h] for h in H), tinv


_DN_W = DN_HEADS * DN_DK


def _dn_inputs(first, xq, xk, xv, hq, hk, hv, ba, cw, avec, dvec):
    conv = lambda scale, x, h, j: _make_fn_conv(scale)(first, [x], [h], [cw[:, _DN_W * j:_DN_W * (j + 1)]])[0]
    return (conv(DN_DK ** -0.5, xq, hq, 0), conv(1.0, xk, hk, 1), conv(None, xv, hv, 2),
            _fn_gates(first, [ba], [], [avec, dvec])[0])


def _delta_specs(C, row_of):
    cols = (CB_Q, CB_K, CB_V)
    specs = [pl.BlockSpec((C, _DN_W), lambda i, cb=cb: (row_of(i), cb)) for cb in cols]
    specs += [pl.BlockSpec((CONV_HALO, _DN_W), lambda i, cb=cb: (jnp.maximum(row_of(i) * (C // CONV_HALO) - 1, 0), cb))
              for cb in cols]
    specs.append(pl.BlockSpec((C, LANES), lambda i: (row_of(i), CB_BA)))
    return specs


def _whole(a):
    return pl.BlockSpec(a.shape, lambda i, nd=a.ndim: (0,) * nd)


def _delta_fwd(proj, cw, avec, dvec):
    T = proj.shape[0]
    C = min(DN_CHUNK, T)
    n = T // C

    def body(xq, xk, xv, hq, hk, hv, ba, cw_ref, a_ref, d_ref, o_ref, hist_ref, tinv_ref,
             q_ref, k_ref, v_ref, gb_ref, s_ref):
        first = pl.program_id(0) == 0

        @pl.when(first)
        def _():
            s_ref[...] = jnp.zeros_like(s_ref)

        S = tuple(s_ref[h] for h in range(DN_HEADS))
        for h in range(DN_HEADS):
            hist_ref[0, h] = S[h]
        q, k, v, gb = _dn_inputs(first, xq[...], xk[...], xv[...], hq[...], hk[...], hv[...], ba[...],
                                 cw_ref[...], a_ref[...], d_ref[...])
        q_ref[...], k_ref[...], v_ref[...], gb_ref[...] = q, k, v, gb
        o, s_new, tinv = _dn_chunk(q, k, v, gb, S)
        o_ref[...] = o
        for h in range(DN_HEADS):
            s_ref[h] = s_new[h]
            tinv_ref[0, h] = tinv[h]

    row = pl.BlockSpec((C, _DN_W), lambda i: (i, 0))
    return pl.pallas_call(
        body, name="delta_fwd", grid=(n,),
        in_specs=_delta_specs(C, lambda i: i) + [_whole(cw), _whole(avec), _whole(dvec)],
        out_specs=[row, pl.BlockSpec((1, DN_HEADS, DN_DK, DN_DK), lambda i: (i, 0, 0, 0)),
                   pl.BlockSpec((1, DN_HEADS, C, C), lambda i: (i, 0, 0, 0)),
                   row, row, row, pl.BlockSpec((C, LANES), lambda i: (i, 0))],
        out_shape=[jax.ShapeDtypeStruct((T, _DN_W), f32), jax.ShapeDtypeStruct((n, DN_HEADS, DN_DK, DN_DK), f32),
                   jax.ShapeDtypeStruct((n, DN_HEADS, C, C), f32)]
                  + [jax.ShapeDtypeStruct((T, _DN_W), f32)] * 3 + [jax.ShapeDtypeStruct((T, LANES), f32)],
        scratch_shapes=[pltpu.VMEM((DN_HEADS, DN_DK, DN_DK), f32)],
        compiler_params=pltpu.CompilerParams(dimension_semantics=("arbitrary",)),
    )(proj, proj, proj, proj, proj, proj, proj, cw, avec, dvec)


def _delta_bwd(qn, kn, vv, gb, hist, tinv, do):
    T = qn.shape[0]
    C = min(DN_CHUNK, T)
    n = T // C

    def body(q_ref, k_ref, v_ref, gb_ref, hist_ref, tinv_ref, do_ref, dq_ref, dk_ref, dv_ref, dgb_ref, ds_ref):
        @pl.when(pl.program_id(0) == 0)
        def _():
            ds_ref[...] = jnp.zeros_like(ds_ref)

        S = tuple(hist_ref[0, h] for h in range(DN_HEADS))
        known = tuple(tinv_ref[0, h] for h in range(DN_HEADS))
        chunk = lambda q, k, v, g, s: _dn_chunk(q, k, v, g, s, tinv_known=known)[:2]
        _, vjp = jax.vjp(chunk, q_ref[...], k_ref[...], v_ref[...], gb_ref[...], S)
        dS = tuple(ds_ref[h] for h in range(DN_HEADS))
        dq, dk, dv, dgb, dS_in = vjp((do_ref[...], dS))
        dq_ref[...] = dq
        dk_ref[...] = dk
        dv_ref[...] = dv
        dgb_ref[...] = dgb
        for h in range(DN_HEADS):
            ds_ref[h] = dS_in[h]

    row = pl.BlockSpec((C, _DN_W), lambda i: (n - 1 - i, 0))
    small = pl.BlockSpec((C, LANES), lambda i: (n - 1 - i, 0))
    return pl.pallas_call(
        body, name="delta_bwd", grid=(n,),
        in_specs=[row, row, row, small, pl.BlockSpec((1, DN_HEADS, DN_DK, DN_DK), lambda i: (n - 1 - i, 0, 0, 0)),
                  pl.BlockSpec((1, DN_HEADS, C, C), lambda i: (n - 1 - i, 0, 0, 0)), row],
        out_specs=[row, row, row, small],
        out_shape=[jax.ShapeDtypeStruct((T, _DN_W), f32)] * 3 + [jax.ShapeDtypeStruct((T, LANES), f32)],
        scratch_shapes=[pltpu.VMEM((DN_HEADS, DN_DK, DN_DK), f32)],
        compiler_params=pltpu.CompilerParams(dimension_semantics=("arbitrary",)),
    )(qn, kn, vv, gb, hist, tinv, do)


TM_ROW = 256
W1 = D_MODEL


def _first_only(fn):
    return lambda *a: fn(*a)[:1]


def _swa_args(proj, cst):
    ins = [(proj, W1, CB_SWQ), (proj, LANES, CB_SWK), (proj, LANES, CB_SWV), (cst["cos"], LANES, 0), (cst["sin"], LANES, 0)]
    return ins, (1, 2, 3, 4)


def _layer_fwd(x, p, cst):
    T = x.shape[0]
    r = {"x": x}
    (h,) = _tile_fwd("prenorm", _first_only(_fn_prenorm), T, TM_ROW, [(x, W1, 0)], [p["g1"]], [(W1, bf16)])
    proj = _matmul("proj", h, p["w_in"], "nn", f32, tm=1024, tn=1536)
    o, hist, tinv, qn, kn, vv, gbt = _delta_fwd(proj, p["conv_w"], p["avec"], p["dvec"])
    (dn_out,) = _tile_fwd("dnpost", _fn_dnpost, T, TM_ROW, [(o, W1, 0), (proj, W1, CB_Z)], [p["ng"]], [(W1, bf16)])
    sw_ins, sw_halo = _swa_args(proj, cst)
    sw_par = [p["sinks"], cst["sel_a0"], cst["sel_b0"], cst["sel_a1"], cst["sel_b1"]]
    (sw_out,) = _tile_fwd("swa", _fn_swa, T, SW_BLOCK, sw_ins, sw_par, [(W1, bf16)], halo_ids=sw_halo, HR=SW_BLOCK)
    y_a = _matmul("up_dn", dn_out, p["w_up_dn"], "nn", f32, tm=1024, tn=1024)
    y_b = _matmul("up_sw", sw_out, p["w_up_sw"], "nn", f32, tm=1024, tn=1024)
    (gated,) = _tile_fwd("merge", _fn_merge, T, TM_ROW,
                         [(proj, W1, CB_GA), (proj, W1, CB_GB), (y_a, W1, 0), (y_b, W1, 0)], [], [(W1, bf16)])
    mix = _matmul("w_o", gated, p["w_o"], "nn", f32, tm=1024, tn=1024)
    x1, h2 = _tile_fwd("postmix", _fn_postmix, T, TM_ROW, [(x, W1, 0), (mix, W1, 0)], [p["g2"], p["g3"]],
                       [(W1, f32), (W1, bf16)])
    ffh, act = _matmul("ff1", h2, p["w_ff1"], "nn", [f32, bf16], tm=1024, tn=1024,
                       epilogue=lambda acc, ex: [acc, jnp.square(jnp.maximum(acc, 0.0))])
    ff = _matmul("ff2", act, p["w_ff2"], "nn", f32, tm=1024, tn=1024)
    (x2,) = _tile_fwd("postmlp", _fn_postmlp, T, TM_ROW, [(x1, W1, 0), (ff, W1, 0)], [p["g4"]], [(W1, f32)])
    r.update(h=h, proj=proj, qn=qn, kn=kn, vv=vv, gbt=gbt, o=o, hist=hist, tinv=tinv, dn_out=dn_out, sw_out=sw_out,
             y_a=y_a, y_b=y_b, gated=gated, mix=mix, h2=h2, ffh=ffh, act=act, ff=ff)
    return x2, r


def _layer_bwd(dx2, r, p, cst, gbuf, layer):
    T = dx2.shape[0]
    x, proj = r["x"], r["proj"]
    g, gbuf = {}, dict(gbuf)
    dff, g["g4"] = _tile_bwd("postmlp_b", _fn_rms_only, T, TM_ROW, [(r["ff"], W1, 0)], [p["g4"]], [(dx2, W1, 0)],
                             [(0, bf16)], [0])
    (dffh,) = _matmul("ff2_dx", dff, p["w_ff2"], "nt", [bf16], tm=1024, tn=1024, extra=[r["ffh"]],
                      epilogue=lambda acc, ex: [acc * (2.0 * jnp.maximum(ex[0], 0.0))])
    gbuf["w_ff2"] = _matmul("ff2_dw", r["act"], dff, "tn", bf16, tm=1024, tn=1024, into=(gbuf["w_ff2"], layer))
    dh2 = _matmul("ff1_dx", dffh, p["w_ff1"], "nt", f32, tm=1024, tn=1024)
    gbuf["w_ff1"] = _matmul("ff1_dw", r["h2"], dffh, "tn", bf16, tm=1024, tn=1024, into=(gbuf["w_ff1"], layer))
    dx1, dmix, g["g2"], g["g3"] = _tile_bwd("postmix_b", _fn_postmix, T, TM_ROW, [(x, W1, 0), (r["mix"], W1, 0)],
                                            [p["g2"], p["g3"]], [(dx2, W1, 0), (dh2, W1, 0)], [(0, f32), (1, bf16)], [0, 1])
    dgated = _matmul("w_o_dx", dmix, p["w_o"], "nt", f32, tm=1024, tn=1024)
    gbuf["w_o"] = _matmul("w_o_dw", r["gated"], dmix, "tn", bf16, tm=1024, tn=1024, into=(gbuf["w_o"], layer))
    dga, dgb, dya, dyb = _tile_bwd("merge_b", _fn_merge, T, TM_ROW,
                                   [(proj, W1, CB_GA), (proj, W1, CB_GB), (r["y_a"], W1, 0), (r["y_b"], W1, 0)], [],
                                   [(dgated, W1, 0)], [(0, bf16), (1, bf16), (2, bf16), (3, bf16)], [])
    d_dn = _matmul("up_dn_dx", dya, p["w_up_dn"], "nt", f32, tm=1024, tn=1024)
    gbuf["w_up_dn"] = _matmul("up_dn_dw", r["dn_out"], dya, "tn", bf16, tm=1024, tn=1024, into=(gbuf["w_up_dn"], layer))
    d_sw = _matmul("up_sw_dx", dyb, p["w_up_sw"], "nt", f32, tm=1024, tn=1024)
    gbuf["w_up_sw"] = _matmul("up_sw_dw", r["sw_out"], dyb, "tn", bf16, tm=1024, tn=1024, into=(gbuf["w_up_sw"], layer))
    do, dz, g["ng"] = _tile_bwd("dnpost_b", _fn_dnpost, T, TM_ROW, [(r["o"], W1, 0), (proj, W1, CB_Z)], [p["ng"]],
                                [(d_dn, W1, 0)], [(0, f32), (1, bf16)], [0])
    dqn, dkn, dvv, dgbt = _delta_bwd(r["qn"], r["kn"], r["vv"], r["gbt"], r["hist"], r["tinv"], do)
    conv_b = lambda nm, cb, scale, ct: _tile_bwd(nm, _make_fn_conv(scale), T, TM_ROW, [(proj, W1, cb)],
                                                 [p["conv_w"][:, W1 * cb:W1 * (cb + 1)]], [(ct, W1, 0)], [(0, bf16)], [0],
                                                 halo_ids=(0,))
    dq_in, dcw_q = conv_b("conv_q_b", CB_Q, DN_DK ** -0.5, dqn)
    dk_in, dcw_k = conv_b("conv_k_b", CB_K, 1.0, dkn)
    dv_in, dcw_v = conv_b("conv_v_b", CB_V, None, dvv)
    g["conv_w"] = jnp.concatenate([dcw_q, dcw_k, dcw_v], axis=-1)
    dba, g["avec"], g["dvec"] = _tile_bwd("gates_b", _fn_gates, T, TM_ROW, [(proj, LANES, CB_BA)], [p["avec"], p["dvec"]],
                                          [(dgbt, LANES, 0)], [(0, bf16)], [0, 1])
    sw_ins, sw_halo = _swa_args(proj, cst)
    sw_par = [p["sinks"], cst["sel_a0"], cst["sel_b0"], cst["sel_a1"], cst["sel_b1"]]
    dswq, dswk, dswv, g["sinks"] = _tile_bwd("swa_b", _fn_swa, T, SW_BLOCK, sw_ins, sw_par, [(d_sw, W1, 0)],
                                             [(0, bf16), (1, bf16), (2, bf16)], [0], halo_ids=sw_halo, HR=SW_BLOCK)
    dproj = jnp.concatenate([dq_in, dk_in, dv_in, dz, dswq, dga, dgb, dswk, dswv, dba, jnp.zeros((T, LANES), bf16)], axis=-1)
    dh = _matmul("proj_dx", dproj, p["w_in"], "nt", f32, tm=1024, tn=1024, tk=1536)
    gbuf["w_in"] = _matmul("proj_dw", r["h"], dproj, "tn", bf16, tm=1024, tn=1536, into=(gbuf["w_in"], layer))
    dx, g["g1"] = _tile_bwd("prenorm_b", _fn_prenorm, T, TM_ROW, [(x, W1, 0)], [p["g1"]], [(dh, W1, 0), (dx1, W1, 0)],
                            [(0, f32)], [0])
    return dx, g, gbuf


_MATMUL_WEIGHTS = ("w_in", "w_up_dn", "w_up_sw", "w_o", "w_ff1", "w_ff2")


def _grad_buffers(n_layers):
    shapes = dict(w_in=(D_MODEL, PROJ_W), w_up_dn=(D_MODEL, D_MODEL), w_up_sw=(D_MODEL, D_MODEL), w_o=(D_MODEL, D_MODEL),
                  w_ff1=(D_MODEL, D_FF), w_ff2=(D_FF, D_MODEL))
    return {n: lax.empty((n_layers,) + shapes[n], bf16) for n in _MATMUL_WEIGHTS}


_OFF_BA, _OFF_SWQ, _OFF_SWK, _OFF_GA, _D_IN = 4096, 4112, 5136, 5392, 7440


def _proj_cols(w):
    pad = lambda n: jnp.zeros(w.shape[:-1] + (n,), w.dtype)
    return jnp.concatenate([w[..., :_OFF_BA], w[..., _OFF_SWQ:_OFF_SWK], w[..., _OFF_GA:_D_IN],
                            w[..., _OFF_SWK:_OFF_GA], w[..., _OFF_BA:_OFF_SWQ], pad(PROJ_W - _D_IN)], axis=-1)


def _proj_cols_inv(w):
    n_ba = _OFF_SWQ - _OFF_BA
    return jnp.concatenate([w[..., :4096], w[..., 7424:7424 + n_ba], w[..., 4096:5120], w[..., 7168:7424],
                            w[..., 5120:7168]], axis=-1)


def _lane_pad(v, at):
    return jnp.pad(v.astype(f32), (at, LANES - at - v.shape[0])).reshape(1, LANES)


def _layer_params(w):
    row = lambda v: v.reshape(1, -1).astype(f32)
    return dict(
        g1=row(w["pre_mix_g"]), g2=row(w["post_mix_g"]), g3=row(w["pre_mlp_g"]), g4=row(w["post_mlp_g"]),
        w_in=_proj_cols(w["w_in"]).astype(bf16), conv_w=w["dn_conv_w"].astype(f32),
        avec=_lane_pad(w["dn_a_log"], DN_HEADS), dvec=_lane_pad(w["dn_dt_bias"], DN_HEADS),
        ng=row(w["dn_norm_g"]), sinks=_lane_pad(w["sw_sinks"], 0),
        w_up_dn=w["w_up_dn"].astype(bf16), w_up_sw=w["w_up_sw"].astype(bf16), w_o=w["w_o"].astype(bf16),
        w_ff1=w["w_ff1"].astype(bf16), w_ff2=w["w_ff2"].astype(bf16))


def _layer_grads_ref_layout(g):
    return dict(
        pre_mix_g=g["g1"][0], post_mix_g=g["g2"][0], pre_mlp_g=g["g3"][0], post_mlp_g=g["g4"][0],
        dn_conv_w=g["conv_w"],
        dn_a_log=g["avec"][0, DN_HEADS:2 * DN_HEADS], dn_dt_bias=g["dvec"][0, DN_HEADS:2 * DN_HEADS],
        dn_norm_g=g["ng"][0], sw_sinks=g["sinks"][0, :SW_Q_HEADS])


def _consts(positions):
    T = positions.shape[0]
    half = ROT_DIM // 2
    inv_freq = ROPE_THETA ** (-jnp.arange(half, dtype=f32) * (2.0 / ROT_DIM))
    ang = positions.astype(f32)[:, None] * inv_freq
    cos8, sin8 = jnp.cos(ang), jnp.sin(ang)
    rest = SW_HEAD_DIM - ROT_DIM
    c64 = jnp.concatenate([cos8, cos8, jnp.ones((T, rest), f32)], axis=-1)
    s64 = jnp.concatenate([sin8, sin8, jnp.zeros((T, rest), f32)], axis=-1)
    sel = np.zeros((2, 2, LANES, LANES), np.float32)
    for hk in range(SW_KV_HEADS):
        for d in range(SW_HEAD_DIM):
            sel[hk, 0, SW_HEAD_DIM * hk + d, d] = 1.0
            sel[hk, 1, SW_HEAD_DIM * hk + d, SW_HEAD_DIM + d] = 1.0
    return dict(cos=jnp.concatenate([c64, c64], axis=-1), sin=jnp.concatenate([s64, s64], axis=-1),
                sel_a0=jnp.asarray(sel[0, 0]), sel_b0=jnp.asarray(sel[0, 1]),
                sel_a1=jnp.asarray(sel[1, 0]), sel_b1=jnp.asarray(sel[1, 1]))


def _loss(y, tgt):
    T, W = y.shape
    TM = min(TM_ROW, T)
    n = T // TM

    def body(y_ref, t_ref, dy_ref, acc_ref):
        @pl.when(pl.program_id(0) == 0)
        def _():
            acc_ref[...] = jnp.zeros_like(acc_ref)

        d = y_ref[...] - t_ref[...]
        dy_ref[...] = d * (1.0 / W)
        acc_ref[...] += jnp.sum(d * d, axis=0, keepdims=True)

    row = pl.BlockSpec((TM, W), lambda i: (i, 0))
    return pl.pallas_call(
        body, name="loss", grid=(n,), in_specs=[row, row],
        out_specs=[row, pl.BlockSpec((1, W), lambda i: (0, 0))],
        out_shape=[jax.ShapeDtypeStruct((T, W), f32), jax.ShapeDtypeStruct((1, W), f32)],
        compiler_params=pltpu.CompilerParams(dimension_semantics=("arbitrary",)),
    )(y, tgt)


N_CHIP = 4
N_SEM = N_DEV - 1


def _exchange(name, arrs, modes):
    n = len(arrs)
    lead = {"gather": N_DEV, "gather2": N_DEV, "to_sibling": N_CHIP, "to_chips": N_CHIP}
    outs_shape = [jax.ShapeDtypeStruct((lead[md],) + (a.shape if md.startswith("gather") else a.shape[1:]), a.dtype)
                  for a, md in zip(arrs, modes)]

    def body(*refs):
        ins, outs = refs[:n], refs[n:2 * n]
        send_sems, recv_sems, loc_sems = refs[2 * n:]
        x, y, c = lax.axis_index("x"), lax.axis_index("y"), lax.axis_index("c")
        me, chip, sib = 4 * x + 2 * y + c, 2 * x + y, (x, y, 1 - c)
        flips = [(1 - x, y), (x, 1 - y), (1 - x, 1 - y)]

        def rcopy(a, k, src, dst, dev):
            return pltpu.make_async_remote_copy(src_ref=src, dst_ref=dst, send_sem=send_sems.at[a, k],
                                                recv_sem=recv_sems.at[a, k], device_id=dev,
                                                device_id_type=pl.DeviceIdType.MESH)

        sends, recvs, local, passes = [], [], [], []
        for a, md in enumerate(modes):
            src_all, out = ins[a], outs[a]
            if md in ("gather", "gather2"):
                local.append(pltpu.make_async_copy(src_all, out.at[me], loc_sems.at[a]))
            if md == "to_chips":
                local.append(pltpu.make_async_copy(src_all.at[chip], out.at[chip], loc_sems.at[a]))
            if md == "gather":
                for k in range(1, N_DEV):
                    px = 1 - x if (k >> 2) & 1 else x
                    py = 1 - y if (k >> 1) & 1 else y
                    pc = 1 - c if k & 1 else c
                    sends.append(rcopy(a, k - 1, src_all, out.at[me], (px, py, pc)))
                    recvs.append(rcopy(a, k - 1, src_all, out.at[4 * px + 2 * py + pc], (px, py, pc)))
            elif md == "gather2":
                sends.append(rcopy(a, 0, src_all, out.at[me], sib))
                recvs.append(rcopy(a, 0, src_all, out.at[4 * x + 2 * y + 1 - c], sib))
                for j, (px, py) in enumerate(flips):
                    sends.append(rcopy(a, 1 + j, src_all, out.at[me], (px, py, c)))
                    theirs = out.at[4 * px + 2 * py + c]
                    arrive = rcopy(a, 1 + j, src_all, theirs, (px, py, c))
                    passes.append((arrive, rcopy(a, 4 + j, theirs, theirs, sib)))
                    recvs.append(rcopy(a, 4 + j, src_all, out.at[4 * px + 2 * py + 1 - c], sib))
            elif md == "to_sibling":
                for j in range(N_CHIP):
                    sends.append(rcopy(a, j, src_all.at[2 * j + 1 - c], out.at[j], sib))
                    recvs.append(rcopy(a, j, src_all.at[2 * j + c], out.at[j], sib))
            elif md == "to_chips":
                for j, (px, py) in enumerate(flips):
                    sends.append(rcopy(a, j, src_all.at[2 * px + py], out.at[chip], (px, py, c)))
                    recvs.append(rcopy(a, j, src_all.at[chip], out.at[2 * px + py], (px, py, c)))
        for cp in local + sends:
            cp.start()
        for arrive, onward in passes:
            arrive.wait_recv()
            onward.start()
        for cp in recvs:
            cp.wait_recv()
        for cp in sends:
            cp.wait_send()
        for _, onward in passes:
            onward.wait_send()
        for cp in local:
            cp.wait()

    hbm = pl.BlockSpec(memory_space=pltpu.HBM)
    res = pl.pallas_call(
        body, name=name,
        in_specs=[hbm] * n, out_specs=[hbm] * n, out_shape=outs_shape,
        scratch_shapes=[pltpu.SemaphoreType.DMA((n, N_SEM)), pltpu.SemaphoreType.DMA((n, N_SEM)),
                        pltpu.SemaphoreType.DMA((n,))],
    )(*arrs)
    return list(res)


def _add_bf16(name, a, b, tr):
    L_, R_, C_ = a.shape
    tr = min(tr, R_)
    assert R_ % tr == 0, (name, R_, tr)

    def body(a_ref, b_ref, o_ref):
        o_ref[...] = (a_ref[...].astype(f32) + b_ref[...].astype(f32)).astype(bf16)

    blk = pl.BlockSpec((1, tr, C_), lambda l, i: (l, i, 0))
    return pl.pallas_call(
        body, name=name, grid=(L_, R_ // tr), in_specs=[blk, blk], out_specs=blk,
        out_shape=jax.ShapeDtypeStruct(a.shape, bf16),
        compiler_params=pltpu.CompilerParams(dimension_semantics=("arbitrary", "arbitrary")),
    )(a, b)


def _adamw(name, land, w, m, v, tr):
    L_, R_, C_ = w.shape
    n_slots = land.shape[0]
    tr = min(tr, R_)
    assert R_ % tr == 0, (name, R_, tr)
    c1 = 1.0 - ADAM_B1 ** ADAM_STEP
    c2 = 1.0 - ADAM_B2 ** ADAM_STEP

    def body(l_ref, w_ref, m_ref, v_ref, g_ref, d_ref, mo_ref, vo_ref):
        g = l_ref[0].astype(f32)
        for s in range(1, n_slots):
            g = g + l_ref[s].astype(f32)
        m_new = ADAM_B1 * m_ref[...] + (1.0 - ADAM_B1) * g
        v_new = ADAM_B2 * v_ref[...] + (1.0 - ADAM_B2) * jnp.square(g)
        m_hat = m_new / c1
        v_hat = v_new / c2
        g_ref[...] = g
        d_ref[...] = -ADAM_LR * (m_hat / (jnp.sqrt(v_hat) + ADAM_EPS) + ADAM_WD * w_ref[...])
        mo_ref[...] = m_new
        vo_ref[...] = v_new

    row = pl.BlockSpec((1, tr, C_), lambda l, i: (l, i, 0))
    return pl.pallas_call(
        body, name=name, grid=(L_, R_ // tr),
        in_specs=[pl.BlockSpec((n_slots, 1, tr, C_), lambda l, i: (0, l, i, 0)), row, row, row],
        out_specs=[row] * 4, out_shape=[jax.ShapeDtypeStruct((L_, R_, C_), f32)] * 4,
        compiler_params=pltpu.CompilerParams(dimension_semantics=("arbitrary", "arbitrary")),
    )(land, w, m, v)


_BIG = ("w_in", "dn_conv_w", "w_up_dn", "w_up_sw", "w_o", "w_ff1", "w_ff2")
_COL_SHARDED = ("w_in", "dn_conv_w", "w_ff1")
_SMALL_ROWS = ("pre_mix_g", "post_mix_g", "pre_mlp_g", "post_mlp_g")
_SMALL_MISC = ("dn_a_log", "dn_dt_bias", "dn_norm_g", "sw_sinks")
_WEIGHTS = ("pre_mix_g", "w_in", "dn_conv_w", "dn_a_log", "dn_dt_bias", "dn_norm_g", "sw_sinks", "w_up_dn", "w_up_sw",
            "w_o", "post_mix_g", "pre_mlp_g", "w_ff1", "w_ff2", "post_mlp_g")
_SMALL_PACK_ROWS = 24


def _unshard(name, g):
    if name in _COL_SHARDED:
        g = jnp.moveaxis(g, 0, -2)
        return g.reshape(g.shape[:-2] + (g.shape[-2] * g.shape[-1],))
    g = jnp.moveaxis(g, 0, 1)
    return g.reshape((g.shape[0], g.shape[1] * g.shape[2]) + g.shape[3:])


def _shard_major(name, full):
    if name in _COL_SHARDED:
        s = full.reshape(full.shape[:-1] + (N_DEV, full.shape[-1] // N_DEV))
        return jnp.moveaxis(s, -2, 0)
    s = full.reshape((full.shape[0], N_DEV, full.shape[1] // N_DEV) + full.shape[2:])
    return jnp.moveaxis(s, 1, 0)


def _pack_small(d):
    rows = jnp.concatenate([d[n] for n in _SMALL_ROWS], axis=0)
    misc = jnp.concatenate([d[n].reshape(-1) for n in _SMALL_MISC])
    misc = jnp.pad(misc, (0, W1 - misc.shape[0])).reshape(1, W1)
    out = jnp.concatenate([rows, misc], axis=0)
    return jnp.pad(out, ((0, _SMALL_PACK_ROWS - out.shape[0]), (0, 0)))


def _unpack_small(a, like):
    out, L = {}, like[_SMALL_ROWS[0]].shape[0]
    for i, n in enumerate(_SMALL_ROWS):
        out[n] = a[L * i:L * (i + 1)]
    at, row = 0, a[L * len(_SMALL_ROWS)]
    for n in _SMALL_MISC:
        size = like[n].size
        out[n] = row[at:at + size].reshape(like[n].shape)
        at += size
    return out


def kernel(x, positions, pre_mix_g, w_in, dn_conv_w, dn_a_log, dn_dt_bias, dn_norm_g, sw_sinks, w_up_dn, w_up_sw, w_o, post_mix_g, pre_mlp_g, w_ff1, w_ff2, post_mlp_g, loss_target, m_pre_mix_g, m_w_in, m_dn_conv_w, m_dn_a_log, m_dn_dt_bias, m_dn_norm_g, m_sw_sinks, m_w_up_dn, m_w_up_sw, m_w_o, m_post_mix_g, m_pre_mlp_g, m_w_ff1, m_w_ff2, m_post_mlp_g, v_pre_mix_g, v_w_in, v_dn_conv_w, v_dn_a_log, v_dn_dt_bias, v_dn_norm_g, v_sw_sinks, v_w_up_dn, v_w_up_sw, v_w_o, v_post_mix_g, v_pre_mlp_g, v_w_ff1, v_w_ff2, v_post_mlp_g):
    w = dict(pre_mix_g=pre_mix_g, w_in=w_in, dn_conv_w=dn_conv_w, dn_a_log=dn_a_log, dn_dt_bias=dn_dt_bias,
             dn_norm_g=dn_norm_g, sw_sinks=sw_sinks, w_up_dn=w_up_dn, w_up_sw=w_up_sw, w_o=w_o, post_mix_g=post_mix_g,
             pre_mlp_g=pre_mlp_g, w_ff1=w_ff1, w_ff2=w_ff2, post_mlp_g=post_mlp_g)
    m = dict(pre_mix_g=m_pre_mix_g, w_in=m_w_in, dn_conv_w=m_dn_conv_w, dn_a_log=m_dn_a_log, dn_dt_bias=m_dn_dt_bias,
             dn_norm_g=m_dn_norm_g, sw_sinks=m_sw_sinks, w_up_dn=m_w_up_dn, w_up_sw=m_w_up_sw, w_o=m_w_o,
             post_mix_g=m_post_mix_g, pre_mlp_g=m_pre_mlp_g, w_ff1=m_w_ff1, w_ff2=m_w_ff2, post_mlp_g=m_post_mlp_g)
    v = dict(pre_mix_g=v_pre_mix_g, w_in=v_w_in, dn_conv_w=v_dn_conv_w, dn_a_log=v_dn_a_log, dn_dt_bias=v_dn_dt_bias,
             dn_norm_g=v_dn_norm_g, sw_sinks=v_sw_sinks, w_up_dn=v_w_up_dn, w_up_sw=v_w_up_sw, w_o=v_w_o,
             post_mix_g=v_post_mix_g, pre_mlp_g=v_pre_mlp_g, w_ff1=v_w_ff1, w_ff2=v_w_ff2, post_mlp_g=v_post_mlp_g)
    n_layers = pre_mix_g.shape[0]
    xs, pos, tgt = x[0], positions[0], loss_target[0]

    payload = [w[n] if n == "dn_conv_w" else w[n].astype(bf16) for n in _BIG]
    gathered = _exchange("allgather_weights", payload, ["gather2"] * len(_BIG))
    full = {n: _unshard(n, g) for n, g in zip(_BIG, gathered)}
    layers = []
    for l in range(n_layers):
        wl = {n: (full[n][l] if n in _BIG else w[n][l]) for n in _WEIGHTS}
        layers.append(_layer_params(wl))
    cst = _consts(pos)

    h, res = xs, []
    for l in range(n_layers):
        h, r = _layer_fwd(h, layers[l], cst)
        res.append(r)
    dy, sq = _loss(h, tgt)
    loss = lax.psum(0.5 / D_MODEL * jnp.sum(sq), ("x", "y", "c"))
    grads, gbuf = [None] * n_layers, _grad_buffers(n_layers)
    for l in reversed(range(n_layers)):
        dy, g, gbuf = _layer_bwd(dy, res[l], layers[l], cst, gbuf, l)
        grads[l] = _layer_grads_ref_layout(g)
    grad_x = dy[None]
    gfull = {n: jnp.stack([grads[l][n] for l in range(n_layers)]) for n in _WEIGHTS if n not in _MATMUL_WEIGHTS}
    gfull.update({n: (_proj_cols_inv(gbuf[n]) if n == "w_in" else gbuf[n]) for n in _MATMUL_WEIGHTS})

    nb = len(_BIG)
    send = [_shard_major(n, gfull[n]).astype(bf16) for n in _BIG]
    stage_a = _exchange("grads_to_sibling", send + [_pack_small(gfull)], ["to_sibling"] * nb + ["gather"])
    core = lax.axis_index("c")
    summed = []
    for n, mine, theirs in zip(_BIG, send, stage_a[:nb]):
        own = lax.dynamic_index_in_dim(mine.reshape((N_CHIP, 2) + mine.shape[1:]), core, axis=1, keepdims=False)
        lead = (-1,) + own.shape[-2:]
        summed.append(_add_bf16("add_" + n, own.reshape(lead), theirs.reshape(lead), tr=512).reshape(own.shape))
    landed = _exchange("grads_to_chips", summed, ["to_chips"] * nb)

    out_g, out_d, out_m, out_v = {}, {}, {}, {}
    for n, land in zip(_BIG, landed):
        out_g[n], out_d[n], out_m[n], out_v[n] = _adamw("adamw_" + n, land, w[n], m[n], v[n], tr=256)
    small = _adamw("adamw_small", stage_a[nb][:, None], _pack_small(w)[None], _pack_small(m)[None], _pack_small(v)[None],
                   tr=_SMALL_PACK_ROWS)
    for dst, a in zip((out_g, out_d, out_m, out_v), small):
        dst.update(_unpack_small(a[0], w))
    return (loss, grad_x, *[out_g[n] for n in _WEIGHTS], *[out_d[n] for n in _WEIGHTS],
            *[out_m[n] for n in _WEIGHTS], *[out_v[n] for n in _WEIGHTS])
```

```python
import math

import numpy as np
import jax
import jax.numpy as jnp
from jax import lax
from jax.experimental import pallas as pl
from jax.experimental.pallas import tpu as pltpu

f32 = jnp.float32
bf16 = jnp.bfloat16
HIGHEST = lax.Precision.HIGHEST

N_DEV = 8
D_MODEL = 1024
DN_HEADS = 8
DN_DK = 128
DN_CHUNK = 128
DN_CONV = 4
SW_Q_HEADS = 16
SW_KV_HEADS = 2
SW_HEAD_DIM = 64
SW_BLOCK = 128
ROPE_THETA = 500000.0
ROT_DIM = SW_HEAD_DIM // 4
D_FF = 4 * D_MODEL
EPS = 1e-6
LANES = 128
CONV_HALO = 8
NEG_BIG = -1e30

ADAM_LR = 0.001
ADAM_B1 = 0.9
ADAM_B2 = 0.999
ADAM_EPS = 1e-08
ADAM_WD = 0.01
ADAM_STEP = 10

PROJ_W = 7680
CB_Q, CB_K, CB_V, CB_Z, CB_SWQ, CB_GA, CB_GB = 0, 1, 2, 3, 4, 5, 6
CB_SWK, CB_SWV, CB_BA = 56, 57, 58

NN = ((1,), (0,))
NT = ((1,), (1,))
TN = ((0,), (0,))


def _mm(a, b, dims, hi=False):
    if hi:
        return lax.dot_general(a.astype(f32), b.astype(f32), (dims, ((), ())), precision=HIGHEST,
                               preferred_element_type=f32)
    return lax.dot_general(a.astype(bf16), b.astype(bf16), (dims, ((), ())), preferred_element_type=f32)


def _matmul(name, a, b, form, out_dtype, tm=512, tn=512, tk=1024, extra=(), epilogue=None):
    if form == "nn":
        (M, K), (_, N) = a.shape, b.shape
    elif form == "nt":
        (M, K), (N, _) = a.shape, b.shape
    else:
        (K, M), (_, N) = a.shape, b.shape
    tm, tn, tk = min(tm, M), min(tn, N), min(tk, K)
    assert M % tm == 0 and N % tn == 0 and K % tk == 0, (name, M, N, K, tm, tn, tk)
    nk = K // tk
    dims = {"nn": NN, "nt": NT, "tn": TN}[form]
    out_dtypes = [out_dtype] if epilogue is None else list(out_dtype)
    ne, no = len(extra), len(out_dtypes)

    def body(a_ref, b_ref, *rest):
        e_refs, o_refs, acc_ref = rest[:ne], rest[ne:ne + no], rest[ne + no]

        def finish(acc):
            vals = [acc] if epilogue is None else epilogue(acc, [e[...] for e in e_refs])
            for o, val in zip(o_refs, vals):
                o[...] = val.astype(o.dtype)

        part = lax.dot_general(a_ref[...], b_ref[...], (dims, ((), ())), preferred_element_type=f32)
        if nk == 1:
            finish(part)
        else:
            k = pl.program_id(2)

            @pl.when(k == 0)
            def _():
                acc_ref[...] = part

            @pl.when(k > 0)
            def _():
                acc_ref[...] += part

            @pl.when(k == nk - 1)
            def _():
                finish(acc_ref[...])

    if form == "tn":
        a_spec = pl.BlockSpec((tk, tm), lambda i, j, k: (k, i))
    else:
        a_spec = pl.BlockSpec((tm, tk), lambda i, j, k: (i, k))
    if form == "nt":
        b_spec = pl.BlockSpec((tn, tk), lambda i, j, k: (j, k))
    else:
        b_spec = pl.BlockSpec((tk, tn), lambda i, j, k: (k, j))
    tile = pl.BlockSpec((tm, tn), lambda i, j, k: (i, j))
    res = pl.pallas_call(
        body, name=name,
        grid=(M // tm, N // tn, nk),
        in_specs=[a_spec, b_spec] + [tile] * ne,
        out_specs=[tile] * no,
        out_shape=[jax.ShapeDtypeStruct((M, N), dt) for dt in out_dtypes],
        scratch_shapes=[pltpu.VMEM((tm, tn) if nk > 1 else (8, 128), f32)],
        compiler_params=pltpu.CompilerParams(dimension_semantics=("parallel", "parallel", "arbitrary")),
    )(a, b, *extra)
    return res[0] if epilogue is None else list(res)


def _tile_specs(ins, halo_ids, params, TM, HR, row_of):
    specs = [pl.BlockSpec((TM, w), lambda i, cb=cb: (row_of(i), cb)) for (_, w, cb) in ins]
    for h in halo_ids:
        _, w, cb = ins[h]
        specs.append(pl.BlockSpec((HR, w), lambda i, cb=cb: (jnp.maximum(row_of(i) * (TM // HR) - 1, 0), cb)))
    for p in params:
        specs.append(pl.BlockSpec(p.shape, lambda i, nd=p.ndim: (0,) * nd))
    return specs


def _tile_fwd(name, fn, T, TM, ins, params, outs, halo_ids=(), HR=CONV_HALO):
    TM = min(TM, T)
    n = T // TM
    ni, nh, npar = len(ins), len(halo_ids), len(params)

    def body(*refs):
        in_v = [r[...] for r in refs[:ni]]
        halo_v = [r[...] for r in refs[ni:ni + nh]]
        par_v = [r[...] for r in refs[ni + nh:ni + nh + npar]]
        o_refs = refs[ni + nh + npar:]
        first = pl.program_id(0) == 0
        vals = fn(first, in_v, halo_v, par_v)
        for o, val in zip(o_refs, vals):
            o[...] = val.astype(o.dtype)

    res = pl.pallas_call(
        body, name=name, grid=(n,),
        in_specs=_tile_specs(ins, halo_ids, params, TM, HR, lambda i: i),
        out_specs=[pl.BlockSpec((TM, w), lambda i: (i, 0)) for (w, _) in outs],
        out_shape=[jax.ShapeDtypeStruct((T, w), dt) for (w, dt) in outs],
        compiler_params=pltpu.CompilerParams(dimension_semantics=("arbitrary",)),
    )(*[a for (a, _, _) in ins], *[ins[h][0] for h in halo_ids], *params)
    return list(res)


def _tile_bwd(name, fn, T, TM, ins, params, cts, din, dpar, halo_ids=(), HR=CONV_HALO):
    TM = min(TM, T)
    n = T // TM
    ni, nh, npar, nc = len(ins), len(halo_ids), len(params), len(cts)
    din_ids = [j for (j, _) in din]
    dh_ids = [h for h in halo_ids if h in din_ids]
    nd, ndp, ndh = len(din), len(dpar), len(dh_ids)

    def body(*refs):
        in_v = [r[...] for r in refs[:ni]]
        halo_v = [r[...] for r in refs[ni:ni + nh]]
        par_v = [r[...] for r in refs[ni + nh:ni + nh + npar]]
        ct_v = [r[...].astype(f32) for r in refs[ni + nh + npar:ni + nh + npar + nc]]
        o_refs = refs[ni + nh + npar + nc:ni + nh + npar + nc + nd + ndp]
        carry_refs = refs[ni + nh + npar + nc + nd + ndp:]
        i = pl.program_id(0)
        first = i == n - 1

        def g(d_in, d_halo, d_par):
            full_in = list(in_v)
            for j, val in zip(din_ids, d_in):
                full_in[j] = val
            full_halo = list(halo_v)
            for h, val in zip(dh_ids, d_halo):
                full_halo[list(halo_ids).index(h)] = val
            full_par = list(par_v)
            for j, val in zip(dpar, d_par):
                full_par[j] = val
            return tuple(fn(first, full_in, full_halo, full_par))

        prim = ([in_v[j].astype(f32) for j in din_ids],
                [halo_v[list(halo_ids).index(h)].astype(f32) for h in dh_ids],
                [par_v[j] for j in dpar])
        _, vjp = jax.vjp(g, *prim)
        g_in, g_halo, g_par = vjp(tuple(ct_v))

        @pl.when(i == 0)
        def _():
            for c in carry_refs:
                c[...] = jnp.zeros_like(c)
            for o in o_refs[nd:]:
                o[...] = jnp.zeros_like(o)

        for slot, (j, _) in enumerate(din):
            val = g_in[slot]
            if j in dh_ids:
                c = carry_refs[dh_ids.index(j)]
                val = jnp.concatenate([val[:TM - HR], val[TM - HR:] + c[...]], axis=0) if TM > HR else val + c[...]
                c[...] = g_halo[dh_ids.index(j)]
            o_refs[slot][...] = val.astype(o_refs[slot].dtype)
        for slot in range(ndp):
            o_refs[nd + slot][...] += g_par[slot]

    rev = lambda i: n - 1 - i
    in_specs = _tile_specs(ins, halo_ids, params, TM, HR, rev)
    ct_specs = [pl.BlockSpec((TM, w), lambda i, cb=cb: (rev(i), cb)) for (_, w, cb) in cts]
    out_specs = [pl.BlockSpec((TM, ins[j][1]), lambda i: (rev(i), 0)) for j in din_ids]
    out_specs += [pl.BlockSpec(params[j].shape, lambda i, nd_=params[j].ndim: (0,) * nd_) for j in dpar]
    out_shape = [jax.ShapeDtypeStruct((T, ins[j][1]), dt) for (j, dt) in din]
    out_shape += [jax.ShapeDtypeStruct(params[j].shape, f32) for j in dpar]
    res = pl.pallas_call(
        body, name=name, grid=(n,),
        in_specs=in_specs + ct_specs,
        out_specs=out_specs,
        out_shape=out_shape,
        scratch_shapes=[pltpu.VMEM((HR, ins[h][1]), f32) for h in dh_ids],
        compiler_params=pltpu.CompilerParams(dimension_semantics=("arbitrary",)),
    )(*[a for (a, _, _) in ins], *[ins[h][0] for h in halo_ids], *params, *[a for (a, _, _) in cts])
    return list(res)


def _rms(x, g):
    return x * lax.rsqrt(jnp.mean(x * x, axis=-1, keepdims=True) + EPS) * g


def _fn_prenorm(first, ins, halos, params):
    (x,), (g,) = ins, params
    x = x.astype(f32)
    return [_rms(x, g), x]


def _fn_postmix(first, ins, halos, params):
    (x, mix), (g2, g3) = ins, params
    x1 = x + _rms(mix, g2)
    return [x1, _rms(x1, g3)]


def _fn_postmlp(first, ins, halos, params):
    (x1, ff), (g4,) = ins, params
    return [x1 + _rms(ff, g4)]


def _fn_rms_only(first, ins, halos, params):
    (ff,), (g4,) = ins, params
    return [_rms(ff, g4)]


def _fn_merge(first, ins, halos, params):
    ga, gb, ya, yb = ins
    return [jax.nn.sigmoid(ga) * ya + jax.nn.sigmoid(gb) * yb]


def _roll_rows(x, shift):
    return pltpu.roll(x, shift, 0)


_row_roll = jax.custom_vjp(_roll_rows, nondiff_argnums=(1,))
_row_roll.defvjp(lambda x, shift: (_roll_rows(x, shift), None),
                 lambda shift, _, ct: (_roll_rows(ct, ct.shape[0] - shift),))


def _make_fn_conv(norm_scale):
    def fn(first, ins, halos, params):
        (x,), (xp,), (w,) = ins, halos, params
        xp = jnp.where(first, 0.0, xp)
        outs = []
        for h in range(DN_HEADS):
            sl = slice(DN_DK * h, DN_DK * (h + 1))
            xe, wh = jnp.concatenate([xp[:, sl], x[:, sl]], axis=0), w[:, sl]
            y = xe[CONV_HALO:] * wh[DN_CONV - 1:DN_CONV]
            for j in range(DN_CONV - 1):
                y = y + _row_roll(xe, DN_CONV - 1 - j)[CONV_HALO:] * wh[j:j + 1]
            y = jax.nn.silu(y)
            if norm_scale is not None:
                y = y * lax.rsqrt(jnp.sum(y * y, axis=-1, keepdims=True) + EPS) * norm_scale
            outs.append(y)
        return [jnp.concatenate(outs, axis=-1)]
    return fn


def _fn_gates(first, ins, halos, params):
    (ba,), (avec, dvec) = ins, params
    lane = lax.broadcasted_iota(jnp.int32, ba.shape, 1)
    beta = jax.nn.sigmoid(ba)
    g = -jnp.exp(avec) * jax.nn.softplus(ba + dvec)
    return [jnp.where(lane < DN_HEADS, beta, jnp.where(lane < 2 * DN_HEADS, g, 0.0))]


def _fn_dnpost(first, ins, halos, params):
    (o, z), (ng,) = ins, params
    outs = []
    for h in range(DN_HEADS):
        sl = slice(DN_DK * h, DN_DK * (h + 1))
        outs.append(_rms(o[:, sl], ng) * jax.nn.silu(z[:, sl]))
    return [jnp.concatenate(outs, axis=-1)]


def _roll_lanes(x, shift):
    return pltpu.roll(x, shift, 1)


_lane_roll = jax.custom_vjp(_roll_lanes, nondiff_argnums=(1,))
_lane_roll.defvjp(lambda x, shift: (_roll_lanes(x, shift), None),
                  lambda shift, _, ct: (_roll_lanes(ct, LANES - shift),))


def _attn_probs(qp, kx, sink, mask):
    heads = range(len(sink))
    s = [jnp.where(mask, _mm(qp[g // 2], kx[g % 2], NT), NEG_BIG) for g in heads]
    m = [jnp.maximum(jnp.max(s[g], axis=-1, keepdims=True), sink[g]) for g in heads]
    p = [jnp.exp(s[g] - m[g]) for g in heads]
    ps = [jnp.exp(sink[g] - m[g]) for g in heads]
    inv = [1.0 / (jnp.sum(p[g], axis=-1, keepdims=True) + ps[g]) for g in heads]
    return [p[g] * inv[g] for g in heads], [ps[g] * inv[g] for g in heads]


@jax.custom_vjp
def _attn_group(qp, kx, vx, sink, mask):
    probs, _ = _attn_probs(qp, kx, sink, mask)
    return tuple(_mm(probs[g], vx[g % 2], NN) for g in range(len(sink)))


def _attn_group_fwd(qp, kx, vx, sink, mask):
    o = _attn_group(qp, kx, vx, sink, mask)
    return o, (qp, kx, vx, sink, mask, o)


def _attn_group_bwd(res, do):
    qp, kx, vx, sink, mask, o = res
    heads = range(len(sink))
    probs, p_sink = _attn_probs(qp, kx, sink, mask)
    d_probs = [_mm(do[g], vx[g % 2], NT) for g in heads]
    dot = [jnp.sum(do[g] * o[g], axis=-1, keepdims=True) for g in heads]
    ds = [probs[g] * (d_probs[g] - dot[g]) for g in heads]
    d_qp = tuple(_mm(ds[2 * j], kx[0], NN) + _mm(ds[2 * j + 1], kx[1], NN) for j in range(len(qp)))
    rows = lambda xs: jnp.concatenate(xs, axis=0)
    d_kx = tuple(_mm(rows([ds[g] for g in heads if g % 2 == c]), rows([qp[g // 2] for g in heads if g % 2 == c]), TN)
                 for c in range(2))
    d_vx = tuple(_mm(rows([probs[g] for g in heads if g % 2 == c]), rows([do[g] for g in heads if g % 2 == c]), TN)
                 for c in range(2))
    d_sink = tuple(-jnp.sum(p_sink[g] * dot[g], axis=0, keepdims=True) for g in heads)
    return d_qp, d_kx, d_vx, d_sink, None


_attn_group.defvjp(_attn_group_fwd, _attn_group_bwd)


def _fn_swa(first, ins, halos, params):
    q, k, v, cos, sin = ins
    kp, vp, cosp, sinp = halos
    sinks, sel_a0, sel_b0, sel_a1, sel_b1 = params
    B = q.shape[0]
    half = ROT_DIM // 2
    in_head = jnp.bitwise_and(lax.broadcasted_iota(jnp.int32, (1, LANES), 1), SW_HEAD_DIM - 1)

    def rope(x, c, s):
        return (x * c + _lane_roll(x, LANES - half) * jnp.where(in_head < half, -s, 0.0)
                + _lane_roll(x, half) * jnp.where(in_head >= half, s, 0.0))

    kcat = jnp.concatenate([rope(kp, cosp, sinp), rope(k, cos, sin)], axis=0)
    vcat = jnp.concatenate([vp, v], axis=0)
    r = lax.broadcasted_iota(jnp.int32, (B, 2 * B), 0)
    c = lax.broadcasted_iota(jnp.int32, (B, 2 * B), 1)
    mask = (c > r) & (c <= r + B) & ((c >= B) | jnp.logical_not(first))
    group = SW_Q_HEADS // SW_KV_HEADS
    sels = ((sel_a0, sel_b0), (sel_a1, sel_b1))
    outs = []
    for hk in range(SW_KV_HEADS):
        heads = range(group)
        kx = [_mm(kcat, sels[hk][half], NN) for half in range(2)]
        vx = [_mm(vcat, sels[hk][half], NN) for half in range(2)]
        qp = [rope(q[:, LANES * j:LANES * (j + 1)], cos, sin) * (SW_HEAD_DIM ** -0.5)
              for j in range(hk * group // 2, (hk + 1) * group // 2)]
        sink = tuple(sinks[:, hk * group + g:hk * group + g + 1] for g in heads)
        o = _attn_group(tuple(qp), tuple(kx), tuple(vx), sink, mask)
        outs += [o[2 * j] + o[2 * j + 1] for j in range(group // 2)]
    return [jnp.concatenate(outs, axis=-1)]


@jax.custom_vjp
def _inv_unit_lower(Ls):
    C = Ls[0].shape[0]
    ii = lax.broadcasted_iota(jnp.int32, (C, C), 0)
    jj = lax.broadcasted_iota(jnp.int32, (C, C), 1)

    def off_mask(level):
        same_pair = jnp.right_shift(ii, level + 1) == jnp.right_shift(jj, level + 1)
        lower_left = (jnp.bitwise_and(jnp.right_shift(ii, level), 1) == 1) & (jnp.bitwise_and(jnp.right_shift(jj, level), 1) == 0)
        return same_pair & lower_left

    eye = (ii == jj).astype(f32)
    m0 = off_mask(0)
    Ts = [eye - jnp.where(m0, L, 0.0) for L in Ls]
    for level in range(1, int(math.log2(C))):
        mk = off_mask(level)
        left = [_mm(T_, jnp.where(mk, L, 0.0), NN) for T_, L in zip(Ts, Ls)]
        Ts = [T_ - _mm(a, T_, NN) for a, T_ in zip(left, Ts)]
    return tuple(Ts)


def _inv_fwd(Ls):
    Ts = _inv_unit_lower(Ls)
    return Ts, Ts


def _inv_bwd(Ts, dTs):
    left = [_mm(T_, dT, TN) for T_, dT in zip(Ts, dTs)]
    return (tuple(-_mm(a, T_, NT) for a, T_ in zip(left, Ts)),)


_inv_unit_lower.defvjp(_inv_fwd, _inv_bwd)


@jax.custom_vjp
def _inv_known(Ls, Ts):
    return Ts


_inv_known.defvjp(lambda Ls, Ts: (Ts, Ts),
                  lambda Ts, dTs: (_inv_bwd(Ts, dTs)[0], tuple(jnp.zeros_like(t) for t in Ts)))


def _mm_01(a, b, dims):
    hi = b.astype(bf16)
    r1 = b - hi.astype(f32)
    mid = r1.astype(bf16)
    lo = (r1 - mid.astype(f32)).astype(bf16)
    a16 = a.astype(bf16)
    dot = lambda part: lax.dot_general(a16, part, (dims, ((), ())), preferred_element_type=f32)
    return dot(hi) + dot(mid) + dot(lo)


def _eye(n):
    return lax.broadcasted_iota(jnp.int32, (n, n), 0) == lax.broadcasted_iota(jnp.int32, (n, n), 1)


def _lower(n):
    return lax.broadcasted_iota(jnp.int32, (n, n), 0) >= lax.broadcasted_iota(jnp.int32, (n, n), 1)


@jax.custom_vjp
def _transpose(x):
    return _mm_01(_eye(x.shape[1]), x, NT)


_transpose.defvjp(lambda x: (_transpose(x), None), lambda _, ct: (_transpose(ct),))


@jax.custom_vjp
def _cumsum_rows(x):
    return _mm_01(_lower(x.shape[0]), x, NN)


_cumsum_rows.defvjp(lambda x: (_cumsum_rows(x), None), lambda _, ct: (_mm_01(_lower(ct.shape[0]), ct, TN),))


def _dn_chunk(q, k, v, gb, S, tinv_known=None):
    C = q.shape[0]
    H = range(DN_HEADS)
    ii = lax.broadcasted_iota(jnp.int32, (C, C), 0)
    jj = lax.broadcasted_iota(jnp.int32, (C, C), 1)
    causal, strict = ii >= jj, ii > jj
    gc_all = _cumsum_rows(gb)
    gc_t = _transpose(gc_all)
    sl = [slice(DN_DK * h, DN_DK * (h + 1)) for h in H]
    qs, ks, vs = [q[:, s] for s in sl], [k[:, s] for s in sl], [v[:, s] for s in sl]
    beta = [gb[:, h:h + 1] for h in H]
    gcol = [gc_all[:, DN_HEADS + h:DN_HEADS + h + 1] for h in H]
    grow = [gc_t[DN_HEADS + h:DN_HEADS + h + 1, :] for h in H]
    decay = [jnp.where(causal, jnp.exp(jnp.where(causal, gcol[h] - grow[h], 0.0)), 0.0) for h in H]
    kb = [ks[h] * beta[h] for h in H]
    kk = [_mm(kb[h], ks[h], NT) for h in H]
    qk = [_mm(qs[h], ks[h], NT) for h in H]
    Ls = tuple(jnp.where(strict, kk[h] * decay[h], 0.0) for h in H)
    tinv = _inv_unit_lower(Ls) if tinv_known is None else _inv_known(Ls, tinv_known)
    eg = [jnp.exp(gcol[h]) for h in H]
    u = [_mm(tinv[h], vs[h] * beta[h], NN) for h in H]
    w = [_mm(tinv[h], kb[h] * eg[h], NN) for h in H]
    gl = [gcol[h][C - 1:C, :] for h in H]
    ws = [_mm(w[h], S[h], NN) for h in H]
    qS = [_mm(qs[h] * eg[h], S[h], NN) for h in H]
    v_new = [u[h] - ws[h] for h in H]
    av = [_mm(qk[h] * decay[h], v_new[h], NN) for h in H]
    kv = [_mm(ks[h] * jnp.exp(gl[h] - gcol[h]), v_new[h], TN) for h in H]
    o = jnp.concatenate([qS[h] + av[h] for h in H], axis=-1)
    return o, tuple(S[h] * jnp.exp(gl[h]) + kv[h] for h in H), tinv


_DN_W = DN_HEADS * DN_DK


def _dn_inputs(first, xq, xk, xv, hq, hk, hv, ba, cw, avec, dvec):
    conv = lambda scale, x, h, j: _make_fn_conv(scale)(first, [x], [h], [cw[:, _DN_W * j:_DN_W * (j + 1)]])[0]
    return (conv(DN_DK ** -0.5, xq, hq, 0), conv(1.0, xk, hk, 1), conv(None, xv, hv, 2),
            _fn_gates(first, [ba], [], [avec, dvec])[0])


def _delta_specs(C, row_of):
    cols = (CB_Q, CB_K, CB_V)
    specs = [pl.BlockSpec((C, _DN_W), lambda i, cb=cb: (row_of(i), cb)) for cb in cols]
    specs += [pl.BlockSpec((CONV_HALO, _DN_W), lambda i, cb=cb: (jnp.maximum(row_of(i) * (C // CONV_HALO) - 1, 0), cb))
              for cb in cols]
    specs.append(pl.BlockSpec((C, LANES), lambda i: (row_of(i), CB_BA)))
    return specs


def _whole(a):
    return pl.BlockSpec(a.shape, lambda i, nd=a.ndim: (0,) * nd)


def _carrying(body, n_in, n_out, n_scratch, n_steps, ex):
    if ex is None:
        return body

    def wrapped(*refs):
        ins, rest = refs[:n_in], refs[n_in:]
        ex_in, rest = rest[:ex.n], rest[ex.n:]
        outs, rest = rest[:n_out], rest[n_out:]
        ex_out, rest = rest[:ex.n], rest[ex.n:]
        scratch, sems = rest[:n_scratch], rest[n_scratch:]
        start, finish = ex.bind(ex_in, ex_out, *sems)
        step = pl.program_id(0)
        pl.when(step == 0)(start)
        body(*ins, *outs, *scratch)
        pl.when(step == n_steps - 1)(finish)

    return wrapped


def _carried_call(name, body, grid, in_specs, out_specs, out_shape, scratch, operands, ex, ex_arrs):
    n_out = len(out_shape)
    if ex is not None:
        in_specs, out_specs = in_specs + ex.specs, out_specs + ex.specs
        out_shape, scratch, operands = out_shape + ex.out_shape, scratch + ex.scratch, tuple(operands) + tuple(ex_arrs)
    res = pl.pallas_call(
        _carrying(body, len(in_specs) - (ex.n if ex else 0), n_out, len(scratch) - (3 if ex else 0), grid[0], ex),
        name=name, grid=grid, in_specs=in_specs, out_specs=out_specs, out_shape=out_shape, scratch_shapes=scratch,
        compiler_params=pltpu.CompilerParams(dimension_semantics=("arbitrary",)),
    )(*operands)
    return list(res[:n_out]), list(res[n_out:])


def _delta_fwd(proj, cw, avec, dvec, comm=None):
    T = proj.shape[0]
    C = min(DN_CHUNK, T)
    n = T // C
    ex = None if comm is None else _Exchange(*comm)

    def body(xq, xk, xv, hq, hk, hv, ba, cw_ref, a_ref, d_ref, o_ref, hist_ref, tinv_ref,
             q_ref, k_ref, v_ref, gb_ref, s_ref):
        first = pl.program_id(0) == 0

        @pl.when(first)
        def _():
            s_ref[...] = jnp.zeros_like(s_ref)

        S = tuple(s_ref[h] for h in range(DN_HEADS))
        for h in range(DN_HEADS):
            hist_ref[0, h] = S[h]
        q, k, v, gb = _dn_inputs(first, xq[...], xk[...], xv[...], hq[...], hk[...], hv[...], ba[...],
                                 cw_ref[...], a_ref[...], d_ref[...])
        q_ref[...], k_ref[...], v_ref[...], gb_ref[...] = q, k, v, gb
        o, s_new, tinv = _dn_chunk(q, k, v, gb, S)
        o_ref[...] = o
        for h in range(DN_HEADS):
            s_ref[h] = s_new[h]
            tinv_ref[0, h] = tinv[h]

    row = pl.BlockSpec((C, _DN_W), lambda i: (i, 0))
    return _carried_call(
        "delta_fwd", body, (n,),
        _delta_specs(C, lambda i: i) + [_whole(cw), _whole(avec), _whole(dvec)],
        [row, pl.BlockSpec((1, DN_HEADS, DN_DK, DN_DK), lambda i: (i, 0, 0, 0)),
         pl.BlockSpec((1, DN_HEADS, C, C), lambda i: (i, 0, 0, 0)),
         row, row, row, pl.BlockSpec((C, LANES), lambda i: (i, 0))],
        [jax.ShapeDtypeStruct((T, _DN_W), f32), jax.ShapeDtypeStruct((n, DN_HEADS, DN_DK, DN_DK), f32),
         jax.ShapeDtypeStruct((n, DN_HEADS, C, C), f32)]
        + [jax.ShapeDtypeStruct((T, _DN_W), f32)] * 3 + [jax.ShapeDtypeStruct((T, LANES), f32)],
        [pltpu.VMEM((DN_HEADS, DN_DK, DN_DK), f32)],
        (proj, proj, proj, proj, proj, proj, proj, cw, avec, dvec), ex, comm[0] if comm else ())


def _delta_bwd(qn, kn, vv, gb, hist, tinv, do, comm=None):
    T = qn.shape[0]
    C = min(DN_CHUNK, T)
    n = T // C
    ex = None if comm is None else _Exchange(*comm)

    def body(q_ref, k_ref, v_ref, gb_ref, hist_ref, tinv_ref, do_ref, dq_ref, dk_ref, dv_ref, dgb_ref, ds_ref):
        @pl.when(pl.program_id(0) == 0)
        def _():
            ds_ref[...] = jnp.zeros_like(ds_ref)

        S = tuple(hist_ref[0, h] for h in range(DN_HEADS))
        known = tuple(tinv_ref[0, h] for h in range(DN_HEADS))
        chunk = lambda q, k, v, g, s: _dn_chunk(q, k, v, g, s, tinv_known=known)[:2]
        _, vjp = jax.vjp(chunk, q_ref[...], k_ref[...], v_ref[...], gb_ref[...], S)
        dS = tuple(ds_ref[h] for h in range(DN_HEADS))
        dq, dk, dv, dgb, dS_in = vjp((do_ref[...], dS))
        dq_ref[...] = dq
        dk_ref[...] = dk
        dv_ref[...] = dv
        dgb_ref[...] = dgb
        for h in range(DN_HEADS):
            ds_ref[h] = dS_in[h]

    row = pl.BlockSpec((C, _DN_W), lambda i: (n - 1 - i, 0))
    small = pl.BlockSpec((C, LANES), lambda i: (n - 1 - i, 0))
    return _carried_call(
        "delta_bwd", body, (n,),
        [row, row, row, small, pl.BlockSpec((1, DN_HEADS, DN_DK, DN_DK), lambda i: (n - 1 - i, 0, 0, 0)),
         pl.BlockSpec((1, DN_HEADS, C, C), lambda i: (n - 1 - i, 0, 0, 0)), row],
        [row, row, row, small],
        [jax.ShapeDtypeStruct((T, _DN_W), f32)] * 3 + [jax.ShapeDtypeStruct((T, LANES), f32)],
        [pltpu.VMEM((DN_HEADS, DN_DK, DN_DK), f32)],
        (qn, kn, vv, gb, hist, tinv, do), ex, comm[0] if comm else ())


TM_ROW = 256
W1 = D_MODEL


def _first_only(fn):
    return lambda *a: fn(*a)[:1]


def _swa_args(proj, cst):
    ins = [(proj, W1, CB_SWQ), (proj, LANES, CB_SWK), (proj, LANES, CB_SWV), (cst["cos"], LANES, 0), (cst["sin"], LANES, 0)]
    return ins, (1, 2, 3, 4)


def _layer_fwd(x, p, cst, comm=None):
    T = x.shape[0]
    r = {"x": x}
    (h,) = _tile_fwd("prenorm", _first_only(_fn_prenorm), T, TM_ROW, [(x, W1, 0)], [p["g1"]], [(W1, bf16)])
    proj = _matmul("proj", h, p["w_in"], "nn", f32, tm=1024, tn=1536)
    (o, hist, tinv, qn, kn, vv, gbt), comm_out = _delta_fwd(proj, p["conv_w"], p["avec"], p["dvec"], comm)
    (dn_out,) = _tile_fwd("dnpost", _fn_dnpost, T, TM_ROW, [(o, W1, 0), (proj, W1, CB_Z)], [p["ng"]], [(W1, bf16)])
    sw_ins, sw_halo = _swa_args(proj, cst)
    sw_par = [p["sinks"], cst["sel_a0"], cst["sel_b0"], cst["sel_a1"], cst["sel_b1"]]
    (sw_out,) = _tile_fwd("swa", _fn_swa, T, SW_BLOCK, sw_ins, sw_par, [(W1, bf16)], halo_ids=sw_halo, HR=SW_BLOCK)
    y_a = _matmul("up_dn", dn_out, p["w_up_dn"], "nn", f32, tm=1024, tn=1024)
    y_b = _matmul("up_sw", sw_out, p["w_up_sw"], "nn", f32, tm=1024, tn=1024)
    (gated,) = _tile_fwd("merge", _fn_merge, T, TM_ROW,
                         [(proj, W1, CB_GA), (proj, W1, CB_GB), (y_a, W1, 0), (y_b, W1, 0)], [], [(W1, bf16)])
    mix = _matmul("w_o", gated, p["w_o"], "nn", f32, tm=1024, tn=1024)
    x1, h2 = _tile_fwd("postmix", _fn_postmix, T, TM_ROW, [(x, W1, 0), (mix, W1, 0)], [p["g2"], p["g3"]],
                       [(W1, f32), (W1, bf16)])
    ffh, act = _matmul("ff1", h2, p["w_ff1"], "nn", [f32, bf16], tm=1024, tn=1024,
                       epilogue=lambda acc, ex: [acc, jnp.square(jnp.maximum(acc, 0.0))])
    ff = _matmul("ff2", act, p["w_ff2"], "nn", f32, tm=1024, tn=1024)
    (x2,) = _tile_fwd("postmlp", _fn_postmlp, T, TM_ROW, [(x1, W1, 0), (ff, W1, 0)], [p["g4"]], [(W1, f32)])
    r.update(h=h, proj=proj, qn=qn, kn=kn, vv=vv, gbt=gbt, o=o, hist=hist, tinv=tinv, dn_out=dn_out, sw_out=sw_out,
             y_a=y_a, y_b=y_b, gated=gated, mix=mix, h2=h2, ffh=ffh, act=act, ff=ff)
    return x2, r, comm_out


def _layer_bwd(dx2, r, p, cst, comm=None):
    T = dx2.shape[0]
    x, proj = r["x"], r["proj"]
    g = {}
    dff, g["g4"] = _tile_bwd("postmlp_b", _fn_rms_only, T, TM_ROW, [(r["ff"], W1, 0)], [p["g4"]], [(dx2, W1, 0)],
                             [(0, bf16)], [0])
    (dffh,) = _matmul("ff2_dx", dff, p["w_ff2"], "nt", [bf16], tm=1024, tn=1024, extra=[r["ffh"]],
                      epilogue=lambda acc, ex: [acc * (2.0 * jnp.maximum(ex[0], 0.0))])
    g["w_ff2"] = _matmul("ff2_dw", r["act"], dff, "tn", bf16, tm=1024, tn=1024)
    dh2 = _matmul("ff1_dx", dffh, p["w_ff1"], "nt", f32, tm=1024, tn=1024)
    g["w_ff1"] = _matmul("ff1_dw", r["h2"], dffh, "tn", bf16, tm=1024, tn=1024)
    dx1, dmix, g["g2"], g["g3"] = _tile_bwd("postmix_b", _fn_postmix, T, TM_ROW, [(x, W1, 0), (r["mix"], W1, 0)],
                                            [p["g2"], p["g3"]], [(dx2, W1, 0), (dh2, W1, 0)], [(0, f32), (1, bf16)], [0, 1])
    dgated = _matmul("w_o_dx", dmix, p["w_o"], "nt", f32, tm=1024, tn=1024)
    g["w_o"] = _matmul("w_o_dw", r["gated"], dmix, "tn", bf16, tm=1024, tn=1024)
    dga, dgb, dya, dyb = _tile_bwd("merge_b", _fn_merge, T, TM_ROW,
                                   [(proj, W1, CB_GA), (proj, W1, CB_GB), (r["y_a"], W1, 0), (r["y_b"], W1, 0)], [],
                                   [(dgated, W1, 0)], [(0, bf16), (1, bf16), (2, bf16), (3, bf16)], [])
    d_dn = _matmul("up_dn_dx", dya, p["w_up_dn"], "nt", f32, tm=1024, tn=1024)
    g["w_up_dn"] = _matmul("up_dn_dw", r["dn_out"], dya, "tn", bf16, tm=1024, tn=1024)
    d_sw = _matmul("up_sw_dx", dyb, p["w_up_sw"], "nt", f32, tm=1024, tn=1024)
    g["w_up_sw"] = _matmul("up_sw_dw", r["sw_out"], dyb, "tn", bf16, tm=1024, tn=1024)
    do, dz, g["ng"] = _tile_bwd("dnpost_b", _fn_dnpost, T, TM_ROW, [(r["o"], W1, 0), (proj, W1, CB_Z)], [p["ng"]],
                                [(d_dn, W1, 0)], [(0, f32), (1, bf16)], [0])
    (dqn, dkn, dvv, dgbt), comm_out = _delta_bwd(r["qn"], r["kn"], r["vv"], r["gbt"], r["hist"], r["tinv"], do, comm)
    conv_b = lambda nm, cb, scale, ct: _tile_bwd(nm, _make_fn_conv(scale), T, TM_ROW, [(proj, W1, cb)],
                                                 [p["conv_w"][:, W1 * cb:W1 * (cb + 1)]], [(ct, W1, 0)], [(0, bf16)], [0],
                                                 halo_ids=(0,))
    dq_in, dcw_q = conv_b("conv_q_b", CB_Q, DN_DK ** -0.5, dqn)
    dk_in, dcw_k = conv_b("conv_k_b", CB_K, 1.0, dkn)
    dv_in, dcw_v = conv_b("conv_v_b", CB_V, None, dvv)
    g["conv_w"] = jnp.concatenate([dcw_q, dcw_k, dcw_v], axis=-1)
    dba, g["avec"], g["dvec"] = _tile_bwd("gates_b", _fn_gates, T, TM_ROW, [(proj, LANES, CB_BA)], [p["avec"], p["dvec"]],
                                          [(dgbt, LANES, 0)], [(0, bf16)], [0, 1])
    sw_ins, sw_halo = _swa_args(proj, cst)
    sw_par = [p["sinks"], cst["sel_a0"], cst["sel_b0"], cst["sel_a1"], cst["sel_b1"]]
    dswq, dswk, dswv, g["sinks"] = _tile_bwd("swa_b", _fn_swa, T, SW_BLOCK, sw_ins, sw_par, [(d_sw, W1, 0)],
                                             [(0, bf16), (1, bf16), (2, bf16)], [0], halo_ids=sw_halo, HR=SW_BLOCK)
    dproj = jnp.concatenate([dq_in, dk_in, dv_in, dz, dswq, dga, dgb, dswk, dswv, dba, jnp.zeros((T, LANES), bf16)], axis=-1)
    dh = _matmul("proj_dx", dproj, p["w_in"], "nt", f32, tm=1024, tn=1024, tk=1536)
    g["w_in"] = _matmul("proj_dw", r["h"], dproj, "tn", bf16, tm=1024, tn=1536)
    dx, g["g1"] = _tile_bwd("prenorm_b", _fn_prenorm, T, TM_ROW, [(x, W1, 0)], [p["g1"]], [(dh, W1, 0), (dx1, W1, 0)],
                            [(0, f32)], [0])
    return dx, g, comm_out


_OFF_BA, _OFF_SWQ, _OFF_SWK, _OFF_GA, _D_IN = 4096, 4112, 5136, 5392, 7440


def _proj_cols(w):
    pad = lambda n: jnp.zeros(w.shape[:-1] + (n,), w.dtype)
    return jnp.concatenate([w[..., :_OFF_BA], w[..., _OFF_SWQ:_OFF_SWK], w[..., _OFF_GA:_D_IN],
                            w[..., _OFF_SWK:_OFF_GA], w[..., _OFF_BA:_OFF_SWQ], pad(PROJ_W - _D_IN)], axis=-1)


def _proj_cols_inv(w):
    n_ba = _OFF_SWQ - _OFF_BA
    return jnp.concatenate([w[..., :4096], w[..., 7424:7424 + n_ba], w[..., 4096:5120], w[..., 7168:7424],
                            w[..., 5120:7168]], axis=-1)


def _lane_pad(v, at):
    return jnp.pad(v.astype(f32), (at, LANES - at - v.shape[0])).reshape(1, LANES)


def _layer_params(w):
    row = lambda v: v.reshape(1, -1).astype(f32)
    return dict(
        g1=row(w["pre_mix_g"]), g2=row(w["post_mix_g"]), g3=row(w["pre_mlp_g"]), g4=row(w["post_mlp_g"]),
        w_in=_proj_cols(w["w_in"]).astype(bf16), conv_w=w["dn_conv_w"].astype(f32),
        avec=_lane_pad(w["dn_a_log"], DN_HEADS), dvec=_lane_pad(w["dn_dt_bias"], DN_HEADS),
        ng=row(w["dn_norm_g"]), sinks=_lane_pad(w["sw_sinks"], 0),
        w_up_dn=w["w_up_dn"].astype(bf16), w_up_sw=w["w_up_sw"].astype(bf16), w_o=w["w_o"].astype(bf16),
        w_ff1=w["w_ff1"].astype(bf16), w_ff2=w["w_ff2"].astype(bf16))


def _layer_grads_ref_layout(g):
    return dict(
        pre_mix_g=g["g1"][0], post_mix_g=g["g2"][0], pre_mlp_g=g["g3"][0], post_mlp_g=g["g4"][0],
        w_in=_proj_cols_inv(g["w_in"]), dn_conv_w=g["conv_w"],
        dn_a_log=g["avec"][0, DN_HEADS:2 * DN_HEADS], dn_dt_bias=g["dvec"][0, DN_HEADS:2 * DN_HEADS],
        dn_norm_g=g["ng"][0], sw_sinks=g["sinks"][0, :SW_Q_HEADS],
        w_up_dn=g["w_up_dn"], w_up_sw=g["w_up_sw"], w_o=g["w_o"], w_ff1=g["w_ff1"], w_ff2=g["w_ff2"])


def _consts(positions):
    T = positions.shape[0]
    half = ROT_DIM // 2
    inv_freq = ROPE_THETA ** (-jnp.arange(half, dtype=f32) * (2.0 / ROT_DIM))
    ang = positions.astype(f32)[:, None] * inv_freq
    cos8, sin8 = jnp.cos(ang), jnp.sin(ang)
    rest = SW_HEAD_DIM - ROT_DIM
    c64 = jnp.concatenate([cos8, cos8, jnp.ones((T, rest), f32)], axis=-1)
    s64 = jnp.concatenate([sin8, sin8, jnp.zeros((T, rest), f32)], axis=-1)
    sel = np.zeros((2, 2, LANES, LANES), np.float32)
    for hk in range(SW_KV_HEADS):
        for d in range(SW_HEAD_DIM):
            sel[hk, 0, SW_HEAD_DIM * hk + d, d] = 1.0
            sel[hk, 1, SW_HEAD_DIM * hk + d, SW_HEAD_DIM + d] = 1.0
    return dict(cos=jnp.concatenate([c64, c64], axis=-1), sin=jnp.concatenate([s64, s64], axis=-1),
                sel_a0=jnp.asarray(sel[0, 0]), sel_b0=jnp.asarray(sel[0, 1]),
                sel_a1=jnp.asarray(sel[1, 0]), sel_b1=jnp.asarray(sel[1, 1]))


def _loss(y, tgt):
    T, W = y.shape
    TM = min(TM_ROW, T)
    n = T // TM

    def body(y_ref, t_ref, dy_ref, acc_ref):
        @pl.when(pl.program_id(0) == 0)
        def _():
            acc_ref[...] = jnp.zeros_like(acc_ref)

        d = y_ref[...] - t_ref[...]
        dy_ref[...] = d * (1.0 / W)
        acc_ref[...] += jnp.sum(d * d, axis=0, keepdims=True)

    row = pl.BlockSpec((TM, W), lambda i: (i, 0))
    return pl.pallas_call(
        body, name="loss", grid=(n,), in_specs=[row, row],
        out_specs=[row, pl.BlockSpec((1, W), lambda i: (0, 0))],
        out_shape=[jax.ShapeDtypeStruct((T, W), f32), jax.ShapeDtypeStruct((1, W), f32)],
        compiler_params=pltpu.CompilerParams(dimension_semantics=("arbitrary",)),
    )(y, tgt)


N_SEM = N_DEV - 1


class _Exchange:
    def __init__(self, arrs, modes):
        self.modes, self.n = list(modes), len(arrs)
        self.out_shape = [jax.ShapeDtypeStruct((N_DEV,) + (a.shape[1:] if md == "scatter" else a.shape), a.dtype)
                          for a, md in zip(arrs, modes)]
        self.scratch = [pltpu.SemaphoreType.DMA((self.n, N_SEM)), pltpu.SemaphoreType.DMA((self.n, N_SEM)),
                        pltpu.SemaphoreType.DMA((self.n,))]
        self.specs = [pl.BlockSpec(memory_space=pltpu.HBM)] * self.n

    def bind(self, ins, outs, send_sems, recv_sems, loc_sems):
        x, y, c = lax.axis_index("x"), lax.axis_index("y"), lax.axis_index("c")
        me, sib = 4 * x + 2 * y + c, (x, y, 1 - c)
        flips = [(1 - x, y), (x, 1 - y), (1 - x, 1 - y)]

        def rcopy(a, k, src, dst, dev):
            return pltpu.make_async_remote_copy(src_ref=src, dst_ref=dst, send_sem=send_sems.at[a, k],
                                                recv_sem=recv_sems.at[a, k], device_id=dev,
                                                device_id_type=pl.DeviceIdType.MESH)

        sends, recvs, local, passes = [], [], [], []
        for a, md in enumerate(self.modes):
            src_all, out = ins[a], outs[a]
            mine = (lambda dev: src_all.at[dev]) if md == "scatter" else (lambda dev: src_all)
            local.append(pltpu.make_async_copy(mine(me), out.at[me], loc_sems.at[a]))
            if md in ("gather", "scatter"):
                for k in range(1, N_DEV):
                    px = 1 - x if (k >> 2) & 1 else x
                    py = 1 - y if (k >> 1) & 1 else y
                    pc = 1 - c if k & 1 else c
                    peer = 4 * px + 2 * py + pc
                    sends.append(rcopy(a, k - 1, mine(peer), out.at[me], (px, py, pc)))
                    recvs.append(rcopy(a, k - 1, mine(me), out.at[peer], (px, py, pc)))
            elif md == "gather2":
                sends.append(rcopy(a, 0, src_all, out.at[me], sib))
                recvs.append(rcopy(a, 0, src_all, out.at[4 * x + 2 * y + 1 - c], sib))
                for j, (px, py) in enumerate(flips):
                    sends.append(rcopy(a, 1 + j, src_all, out.at[me], (px, py, c)))
                    theirs = out.at[4 * px + 2 * py + c]
                    arrive = rcopy(a, 1 + j, src_all, theirs, (px, py, c))
                    passes.append((arrive, rcopy(a, 4 + j, theirs, theirs, sib)))
                    recvs.append(rcopy(a, 4 + j, src_all, out.at[4 * px + 2 * py + 1 - c], sib))

        def start():
            for cp in local + sends:
                cp.start()

        def finish():
            for arrive, onward in passes:
                arrive.wait_recv()
                onward.start()
            for cp in recvs:
                cp.wait_recv()
            for cp in sends:
                cp.wait_send()
            for _, onward in passes:
                onward.wait_send()
            for cp in local:
                cp.wait()

        return start, finish


def _exchange(name, arrs, modes):
    ex = _Exchange(arrs, modes)
    n = ex.n

    def body(*refs):
        start, finish = ex.bind(refs[:n], refs[n:2 * n], *refs[2 * n:])
        start()
        finish()

    res = pl.pallas_call(body, name=name, in_specs=ex.specs, out_specs=ex.specs, out_shape=ex.out_shape,
                         scratch_shapes=ex.scratch)(*arrs)
    return list(res)


def _adamw(name, land, w, m, v, tr):
    L_, R_, C_ = w.shape
    n_slots = land.shape[0]
    tr = min(tr, R_)
    assert R_ % tr == 0, (name, R_, tr)
    c1 = 1.0 - ADAM_B1 ** ADAM_STEP
    c2 = 1.0 - ADAM_B2 ** ADAM_STEP

    def body(l_ref, w_ref, m_ref, v_ref, g_ref, d_ref, mo_ref, vo_ref):
        g = l_ref[0].astype(f32)
        for s in range(1, n_slots):
            g = g + l_ref[s].astype(f32)
        m_new = ADAM_B1 * m_ref[...] + (1.0 - ADAM_B1) * g
        v_new = ADAM_B2 * v_ref[...] + (1.0 - ADAM_B2) * jnp.square(g)
        m_hat = m_new / c1
        v_hat = v_new / c2
        g_ref[...] = g
        d_ref[...] = -ADAM_LR * (m_hat / (jnp.sqrt(v_hat) + ADAM_EPS) + ADAM_WD * w_ref[...])
        mo_ref[...] = m_new
        vo_ref[...] = v_new

    row = pl.BlockSpec((1, tr, C_), lambda l, i: (l, i, 0))
    return pl.pallas_call(
        body, name=name, grid=(L_, R_ // tr),
        in_specs=[pl.BlockSpec((n_slots, 1, tr, C_), lambda l, i: (0, l, i, 0)), row, row, row],
        out_specs=[row] * 4, out_shape=[jax.ShapeDtypeStruct((L_, R_, C_), f32)] * 4,
        compiler_params=pltpu.CompilerParams(dimension_semantics=("arbitrary", "arbitrary")),
    )(land, w, m, v)


_BIG = ("w_in", "dn_conv_w", "w_up_dn", "w_up_sw", "w_o", "w_ff1", "w_ff2")
_COL_SHARDED = ("w_in", "dn_conv_w", "w_ff1")
_SMALL_ROWS = ("pre_mix_g", "post_mix_g", "pre_mlp_g", "post_mlp_g")
_SMALL_MISC = ("dn_a_log", "dn_dt_bias", "dn_norm_g", "sw_sinks")
_WEIGHTS = ("pre_mix_g", "w_in", "dn_conv_w", "dn_a_log", "dn_dt_bias", "dn_norm_g", "sw_sinks", "w_up_dn", "w_up_sw",
            "w_o", "post_mix_g", "pre_mlp_g", "w_ff1", "w_ff2", "post_mlp_g")
_SMALL_PACK_ROWS = 24


def _unshard(name, g):
    if name in _COL_SHARDED:
        g = jnp.moveaxis(g, 0, -2)
        return g.reshape(g.shape[:-2] + (g.shape[-2] * g.shape[-1],))
    g = jnp.moveaxis(g, 0, 1)
    return g.reshape((g.shape[0], g.shape[1] * g.shape[2]) + g.shape[3:])


def _shard_major(name, full):
    if name in _COL_SHARDED:
        s = full.reshape(full.shape[:-1] + (N_DEV, full.shape[-1] // N_DEV))
        return jnp.moveaxis(s, -2, 0)
    s = full.reshape((full.shape[0], N_DEV, full.shape[1] // N_DEV) + full.shape[2:])
    return jnp.moveaxis(s, 1, 0)


def _pack_small(d):
    rows = jnp.concatenate([d[n] for n in _SMALL_ROWS], axis=0)
    misc = jnp.concatenate([d[n].reshape(-1) for n in _SMALL_MISC])
    misc = jnp.pad(misc, (0, W1 - misc.shape[0])).reshape(1, W1)
    out = jnp.concatenate([rows, misc], axis=0)
    return jnp.pad(out, ((0, _SMALL_PACK_ROWS - out.shape[0]), (0, 0)))


def _unpack_small(a, like):
    out, L = {}, like[_SMALL_ROWS[0]].shape[0]
    for i, n in enumerate(_SMALL_ROWS):
        out[n] = a[L * i:L * (i + 1)]
    at, row = 0, a[L * len(_SMALL_ROWS)]
    for n in _SMALL_MISC:
        size = like[n].size
        out[n] = row[at:at + size].reshape(like[n].shape)
        at += size
    return out


def kernel(x, positions, pre_mix_g, w_in, dn_conv_w, dn_a_log, dn_dt_bias, dn_norm_g, sw_sinks, w_up_dn, w_up_sw, w_o, post_mix_g, pre_mlp_g, w_ff1, w_ff2, post_mlp_g, loss_target, m_pre_mix_g, m_w_in, m_dn_conv_w, m_dn_a_log, m_dn_dt_bias, m_dn_norm_g, m_sw_sinks, m_w_up_dn, m_w_up_sw, m_w_o, m_post_mix_g, m_pre_mlp_g, m_w_ff1, m_w_ff2, m_post_mlp_g, v_pre_mix_g, v_w_in, v_dn_conv_w, v_dn_a_log, v_dn_dt_bias, v_dn_norm_g, v_sw_sinks, v_w_up_dn, v_w_up_sw, v_w_o, v_post_mix_g, v_pre_mlp_g, v_w_ff1, v_w_ff2, v_post_mlp_g):
    w = dict(pre_mix_g=pre_mix_g, w_in=w_in, dn_conv_w=dn_conv_w, dn_a_log=dn_a_log, dn_dt_bias=dn_dt_bias,
             dn_norm_g=dn_norm_g, sw_sinks=sw_sinks, w_up_dn=w_up_dn, w_up_sw=w_up_sw, w_o=w_o, post_mix_g=post_mix_g,
             pre_mlp_g=pre_mlp_g, w_ff1=w_ff1, w_ff2=w_ff2, post_mlp_g=post_mlp_g)
    m = dict(pre_mix_g=m_pre_mix_g, w_in=m_w_in, dn_conv_w=m_dn_conv_w, dn_a_log=m_dn_a_log, dn_dt_bias=m_dn_dt_bias,
             dn_norm_g=m_dn_norm_g, sw_sinks=m_sw_sinks, w_up_dn=m_w_up_dn, w_up_sw=m_w_up_sw, w_o=m_w_o,
             post_mix_g=m_post_mix_g, pre_mlp_g=m_pre_mlp_g, w_ff1=m_w_ff1, w_ff2=m_w_ff2, post_mlp_g=m_post_mlp_g)
    v = dict(pre_mix_g=v_pre_mix_g, w_in=v_w_in, dn_conv_w=v_dn_conv_w, dn_a_log=v_dn_a_log, dn_dt_bias=v_dn_dt_bias,
             dn_norm_g=v_dn_norm_g, sw_sinks=v_sw_sinks, w_up_dn=v_w_up_dn, w_up_sw=v_w_up_sw, w_o=v_w_o,
             post_mix_g=v_post_mix_g, pre_mlp_g=v_pre_mlp_g, w_ff1=v_w_ff1, w_ff2=v_w_ff2, post_mlp_g=v_post_mlp_g)
    n_layers = pre_mix_g.shape[0]
    xs, pos, tgt = x[0], positions[0], loss_target[0]

    nb = len(_BIG)
    cst = _consts(pos)

    def payload(l):
        return [w[n][l] if n == "dn_conv_w" else w[n][l].astype(bf16) for n in _BIG]

    def layer_of(l, gathered):
        wl = {n: _unshard(n, g[:, None])[0] for n, g in zip(_BIG, gathered)}
        wl.update({n: w[n][l] for n in _WEIGHTS if n not in _BIG})
        return _layer_params(wl)

    gathered = _exchange("allgather_first", payload(0), ["gather2"] * nb)
    h, res, layers = xs, [], []
    for l in range(n_layers):
        layers.append(layer_of(l, gathered))
        comm = (payload(l + 1), ["gather2"] * nb) if l + 1 < n_layers else None
        h, r, gathered = _layer_fwd(h, layers[l], cst, comm)
        res.append(r)
    dy, sq = _loss(h, tgt)
    loss = lax.psum(0.5 / D_MODEL * jnp.sum(sq), ("x", "y", "c"))

    grads, landed, comm, sent = [None] * n_layers, [None] * n_layers, None, None
    for l in reversed(range(n_layers)):
        dy, g, arrived = _layer_bwd(dy, res[l], layers[l], cst, comm)
        if sent is not None:
            landed[sent] = arrived
        grads[l] = _layer_grads_ref_layout(g)
        comm, sent = ([_shard_major(n, grads[l][n][None])[:, 0].astype(bf16) for n in _BIG], ["scatter"] * nb), l
    grad_x = dy[None]
    small_grads = _pack_small({n: jnp.stack([grads[l][n] for l in range(n_layers)]) for n in _SMALL_ROWS + _SMALL_MISC})
    last = _exchange("exchange_last", comm[0] + [small_grads], comm[1] + ["gather"])
    landed[sent] = last[:nb]

    out_g, out_d, out_m, out_v = {}, {}, {}, {}
    for i, n in enumerate(_BIG):
        land = jnp.stack([landed[l][i] for l in range(n_layers)], axis=1)
        out_g[n], out_d[n], out_m[n], out_v[n] = _adamw("adamw_" + n, land, w[n], m[n], v[n], tr=256)
    small = _adamw("adamw_small", last[nb][:, None], _pack_small(w)[None], _pack_small(m)[None], _pack_small(v)[None],
                   tr=_SMALL_PACK_ROWS)
    for dst, a in zip((out_g, out_d, out_m, out_v), small):
        dst.update(_unpack_small(a[0], w))
    return (loss, grad_x, *[out_g[n] for n in _WEIGHTS], *[out_d[n] for n in _WEIGHTS],
            *[out_m[n] for n in _WEIGHTS], *[out_v[n] for n in _WEIGHTS])
```

```python
import math

import numpy as np
import jax
import jax.numpy as jnp
from jax import lax
from jax.experimental import pallas as pl
from jax.experimental.pallas import tpu as pltpu

f32 = jnp.float32
bf16 = jnp.bfloat16
HIGHEST = lax.Precision.HIGHEST

N_DEV = 8
D_MODEL = 1024
DN_HEADS = 8
DN_DK = 128
DN_CHUNK = 128
DN_CONV = 4
SW_Q_HEADS = 16
SW_KV_HEADS = 2
SW_HEAD_DIM = 64
SW_BLOCK = 128
ROPE_THETA = 500000.0
ROT_DIM = SW_HEAD_DIM // 4
D_FF = 4 * D_MODEL
EPS = 1e-6
LANES = 128
CONV_HALO = 8
NEG_BIG = -1e30

ADAM_LR = 0.001
ADAM_B1 = 0.9
ADAM_B2 = 0.999
ADAM_EPS = 1e-08
ADAM_WD = 0.01
ADAM_STEP = 10

PROJ_W = 7680
CB_Q, CB_K, CB_V, CB_Z, CB_SWQ, CB_GA, CB_GB = 0, 1, 2, 3, 4, 5, 6
CB_SWK, CB_SWV, CB_BA = 56, 57, 58

NN = ((1,), (0,))
NT = ((1,), (1,))
TN = ((0,), (0,))


def _mm(a, b, dims, hi=False):
    if hi:
        return lax.dot_general(a.astype(f32), b.astype(f32), (dims, ((), ())), precision=HIGHEST,
                               preferred_element_type=f32)
    return lax.dot_general(a.astype(bf16), b.astype(bf16), (dims, ((), ())), preferred_element_type=f32)


def _matmul(name, a, b, form, out_dtype, tm=512, tn=512, tk=1024, extra=(), epilogue=None):
    if form == "nn":
        (M, K), (_, N) = a.shape, b.shape
    elif form == "nt":
        (M, K), (N, _) = a.shape, b.shape
    else:
        (K, M), (_, N) = a.shape, b.shape
    tm, tn, tk = min(tm, M), min(tn, N), min(tk, K)
    assert M % tm == 0 and N % tn == 0 and K % tk == 0, (name, M, N, K, tm, tn, tk)
    nk = K // tk
    dims = {"nn": NN, "nt": NT, "tn": TN}[form]
    out_dtypes = [out_dtype] if epilogue is None else list(out_dtype)
    ne, no = len(extra), len(out_dtypes)

    def body(a_ref, b_ref, *rest):
        e_refs, o_refs, acc_ref = rest[:ne], rest[ne:ne + no], rest[ne + no]

        def finish(acc):
            vals = [acc] if epilogue is None else epilogue(acc, [e[...] for e in e_refs])
            for o, val in zip(o_refs, vals):
                o[...] = val.astype(o.dtype)

        part = lax.dot_general(a_ref[...], b_ref[...], (dims, ((), ())), preferred_element_type=f32)
        if nk == 1:
            finish(part)
        else:
            k = pl.program_id(2)

            @pl.when(k == 0)
            def _():
                acc_ref[...] = part

            @pl.when(k > 0)
            def _():
                acc_ref[...] += part

            @pl.when(k == nk - 1)
            def _():
                finish(acc_ref[...])

    if form == "tn":
        a_spec = pl.BlockSpec((tk, tm), lambda i, j, k: (k, i))
    else:
        a_spec = pl.BlockSpec((tm, tk), lambda i, j, k: (i, k))
    if form == "nt":
        b_spec = pl.BlockSpec((tn, tk), lambda i, j, k: (j, k))
    else:
        b_spec = pl.BlockSpec((tk, tn), lambda i, j, k: (k, j))
    tile = pl.BlockSpec((tm, tn), lambda i, j, k: (i, j))
    res = pl.pallas_call(
        body, name=name,
        grid=(M // tm, N // tn, nk),
        in_specs=[a_spec, b_spec] + [tile] * ne,
        out_specs=[tile] * no,
        out_shape=[jax.ShapeDtypeStruct((M, N), dt) for dt in out_dtypes],
        scratch_shapes=[pltpu.VMEM((tm, tn) if nk > 1 else (8, 128), f32)],
        compiler_params=pltpu.CompilerParams(dimension_semantics=("parallel", "parallel", "arbitrary")),
    )(a, b, *extra)
    return res[0] if epilogue is None else list(res)


def _tile_specs(ins, halo_ids, params, TM, HR, row_of):
    specs = [pl.BlockSpec((TM, w), lambda i, cb=cb: (row_of(i), cb)) for (_, w, cb) in ins]
    for h in halo_ids:
        _, w, cb = ins[h]
        specs.append(pl.BlockSpec((HR, w), lambda i, cb=cb: (jnp.maximum(row_of(i) * (TM // HR) - 1, 0), cb)))
    for p in params:
        specs.append(pl.BlockSpec(p.shape, lambda i, nd=p.ndim: (0,) * nd))
    return specs


def _tile_fwd(name, fn, T, TM, ins, params, outs, halo_ids=(), HR=CONV_HALO):
    TM = min(TM, T)
    n = T // TM
    ni, nh, npar = len(ins), len(halo_ids), len(params)

    def body(*refs):
        in_v = [r[...] for r in refs[:ni]]
        halo_v = [r[...] for r in refs[ni:ni + nh]]
        par_v = [r[...] for r in refs[ni + nh:ni + nh + npar]]
        o_refs = refs[ni + nh + npar:]
        first = pl.program_id(0) == 0
        vals = fn(first, in_v, halo_v, par_v)
        for o, val in zip(o_refs, vals):
            o[...] = val.astype(o.dtype)

    res = pl.pallas_call(
        body, name=name, grid=(n,),
        in_specs=_tile_specs(ins, halo_ids, params, TM, HR, lambda i: i),
        out_specs=[pl.BlockSpec((TM, w), lambda i: (i, 0)) for (w, _) in outs],
        out_shape=[jax.ShapeDtypeStruct((T, w), dt) for (w, dt) in outs],
        compiler_params=pltpu.CompilerParams(dimension_semantics=("arbitrary",)),
    )(*[a for (a, _, _) in ins], *[ins[h][0] for h in halo_ids], *params)
    return list(res)


def _tile_bwd(name, fn, T, TM, ins, params, cts, din, dpar, halo_ids=(), HR=CONV_HALO):
    TM = min(TM, T)
    n = T // TM
    ni, nh, npar, nc = len(ins), len(halo_ids), len(params), len(cts)
    din_ids = [j for (j, _) in din]
    dh_ids = [h for h in halo_ids if h in din_ids]
    nd, ndp, ndh = len(din), len(dpar), len(dh_ids)

    def body(*refs):
        in_v = [r[...] for r in refs[:ni]]
        halo_v = [r[...] for r in refs[ni:ni + nh]]
        par_v = [r[...] for r in refs[ni + nh:ni + nh + npar]]
        ct_v = [r[...].astype(f32) for r in refs[ni + nh + npar:ni + nh + npar + nc]]
        o_refs = refs[ni + nh + npar + nc:ni + nh + npar + nc + nd + ndp]
        carry_refs = refs[ni + nh + npar + nc + nd + ndp:]
        i = pl.program_id(0)
        first = i == n - 1

        def g(d_in, d_halo, d_par):
            full_in = list(in_v)
            for j, val in zip(din_ids, d_in):
                full_in[j] = val
            full_halo = list(halo_v)
            for h, val in zip(dh_ids, d_halo):
                full_halo[list(halo_ids).index(h)] = val
            full_par = list(par_v)
            for j, val in zip(dpar, d_par):
                full_par[j] = val
            return tuple(fn(first, full_in, full_halo, full_par))

        prim = ([in_v[j].astype(f32) for j in din_ids],
                [halo_v[list(halo_ids).index(h)].astype(f32) for h in dh_ids],
                [par_v[j] for j in dpar])
        _, vjp = jax.vjp(g, *prim)
        g_in, g_halo, g_par = vjp(tuple(ct_v))

        @pl.when(i == 0)
        def _():
            for c in carry_refs:
                c[...] = jnp.zeros_like(c)
            for o in o_refs[nd:]:
                o[...] = jnp.zeros_like(o)

        for slot, (j, _) in enumerate(din):
            val = g_in[slot]
            if j in dh_ids:
                c = carry_refs[dh_ids.index(j)]
                val = jnp.concatenate([val[:TM - HR], val[TM - HR:] + c[...]], axis=0) if TM > HR else val + c[...]
                c[...] = g_halo[dh_ids.index(j)]
            o_refs[slot][...] = val.astype(o_refs[slot].dtype)
        for slot in range(ndp):
            o_refs[nd + slot][...] += g_par[slot]

    rev = lambda i: n - 1 - i
    in_specs = _tile_specs(ins, halo_ids, params, TM, HR, rev)
    ct_specs = [pl.BlockSpec((TM, w), lambda i, cb=cb: (rev(i), cb)) for (_, w, cb) in cts]
    out_specs = [pl.BlockSpec((TM, ins[j][1]), lambda i: (rev(i), 0)) for j in din_ids]
    out_specs += [pl.BlockSpec(params[j].shape, lambda i, nd_=params[j].ndim: (0,) * nd_) for j in dpar]
    out_shape = [jax.ShapeDtypeStruct((T, ins[j][1]), dt) for (j, dt) in din]
    out_shape += [jax.ShapeDtypeStruct(params[j].shape, f32) for j in dpar]
    res = pl.pallas_call(
        body, name=name, grid=(n,),
        in_specs=in_specs + ct_specs,
        out_specs=out_specs,
        out_shape=out_shape,
        scratch_shapes=[pltpu.VMEM((HR, ins[h][1]), f32) for h in dh_ids],
        compiler_params=pltpu.CompilerParams(dimension_semantics=("arbitrary",)),
    )(*[a for (a, _, _) in ins], *[ins[h][0] for h in halo_ids], *params, *[a for (a, _, _) in cts])
    return list(res)


def _rms(x, g):
    return x * lax.rsqrt(jnp.mean(x * x, axis=-1, keepdims=True) + EPS) * g


def _fn_prenorm(first, ins, halos, params):
    (x,), (g,) = ins, params
    x = x.astype(f32)
    return [_rms(x, g), x]


def _fn_postmix(first, ins, halos, params):
    (x, mix), (g2, g3) = ins, params
    x1 = x + _rms(mix, g2)
    return [x1, _rms(x1, g3)]


def _fn_postmlp(first, ins, halos, params):
    (x1, ff), (g4,) = ins, params
    return [x1 + _rms(ff, g4)]


def _fn_rms_only(first, ins, halos, params):
    (ff,), (g4,) = ins, params
    return [_rms(ff, g4)]


def _fn_merge(first, ins, halos, params):
    ga, gb, ya, yb = ins
    return [jax.nn.sigmoid(ga) * ya + jax.nn.sigmoid(gb) * yb]


def _roll_rows(x, shift):
    return pltpu.roll(x, shift, 0)


_row_roll = jax.custom_vjp(_roll_rows, nondiff_argnums=(1,))
_row_roll.defvjp(lambda x, shift: (_roll_rows(x, shift), None),
                 lambda shift, _, ct: (_roll_rows(ct, ct.shape[0] - shift),))


def _make_fn_conv(norm_scale):
    def fn(first, ins, halos, params):
        (x,), (xp,), (w,) = ins, halos, params
        xp = jnp.where(first, 0.0, xp)
        outs = []
        for h in range(DN_HEADS):
            sl = slice(DN_DK * h, DN_DK * (h + 1))
            xe, wh = jnp.concatenate([xp[:, sl], x[:, sl]], axis=0), w[:, sl]
            y = xe[CONV_HALO:] * wh[DN_CONV - 1:DN_CONV]
            for j in range(DN_CONV - 1):
                y = y + _row_roll(xe, DN_CONV - 1 - j)[CONV_HALO:] * wh[j:j + 1]
            y = jax.nn.silu(y)
            if norm_scale is not None:
                y = y * lax.rsqrt(jnp.sum(y * y, axis=-1, keepdims=True) + EPS) * norm_scale
            outs.append(y)
        return [jnp.concatenate(outs, axis=-1)]
    return fn


def _fn_gates(first, ins, halos, params):
    (ba,), (avec, dvec) = ins, params
    lane = lax.broadcasted_iota(jnp.int32, ba.shape, 1)
    beta = jax.nn.sigmoid(ba)
    g = -jnp.exp(avec) * jax.nn.softplus(ba + dvec)
    return [jnp.where(lane < DN_HEADS, beta, jnp.where(lane < 2 * DN_HEADS, g, 0.0))]


def _fn_dnpost(first, ins, halos, params):
    (o, z), (ng,) = ins, params
    outs = []
    for h in range(DN_HEADS):
        sl = slice(DN_DK * h, DN_DK * (h + 1))
        outs.append(_rms(o[:, sl], ng) * jax.nn.silu(z[:, sl]))
    return [jnp.concatenate(outs, axis=-1)]


def _roll_lanes(x, shift):
    return pltpu.roll(x, shift, 1)


_lane_roll = jax.custom_vjp(_roll_lanes, nondiff_argnums=(1,))
_lane_roll.defvjp(lambda x, shift: (_roll_lanes(x, shift), None),
                  lambda shift, _, ct: (_roll_lanes(ct, LANES - shift),))


def _attn_probs(qp, kx, sink, mask):
    heads = range(len(sink))
    s = [jnp.where(mask, _mm(qp[g // 2], kx[g % 2], NT), NEG_BIG) for g in heads]
    m = [jnp.maximum(jnp.max(s[g], axis=-1, keepdims=True), sink[g]) for g in heads]
    p = [jnp.exp(s[g] - m[g]) for g in heads]
    ps = [jnp.exp(sink[g] - m[g]) for g in heads]
    inv = [1.0 / (jnp.sum(p[g], axis=-1, keepdims=True) + ps[g]) for g in heads]
    return [p[g] * inv[g] for g in heads], [ps[g] * inv[g] for g in heads]


@jax.custom_vjp
def _attn_group(qp, kx, vx, sink, mask):
    probs, _ = _attn_probs(qp, kx, sink, mask)
    return tuple(_mm(probs[g], vx[g % 2], NN) for g in range(len(sink)))


def _attn_group_fwd(qp, kx, vx, sink, mask):
    o = _attn_group(qp, kx, vx, sink, mask)
    return o, (qp, kx, vx, sink, mask, o)


def _attn_group_bwd(res, do):
    qp, kx, vx, sink, mask, o = res
    heads = range(len(sink))
    probs, p_sink = _attn_probs(qp, kx, sink, mask)
    d_probs = [_mm(do[g], vx[g % 2], NT) for g in heads]
    dot = [jnp.sum(do[g] * o[g], axis=-1, keepdims=True) for g in heads]
    ds = [probs[g] * (d_probs[g] - dot[g]) for g in heads]
    d_qp = tuple(_mm(ds[2 * j], kx[0], NN) + _mm(ds[2 * j + 1], kx[1], NN) for j in range(len(qp)))
    rows = lambda xs: jnp.concatenate(xs, axis=0)
    d_kx = tuple(_mm(rows([ds[g] for g in heads if g % 2 == c]), rows([qp[g // 2] for g in heads if g % 2 == c]), TN)
                 for c in range(2))
    d_vx = tuple(_mm(rows([probs[g] for g in heads if g % 2 == c]), rows([do[g] for g in heads if g % 2 == c]), TN)
                 for c in range(2))
    d_sink = tuple(-jnp.sum(p_sink[g] * dot[g], axis=0, keepdims=True) for g in heads)
    return d_qp, d_kx, d_vx, d_sink, None


_attn_group.defvjp(_attn_group_fwd, _attn_group_bwd)


def _fn_swa(first, ins, halos, params):
    q, k, v, cos, sin = ins
    kp, vp, cosp, sinp = halos
    sinks, sel_a0, sel_b0, sel_a1, sel_b1 = params
    B = q.shape[0]
    half = ROT_DIM // 2
    in_head = jnp.bitwise_and(lax.broadcasted_iota(jnp.int32, (1, LANES), 1), SW_HEAD_DIM - 1)

    def rope(x, c, s):
        return (x * c + _lane_roll(x, LANES - half) * jnp.where(in_head < half, -s, 0.0)
                + _lane_roll(x, half) * jnp.where(in_head >= half, s, 0.0))

    kcat = jnp.concatenate([rope(kp, cosp, sinp), rope(k, cos, sin)], axis=0)
    vcat = jnp.concatenate([vp, v], axis=0)
    r = lax.broadcasted_iota(jnp.int32, (B, 2 * B), 0)
    c = lax.broadcasted_iota(jnp.int32, (B, 2 * B), 1)
    mask = (c > r) & (c <= r + B) & ((c >= B) | jnp.logical_not(first))
    group = SW_Q_HEADS // SW_KV_HEADS
    sels = ((sel_a0, sel_b0), (sel_a1, sel_b1))
    outs = []
    for hk in range(SW_KV_HEADS):
        heads = range(group)
        kx = [_mm(kcat, sels[hk][half], NN) for half in range(2)]
        vx = [_mm(vcat, sels[hk][half], NN) for half in range(2)]
        qp = [rope(q[:, LANES * j:LANES * (j + 1)], cos, sin) * (SW_HEAD_DIM ** -0.5)
              for j in range(hk * group // 2, (hk + 1) * group // 2)]
        sink = tuple(sinks[:, hk * group + g:hk * group + g + 1] for g in heads)
        o = _attn_group(tuple(qp), tuple(kx), tuple(vx), sink, mask)
        outs += [o[2 * j] + o[2 * j + 1] for j in range(group // 2)]
    return [jnp.concatenate(outs, axis=-1)]


@jax.custom_vjp
def _inv_unit_lower(Ls):
    C = Ls[0].shape[0]
    ii = lax.broadcasted_iota(jnp.int32, (C, C), 0)
    jj = lax.broadcasted_iota(jnp.int32, (C, C), 1)

    def off_mask(level):
        same_pair = jnp.right_shift(ii, level + 1) == jnp.right_shift(jj, level + 1)
        lower_left = (jnp.bitwise_and(jnp.right_shift(ii, level), 1) == 1) & (jnp.bitwise_and(jnp.right_shift(jj, level), 1) == 0)
        return same_pair & lower_left

    eye = (ii == jj).astype(f32)
    m0 = off_mask(0)
    Ts = [eye - jnp.where(m0, L, 0.0) for L in Ls]
    for level in range(1, int(math.log2(C))):
        mk = off_mask(level)
        left = [_mm(T_, jnp.where(mk, L, 0.0), NN) for T_, L in zip(Ts, Ls)]
        Ts = [T_ - _mm(a, T_, NN) for a, T_ in zip(left, Ts)]
    return tuple(Ts)


def _inv_fwd(Ls):
    Ts = _inv_unit_lower(Ls)
    return Ts, Ts


def _inv_bwd(Ts, dTs):
    left = [_mm(T_, dT, TN) for T_, dT in zip(Ts, dTs)]
    return (tuple(-_mm(a, T_, NT) for a, T_ in zip(left, Ts)),)


_inv_unit_lower.defvjp(_inv_fwd, _inv_bwd)


@jax.custom_vjp
def _inv_known(Ls, Ts):
    return Ts


_inv_known.defvjp(lambda Ls, Ts: (Ts, Ts),
                  lambda Ts, dTs: (_inv_bwd(Ts, dTs)[0], tuple(jnp.zeros_like(t) for t in Ts)))


def _mm_01(a, b, dims):
    hi = b.astype(bf16)
    r1 = b - hi.astype(f32)
    mid = r1.astype(bf16)
    lo = (r1 - mid.astype(f32)).astype(bf16)
    a16 = a.astype(bf16)
    dot = lambda part: lax.dot_general(a16, part, (dims, ((), ())), preferred_element_type=f32)
    return dot(hi) + dot(mid) + dot(lo)


def _eye(n):
    return lax.broadcasted_iota(jnp.int32, (n, n), 0) == lax.broadcasted_iota(jnp.int32, (n, n), 1)


def _lower(n):
    return lax.broadcasted_iota(jnp.int32, (n, n), 0) >= lax.broadcasted_iota(jnp.int32, (n, n), 1)


@jax.custom_vjp
def _transpose(x):
    return _mm_01(_eye(x.shape[1]), x, NT)


_transpose.defvjp(lambda x: (_transpose(x), None), lambda _, ct: (_transpose(ct),))


@jax.custom_vjp
def _cumsum_rows(x):
    return _mm_01(_lower(x.shape[0]), x, NN)


_cumsum_rows.defvjp(lambda x: (_cumsum_rows(x), None), lambda _, ct: (_mm_01(_lower(ct.shape[0]), ct, TN),))


def _dn_chunk(q, k, v, gb, S, tinv_known=None):
    C = q.shape[0]
    H = range(DN_HEADS)
    ii = lax.broadcasted_iota(jnp.int32, (C, C), 0)
    jj = lax.broadcasted_iota(jnp.int32, (C, C), 1)
    causal, strict = ii >= jj, ii > jj
    gc_all = _cumsum_rows(gb)
    gc_t = _transpose(gc_all)
    sl = [slice(DN_DK * h, DN_DK * (h + 1)) for h in H]
    qs, ks, vs = [q[:, s] for s in sl], [k[:, s] for s in sl], [v[:, s] for s in sl]
    beta = [gb[:, h:h + 1] for h in H]
    gcol = [gc_all[:, DN_HEADS + h:DN_HEADS + h + 1] for h in H]
    grow = [gc_t[DN_HEADS + h:DN_HEADS + h + 1, :] for h in H]
    decay = [jnp.where(causal, jnp.exp(jnp.where(causal, gcol[h] - grow[h], 0.0)), 0.0) for h in H]
    kb = [ks[h] * beta[h] for h in H]
    kk = [_mm(kb[h], ks[h], NT) for h in H]
    qk = [_mm(qs[h], ks[h], NT) for h in H]
    Ls = tuple(jnp.where(strict, kk[h] * decay[h], 0.0) for h in H)
    tinv = _inv_unit_lower(Ls) if tinv_known is None else _inv_known(Ls, tinv_known)
    eg = [jnp.exp(gcol[h]) for h in H]
    u = [_mm(tinv[h], vs[h] * beta[h], NN) for h in H]
    w = [_mm(tinv[h], kb[h] * eg[h], NN) for h in H]
    gl = [gcol[h][C - 1:C, :] for h in H]
    ws = [_mm(w[h], S[h], NN) for h in H]
    qS = [_mm(qs[h] * eg[h], S[h], NN) for h in H]
    v_new = [u[h] - ws[h] for h in H]
    av = [_mm(qk[h] * decay[h], v_new[h], NN) for h in H]
    kv = [_mm(ks[h] * jnp.exp(gl[h] - gcol[h]), v_new[h], TN) for h in H]
    o = jnp.concatenate([qS[h] + av[h] for h in H], axis=-1)
    return o, tuple(S[h] * jnp.exp(gl[h]) + kv[h] for h in H), tinv


_DN_W = DN_HEADS * DN_DK


def _dn_inputs(first, xq, xk, xv, hq, hk, hv, ba, cw, avec, dvec):
    conv = lambda scale, x, h, j: _make_fn_conv(scale)(first, [x], [h], [cw[:, _DN_W * j:_DN_W * (j + 1)]])[0]
    return (conv(DN_DK ** -0.5, xq, hq, 0), conv(1.0, xk, hk, 1), conv(None, xv, hv, 2),
            _fn_gates(first, [ba], [], [avec, dvec])[0])


def _delta_specs(C, row_of):
    cols = (CB_Q, CB_K, CB_V)
    specs = [pl.BlockSpec((C, _DN_W), lambda i, cb=cb: (row_of(i), cb)) for cb in cols]
    specs += [pl.BlockSpec((CONV_HALO, _DN_W), lambda i, cb=cb: (jnp.maximum(row_of(i) * (C // CONV_HALO) - 1, 0), cb))
              for cb in cols]
    specs.append(pl.BlockSpec((C, LANES), lambda i: (row_of(i), CB_BA)))
    return specs


def _whole(a):
    return pl.BlockSpec(a.shape, lambda i, nd=a.ndim: (0,) * nd)


def _carrying(body, n_in, n_out, n_scratch, n_steps, ex):
    if ex is None:
        return body

    def wrapped(*refs):
        ins, rest = refs[:n_in], refs[n_in:]
        ex_in, rest = rest[:ex.n], rest[ex.n:]
        outs, rest = rest[:n_out], rest[n_out:]
        ex_out, rest = rest[:ex.n], rest[ex.n:]
        scratch, sems = rest[:n_scratch], rest[n_scratch:]
        start, finish = ex.bind(ex_in, ex_out, *sems)
        step = pl.program_id(0)
        pl.when(step == 0)(start)
        body(*ins, *outs, *scratch)
        pl.when(step == n_steps - 1)(finish)

    return wrapped


def _carried_call(name, body, grid, in_specs, out_specs, out_shape, scratch, operands, ex, ex_arrs):
    n_out = len(out_shape)
    if ex is not None:
        in_specs, out_specs = in_specs + ex.specs, out_specs + ex.specs
        out_shape, scratch, operands = out_shape + ex.out_shape, scratch + ex.scratch, tuple(operands) + tuple(ex_arrs)
    res = pl.pallas_call(
        _carrying(body, len(in_specs) - (ex.n if ex else 0), n_out, len(scratch) - (3 if ex else 0), grid[0], ex),
        name=name, grid=grid, in_specs=in_specs, out_specs=out_specs, out_shape=out_shape, scratch_shapes=scratch,
        compiler_params=pltpu.CompilerParams(dimension_semantics=("arbitrary",)),
    )(*operands)
    return list(res[:n_out]), list(res[n_out:])


def _delta_fwd(proj, cw, avec, dvec, comm=None):
    T = proj.shape[0]
    C = min(DN_CHUNK, T)
    n = T // C
    ex = None if comm is None else _Exchange(*comm)

    def body(xq, xk, xv, hq, hk, hv, ba, cw_ref, a_ref, d_ref, o_ref, hist_ref, tinv_ref,
             q_ref, k_ref, v_ref, gb_ref, s_ref):
        first = pl.program_id(0) == 0

        @pl.when(first)
        def _():
            s_ref[...] = jnp.zeros_like(s_ref)

        S = tuple(s_ref[h] for h in range(DN_HEADS))
        for h in range(DN_HEADS):
            hist_ref[0, h] = S[h]
        q, k, v, gb = _dn_inputs(first, xq[...], xk[...], xv[...], hq[...], hk[...], hv[...], ba[...],
                                 cw_ref[...], a_ref[...], d_ref[...])
        q_ref[...], k_ref[...], v_ref[...], gb_ref[...] = q, k, v, gb
        o, s_new, tinv = _dn_chunk(q, k, v, gb, S)
        o_ref[...] = o
        for h in range(DN_HEADS):
            s_ref[h] = s_new[h]
            tinv_ref[0, h] = tinv[h]

    row = pl.BlockSpec((C, _DN_W), lambda i: (i, 0))
    return _carried_call(
        "delta_fwd", body, (n,),
        _delta_specs(C, lambda i: i) + [_whole(cw), _whole(avec), _whole(dvec)],
        [row, pl.BlockSpec((1, DN_HEADS, DN_DK, DN_DK), lambda i: (i, 0, 0, 0)),
         pl.BlockSpec((1, DN_HEADS, C, C), lambda i: (i, 0, 0, 0)),
         row, row, row, pl.BlockSpec((C, LANES), lambda i: (i, 0))],
        [jax.ShapeDtypeStruct((T, _DN_W), f32), jax.ShapeDtypeStruct((n, DN_HEADS, DN_DK, DN_DK), f32),
         jax.ShapeDtypeStruct((n, DN_HEADS, C, C), f32)]
        + [jax.ShapeDtypeStruct((T, _DN_W), f32)] * 3 + [jax.ShapeDtypeStruct((T, LANES), f32)],
        [pltpu.VMEM((DN_HEADS, DN_DK, DN_DK), f32)],
        (proj, proj, proj, proj, proj, proj, proj, cw, avec, dvec), ex, comm[0] if comm else ())


def _delta_bwd(qn, kn, vv, gb, hist, tinv, do, comm=None):
    T = qn.shape[0]
    C = min(DN_CHUNK, T)
    n = T // C
    ex = None if comm is None else _Exchange(*comm)

    def body(q_ref, k_ref, v_ref, gb_ref, hist_ref, tinv_ref, do_ref, dq_ref, dk_ref, dv_ref, dgb_ref, ds_ref):
        @pl.when(pl.program_id(0) == 0)
        def _():
            ds_ref[...] = jnp.zeros_like(ds_ref)

        S = tuple(hist_ref[0, h] for h in range(DN_HEADS))
        known = tuple(tinv_ref[0, h] for h in range(DN_HEADS))
        chunk = lambda q, k, v, g, s: _dn_chunk(q, k, v, g, s, tinv_known=known)[:2]
        _, vjp = jax.vjp(chunk, q_ref[...], k_ref[...], v_ref[...], gb_ref[...], S)
        dS = tuple(ds_ref[h] for h in range(DN_HEADS))
        dq, dk, dv, dgb, dS_in = vjp((do_ref[...], dS))
        dq_ref[...] = dq
        dk_ref[...] = dk
        dv_ref[...] = dv
        dgb_ref[...] = dgb
        for h in range(DN_HEADS):
            ds_ref[h] = dS_in[h]

    row = pl.BlockSpec((C, _DN_W), lambda i: (n - 1 - i, 0))
    small = pl.BlockSpec((C, LANES), lambda i: (n - 1 - i, 0))
    return _carried_call(
        "delta_bwd", body, (n,),
        [row, row, row, small, pl.BlockSpec((1, DN_HEADS, DN_DK, DN_DK), lambda i: (n - 1 - i, 0, 0, 0)),
         pl.BlockSpec((1, DN_HEADS, C, C), lambda i: (n - 1 - i, 0, 0, 0)), row],
        [row, row, row, small],
        [jax.ShapeDtypeStruct((T, _DN_W), f32)] * 3 + [jax.ShapeDtypeStruct((T, LANES), f32)],
        [pltpu.VMEM((DN_HEADS, DN_DK, DN_DK), f32)],
        (qn, kn, vv, gb, hist, tinv, do), ex, comm[0] if comm else ())


TM_ROW = 512
TM_CONV = 256
W1 = D_MODEL


def _first_only(fn):
    return lambda *a: fn(*a)[:1]


def _swa_args(proj, cst):
    ins = [(proj, W1, CB_SWQ), (proj, LANES, CB_SWK), (proj, LANES, CB_SWV), (cst["cos"], LANES, 0), (cst["sin"], LANES, 0)]
    return ins, (1, 2, 3, 4)


def _layer_fwd(x, p, cst, comm=None, late=None):
    T = x.shape[0]
    r = {"x": x}
    (h,) = _tile_fwd("prenorm", _first_only(_fn_prenorm), T, TM_ROW, [(x, W1, 0)], [p["g1"]], [(W1, bf16)])
    proj = _matmul("proj", h, p["w_in"], "nn", f32, tm=2048, tn=1536)
    (o, hist, tinv, qn, kn, vv, gbt), comm_out = _delta_fwd(proj, p["conv_w"], p["avec"], p["dvec"], comm)
    if late is not None:
        p = {**p, **late(comm_out)}
    (dn_out,) = _tile_fwd("dnpost", _fn_dnpost, T, TM_ROW, [(o, W1, 0), (proj, W1, CB_Z)], [p["ng"]], [(W1, bf16)])
    sw_ins, sw_halo = _swa_args(proj, cst)
    sw_par = [p["sinks"], cst["sel_a0"], cst["sel_b0"], cst["sel_a1"], cst["sel_b1"]]
    (sw_out,) = _tile_fwd("swa", _fn_swa, T, SW_BLOCK, sw_ins, sw_par, [(W1, bf16)], halo_ids=sw_halo, HR=SW_BLOCK)
    y_a = _matmul("up_dn", dn_out, p["w_up_dn"], "nn", f32, tm=2048, tn=1024)
    y_b = _matmul("up_sw", sw_out, p["w_up_sw"], "nn", f32, tm=2048, tn=1024)
    (gated,) = _tile_fwd("merge", _fn_merge, T, TM_ROW,
                         [(proj, W1, CB_GA), (proj, W1, CB_GB), (y_a, W1, 0), (y_b, W1, 0)], [], [(W1, bf16)])
    mix = _matmul("w_o", gated, p["w_o"], "nn", f32, tm=2048, tn=1024)
    x1, h2 = _tile_fwd("postmix", _fn_postmix, T, TM_ROW, [(x, W1, 0), (mix, W1, 0)], [p["g2"], p["g3"]],
                       [(W1, f32), (W1, bf16)])
    ffh, act = _matmul("ff1", h2, p["w_ff1"], "nn", [f32, bf16], tm=2048, tn=1024,
                       epilogue=lambda acc, ex: [acc, jnp.square(jnp.maximum(acc, 0.0))])
    ff = _matmul("ff2", act, p["w_ff2"], "nn", f32, tm=2048, tn=1024)
    (x2,) = _tile_fwd("postmlp", _fn_postmlp, T, TM_ROW, [(x1, W1, 0), (ff, W1, 0)], [p["g4"]], [(W1, f32)])
    r.update(h=h, proj=proj, qn=qn, kn=kn, vv=vv, gbt=gbt, o=o, hist=hist, tinv=tinv, dn_out=dn_out, sw_out=sw_out,
             y_a=y_a, y_b=y_b, gated=gated, mix=mix, h2=h2, ffh=ffh, act=act, ff=ff)
    return x2, r, p, comm_out


def _pieces(n, full):
    return _shard_major(n, full[None])[:, 0].astype(bf16)


def _layer_bwd(dx2, r, p, cst, carry=None):
    T = dx2.shape[0]
    x, proj = r["x"], r["proj"]
    g = {}
    dff, g["g4"] = _tile_bwd("postmlp_b", _fn_rms_only, T, TM_ROW, [(r["ff"], W1, 0)], [p["g4"]], [(dx2, W1, 0)],
                             [(0, bf16)], [0])
    (dffh,) = _matmul("ff2_dx", dff, p["w_ff2"], "nt", [bf16], tm=2048, tn=1024, extra=[r["ffh"]],
                      epilogue=lambda acc, ex: [acc * (2.0 * jnp.maximum(ex[0], 0.0))])
    g["w_ff2"] = _matmul("ff2_dw", r["act"], dff, "tn", bf16, tm=1024, tn=1024, tk=2048)
    dh2 = _matmul("ff1_dx", dffh, p["w_ff1"], "nt", f32, tm=2048, tn=1024)
    g["w_ff1"] = _matmul("ff1_dw", r["h2"], dffh, "tn", bf16, tm=1024, tn=1024, tk=2048)
    dx1, dmix, g["g2"], g["g3"] = _tile_bwd("postmix_b", _fn_postmix, T, TM_ROW, [(x, W1, 0), (r["mix"], W1, 0)],
                                            [p["g2"], p["g3"]], [(dx2, W1, 0), (dh2, W1, 0)], [(0, f32), (1, bf16)], [0, 1])
    dgated = _matmul("w_o_dx", dmix, p["w_o"], "nt", f32, tm=2048, tn=1024)
    g["w_o"] = _matmul("w_o_dw", r["gated"], dmix, "tn", bf16, tm=1024, tn=1024, tk=2048)
    dga, dgb, dya, dyb = _tile_bwd("merge_b", _fn_merge, T, TM_ROW,
                                   [(proj, W1, CB_GA), (proj, W1, CB_GB), (r["y_a"], W1, 0), (r["y_b"], W1, 0)], [],
                                   [(dgated, W1, 0)], [(0, bf16), (1, bf16), (2, bf16), (3, bf16)], [])
    d_dn = _matmul("up_dn_dx", dya, p["w_up_dn"], "nt", f32, tm=2048, tn=1024)
    g["w_up_dn"] = _matmul("up_dn_dw", r["dn_out"], dya, "tn", bf16, tm=1024, tn=1024, tk=2048)
    d_sw = _matmul("up_sw_dx", dyb, p["w_up_sw"], "nt", f32, tm=2048, tn=1024)
    g["w_up_sw"] = _matmul("up_sw_dw", r["sw_out"], dyb, "tn", bf16, tm=1024, tn=1024, tk=2048)
    do, dz, g["ng"] = _tile_bwd("dnpost_b", _fn_dnpost, T, TM_ROW, [(r["o"], W1, 0), (proj, W1, CB_Z)], [p["ng"]],
                                [(d_dn, W1, 0)], [(0, f32), (1, bf16)], [0])
    comm = None
    if carry is not None:
        send = list(carry) + [_pieces(n, g[n]) for n in _LATE]
        comm = (send, ["scatter"] * len(send))
    (dqn, dkn, dvv, dgbt), comm_out = _delta_bwd(r["qn"], r["kn"], r["vv"], r["gbt"], r["hist"], r["tinv"], do, comm)
    conv_b = lambda nm, cb, scale, ct: _tile_bwd(nm, _make_fn_conv(scale), T, TM_CONV, [(proj, W1, cb)],
                                                 [p["conv_w"][:, W1 * cb:W1 * (cb + 1)]], [(ct, W1, 0)], [(0, bf16)], [0],
                                                 halo_ids=(0,))
    dq_in, dcw_q = conv_b("conv_q_b", CB_Q, DN_DK ** -0.5, dqn)
    dk_in, dcw_k = conv_b("conv_k_b", CB_K, 1.0, dkn)
    dv_in, dcw_v = conv_b("conv_v_b", CB_V, None, dvv)
    g["conv_w"] = jnp.concatenate([dcw_q, dcw_k, dcw_v], axis=-1)
    dba, g["avec"], g["dvec"] = _tile_bwd("gates_b", _fn_gates, T, TM_ROW, [(proj, LANES, CB_BA)], [p["avec"], p["dvec"]],
                                          [(dgbt, LANES, 0)], [(0, bf16)], [0, 1])
    sw_ins, sw_halo = _swa_args(proj, cst)
    sw_par = [p["sinks"], cst["sel_a0"], cst["sel_b0"], cst["sel_a1"], cst["sel_b1"]]
    dswq, dswk, dswv, g["sinks"] = _tile_bwd("swa_b", _fn_swa, T, SW_BLOCK, sw_ins, sw_par, [(d_sw, W1, 0)],
                                             [(0, bf16), (1, bf16), (2, bf16)], [0], halo_ids=sw_halo, HR=SW_BLOCK)
    dproj = jnp.concatenate([dq_in, dk_in, dv_in, dz, dswq, dga, dgb, dswk, dswv, dba, jnp.zeros((T, LANES), bf16)], axis=-1)
    dh = _matmul("proj_dx", dproj, p["w_in"], "nt", f32, tm=2048, tn=1024, tk=1536)
    g["w_in"] = _matmul("proj_dw", r["h"], dproj, "tn", bf16, tm=1024, tn=1536, tk=2048)
    dx, g["g1"] = _tile_bwd("prenorm_b", _fn_prenorm, T, TM_ROW, [(x, W1, 0)], [p["g1"]], [(dh, W1, 0), (dx1, W1, 0)],
                            [(0, f32)], [0])
    return dx, g, comm_out


_OFF_BA, _OFF_SWQ, _OFF_SWK, _OFF_GA, _D_IN = 4096, 4112, 5136, 5392, 7440


def _proj_cols(w):
    pad = lambda n: jnp.zeros(w.shape[:-1] + (n,), w.dtype)
    return jnp.concatenate([w[..., :_OFF_BA], w[..., _OFF_SWQ:_OFF_SWK], w[..., _OFF_GA:_D_IN],
                            w[..., _OFF_SWK:_OFF_GA], w[..., _OFF_BA:_OFF_SWQ], pad(PROJ_W - _D_IN)], axis=-1)


def _proj_cols_inv(w):
    n_ba = _OFF_SWQ - _OFF_BA
    return jnp.concatenate([w[..., :4096], w[..., 7424:7424 + n_ba], w[..., 4096:5120], w[..., 7168:7424],
                            w[..., 5120:7168]], axis=-1)


def _lane_pad(v, at):
    return jnp.pad(v.astype(f32), (at, LANES - at - v.shape[0])).reshape(1, LANES)


_EARLY = ("w_in", "dn_conv_w")
_LATE = ("w_up_dn", "w_up_sw", "w_o", "w_ff1", "w_ff2")


def _early_params(w):
    row = lambda v: v.reshape(1, -1).astype(f32)
    return dict(
        g1=row(w["pre_mix_g"]), g2=row(w["post_mix_g"]), g3=row(w["pre_mlp_g"]), g4=row(w["post_mlp_g"]),
        w_in=_proj_cols(w["w_in"]).astype(bf16), conv_w=w["dn_conv_w"].astype(f32),
        avec=_lane_pad(w["dn_a_log"], DN_HEADS), dvec=_lane_pad(w["dn_dt_bias"], DN_HEADS),
        ng=row(w["dn_norm_g"]), sinks=_lane_pad(w["sw_sinks"], 0))


def _late_params(w):
    return {n: w[n].astype(bf16) for n in _LATE}


def _layer_params(w):
    return {**_early_params(w), **_late_params(w)}


def _layer_grads_ref_layout(g):
    return dict(
        pre_mix_g=g["g1"][0], post_mix_g=g["g2"][0], pre_mlp_g=g["g3"][0], post_mlp_g=g["g4"][0],
        w_in=_proj_cols_inv(g["w_in"]), dn_conv_w=g["conv_w"],
        dn_a_log=g["avec"][0, DN_HEADS:2 * DN_HEADS], dn_dt_bias=g["dvec"][0, DN_HEADS:2 * DN_HEADS],
        dn_norm_g=g["ng"][0], sw_sinks=g["sinks"][0, :SW_Q_HEADS],
        w_up_dn=g["w_up_dn"], w_up_sw=g["w_up_sw"], w_o=g["w_o"], w_ff1=g["w_ff1"], w_ff2=g["w_ff2"])


def _consts(positions):
    T = positions.shape[0]
    half = ROT_DIM // 2
    inv_freq = ROPE_THETA ** (-jnp.arange(half, dtype=f32) * (2.0 / ROT_DIM))
    ang = positions.astype(f32)[:, None] * inv_freq
    cos8, sin8 = jnp.cos(ang), jnp.sin(ang)
    rest = SW_HEAD_DIM - ROT_DIM
    c64 = jnp.concatenate([cos8, cos8, jnp.ones((T, rest), f32)], axis=-1)
    s64 = jnp.concatenate([sin8, sin8, jnp.zeros((T, rest), f32)], axis=-1)
    sel = np.zeros((2, 2, LANES, LANES), np.float32)
    for hk in range(SW_KV_HEADS):
        for d in range(SW_HEAD_DIM):
            sel[hk, 0, SW_HEAD_DIM * hk + d, d] = 1.0
            sel[hk, 1, SW_HEAD_DIM * hk + d, SW_HEAD_DIM + d] = 1.0
    return dict(cos=jnp.concatenate([c64, c64], axis=-1), sin=jnp.concatenate([s64, s64], axis=-1),
                sel_a0=jnp.asarray(sel[0, 0]), sel_b0=jnp.asarray(sel[0, 1]),
                sel_a1=jnp.asarray(sel[1, 0]), sel_b1=jnp.asarray(sel[1, 1]))


def _loss(y, tgt):
    T, W = y.shape
    TM = min(TM_ROW, T)
    n = T // TM

    def body(y_ref, t_ref, dy_ref, acc_ref):
        @pl.when(pl.program_id(0) == 0)
        def _():
            acc_ref[...] = jnp.zeros_like(acc_ref)

        d = y_ref[...] - t_ref[...]
        dy_ref[...] = d * (1.0 / W)
        acc_ref[...] += jnp.sum(d * d, axis=0, keepdims=True)

    row = pl.BlockSpec((TM, W), lambda i: (i, 0))
    return pl.pallas_call(
        body, name="loss", grid=(n,), in_specs=[row, row],
        out_specs=[row, pl.BlockSpec((1, W), lambda i: (0, 0))],
        out_shape=[jax.ShapeDtypeStruct((T, W), f32), jax.ShapeDtypeStruct((1, W), f32)],
        compiler_params=pltpu.CompilerParams(dimension_semantics=("arbitrary",)),
    )(y, tgt)


N_SEM = N_DEV - 1


class _Exchange:
    def __init__(self, arrs, modes):
        self.modes, self.n = list(modes), len(arrs)
        self.out_shape = [jax.ShapeDtypeStruct((N_DEV,) + (a.shape[1:] if md == "scatter" else a.shape), a.dtype)
                          for a, md in zip(arrs, modes)]
        self.scratch = [pltpu.SemaphoreType.DMA((self.n, N_SEM)), pltpu.SemaphoreType.DMA((self.n, N_SEM)),
                        pltpu.SemaphoreType.DMA((self.n,))]
        self.specs = [pl.BlockSpec(memory_space=pltpu.HBM)] * self.n

    def bind(self, ins, outs, send_sems, recv_sems, loc_sems):
        x, y, c = lax.axis_index("x"), lax.axis_index("y"), lax.axis_index("c")
        me, sib = 4 * x + 2 * y + c, (x, y, 1 - c)
        flips = [(1 - x, y), (x, 1 - y), (1 - x, 1 - y)]

        def rcopy(a, k, src, dst, dev):
            return pltpu.make_async_remote_copy(src_ref=src, dst_ref=dst, send_sem=send_sems.at[a, k],
                                                recv_sem=recv_sems.at[a, k], device_id=dev,
                                                device_id_type=pl.DeviceIdType.MESH)

        sends, recvs, local, passes = [], [], [], []
        for a, md in enumerate(self.modes):
            src_all, out = ins[a], outs[a]
            mine = (lambda dev: src_all.at[dev]) if md == "scatter" else (lambda dev: src_all)
            local.append(pltpu.make_async_copy(mine(me), out.at[me], loc_sems.at[a]))
            if md in ("gather", "scatter"):
                for k in range(1, N_DEV):
                    px = 1 - x if (k >> 2) & 1 else x
                    py = 1 - y if (k >> 1) & 1 else y
                    pc = 1 - c if k & 1 else c
                    peer = 4 * px + 2 * py + pc
                    sends.append(rcopy(a, k - 1, mine(peer), out.at[me], (px, py, pc)))
                    recvs.append(rcopy(a, k - 1, mine(me), out.at[peer], (px, py, pc)))
            elif md == "gather2":
                sends.append(rcopy(a, 0, src_all, out.at[me], sib))
                recvs.append(rcopy(a, 0, src_all, out.at[4 * x + 2 * y + 1 - c], sib))
                for j, (px, py) in enumerate(flips):
                    sends.append(rcopy(a, 1 + j, src_all, out.at[me], (px, py, c)))
                    theirs = out.at[4 * px + 2 * py + c]
                    arrive = rcopy(a, 1 + j, src_all, theirs, (px, py, c))
                    passes.append((arrive, rcopy(a, 4 + j, theirs, theirs, sib)))
                    recvs.append(rcopy(a, 4 + j, src_all, out.at[4 * px + 2 * py + 1 - c], sib))

        def start():
            for cp in local + sends:
                cp.start()

        def finish():
            for arrive, onward in passes:
                arrive.wait_recv()
                onward.start()
            for cp in recvs:
                cp.wait_recv()
            for cp in sends:
                cp.wait_send()
            for _, onward in passes:
                onward.wait_send()
            for cp in local:
                cp.wait()

        return start, finish


def _exchange(name, arrs, modes):
    ex = _Exchange(arrs, modes)
    n = ex.n

    def body(*refs):
        start, finish = ex.bind(refs[:n], refs[n:2 * n], *refs[2 * n:])
        start()
        finish()

    res = pl.pallas_call(body, name=name, in_specs=ex.specs, out_specs=ex.specs, out_shape=ex.out_shape,
                         scratch_shapes=ex.scratch)(*arrs)
    return list(res)


def _adamw(name, land, w, m, v, tr):
    L_, R_, C_ = w.shape
    n_slots = land.shape[0]
    tr = min(tr, R_)
    assert R_ % tr == 0, (name, R_, tr)
    c1 = 1.0 - ADAM_B1 ** ADAM_STEP
    c2 = 1.0 - ADAM_B2 ** ADAM_STEP

    def body(l_ref, w_ref, m_ref, v_ref, g_ref, d_ref, mo_ref, vo_ref):
        g = l_ref[0].astype(f32)
        for s in range(1, n_slots):
            g = g + l_ref[s].astype(f32)
        m_new = ADAM_B1 * m_ref[...] + (1.0 - ADAM_B1) * g
        v_new = ADAM_B2 * v_ref[...] + (1.0 - ADAM_B2) * jnp.square(g)
        m_hat = m_new / c1
        v_hat = v_new / c2
        g_ref[...] = g
        d_ref[...] = -ADAM_LR * (m_hat / (jnp.sqrt(v_hat) + ADAM_EPS) + ADAM_WD * w_ref[...])
        mo_ref[...] = m_new
        vo_ref[...] = v_new

    row = pl.BlockSpec((1, tr, C_), lambda l, i: (l, i, 0))
    return pl.pallas_call(
        body, name=name, grid=(L_, R_ // tr),
        in_specs=[pl.BlockSpec((n_slots, 1, tr, C_), lambda l, i: (0, l, i, 0)), row, row, row],
        out_specs=[row] * 4, out_shape=[jax.ShapeDtypeStruct((L_, R_, C_), f32)] * 4,
        compiler_params=pltpu.CompilerParams(dimension_semantics=("arbitrary", "arbitrary")),
    )(land, w, m, v)


_BIG = ("w_in", "dn_conv_w", "w_up_dn", "w_up_sw", "w_o", "w_ff1", "w_ff2")
_COL_SHARDED = ("w_in", "dn_conv_w", "w_ff1")
_SMALL_ROWS = ("pre_mix_g", "post_mix_g", "pre_mlp_g", "post_mlp_g")
_SMALL_MISC = ("dn_a_log", "dn_dt_bias", "dn_norm_g", "sw_sinks")
_WEIGHTS = ("pre_mix_g", "w_in", "dn_conv_w", "dn_a_log", "dn_dt_bias", "dn_norm_g", "sw_sinks", "w_up_dn", "w_up_sw",
            "w_o", "post_mix_g", "pre_mlp_g", "w_ff1", "w_ff2", "post_mlp_g")
_SMALL_PACK_ROWS = 24


def _unshard(name, g):
    if name in _COL_SHARDED:
        g = jnp.moveaxis(g, 0, -2)
        return g.reshape(g.shape[:-2] + (g.shape[-2] * g.shape[-1],))
    g = jnp.moveaxis(g, 0, 1)
    return g.reshape((g.shape[0], g.shape[1] * g.shape[2]) + g.shape[3:])


def _shard_major(name, full):
    if name in _COL_SHARDED:
        s = full.reshape(full.shape[:-1] + (N_DEV, full.shape[-1] // N_DEV))
        return jnp.moveaxis(s, -2, 0)
    s = full.reshape((full.shape[0], N_DEV, full.shape[1] // N_DEV) + full.shape[2:])
    return jnp.moveaxis(s, 1, 0)


def _pack_small(d):
    rows = jnp.concatenate([d[n] for n in _SMALL_ROWS], axis=0)
    misc = jnp.concatenate([d[n].reshape(-1) for n in _SMALL_MISC])
    misc = jnp.pad(misc, (0, W1 - misc.shape[0])).reshape(1, W1)
    out = jnp.concatenate([rows, misc], axis=0)
    return jnp.pad(out, ((0, _SMALL_PACK_ROWS - out.shape[0]), (0, 0)))


def _unpack_small(a, like):
    out, L = {}, like[_SMALL_ROWS[0]].shape[0]
    for i, n in enumerate(_SMALL_ROWS):
        out[n] = a[L * i:L * (i + 1)]
    at, row = 0, a[L * len(_SMALL_ROWS)]
    for n in _SMALL_MISC:
        size = like[n].size
        out[n] = row[at:at + size].reshape(like[n].shape)
        at += size
    return out


def kernel(x, positions, pre_mix_g, w_in, dn_conv_w, dn_a_log, dn_dt_bias, dn_norm_g, sw_sinks, w_up_dn, w_up_sw, w_o, post_mix_g, pre_mlp_g, w_ff1, w_ff2, post_mlp_g, loss_target, m_pre_mix_g, m_w_in, m_dn_conv_w, m_dn_a_log, m_dn_dt_bias, m_dn_norm_g, m_sw_sinks, m_w_up_dn, m_w_up_sw, m_w_o, m_post_mix_g, m_pre_mlp_g, m_w_ff1, m_w_ff2, m_post_mlp_g, v_pre_mix_g, v_w_in, v_dn_conv_w, v_dn_a_log, v_dn_dt_bias, v_dn_norm_g, v_sw_sinks, v_w_up_dn, v_w_up_sw, v_w_o, v_post_mix_g, v_pre_mlp_g, v_w_ff1, v_w_ff2, v_post_mlp_g):
    w = dict(pre_mix_g=pre_mix_g, w_in=w_in, dn_conv_w=dn_conv_w, dn_a_log=dn_a_log, dn_dt_bias=dn_dt_bias,
             dn_norm_g=dn_norm_g, sw_sinks=sw_sinks, w_up_dn=w_up_dn, w_up_sw=w_up_sw, w_o=w_o, post_mix_g=post_mix_g,
             pre_mlp_g=pre_mlp_g, w_ff1=w_ff1, w_ff2=w_ff2, post_mlp_g=post_mlp_g)
    m = dict(pre_mix_g=m_pre_mix_g, w_in=m_w_in, dn_conv_w=m_dn_conv_w, dn_a_log=m_dn_a_log, dn_dt_bias=m_dn_dt_bias,
             dn_norm_g=m_dn_norm_g, sw_sinks=m_sw_sinks, w_up_dn=m_w_up_dn, w_up_sw=m_w_up_sw, w_o=m_w_o,
             post_mix_g=m_post_mix_g, pre_mlp_g=m_pre_mlp_g, w_ff1=m_w_ff1, w_ff2=m_w_ff2, post_mlp_g=m_post_mlp_g)
    v = dict(pre_mix_g=v_pre_mix_g, w_in=v_w_in, dn_conv_w=v_dn_conv_w, dn_a_log=v_dn_a_log, dn_dt_bias=v_dn_dt_bias,
             dn_norm_g=v_dn_norm_g, sw_sinks=v_sw_sinks, w_up_dn=v_w_up_dn, w_up_sw=v_w_up_sw, w_o=v_w_o,
             post_mix_g=v_post_mix_g, pre_mlp_g=v_pre_mlp_g, w_ff1=v_w_ff1, w_ff2=v_w_ff2, post_mlp_g=v_post_mlp_g)
    n_layers = pre_mix_g.shape[0]
    xs, pos, tgt = x[0], positions[0], loss_target[0]

    nb = len(_BIG)
    cst = _consts(pos)

    def payload(l, names):
        return [w[n][l] if n == "dn_conv_w" else w[n][l].astype(bf16) for n in names]

    def full(names, gathered):
        return {n: _unshard(n, g[:, None])[0] for n, g in zip(names, gathered)}

    early = _exchange("allgather_first", payload(0, _EARLY), ["gather2"] * len(_EARLY))
    h, res, layers = xs, [], []
    for l in range(n_layers):
        wl = {**full(_EARLY, early), **{n: w[n][l] for n in _WEIGHTS if n not in _BIG}}
        send = payload(l, _LATE) + (payload(l + 1, _EARLY) if l + 1 < n_layers else [])
        h, r, p, got = _layer_fwd(h, _early_params(wl), cst, (send, ["gather2"] * len(send)),
                                  late=lambda got: _late_params(full(_LATE, got[:len(_LATE)])))
        early = got[len(_LATE):]
        res.append(r)
        layers.append(p)
    dy, sq = _loss(h, tgt)
    loss = lax.psum(0.5 / D_MODEL * jnp.sum(sq), ("x", "y", "c"))

    grads, landed, carry = [None] * n_layers, {n: [None] * n_layers for n in _BIG}, []
    for l in reversed(range(n_layers)):
        dy, g, arrived = _layer_bwd(dy, res[l], layers[l], cst, carry)
        for n, a in zip(_EARLY, arrived[:len(carry)]):
            landed[n][l + 1] = a
        for n, a in zip(_LATE, arrived[len(carry):]):
            landed[n][l] = a
        grads[l] = _layer_grads_ref_layout(g)
        carry = [_pieces(n, grads[l][n]) for n in _EARLY]
    grad_x = dy[None]
    small_grads = _pack_small({n: jnp.stack([grads[l][n] for l in range(n_layers)]) for n in _SMALL_ROWS + _SMALL_MISC})
    last = _exchange("exchange_last", carry + [small_grads], ["scatter"] * len(carry) + ["gather"])
    for n, a in zip(_EARLY, last):
        landed[n][0] = a

    out_g, out_d, out_m, out_v = {}, {}, {}, {}
    for n in _BIG:
        land = jnp.stack(landed[n], axis=1)
        out_g[n], out_d[n], out_m[n], out_v[n] = _adamw("adamw_" + n, land, w[n], m[n], v[n], tr=256)
    small = _adamw("adamw_small", last[-1][:, None], _pack_small(w)[None], _pack_small(m)[None], _pack_small(v)[None],
                   tr=_SMALL_PACK_ROWS)
    for dst, a in zip((out_g, out_d, out_m, out_v), small):
        dst.update(_unpack_small(a[0], w))
    return (loss, grad_x, *[out_g[n] for n in _WEIGHTS], *[out_d[n] for n in _WEIGHTS],
            *[out_m[n] for n in _WEIGHTS], *[out_v[n] for n in _WEIGHTS])
```

```python
import math

import numpy as np
import jax
import jax.numpy as jnp
from jax import lax
from jax.experimental import pallas as pl
from jax.experimental.pallas import tpu as pltpu

f32 = jnp.float32
bf16 = jnp.bfloat16
HIGHEST = lax.Precision.HIGHEST

N_DEV = 8
D_MODEL = 1024
DN_HEADS = 8
DN_DK = 128
DN_CHUNK = 128
DN_GROUP = 8
DN_CONV = 4
SW_Q_HEADS = 16
SW_KV_HEADS = 2
SW_HEAD_DIM = 64
SW_BLOCK = 128
ROPE_THETA = 500000.0
ROT_DIM = SW_HEAD_DIM // 4
D_FF = 4 * D_MODEL
EPS = 1e-6
LANES = 128
CONV_HALO = 8
NEG_BIG = -1e30

ADAM_LR = 0.001
ADAM_B1 = 0.9
ADAM_B2 = 0.999
ADAM_EPS = 1e-08
ADAM_WD = 0.01
ADAM_STEP = 10

PROJ_W = 7680
CB_Q, CB_K, CB_V, CB_Z, CB_SWQ, CB_GA, CB_GB = 0, 1, 2, 3, 4, 5, 6
CB_SWK, CB_SWV, CB_BA = 56, 57, 58

NN = ((1,), (0,))
NT = ((1,), (1,))
TN = ((0,), (0,))


def _mm(a, b, dims, hi=False):
    if hi:
        return lax.dot_general(a.astype(f32), b.astype(f32), (dims, ((), ())), precision=HIGHEST,
                               preferred_element_type=f32)
    return lax.dot_general(a.astype(bf16), b.astype(bf16), (dims, ((), ())), preferred_element_type=f32)


def _matmul(name, a, b, form, out_dtype, tm=512, tn=512, tk=1024, extra=(), epilogue=None):
    if form == "nn":
        (M, K), (_, N) = a.shape, b.shape
    elif form == "nt":
        (M, K), (N, _) = a.shape, b.shape
    else:
        (K, M), (_, N) = a.shape, b.shape
    tm, tn, tk = min(tm, M), min(tn, N), min(tk, K)
    assert M % tm == 0 and N % tn == 0 and K % tk == 0, (name, M, N, K, tm, tn, tk)
    nk = K // tk
    dims = {"nn": NN, "nt": NT, "tn": TN}[form]
    out_dtypes = [out_dtype] if epilogue is None else list(out_dtype)
    ne, no = len(extra), len(out_dtypes)

    def body(a_ref, b_ref, *rest):
        e_refs, o_refs, acc_ref = rest[:ne], rest[ne:ne + no], rest[ne + no]

        def finish(acc):
            vals = [acc] if epilogue is None else epilogue(acc, [e[...] for e in e_refs])
            for o, val in zip(o_refs, vals):
                o[...] = val.astype(o.dtype)

        part = lax.dot_general(a_ref[...], b_ref[...], (dims, ((), ())), preferred_element_type=f32)
        if nk == 1:
            finish(part)
        else:
            k = pl.program_id(2)

            @pl.when(k == 0)
            def _():
                acc_ref[...] = part

            @pl.when(k > 0)
            def _():
                acc_ref[...] += part

            @pl.when(k == nk - 1)
            def _():
                finish(acc_ref[...])

    if form == "tn":
        a_spec = pl.BlockSpec((tk, tm), lambda i, j, k: (k, i))
    else:
        a_spec = pl.BlockSpec((tm, tk), lambda i, j, k: (i, k))
    if form == "nt":
        b_spec = pl.BlockSpec((tn, tk), lambda i, j, k: (j, k))
    else:
        b_spec = pl.BlockSpec((tk, tn), lambda i, j, k: (k, j))
    tile = pl.BlockSpec((tm, tn), lambda i, j, k: (i, j))
    res = pl.pallas_call(
        body, name=name,
        grid=(M // tm, N // tn, nk),
        in_specs=[a_spec, b_spec] + [tile] * ne,
        out_specs=[tile] * no,
        out_shape=[jax.ShapeDtypeStruct((M, N), dt) for dt in out_dtypes],
        scratch_shapes=[pltpu.VMEM((tm, tn) if nk > 1 else (8, 128), f32)],
        compiler_params=pltpu.CompilerParams(dimension_semantics=("parallel", "parallel", "arbitrary")),
    )(a, b, *extra)
    return res[0] if epilogue is None else list(res)


def _tile_specs(ins, halo_ids, params, TM, HR, row_of):
    specs = [pl.BlockSpec((TM, w), lambda i, cb=cb: (row_of(i), cb)) for (_, w, cb) in ins]
    for h in halo_ids:
        _, w, cb = ins[h]
        specs.append(pl.BlockSpec((HR, w), lambda i, cb=cb: (jnp.maximum(row_of(i) * (TM // HR) - 1, 0), cb)))
    for p in params:
        specs.append(pl.BlockSpec(p.shape, lambda i, nd=p.ndim: (0,) * nd))
    return specs


def _tile_fwd(name, fn, T, TM, ins, params, outs, halo_ids=(), HR=CONV_HALO):
    TM = min(TM, T)
    n = T // TM
    ni, nh, npar = len(ins), len(halo_ids), len(params)

    def body(*refs):
        in_v = [r[...] for r in refs[:ni]]
        halo_v = [r[...] for r in refs[ni:ni + nh]]
        par_v = [r[...] for r in refs[ni + nh:ni + nh + npar]]
        o_refs = refs[ni + nh + npar:]
        first = pl.program_id(0) == 0
        vals = fn(first, in_v, halo_v, par_v)
        for o, val in zip(o_refs, vals):
            o[...] = val.astype(o.dtype)

    res = pl.pallas_call(
        body, name=name, grid=(n,),
        in_specs=_tile_specs(ins, halo_ids, params, TM, HR, lambda i: i),
        out_specs=[pl.BlockSpec((TM, w), lambda i: (i, 0)) for (w, _) in outs],
        out_shape=[jax.ShapeDtypeStruct((T, w), dt) for (w, dt) in outs],
        compiler_params=pltpu.CompilerParams(dimension_semantics=("arbitrary",)),
    )(*[a for (a, _, _) in ins], *[ins[h][0] for h in halo_ids], *params)
    return list(res)


def _tile_bwd(name, fn, T, TM, ins, params, cts, din, dpar, halo_ids=(), HR=CONV_HALO):
    TM = min(TM, T)
    n = T // TM
    ni, nh, npar, nc = len(ins), len(halo_ids), len(params), len(cts)
    din_ids = [j for (j, _) in din]
    dh_ids = [h for h in halo_ids if h in din_ids]
    nd, ndp, ndh = len(din), len(dpar), len(dh_ids)

    def body(*refs):
        in_v = [r[...] for r in refs[:ni]]
        halo_v = [r[...] for r in refs[ni:ni + nh]]
        par_v = [r[...] for r in refs[ni + nh:ni + nh + npar]]
        ct_v = [r[...].astype(f32) for r in refs[ni + nh + npar:ni + nh + npar + nc]]
        o_refs = refs[ni + nh + npar + nc:ni + nh + npar + nc + nd + ndp]
        carry_refs = refs[ni + nh + npar + nc + nd + ndp:]
        i = pl.program_id(0)
        first = i == n - 1

        def g(d_in, d_halo, d_par):
            full_in = list(in_v)
            for j, val in zip(din_ids, d_in):
                full_in[j] = val
            full_halo = list(halo_v)
            for h, val in zip(dh_ids, d_halo):
                full_halo[list(halo_ids).index(h)] = val
            full_par = list(par_v)
            for j, val in zip(dpar, d_par):
                full_par[j] = val
            return tuple(fn(first, full_in, full_halo, full_par))

        prim = ([in_v[j].astype(f32) for j in din_ids],
                [halo_v[list(halo_ids).index(h)].astype(f32) for h in dh_ids],
                [par_v[j] for j in dpar])
        _, vjp = jax.vjp(g, *prim)
        g_in, g_halo, g_par = vjp(tuple(ct_v))

        @pl.when(i == 0)
        def _():
            for c in carry_refs:
                c[...] = jnp.zeros_like(c)
            for o in o_refs[nd:]:
                o[...] = jnp.zeros_like(o)

        for slot, (j, _) in enumerate(din):
            val = g_in[slot]
            if j in dh_ids:
                c = carry_refs[dh_ids.index(j)]
                val = jnp.concatenate([val[:TM - HR], val[TM - HR:] + c[...]], axis=0) if TM > HR else val + c[...]
                c[...] = g_halo[dh_ids.index(j)]
            o_refs[slot][...] = val.astype(o_refs[slot].dtype)
        for slot in range(ndp):
            o_refs[nd + slot][...] += g_par[slot]

    rev = lambda i: n - 1 - i
    in_specs = _tile_specs(ins, halo_ids, params, TM, HR, rev)
    ct_specs = [pl.BlockSpec((TM, w), lambda i, cb=cb: (rev(i), cb)) for (_, w, cb) in cts]
    out_specs = [pl.BlockSpec((TM, ins[j][1]), lambda i: (rev(i), 0)) for j in din_ids]
    out_specs += [pl.BlockSpec(params[j].shape, lambda i, nd_=params[j].ndim: (0,) * nd_) for j in dpar]
    out_shape = [jax.ShapeDtypeStruct((T, ins[j][1]), dt) for (j, dt) in din]
    out_shape += [jax.ShapeDtypeStruct(params[j].shape, f32) for j in dpar]
    res = pl.pallas_call(
        body, name=name, grid=(n,),
        in_specs=in_specs + ct_specs,
        out_specs=out_specs,
        out_shape=out_shape,
        scratch_shapes=[pltpu.VMEM((HR, ins[h][1]), f32) for h in dh_ids],
        compiler_params=pltpu.CompilerParams(dimension_semantics=("arbitrary",)),
    )(*[a for (a, _, _) in ins], *[ins[h][0] for h in halo_ids], *params, *[a for (a, _, _) in cts])
    return list(res)


def _rms(x, g):
    return x * lax.rsqrt(jnp.mean(x * x, axis=-1, keepdims=True) + EPS) * g


def _fn_prenorm(first, ins, halos, params):
    (x,), (g,) = ins, params
    x = x.astype(f32)
    return [_rms(x, g), x]


def _fn_postmix(first, ins, halos, params):
    (x, mix), (g2, g3) = ins, params
    x1 = x + _rms(mix, g2)
    return [x1, _rms(x1, g3)]


def _fn_postmlp(first, ins, halos, params):
    (x1, ff), (g4,) = ins, params
    return [x1 + _rms(ff, g4)]


def _fn_rms_only(first, ins, halos, params):
    (ff,), (g4,) = ins, params
    return [_rms(ff, g4)]


def _fn_merge(first, ins, halos, params):
    ga, gb, ya, yb = ins
    return [jax.nn.sigmoid(ga) * ya + jax.nn.sigmoid(gb) * yb]


def _roll_rows(x, shift):
    return pltpu.roll(x, shift, 0)


_row_roll = jax.custom_vjp(_roll_rows, nondiff_argnums=(1,))
_row_roll.defvjp(lambda x, shift: (_roll_rows(x, shift), None),
                 lambda shift, _, ct: (_roll_rows(ct, ct.shape[0] - shift),))


def _make_fn_conv(norm_scale):
    def fn(first, ins, halos, params):
        (x,), (xp,), (w,) = ins, halos, params
        xp = jnp.where(first, 0.0, xp)
        outs = []
        for h in range(DN_HEADS):
            sl = slice(DN_DK * h, DN_DK * (h + 1))
            xe, wh = jnp.concatenate([xp[:, sl], x[:, sl]], axis=0), w[:, sl]
            y = xe[CONV_HALO:] * wh[DN_CONV - 1:DN_CONV]
            for j in range(DN_CONV - 1):
                y = y + _row_roll(xe, DN_CONV - 1 - j)[CONV_HALO:] * wh[j:j + 1]
            y = jax.nn.silu(y)
            if norm_scale is not None:
                y = y * lax.rsqrt(jnp.sum(y * y, axis=-1, keepdims=True) + EPS) * norm_scale
            outs.append(y)
        return [jnp.concatenate(outs, axis=-1)]
    return fn


def _fn_gates(first, ins, halos, params):
    (ba,), (avec, dvec) = ins, params
    lane = lax.broadcasted_iota(jnp.int32, ba.shape, 1)
    beta = jax.nn.sigmoid(ba)
    g = -jnp.exp(avec) * jax.nn.softplus(ba + dvec)
    return [jnp.where(lane < DN_HEADS, beta, jnp.where(lane < 2 * DN_HEADS, g, 0.0))]


def _fn_dnpost(first, ins, halos, params):
    (o, z), (ng,) = ins, params
    outs = []
    for h in range(DN_HEADS):
        sl = slice(DN_DK * h, DN_DK * (h + 1))
        outs.append(_rms(o[:, sl], ng) * jax.nn.silu(z[:, sl]))
    return [jnp.concatenate(outs, axis=-1)]


def _roll_lanes(x, shift):
    return pltpu.roll(x, shift, 1)


_lane_roll = jax.custom_vjp(_roll_lanes, nondiff_argnums=(1,))
_lane_roll.defvjp(lambda x, shift: (_roll_lanes(x, shift), None),
                  lambda shift, _, ct: (_roll_lanes(ct, LANES - shift),))


def _kcopy(g):
    return (g // (SW_Q_HEADS // SW_KV_HEADS)) * 2 + g % 2


def _attn_probs(qp, kx, sink, mask):
    heads = range(len(sink))
    s = [jnp.where(mask, _mm(qp[g // 2], kx[_kcopy(g)], NT), NEG_BIG) for g in heads]
    m = [jnp.maximum(jnp.max(s[g], axis=-1, keepdims=True), sink[g]) for g in heads]
    p = [jnp.exp(s[g] - m[g]) for g in heads]
    ps = [jnp.exp(sink[g] - m[g]) for g in heads]
    inv = [1.0 / (jnp.sum(p[g], axis=-1, keepdims=True) + ps[g]) for g in heads]
    return [p[g] * inv[g] for g in heads], [ps[g] * inv[g] for g in heads]


@jax.custom_vjp
def _attn_group(qp, kx, vx, sink, mask):
    probs, _ = _attn_probs(qp, kx, sink, mask)
    return tuple(_mm(probs[g], vx[_kcopy(g)], NN) for g in range(len(sink)))


def _attn_group_fwd(qp, kx, vx, sink, mask):
    o = _attn_group(qp, kx, vx, sink, mask)
    return o, (qp, kx, vx, sink, mask, o)


def _attn_group_bwd(res, do):
    qp, kx, vx, sink, mask, o = res
    heads = range(len(sink))
    probs, p_sink = _attn_probs(qp, kx, sink, mask)
    d_probs = [_mm(do[g], vx[_kcopy(g)], NT) for g in heads]
    dot = [jnp.sum(do[g] * o[g], axis=-1, keepdims=True) for g in heads]
    ds = [probs[g] * (d_probs[g] - dot[g]) for g in heads]
    d_qp = tuple(_mm(ds[2 * j], kx[_kcopy(2 * j)], NN) + _mm(ds[2 * j + 1], kx[_kcopy(2 * j + 1)], NN)
                 for j in range(len(qp)))
    rows = lambda xs: jnp.concatenate(xs, axis=0)
    d_kx = tuple(_mm(rows([ds[g] for g in heads if _kcopy(g) == c]), rows([qp[g // 2] for g in heads if _kcopy(g) == c]), TN)
                 for c in range(len(kx)))
    d_vx = tuple(_mm(rows([probs[g] for g in heads if _kcopy(g) == c]), rows([do[g] for g in heads if _kcopy(g) == c]), TN)
                 for c in range(len(kx)))
    d_sink = tuple(-jnp.sum(p_sink[g] * dot[g], axis=0, keepdims=True) for g in heads)
    return d_qp, d_kx, d_vx, d_sink, None


_attn_group.defvjp(_attn_group_fwd, _attn_group_bwd)


def _fn_swa(first, ins, halos, params):
    q, k, v, cos, sin = ins
    kp, vp, cosp, sinp = halos
    sinks, sel_a0, sel_b0, sel_a1, sel_b1 = params
    B = q.shape[0]
    half = ROT_DIM // 2
    in_head = jnp.bitwise_and(lax.broadcasted_iota(jnp.int32, (1, LANES), 1), SW_HEAD_DIM - 1)

    def rope(x, c, s):
        return (x * c + _lane_roll(x, LANES - half) * jnp.where(in_head < half, -s, 0.0)
                + _lane_roll(x, half) * jnp.where(in_head >= half, s, 0.0))

    kcat = jnp.concatenate([rope(kp, cosp, sinp), rope(k, cos, sin)], axis=0)
    vcat = jnp.concatenate([vp, v], axis=0)
    r = lax.broadcasted_iota(jnp.int32, (B, 2 * B), 0)
    c = lax.broadcasted_iota(jnp.int32, (B, 2 * B), 1)
    mask = (c > r) & (c <= r + B) & ((c >= B) | jnp.logical_not(first))
    sels = (sel_a0, sel_b0, sel_a1, sel_b1)
    kx = tuple(_mm(kcat, s, NN) for s in sels)
    vx = tuple(_mm(vcat, s, NN) for s in sels)
    qp = tuple(rope(q[:, LANES * j:LANES * (j + 1)], cos, sin) * (SW_HEAD_DIM ** -0.5) for j in range(SW_Q_HEADS // 2))
    sink = tuple(sinks[:, g:g + 1] for g in range(SW_Q_HEADS))
    o = _attn_group(qp, kx, vx, sink, mask)
    return [jnp.concatenate([o[2 * j] + o[2 * j + 1] for j in range(SW_Q_HEADS // 2)], axis=-1)]


@jax.custom_vjp
def _inv_unit_lower(Ls):
    C = Ls[0].shape[0]
    ii = lax.broadcasted_iota(jnp.int32, (C, C), 0)
    jj = lax.broadcasted_iota(jnp.int32, (C, C), 1)

    def off_mask(level):
        same_pair = jnp.right_shift(ii, level + 1) == jnp.right_shift(jj, level + 1)
        lower_left = (jnp.bitwise_and(jnp.right_shift(ii, level), 1) == 1) & (jnp.bitwise_and(jnp.right_shift(jj, level), 1) == 0)
        return same_pair & lower_left

    eye = (ii == jj).astype(f32)
    m0 = off_mask(0)
    Ts = [eye - jnp.where(m0, L, 0.0) for L in Ls]
    for level in range(1, int(math.log2(C))):
        mk = off_mask(level)
        left = [_mm(T_, jnp.where(mk, L, 0.0), NN) for T_, L in zip(Ts, Ls)]
        Ts = [T_ - _mm(a, T_, NN) for a, T_ in zip(left, Ts)]
    return tuple(Ts)


def _inv_fwd(Ls):
    Ts = _inv_unit_lower(Ls)
    return Ts, Ts


def _inv_bwd(Ts, dTs):
    left = [_mm(T_, dT, TN) for T_, dT in zip(Ts, dTs)]
    return (tuple(-_mm(a, T_, NT) for a, T_ in zip(left, Ts)),)


_inv_unit_lower.defvjp(_inv_fwd, _inv_bwd)


@jax.custom_vjp
def _inv_known(Ls, Ts):
    return Ts


_inv_known.defvjp(lambda Ls, Ts: (Ts, Ts),
                  lambda Ts, dTs: (_inv_bwd(Ts, dTs)[0], tuple(jnp.zeros_like(t) for t in Ts)))


def _mm_01(a, b, dims):
    hi = b.astype(bf16)
    r1 = b - hi.astype(f32)
    mid = r1.astype(bf16)
    lo = (r1 - mid.astype(f32)).astype(bf16)
    a16 = a.astype(bf16)
    dot = lambda part: lax.dot_general(a16, part, (dims, ((), ())), preferred_element_type=f32)
    return dot(hi) + dot(mid) + dot(lo)


def _eye(n):
    return lax.broadcasted_iota(jnp.int32, (n, n), 0) == lax.broadcasted_iota(jnp.int32, (n, n), 1)


def _lower(n):
    return lax.broadcasted_iota(jnp.int32, (n, n), 0) >= lax.broadcasted_iota(jnp.int32, (n, n), 1)


@jax.custom_vjp
def _transpose(x):
    return _mm_01(_eye(x.shape[1]), x, NT)


_transpose.defvjp(lambda x: (_transpose(x), None), lambda _, ct: (_transpose(ct),))


@jax.custom_vjp
def _cumsum_rows(x):
    return _mm_01(_lower(x.shape[0]), x, NN)


_cumsum_rows.defvjp(lambda x: (_cumsum_rows(x), None), lambda _, ct: (_mm_01(_lower(ct.shape[0]), ct, TN),))


def _dn_chunk(q, k, v, gb, S, tinv_known=None):
    C = q.shape[0]
    ii = lax.broadcasted_iota(jnp.int32, (C, C), 0)
    jj = lax.broadcasted_iota(jnp.int32, (C, C), 1)
    causal, strict = ii >= jj, ii > jj
    gc_all = _cumsum_rows(gb)
    gc_t = _transpose(gc_all)
    o, s_new, tinv = [None] * DN_HEADS, [None] * DN_HEADS, [None] * DN_HEADS
    for first in range(0, DN_HEADS, DN_GROUP):
        H = range(first, first + DN_GROUP)
        at = lambda xs: dict(zip(H, xs))
        qs, ks, vs = (at([x[:, DN_DK * h:DN_DK * (h + 1)] for h in H]) for x in (q, k, v))
        beta = at([gb[:, h:h + 1] for h in H])
        gcol = at([gc_all[:, DN_HEADS + h:DN_HEADS + h + 1] for h in H])
        grow = at([gc_t[DN_HEADS + h:DN_HEADS + h + 1, :] for h in H])
        decay = at([jnp.where(causal, jnp.exp(jnp.where(causal, gcol[h] - grow[h], 0.0)), 0.0) for h in H])
        kb = at([ks[h] * beta[h] for h in H])
        kk = at([_mm(kb[h], ks[h], NT) for h in H])
        qk = at([_mm(qs[h], ks[h], NT) for h in H])
        Ls = tuple(jnp.where(strict, kk[h] * decay[h], 0.0) for h in H)
        ti = at(_inv_unit_lower(Ls) if tinv_known is None else _inv_known(Ls, tuple(tinv_known[h] for h in H)))
        eg = at([jnp.exp(gcol[h]) for h in H])
        u = at([_mm(ti[h], vs[h] * beta[h], NN) for h in H])
        w = at([_mm(ti[h], kb[h] * eg[h], NN) for h in H])
        gl = at([gcol[h][C - 1:C, :] for h in H])
        ws = at([_mm(w[h], S[h], NN) for h in H])
        qS = at([_mm(qs[h] * eg[h], S[h], NN) for h in H])
        v_new = at([u[h] - ws[h] for h in H])
        av = at([_mm(qk[h] * decay[h], v_new[h], NN) for h in H])
        kv = at([_mm(ks[h] * jnp.exp(gl[h] - gcol[h]), v_new[h], TN) for h in H])
        for h in H:
            o[h], s_new[h], tinv[h] = qS[h] + av[h], S[h] * jnp.exp(gl[h]) + kv[h], ti[h]
    return jnp.concatenate(o, axis=-1), tuple(s_new), tuple(tinv)


_DN_W = DN_HEADS * DN_DK


def _dn_inputs(first, xq, xk, xv, hq, hk, hv, ba, cw, avec, dvec):
    conv = lambda scale, x, h, j: _make_fn_conv(scale)(first, [x], [h], [cw[:, _DN_W * j:_DN_W * (j + 1)]])[0]
    return (conv(DN_DK ** -0.5, xq, hq, 0), conv(1.0, xk, hk, 1), conv(None, xv, hv, 2),
            _fn_gates(first, [ba], [], [avec, dvec])[0])


def _delta_specs(C, row_of):
    cols = (CB_Q, CB_K, CB_V)
    specs = [pl.BlockSpec((C, _DN_W), lambda i, cb=cb: (row_of(i), cb)) for cb in cols]
    specs += [pl.BlockSpec((CONV_HALO, _DN_W), lambda i, cb=cb: (jnp.maximum(row_of(i) * (C // CONV_HALO) - 1, 0), cb))
              for cb in cols]
    specs.append(pl.BlockSpec((C, LANES), lambda i: (row_of(i), CB_BA)))
    return specs


def _whole(a):
    return pl.BlockSpec(a.shape, lambda i, nd=a.ndim: (0,) * nd)


def _carrying(body, n_in, n_out, n_scratch, n_steps, ex):
    if ex is None:
        return body

    def wrapped(*refs):
        ins, rest = refs[:n_in], refs[n_in:]
        ex_in, rest = rest[:ex.n], rest[ex.n:]
        outs, rest = rest[:n_out], rest[n_out:]
        ex_out, rest = rest[:ex.n], rest[ex.n:]
        scratch, sems = rest[:n_scratch], rest[n_scratch:]
        start, finish = ex.bind(ex_in, ex_out, *sems)
        step = pl.program_id(0)
        pl.when(step == 0)(start)
        body(*ins, *outs, *scratch)
        pl.when(step == n_steps - 1)(finish)

    return wrapped


def _carried_call(name, body, grid, in_specs, out_specs, out_shape, scratch, operands, ex, ex_arrs):
    n_out = len(out_shape)
    if ex is not None:
        in_specs, out_specs = in_specs + ex.specs, out_specs + ex.specs
        out_shape, scratch, operands = out_shape + ex.out_shape, scratch + ex.scratch, tuple(operands) + tuple(ex_arrs)
    res = pl.pallas_call(
        _carrying(body, len(in_specs) - (ex.n if ex else 0), n_out, len(scratch) - (3 if ex else 0), grid[0], ex),
        name=name, grid=grid, in_specs=in_specs, out_specs=out_specs, out_shape=out_shape, scratch_shapes=scratch,
        compiler_params=pltpu.CompilerParams(dimension_semantics=("arbitrary",)),
    )(*operands)
    return list(res[:n_out]), list(res[n_out:])


def _delta_fwd(proj, cw, avec, dvec, comm=None):
    T = proj.shape[0]
    C = min(DN_CHUNK, T)
    n = T // C
    ex = None if comm is None else _Exchange(*comm)

    def body(xq, xk, xv, hq, hk, hv, ba, cw_ref, a_ref, d_ref, o_ref, hist_ref, tinv_ref,
             q_ref, k_ref, v_ref, gb_ref, s_ref):
        first = pl.program_id(0) == 0

        @pl.when(first)
        def _():
            s_ref[...] = jnp.zeros_like(s_ref)

        S = tuple(s_ref[h] for h in range(DN_HEADS))
        for h in range(DN_HEADS):
            hist_ref[0, h] = S[h]
        q, k, v, gb = _dn_inputs(first, xq[...], xk[...], xv[...], hq[...], hk[...], hv[...], ba[...],
                                 cw_ref[...], a_ref[...], d_ref[...])
        q_ref[...], k_ref[...], v_ref[...], gb_ref[...] = q, k, v, gb
        o, s_new, tinv = _dn_chunk(q, k, v, gb, S)
        o_ref[...] = o
        for h in range(DN_HEADS):
            s_ref[h] = s_new[h]
            tinv_ref[0, h] = tinv[h]

    row = pl.BlockSpec((C, _DN_W), lambda i: (i, 0))
    return _carried_call(
        "delta_fwd", body, (n,),
        _delta_specs(C, lambda i: i) + [_whole(cw), _whole(avec), _whole(dvec)],
        [row, pl.BlockSpec((1, DN_HEADS, DN_DK, DN_DK), lambda i: (i, 0, 0, 0)),
         pl.BlockSpec((1, DN_HEADS, C, C), lambda i: (i, 0, 0, 0)),
         row, row, row, pl.BlockSpec((C, LANES), lambda i: (i, 0))],
        [jax.ShapeDtypeStruct((T, _DN_W), f32), jax.ShapeDtypeStruct((n, DN_HEADS, DN_DK, DN_DK), f32),
         jax.ShapeDtypeStruct((n, DN_HEADS, C, C), f32)]
        + [jax.ShapeDtypeStruct((T, _DN_W), f32)] * 3 + [jax.ShapeDtypeStruct((T, LANES), f32)],
        [pltpu.VMEM((DN_HEADS, DN_DK, DN_DK), f32)],
        (proj, proj, proj, proj, proj, proj, proj, cw, avec, dvec), ex, comm[0] if comm else ())


def _delta_bwd(qn, kn, vv, gb, hist, tinv, do, comm=None):
    T = qn.shape[0]
    C = min(DN_CHUNK, T)
    n = T // C
    ex = None if comm is None else _Exchange(*comm)

    def body(q_ref, k_ref, v_ref, gb_ref, hist_ref, tinv_ref, do_ref, dq_ref, dk_ref, dv_ref, dgb_ref, ds_ref):
        @pl.when(pl.program_id(0) == 0)
        def _():
            ds_ref[...] = jnp.zeros_like(ds_ref)

        S = tuple(hist_ref[0, h] for h in range(DN_HEADS))
        known = tuple(tinv_ref[0, h] for h in range(DN_HEADS))
        chunk = lambda q, k, v, g, s: _dn_chunk(q, k, v, g, s, tinv_known=known)[:2]
        _, vjp = jax.vjp(chunk, q_ref[...], k_ref[...], v_ref[...], gb_ref[...], S)
        dS = tuple(ds_ref[h] for h in range(DN_HEADS))
        dq, dk, dv, dgb, dS_in = vjp((do_ref[...], dS))
        dq_ref[...] = dq
        dk_ref[...] = dk
        dv_ref[...] = dv
        dgb_ref[...] = dgb
        for h in range(DN_HEADS):
            ds_ref[h] = dS_in[h]

    row = pl.BlockSpec((C, _DN_W), lambda i: (n - 1 - i, 0))
    small = pl.BlockSpec((C, LANES), lambda i: (n - 1 - i, 0))
    return _carried_call(
        "delta_bwd", body, (n,),
        [row, row, row, small, pl.BlockSpec((1, DN_HEADS, DN_DK, DN_DK), lambda i: (n - 1 - i, 0, 0, 0)),
         pl.BlockSpec((1, DN_HEADS, C, C), lambda i: (n - 1 - i, 0, 0, 0)), row],
        [row, row, row, small],
        [jax.ShapeDtypeStruct((T, _DN_W), f32)] * 3 + [jax.ShapeDtypeStruct((T, LANES), f32)],
        [pltpu.VMEM((DN_HEADS, DN_DK, DN_DK), f32)],
        (qn, kn, vv, gb, hist, tinv, do), ex, comm[0] if comm else ())


TM_ROW = 512
TM_CONV = 256
W1 = D_MODEL


def _first_only(fn):
    return lambda *a: fn(*a)[:1]


def _swa_args(proj, cst):
    ins = [(proj, W1, CB_SWQ), (proj, LANES, CB_SWK), (proj, LANES, CB_SWV), (cst["cos"], LANES, 0), (cst["sin"], LANES, 0)]
    return ins, (1, 2, 3, 4)


def _layer_fwd(x, p, cst, comm=None, late=None):
    T = x.shape[0]
    r = {"x": x}
    (h,) = _tile_fwd("prenorm", _first_only(_fn_prenorm), T, TM_ROW, [(x, W1, 0)], [p["g1"]], [(W1, bf16)])
    proj = _matmul("proj", h, p["w_in"], "nn", f32, tm=2048, tn=1536)
    (o, hist, tinv, qn, kn, vv, gbt), comm_out = _delta_fwd(proj, p["conv_w"], p["avec"], p["dvec"], comm)
    if late is not None:
        p = {**p, **late(comm_out)}
    (dn_out,) = _tile_fwd("dnpost", _fn_dnpost, T, TM_ROW, [(o, W1, 0), (proj, W1, CB_Z)], [p["ng"]], [(W1, bf16)])
    sw_ins, sw_halo = _swa_args(proj, cst)
    sw_par = [p["sinks"], cst["sel_a0"], cst["sel_b0"], cst["sel_a1"], cst["sel_b1"]]
    (sw_out,) = _tile_fwd("swa", _fn_swa, T, SW_BLOCK, sw_ins, sw_par, [(W1, bf16)], halo_ids=sw_halo, HR=SW_BLOCK)
    y_a = _matmul("up_dn", dn_out, p["w_up_dn"], "nn", f32, tm=2048, tn=1024)
    y_b = _matmul("up_sw", sw_out, p["w_up_sw"], "nn", f32, tm=2048, tn=1024)
    (gated,) = _tile_fwd("merge", _fn_merge, T, TM_ROW,
                         [(proj, W1, CB_GA), (proj, W1, CB_GB), (y_a, W1, 0), (y_b, W1, 0)], [], [(W1, bf16)])
    mix = _matmul("w_o", gated, p["w_o"], "nn", f32, tm=2048, tn=1024)
    x1, h2 = _tile_fwd("postmix", _fn_postmix, T, TM_ROW, [(x, W1, 0), (mix, W1, 0)], [p["g2"], p["g3"]],
                       [(W1, f32), (W1, bf16)])
    ffh, act = _matmul("ff1", h2, p["w_ff1"], "nn", [f32, bf16], tm=2048, tn=1024,
                       epilogue=lambda acc, ex: [acc, jnp.square(jnp.maximum(acc, 0.0))])
    ff = _matmul("ff2", act, p["w_ff2"], "nn", f32, tm=2048, tn=1024)
    (x2,) = _tile_fwd("postmlp", _fn_postmlp, T, TM_ROW, [(x1, W1, 0), (ff, W1, 0)], [p["g4"]], [(W1, f32)])
    r.update(h=h, proj=proj, qn=qn, kn=kn, vv=vv, gbt=gbt, o=o, hist=hist, tinv=tinv, dn_out=dn_out, sw_out=sw_out,
             y_a=y_a, y_b=y_b, gated=gated, mix=mix, h2=h2, ffh=ffh, act=act, ff=ff)
    return x2, r, p, comm_out


def _pieces(n, full):
    return _shard_major(n, full[None])[:, 0].astype(bf16)


def _layer_bwd(dx2, r, p, cst, carry=None):
    T = dx2.shape[0]
    x, proj = r["x"], r["proj"]
    g = {}
    dff, g["g4"] = _tile_bwd("postmlp_b", _fn_rms_only, T, TM_ROW, [(r["ff"], W1, 0)], [p["g4"]], [(dx2, W1, 0)],
                             [(0, bf16)], [0])
    (dffh,) = _matmul("ff2_dx", dff, p["w_ff2"], "nt", [bf16], tm=2048, tn=1024, extra=[r["ffh"]],
                      epilogue=lambda acc, ex: [acc * (2.0 * jnp.maximum(ex[0], 0.0))])
    g["w_ff2"] = _matmul("ff2_dw", r["act"], dff, "tn", bf16, tm=1024, tn=1024, tk=2048)
    dh2 = _matmul("ff1_dx", dffh, p["w_ff1"], "nt", f32, tm=2048, tn=1024)
    g["w_ff1"] = _matmul("ff1_dw", r["h2"], dffh, "tn", bf16, tm=1024, tn=1024, tk=2048)
    dx1, dmix, g["g2"], g["g3"] = _tile_bwd("postmix_b", _fn_postmix, T, TM_ROW, [(x, W1, 0), (r["mix"], W1, 0)],
                                            [p["g2"], p["g3"]], [(dx2, W1, 0), (dh2, W1, 0)], [(0, f32), (1, bf16)], [0, 1])
    dgated = _matmul("w_o_dx", dmix, p["w_o"], "nt", f32, tm=2048, tn=1024)
    g["w_o"] = _matmul("w_o_dw", r["gated"], dmix, "tn", bf16, tm=1024, tn=1024, tk=2048)
    dga, dgb, dya, dyb = _tile_bwd("merge_b", _fn_merge, T, TM_ROW,
                                   [(proj, W1, CB_GA), (proj, W1, CB_GB), (r["y_a"], W1, 0), (r["y_b"], W1, 0)], [],
                                   [(dgated, W1, 0)], [(0, bf16), (1, bf16), (2, bf16), (3, bf16)], [])
    d_dn = _matmul("up_dn_dx", dya, p["w_up_dn"], "nt", f32, tm=2048, tn=1024)
    g["w_up_dn"] = _matmul("up_dn_dw", r["dn_out"], dya, "tn", bf16, tm=1024, tn=1024, tk=2048)
    d_sw = _matmul("up_sw_dx", dyb, p["w_up_sw"], "nt", f32, tm=2048, tn=1024)
    g["w_up_sw"] = _matmul("up_sw_dw", r["sw_out"], dyb, "tn", bf16, tm=1024, tn=1024, tk=2048)
    do, dz, g["ng"] = _tile_bwd("dnpost_b", _fn_dnpost, T, TM_ROW, [(r["o"], W1, 0), (proj, W1, CB_Z)], [p["ng"]],
                                [(d_dn, W1, 0)], [(0, f32), (1, bf16)], [0])
    comm = None
    if carry is not None:
        send = list(carry) + [_pieces(n, g[n]) for n in _LATE]
        comm = (send, ["scatter"] * len(send))
    (dqn, dkn, dvv, dgbt), comm_out = _delta_bwd(r["qn"], r["kn"], r["vv"], r["gbt"], r["hist"], r["tinv"], do, comm)
    conv_b = lambda nm, cb, scale, ct: _tile_bwd(nm, _make_fn_conv(scale), T, TM_CONV, [(proj, W1, cb)],
                                                 [p["conv_w"][:, W1 * cb:W1 * (cb + 1)]], [(ct, W1, 0)], [(0, bf16)], [0],
                                                 halo_ids=(0,))
    dq_in, dcw_q = conv_b("conv_q_b", CB_Q, DN_DK ** -0.5, dqn)
    dk_in, dcw_k = conv_b("conv_k_b", CB_K, 1.0, dkn)
    dv_in, dcw_v = conv_b("conv_v_b", CB_V, None, dvv)
    g["conv_w"] = jnp.concatenate([dcw_q, dcw_k, dcw_v], axis=-1)
    dba, g["avec"], g["dvec"] = _tile_bwd("gates_b", _fn_gates, T, TM_ROW, [(proj, LANES, CB_BA)], [p["avec"], p["dvec"]],
                                          [(dgbt, LANES, 0)], [(0, bf16)], [0, 1])
    sw_ins, sw_halo = _swa_args(proj, cst)
    sw_par = [p["sinks"], cst["sel_a0"], cst["sel_b0"], cst["sel_a1"], cst["sel_b1"]]
    dswq, dswk, dswv, g["sinks"] = _tile_bwd("swa_b", _fn_swa, T, SW_BLOCK, sw_ins, sw_par, [(d_sw, W1, 0)],
                                             [(0, bf16), (1, bf16), (2, bf16)], [0], halo_ids=sw_halo, HR=SW_BLOCK)
    dproj = jnp.concatenate([dq_in, dk_in, dv_in, dz, dswq, dga, dgb, dswk, dswv, dba, jnp.zeros((T, LANES), bf16)], axis=-1)
    dh = _matmul("proj_dx", dproj, p["w_in"], "nt", f32, tm=2048, tn=1024, tk=1536)
    g["w_in"] = _matmul("proj_dw", r["h"], dproj, "tn", bf16, tm=1024, tn=1536, tk=2048)
    dx, g["g1"] = _tile_bwd("prenorm_b", _fn_prenorm, T, TM_ROW, [(x, W1, 0)], [p["g1"]], [(dh, W1, 0), (dx1, W1, 0)],
                            [(0, f32)], [0])
    return dx, g, comm_out


_OFF_BA, _OFF_SWQ, _OFF_SWK, _OFF_GA, _D_IN = 4096, 4112, 5136, 5392, 7440


def _proj_cols(w):
    pad = lambda n: jnp.zeros(w.shape[:-1] + (n,), w.dtype)
    return jnp.concatenate([w[..., :_OFF_BA], w[..., _OFF_SWQ:_OFF_SWK], w[..., _OFF_GA:_D_IN],
                            w[..., _OFF_SWK:_OFF_GA], w[..., _OFF_BA:_OFF_SWQ], pad(PROJ_W - _D_IN)], axis=-1)


def _proj_cols_inv(w):
    n_ba = _OFF_SWQ - _OFF_BA
    return jnp.concatenate([w[..., :4096], w[..., 7424:7424 + n_ba], w[..., 4096:5120], w[..., 7168:7424],
                            w[..., 5120:7168]], axis=-1)


def _lane_pad(v, at):
    return jnp.pad(v.astype(f32), (at, LANES - at - v.shape[0])).reshape(1, LANES)


_EARLY = ("w_in", "dn_conv_w")
_LATE = ("w_up_dn", "w_up_sw", "w_o", "w_ff1", "w_ff2")


def _early_params(w):
    row = lambda v: v.reshape(1, -1).astype(f32)
    return dict(
        g1=row(w["pre_mix_g"]), g2=row(w["post_mix_g"]), g3=row(w["pre_mlp_g"]), g4=row(w["post_mlp_g"]),
        w_in=_proj_cols(w["w_in"]).astype(bf16), conv_w=w["dn_conv_w"].astype(f32),
        avec=_lane_pad(w["dn_a_log"], DN_HEADS), dvec=_lane_pad(w["dn_dt_bias"], DN_HEADS),
        ng=row(w["dn_norm_g"]), sinks=_lane_pad(w["sw_sinks"], 0))


def _late_params(w):
    return {n: w[n].astype(bf16) for n in _LATE}


def _layer_params(w):
    return {**_early_params(w), **_late_params(w)}


def _layer_grads_ref_layout(g):
    return dict(
        pre_mix_g=g["g1"][0], post_mix_g=g["g2"][0], pre_mlp_g=g["g3"][0], post_mlp_g=g["g4"][0],
        w_in=_proj_cols_inv(g["w_in"]), dn_conv_w=g["conv_w"],
        dn_a_log=g["avec"][0, DN_HEADS:2 * DN_HEADS], dn_dt_bias=g["dvec"][0, DN_HEADS:2 * DN_HEADS],
        dn_norm_g=g["ng"][0], sw_sinks=g["sinks"][0, :SW_Q_HEADS],
        w_up_dn=g["w_up_dn"], w_up_sw=g["w_up_sw"], w_o=g["w_o"], w_ff1=g["w_ff1"], w_ff2=g["w_ff2"])


def _consts(positions):
    T = positions.shape[0]
    half = ROT_DIM // 2
    inv_freq = ROPE_THETA ** (-jnp.arange(half, dtype=f32) * (2.0 / ROT_DIM))
    ang = positions.astype(f32)[:, None] * inv_freq
    cos8, sin8 = jnp.cos(ang), jnp.sin(ang)
    rest = SW_HEAD_DIM - ROT_DIM
    c64 = jnp.concatenate([cos8, cos8, jnp.ones((T, rest), f32)], axis=-1)
    s64 = jnp.concatenate([sin8, sin8, jnp.zeros((T, rest), f32)], axis=-1)
    sel = np.zeros((2, 2, LANES, LANES), np.float32)
    for hk in range(SW_KV_HEADS):
        for d in range(SW_HEAD_DIM):
            sel[hk, 0, SW_HEAD_DIM * hk + d, d] = 1.0
            sel[hk, 1, SW_HEAD_DIM * hk + d, SW_HEAD_DIM + d] = 1.0
    return dict(cos=jnp.concatenate([c64, c64], axis=-1), sin=jnp.concatenate([s64, s64], axis=-1),
                sel_a0=jnp.asarray(sel[0, 0]), sel_b0=jnp.asarray(sel[0, 1]),
                sel_a1=jnp.asarray(sel[1, 0]), sel_b1=jnp.asarray(sel[1, 1]))


def _loss(y, tgt):
    T, W = y.shape
    TM = min(TM_ROW, T)
    n = T // TM

    def body(y_ref, t_ref, dy_ref, acc_ref):
        @pl.when(pl.program_id(0) == 0)
        def _():
            acc_ref[...] = jnp.zeros_like(acc_ref)

        d = y_ref[...] - t_ref[...]
        dy_ref[...] = d * (1.0 / W)
        acc_ref[...] += jnp.sum(d * d, axis=0, keepdims=True)

    row = pl.BlockSpec((TM, W), lambda i: (i, 0))
    return pl.pallas_call(
        body, name="loss", grid=(n,), in_specs=[row, row],
        out_specs=[row, pl.BlockSpec((1, W), lambda i: (0, 0))],
        out_shape=[jax.ShapeDtypeStruct((T, W), f32), jax.ShapeDtypeStruct((1, W), f32)],
        compiler_params=pltpu.CompilerParams(dimension_semantics=("arbitrary",)),
    )(y, tgt)


N_SEM = N_DEV - 1


class _Exchange:
    def __init__(self, arrs, modes):
        self.modes, self.n = list(modes), len(arrs)
        self.out_shape = [jax.ShapeDtypeStruct((N_DEV,) + (a.shape[1:] if md == "scatter" else a.shape), a.dtype)
                          for a, md in zip(arrs, modes)]
        self.scratch = [pltpu.SemaphoreType.DMA((self.n, N_SEM)), pltpu.SemaphoreType.DMA((self.n, N_SEM)),
                        pltpu.SemaphoreType.DMA((self.n,))]
        self.specs = [pl.BlockSpec(memory_space=pltpu.HBM)] * self.n

    def bind(self, ins, outs, send_sems, recv_sems, loc_sems):
        x, y, c = lax.axis_index("x"), lax.axis_index("y"), lax.axis_index("c")
        me, sib = 4 * x + 2 * y + c, (x, y, 1 - c)
        flips = [(1 - x, y), (x, 1 - y), (1 - x, 1 - y)]

        def rcopy(a, k, src, dst, dev):
            return pltpu.make_async_remote_copy(src_ref=src, dst_ref=dst, send_sem=send_sems.at[a, k],
                                                recv_sem=recv_sems.at[a, k], device_id=dev,
                                                device_id_type=pl.DeviceIdType.MESH)

        sends, recvs, local, passes = [], [], [], []
        for a, md in enumerate(self.modes):
            src_all, out = ins[a], outs[a]
            mine = (lambda dev: src_all.at[dev]) if md == "scatter" else (lambda dev: src_all)
            local.append(pltpu.make_async_copy(mine(me), out.at[me], loc_sems.at[a]))
            if md in ("gather", "scatter"):
                for k in range(1, N_DEV):
                    px = 1 - x if (k >> 2) & 1 else x
                    py = 1 - y if (k >> 1) & 1 else y
                    pc = 1 - c if k & 1 else c
                    peer = 4 * px + 2 * py + pc
                    sends.append(rcopy(a, k - 1, mine(peer), out.at[me], (px, py, pc)))
                    recvs.append(rcopy(a, k - 1, mine(me), out.at[peer], (px, py, pc)))
            elif md == "gather2":
                sends.append(rcopy(a, 0, src_all, out.at[me], sib))
                recvs.append(rcopy(a, 0, src_all, out.at[4 * x + 2 * y + 1 - c], sib))
                for j, (px, py) in enumerate(flips):
                    sends.append(rcopy(a, 1 + j, src_all, out.at[me], (px, py, c)))
                    theirs = out.at[4 * px + 2 * py + c]
                    arrive = rcopy(a, 1 + j, src_all, theirs, (px, py, c))
                    passes.append((arrive, rcopy(a, 4 + j, theirs, theirs, sib)))
                    recvs.append(rcopy(a, 4 + j, src_all, out.at[4 * px + 2 * py + 1 - c], sib))

        def start():
            for cp in local + sends:
                cp.start()

        def finish():
            for arrive, onward in passes:
                arrive.wait_recv()
                onward.start()
            for cp in recvs:
                cp.wait_recv()
            for cp in sends:
                cp.wait_send()
            for _, onward in passes:
                onward.wait_send()
            for cp in local:
                cp.wait()

        return start, finish


def _exchange(name, arrs, modes):
    ex = _Exchange(arrs, modes)
    n = ex.n

    def body(*refs):
        start, finish = ex.bind(refs[:n], refs[n:2 * n], *refs[2 * n:])
        start()
        finish()

    res = pl.pallas_call(body, name=name, in_specs=ex.specs, out_specs=ex.specs, out_shape=ex.out_shape,
                         scratch_shapes=ex.scratch)(*arrs)
    return list(res)


def _adamw(name, land, w, m, v, tr):
    L_, R_, C_ = w.shape
    n_slots = land.shape[0]
    tr = min(tr, R_)
    assert R_ % tr == 0, (name, R_, tr)
    c1 = 1.0 - ADAM_B1 ** ADAM_STEP
    c2 = 1.0 - ADAM_B2 ** ADAM_STEP

    def body(l_ref, w_ref, m_ref, v_ref, g_ref, d_ref, mo_ref, vo_ref):
        g = l_ref[0].astype(f32)
        for s in range(1, n_slots):
            g = g + l_ref[s].astype(f32)
        m_new = ADAM_B1 * m_ref[...] + (1.0 - ADAM_B1) * g
        v_new = ADAM_B2 * v_ref[...] + (1.0 - ADAM_B2) * jnp.square(g)
        m_hat = m_new / c1
        v_hat = v_new / c2
        g_ref[...] = g
        d_ref[...] = -ADAM_LR * (m_hat / (jnp.sqrt(v_hat) + ADAM_EPS) + ADAM_WD * w_ref[...])
        mo_ref[...] = m_new
        vo_ref[...] = v_new

    row = pl.BlockSpec((1, tr, C_), lambda l, i: (l, i, 0))
    return pl.pallas_call(
        body, name=name, grid=(L_, R_ // tr),
        in_specs=[pl.BlockSpec((n_slots, 1, tr, C_), lambda l, i: (0, l, i, 0)), row, row, row],
        out_specs=[row] * 4, out_shape=[jax.ShapeDtypeStruct((L_, R_, C_), f32)] * 4,
        compiler_params=pltpu.CompilerParams(dimension_semantics=("arbitrary", "arbitrary")),
    )(land, w, m, v)


_BIG = ("w_in", "dn_conv_w", "w_up_dn", "w_up_sw", "w_o", "w_ff1", "w_ff2")
_COL_SHARDED = ("w_in", "dn_conv_w", "w_ff1")
_SMALL_ROWS = ("pre_mix_g", "post_mix_g", "pre_mlp_g", "post_mlp_g")
_SMALL_MISC = ("dn_a_log", "dn_dt_bias", "dn_norm_g", "sw_sinks")
_WEIGHTS = ("pre_mix_g", "w_in", "dn_conv_w", "dn_a_log", "dn_dt_bias", "dn_norm_g", "sw_sinks", "w_up_dn", "w_up_sw",
            "w_o", "post_mix_g", "pre_mlp_g", "w_ff1", "w_ff2", "post_mlp_g")
_SMALL_PACK_ROWS = 24


def _unshard(name, g):
    if name in _COL_SHARDED:
        g = jnp.moveaxis(g, 0, -2)
        return g.reshape(g.shape[:-2] + (g.shape[-2] * g.shape[-1],))
    g = jnp.moveaxis(g, 0, 1)
    return g.reshape((g.shape[0], g.shape[1] * g.shape[2]) + g.shape[3:])


def _shard_major(name, full):
    if name in _COL_SHARDED:
        s = full.reshape(full.shape[:-1] + (N_DEV, full.shape[-1] // N_DEV))
        return jnp.moveaxis(s, -2, 0)
    s = full.reshape((full.shape[0], N_DEV, full.shape[1] // N_DEV) + full.shape[2:])
    return jnp.moveaxis(s, 1, 0)


def _pack_small(d):
    rows = jnp.concatenate([d[n] for n in _SMALL_ROWS], axis=0)
    misc = jnp.concatenate([d[n].reshape(-1) for n in _SMALL_MISC])
    misc = jnp.pad(misc, (0, W1 - misc.shape[0])).reshape(1, W1)
    out = jnp.concatenate([rows, misc], axis=0)
    return jnp.pad(out, ((0, _SMALL_PACK_ROWS - out.shape[0]), (0, 0)))


def _unpack_small(a, like):
    out, L = {}, like[_SMALL_ROWS[0]].shape[0]
    for i, n in enumerate(_SMALL_ROWS):
        out[n] = a[L * i:L * (i + 1)]
    at, row = 0, a[L * len(_SMALL_ROWS)]
    for n in _SMALL_MISC:
        size = like[n].size
        out[n] = row[at:at + size].reshape(like[n].shape)
        at += size
    return out


def kernel(x, positions, pre_mix_g, w_in, dn_conv_w, dn_a_log, dn_dt_bias, dn_norm_g, sw_sinks, w_up_dn, w_up_sw, w_o, post_mix_g, pre_mlp_g, w_ff1, w_ff2, post_mlp_g, loss_target, m_pre_mix_g, m_w_in, m_dn_conv_w, m_dn_a_log, m_dn_dt_bias, m_dn_norm_g, m_sw_sinks, m_w_up_dn, m_w_up_sw, m_w_o, m_post_mix_g, m_pre_mlp_g, m_w_ff1, m_w_ff2, m_post_mlp_g, v_pre_mix_g, v_w_in, v_dn_conv_w, v_dn_a_log, v_dn_dt_bias, v_dn_norm_g, v_sw_sinks, v_w_up_dn, v_w_up_sw, v_w_o, v_post_mix_g, v_pre_mlp_g, v_w_ff1, v_w_ff2, v_post_mlp_g):
    w = dict(pre_mix_g=pre_mix_g, w_in=w_in, dn_conv_w=dn_conv_w, dn_a_log=dn_a_log, dn_dt_bias=dn_dt_bias,
             dn_norm_g=dn_norm_g, sw_sinks=sw_sinks, w_up_dn=w_up_dn, w_up_sw=w_up_sw, w_o=w_o, post_mix_g=post_mix_g,
             pre_mlp_g=pre_mlp_g, w_ff1=w_ff1, w_ff2=w_ff2, post_mlp_g=post_mlp_g)
    m = dict(pre_mix_g=m_pre_mix_g, w_in=m_w_in, dn_conv_w=m_dn_conv_w, dn_a_log=m_dn_a_log, dn_dt_bias=m_dn_dt_bias,
             dn_norm_g=m_dn_norm_g, sw_sinks=m_sw_sinks, w_up_dn=m_w_up_dn, w_up_sw=m_w_up_sw, w_o=m_w_o,
             post_mix_g=m_post_mix_g, pre_mlp_g=m_pre_mlp_g, w_ff1=m_w_ff1, w_ff2=m_w_ff2, post_mlp_g=m_post_mlp_g)
    v = dict(pre_mix_g=v_pre_mix_g, w_in=v_w_in, dn_conv_w=v_dn_conv_w, dn_a_log=v_dn_a_log, dn_dt_bias=v_dn_dt_bias,
             dn_norm_g=v_dn_norm_g, sw_sinks=v_sw_sinks, w_up_dn=v_w_up_dn, w_up_sw=v_w_up_sw, w_o=v_w_o,
             post_mix_g=v_post_mix_g, pre_mlp_g=v_pre_mlp_g, w_ff1=v_w_ff1, w_ff2=v_w_ff2, post_mlp_g=v_post_mlp_g)
    n_layers = pre_mix_g.shape[0]
    xs, pos, tgt = x[0], positions[0], loss_target[0]

    nb = len(_BIG)
    cst = _consts(pos)

    def payload(l, names):
        return [w[n][l] if n == "dn_conv_w" else w[n][l].astype(bf16) for n in names]

    def full(names, gathered):
        return {n: _unshard(n, g[:, None])[0] for n, g in zip(names, gathered)}

    early = _exchange("allgather_first", payload(0, _EARLY), ["gather2"] * len(_EARLY))
    h, res, layers = xs, [], []
    for l in range(n_layers):
        wl = {**full(_EARLY, early), **{n: w[n][l] for n in _WEIGHTS if n not in _BIG}}
        send = payload(l, _LATE) + (payload(l + 1, _EARLY) if l + 1 < n_layers else [])
        h, r, p, got = _layer_fwd(h, _early_params(wl), cst, (send, ["gather2"] * len(send)),
                                  late=lambda got: _late_params(full(_LATE, got[:len(_LATE)])))
        early = got[len(_LATE):]
        res.append(r)
        layers.append(p)
    dy, sq = _loss(h, tgt)
    loss = lax.psum(0.5 / D_MODEL * jnp.sum(sq), ("x", "y", "c"))

    grads, landed, carry = [None] * n_layers, {n: [None] * n_layers for n in _BIG}, []
    for l in reversed(range(n_layers)):
        dy, g, arrived = _layer_bwd(dy, res[l], layers[l], cst, carry)
        for n, a in zip(_EARLY, arrived[:len(carry)]):
            landed[n][l + 1] = a
        for n, a in zip(_LATE, arrived[len(carry):]):
            landed[n][l] = a
        grads[l] = _layer_grads_ref_layout(g)
        carry = [_pieces(n, grads[l][n]) for n in _EARLY]
    grad_x = dy[None]
    small_grads = _pack_small({n: jnp.stack([grads[l][n] for l in range(n_layers)]) for n in _SMALL_ROWS + _SMALL_MISC})
    last = _exchange("exchange_last", carry + [small_grads], ["scatter"] * len(carry) + ["gather"])
    for n, a in zip(_EARLY, last):
        landed[n][0] = a

    out_g, out_d, out_m, out_v = {}, {}, {}, {}
    for n in _BIG:
        land = jnp.stack(landed[n], axis=1)
        out_g[n], out_d[n], out_m[n], out_v[n] = _adamw("adamw_" + n, land, w[n], m[n], v[n], tr=256)
    small = _adamw("adamw_small", last[-1][:, None], _pack_small(w)[None], _pack_small(m)[None], _pack_small(v)[None],
                   tr=_SMALL_PACK_ROWS)
    for dst, a in zip((out_g, out_d, out_m, out_v), small):
        dst.update(_unpack_small(a[0], w))
    return (loss, grad_x, *[out_g[n] for n in _WEIGHTS], *[out_d[n] for n in _WEIGHTS],
            *[out_m[n] for n in _WEIGHTS], *[out_v[n] for n in _WEIGHTS])
```

```python
import math

import numpy as np
import jax
import jax.numpy as jnp
from jax import lax
from jax.experimental import pallas as pl
from jax.experimental.pallas import tpu as pltpu

f32 = jnp.float32
bf16 = jnp.bfloat16
HIGHEST = lax.Precision.HIGHEST

N_DEV = 8
D_MODEL = 1024
DN_HEADS = 8
DN_DK = 128
DN_CHUNK = 128
DN_STEP = 2
DN_CONV = 4
SW_Q_HEADS = 16
SW_KV_HEADS = 2
SW_HEAD_DIM = 64
SW_BLOCK = 128
ROPE_THETA = 500000.0
ROT_DIM = SW_HEAD_DIM // 4
D_FF = 4 * D_MODEL
EPS = 1e-6
LANES = 128
CONV_HALO = 8
NEG_BIG = -1e30

ADAM_LR = 0.001
ADAM_B1 = 0.9
ADAM_B2 = 0.999
ADAM_EPS = 1e-08
ADAM_WD = 0.01
ADAM_STEP = 10

PROJ_W = 7680
CB_Q, CB_K, CB_V, CB_Z, CB_SWQ, CB_GA, CB_GB = 0, 1, 2, 3, 4, 5, 6
CB_SWK, CB_SWV, CB_BA = 56, 57, 58

NN = ((1,), (0,))
NT = ((1,), (1,))
TN = ((0,), (0,))


def _mm(a, b, dims, hi=False):
    if hi:
        return lax.dot_general(a.astype(f32), b.astype(f32), (dims, ((), ())), precision=HIGHEST,
                               preferred_element_type=f32)
    return lax.dot_general(a.astype(bf16), b.astype(bf16), (dims, ((), ())), preferred_element_type=f32)


def _matmul(name, a, b, form, out_dtype, tm=512, tn=512, tk=1024, extra=(), epilogue=None):
    if form == "nn":
        (M, K), (_, N) = a.shape, b.shape
    elif form == "nt":
        (M, K), (N, _) = a.shape, b.shape
    else:
        (K, M), (_, N) = a.shape, b.shape
    tm, tn, tk = min(tm, M), min(tn, N), min(tk, K)
    assert M % tm == 0 and N % tn == 0 and K % tk == 0, (name, M, N, K, tm, tn, tk)
    nk = K // tk
    dims = {"nn": NN, "nt": NT, "tn": TN}[form]
    out_dtypes = [out_dtype] if epilogue is None else list(out_dtype)
    ne, no = len(extra), len(out_dtypes)

    def body(a_ref, b_ref, *rest):
        e_refs, o_refs, acc_ref = rest[:ne], rest[ne:ne + no], rest[ne + no]

        def finish(acc):
            vals = [acc] if epilogue is None else epilogue(acc, [e[...] for e in e_refs])
            for o, val in zip(o_refs, vals):
                o[...] = val.astype(o.dtype)

        part = lax.dot_general(a_ref[...], b_ref[...], (dims, ((), ())), preferred_element_type=f32)
        if nk == 1:
            finish(part)
        else:
            k = pl.program_id(2)

            @pl.when(k == 0)
            def _():
                acc_ref[...] = part

            @pl.when(k > 0)
            def _():
                acc_ref[...] += part

            @pl.when(k == nk - 1)
            def _():
                finish(acc_ref[...])

    if form == "tn":
        a_spec = pl.BlockSpec((tk, tm), lambda i, j, k: (k, i))
    else:
        a_spec = pl.BlockSpec((tm, tk), lambda i, j, k: (i, k))
    if form == "nt":
        b_spec = pl.BlockSpec((tn, tk), lambda i, j, k: (j, k))
    else:
        b_spec = pl.BlockSpec((tk, tn), lambda i, j, k: (k, j))
    tile = pl.BlockSpec((tm, tn), lambda i, j, k: (i, j))
    res = pl.pallas_call(
        body, name=name,
        grid=(M // tm, N // tn, nk),
        in_specs=[a_spec, b_spec] + [tile] * ne,
        out_specs=[tile] * no,
        out_shape=[jax.ShapeDtypeStruct((M, N), dt) for dt in out_dtypes],
        scratch_shapes=[pltpu.VMEM((tm, tn) if nk > 1 else (8, 128), f32)],
        compiler_params=pltpu.CompilerParams(dimension_semantics=("parallel", "parallel", "arbitrary")),
    )(a, b, *extra)
    return res[0] if epilogue is None else list(res)


def _tile_specs(ins, halo_ids, params, TM, HR, row_of):
    specs = [pl.BlockSpec((TM, w), lambda i, cb=cb: (row_of(i), cb)) for (_, w, cb) in ins]
    for h in halo_ids:
        _, w, cb = ins[h]
        specs.append(pl.BlockSpec((HR, w), lambda i, cb=cb: (jnp.maximum(row_of(i) * (TM // HR) - 1, 0), cb)))
    for p in params:
        specs.append(pl.BlockSpec(p.shape, lambda i, nd=p.ndim: (0,) * nd))
    return specs


def _tile_fwd(name, fn, T, TM, ins, params, outs, halo_ids=(), HR=CONV_HALO):
    TM = min(TM, T)
    n = T // TM
    ni, nh, npar = len(ins), len(halo_ids), len(params)

    def body(*refs):
        in_v = [r[...] for r in refs[:ni]]
        halo_v = [r[...] for r in refs[ni:ni + nh]]
        par_v = [r[...] for r in refs[ni + nh:ni + nh + npar]]
        o_refs = refs[ni + nh + npar:]
        first = pl.program_id(0) == 0
        vals = fn(first, in_v, halo_v, par_v)
        for o, val in zip(o_refs, vals):
            o[...] = val.astype(o.dtype)

    res = pl.pallas_call(
        body, name=name, grid=(n,),
        in_specs=_tile_specs(ins, halo_ids, params, TM, HR, lambda i: i),
        out_specs=[pl.BlockSpec((TM, w), lambda i: (i, 0)) for (w, _) in outs],
        out_shape=[jax.ShapeDtypeStruct((T, w), dt) for (w, dt) in outs],
        compiler_params=pltpu.CompilerParams(dimension_semantics=("arbitrary",)),
    )(*[a for (a, _, _) in ins], *[ins[h][0] for h in halo_ids], *params)
    return list(res)


def _tile_bwd(name, fn, T, TM, ins, params, cts, din, dpar, halo_ids=(), HR=CONV_HALO):
    TM = min(TM, T)
    n = T // TM
    ni, nh, npar, nc = len(ins), len(halo_ids), len(params), len(cts)
    din_ids = [j for (j, _) in din]
    dh_ids = [h for h in halo_ids if h in din_ids]
    nd, ndp, ndh = len(din), len(dpar), len(dh_ids)

    def body(*refs):
        in_v = [r[...] for r in refs[:ni]]
        halo_v = [r[...] for r in refs[ni:ni + nh]]
        par_v = [r[...] for r in refs[ni + nh:ni + nh + npar]]
        ct_v = [r[...].astype(f32) for r in refs[ni + nh + npar:ni + nh + npar + nc]]
        o_refs = refs[ni + nh + npar + nc:ni + nh + npar + nc + nd + ndp]
        carry_refs = refs[ni + nh + npar + nc + nd + ndp:]
        i = pl.program_id(0)
        first = i == n - 1

        def g(d_in, d_halo, d_par):
            full_in = list(in_v)
            for j, val in zip(din_ids, d_in):
                full_in[j] = val
            full_halo = list(halo_v)
            for h, val in zip(dh_ids, d_halo):
                full_halo[list(halo_ids).index(h)] = val
            full_par = list(par_v)
            for j, val in zip(dpar, d_par):
                full_par[j] = val
            return tuple(fn(first, full_in, full_halo, full_par))

        prim = ([in_v[j].astype(f32) for j in din_ids],
                [halo_v[list(halo_ids).index(h)].astype(f32) for h in dh_ids],
                [par_v[j] for j in dpar])
        _, vjp = jax.vjp(g, *prim)
        g_in, g_halo, g_par = vjp(tuple(ct_v))

        @pl.when(i == 0)
        def _():
            for c in carry_refs:
                c[...] = jnp.zeros_like(c)
            for o in o_refs[nd:]:
                o[...] = jnp.zeros_like(o)

        for slot, (j, _) in enumerate(din):
            val = g_in[slot]
            if j in dh_ids:
                c = carry_refs[dh_ids.index(j)]
                val = jnp.concatenate([val[:TM - HR], val[TM - HR:] + c[...]], axis=0) if TM > HR else val + c[...]
                c[...] = g_halo[dh_ids.index(j)]
            o_refs[slot][...] = val.astype(o_refs[slot].dtype)
        for slot in range(ndp):
            o_refs[nd + slot][...] += g_par[slot]

    rev = lambda i: n - 1 - i
    in_specs = _tile_specs(ins, halo_ids, params, TM, HR, rev)
    ct_specs = [pl.BlockSpec((TM, w), lambda i, cb=cb: (rev(i), cb)) for (_, w, cb) in cts]
    out_specs = [pl.BlockSpec((TM, ins[j][1]), lambda i: (rev(i), 0)) for j in din_ids]
    out_specs += [pl.BlockSpec(params[j].shape, lambda i, nd_=params[j].ndim: (0,) * nd_) for j in dpar]
    out_shape = [jax.ShapeDtypeStruct((T, ins[j][1]), dt) for (j, dt) in din]
    out_shape += [jax.ShapeDtypeStruct(params[j].shape, f32) for j in dpar]
    res = pl.pallas_call(
        body, name=name, grid=(n,),
        in_specs=in_specs + ct_specs,
        out_specs=out_specs,
        out_shape=out_shape,
        scratch_shapes=[pltpu.VMEM((HR, ins[h][1]), f32) for h in dh_ids],
        compiler_params=pltpu.CompilerParams(dimension_semantics=("arbitrary",)),
    )(*[a for (a, _, _) in ins], *[ins[h][0] for h in halo_ids], *params, *[a for (a, _, _) in cts])
    return list(res)


def _rms(x, g):
    return x * lax.rsqrt(jnp.mean(x * x, axis=-1, keepdims=True) + EPS) * g


def _fn_prenorm(first, ins, halos, params):
    (x,), (g,) = ins, params
    x = x.astype(f32)
    return [_rms(x, g), x]


def _fn_postmix(first, ins, halos, params):
    (x, mix), (g2, g3) = ins, params
    x1 = x + _rms(mix, g2)
    return [x1, _rms(x1, g3)]


def _fn_postmlp(first, ins, halos, params):
    (x1, ff), (g4,) = ins, params
    return [x1 + _rms(ff, g4)]


def _fn_rms_only(first, ins, halos, params):
    (ff,), (g4,) = ins, params
    return [_rms(ff, g4)]


def _fn_merge(first, ins, halos, params):
    ga, gb, ya, yb = ins
    return [jax.nn.sigmoid(ga) * ya + jax.nn.sigmoid(gb) * yb]


def _roll_rows(x, shift):
    return pltpu.roll(x, shift, 0)


_row_roll = jax.custom_vjp(_roll_rows, nondiff_argnums=(1,))
_row_roll.defvjp(lambda x, shift: (_roll_rows(x, shift), None),
                 lambda shift, _, ct: (_roll_rows(ct, ct.shape[0] - shift),))


def _make_fn_conv(norm_scale):
    def fn(first, ins, halos, params):
        (x,), (xp,), (w,) = ins, halos, params
        xp = jnp.where(first, 0.0, xp)
        outs = []
        for h in range(DN_HEADS):
            sl = slice(DN_DK * h, DN_DK * (h + 1))
            xe, wh = jnp.concatenate([xp[:, sl], x[:, sl]], axis=0), w[:, sl]
            y = xe[CONV_HALO:] * wh[DN_CONV - 1:DN_CONV]
            for j in range(DN_CONV - 1):
                y = y + _row_roll(xe, DN_CONV - 1 - j)[CONV_HALO:] * wh[j:j + 1]
            y = jax.nn.silu(y)
            if norm_scale is not None:
                y = y * lax.rsqrt(jnp.sum(y * y, axis=-1, keepdims=True) + EPS) * norm_scale
            outs.append(y)
        return [jnp.concatenate(outs, axis=-1)]
    return fn


def _fn_gates(first, ins, halos, params):
    (ba,), (avec, dvec) = ins, params
    lane = lax.broadcasted_iota(jnp.int32, ba.shape, 1)
    beta = jax.nn.sigmoid(ba)
    g = -jnp.exp(avec) * jax.nn.softplus(ba + dvec)
    return [jnp.where(lane < DN_HEADS, beta, jnp.where(lane < 2 * DN_HEADS, g, 0.0))]


def _fn_dnpost(first, ins, halos, params):
    (o, z), (ng,) = ins, params
    outs = []
    for h in range(DN_HEADS):
        sl = slice(DN_DK * h, DN_DK * (h + 1))
        outs.append(_rms(o[:, sl], ng) * jax.nn.silu(z[:, sl]))
    return [jnp.concatenate(outs, axis=-1)]


def _roll_lanes(x, shift):
    return pltpu.roll(x, shift, 1)


_lane_roll = jax.custom_vjp(_roll_lanes, nondiff_argnums=(1,))
_lane_roll.defvjp(lambda x, shift: (_roll_lanes(x, shift), None),
                  lambda shift, _, ct: (_roll_lanes(ct, LANES - shift),))


def _kcopy(g):
    return (g // (SW_Q_HEADS // SW_KV_HEADS)) * 2 + g % 2


def _attn_probs(qp, kx, sink, mask):
    heads = range(len(sink))
    s = [jnp.where(mask, _mm(qp[g // 2], kx[_kcopy(g)], NT), NEG_BIG) for g in heads]
    m = [jnp.maximum(jnp.max(s[g], axis=-1, keepdims=True), sink[g]) for g in heads]
    p = [jnp.exp(s[g] - m[g]) for g in heads]
    ps = [jnp.exp(sink[g] - m[g]) for g in heads]
    inv = [1.0 / (jnp.sum(p[g], axis=-1, keepdims=True) + ps[g]) for g in heads]
    return [p[g] * inv[g] for g in heads], [ps[g] * inv[g] for g in heads]


@jax.custom_vjp
def _attn_group(qp, kx, vx, sink, mask):
    probs, _ = _attn_probs(qp, kx, sink, mask)
    return tuple(_mm(probs[g], vx[_kcopy(g)], NN) for g in range(len(sink)))


def _attn_group_fwd(qp, kx, vx, sink, mask):
    o = _attn_group(qp, kx, vx, sink, mask)
    return o, (qp, kx, vx, sink, mask, o)


def _attn_group_bwd(res, do):
    qp, kx, vx, sink, mask, o = res
    heads = range(len(sink))
    probs, p_sink = _attn_probs(qp, kx, sink, mask)
    d_probs = [_mm(do[g], vx[_kcopy(g)], NT) for g in heads]
    dot = [jnp.sum(do[g] * o[g], axis=-1, keepdims=True) for g in heads]
    ds = [probs[g] * (d_probs[g] - dot[g]) for g in heads]
    d_qp = tuple(_mm(ds[2 * j], kx[_kcopy(2 * j)], NN) + _mm(ds[2 * j + 1], kx[_kcopy(2 * j + 1)], NN)
                 for j in range(len(qp)))
    rows = lambda xs: jnp.concatenate(xs, axis=0)
    d_kx = tuple(_mm(rows([ds[g] for g in heads if _kcopy(g) == c]), rows([qp[g // 2] for g in heads if _kcopy(g) == c]), TN)
                 for c in range(len(kx)))
    d_vx = tuple(_mm(rows([probs[g] for g in heads if _kcopy(g) == c]), rows([do[g] for g in heads if _kcopy(g) == c]), TN)
                 for c in range(len(kx)))
    d_sink = tuple(-jnp.sum(p_sink[g] * dot[g], axis=0, keepdims=True) for g in heads)
    return d_qp, d_kx, d_vx, d_sink, None


_attn_group.defvjp(_attn_group_fwd, _attn_group_bwd)


def _fn_swa(first, ins, halos, params):
    q, k, v, cos, sin = ins
    kp, vp, cosp, sinp = halos
    sinks, sel_a0, sel_b0, sel_a1, sel_b1 = params
    B = q.shape[0]
    half = ROT_DIM // 2
    in_head = jnp.bitwise_and(lax.broadcasted_iota(jnp.int32, (1, LANES), 1), SW_HEAD_DIM - 1)

    def rope(x, c, s):
        return (x * c + _lane_roll(x, LANES - half) * jnp.where(in_head < half, -s, 0.0)
                + _lane_roll(x, half) * jnp.where(in_head >= half, s, 0.0))

    kcat = jnp.concatenate([rope(kp, cosp, sinp), rope(k, cos, sin)], axis=0)
    vcat = jnp.concatenate([vp, v], axis=0)
    r = lax.broadcasted_iota(jnp.int32, (B, 2 * B), 0)
    c = lax.broadcasted_iota(jnp.int32, (B, 2 * B), 1)
    mask = (c > r) & (c <= r + B) & ((c >= B) | jnp.logical_not(first))
    sels = (sel_a0, sel_b0, sel_a1, sel_b1)
    kx = tuple(_mm(kcat, s, NN) for s in sels)
    vx = tuple(_mm(vcat, s, NN) for s in sels)
    qp = tuple(rope(q[:, LANES * j:LANES * (j + 1)], cos, sin) * (SW_HEAD_DIM ** -0.5) for j in range(SW_Q_HEADS // 2))
    sink = tuple(sinks[:, g:g + 1] for g in range(SW_Q_HEADS))
    o = _attn_group(qp, kx, vx, sink, mask)
    return [jnp.concatenate([o[2 * j] + o[2 * j + 1] for j in range(SW_Q_HEADS // 2)], axis=-1)]


@jax.custom_vjp
def _inv_unit_lower(Ls):
    C = Ls[0].shape[0]
    ii = lax.broadcasted_iota(jnp.int32, (C, C), 0)
    jj = lax.broadcasted_iota(jnp.int32, (C, C), 1)

    def off_mask(level):
        same_pair = jnp.right_shift(ii, level + 1) == jnp.right_shift(jj, level + 1)
        lower_left = (jnp.bitwise_and(jnp.right_shift(ii, level), 1) == 1) & (jnp.bitwise_and(jnp.right_shift(jj, level), 1) == 0)
        return same_pair & lower_left

    eye = (ii == jj).astype(f32)
    m0 = off_mask(0)
    Ts = [eye - jnp.where(m0, L, 0.0) for L in Ls]
    for level in range(1, int(math.log2(C))):
        mk = off_mask(level)
        left = [_mm(T_, jnp.where(mk, L, 0.0), NN) for T_, L in zip(Ts, Ls)]
        Ts = [T_ - _mm(a, T_, NN) for a, T_ in zip(left, Ts)]
    return tuple(Ts)


def _inv_fwd(Ls):
    Ts = _inv_unit_lower(Ls)
    return Ts, Ts


def _inv_bwd(Ts, dTs):
    left = [_mm(T_, dT, TN) for T_, dT in zip(Ts, dTs)]
    return (tuple(-_mm(a, T_, NT) for a, T_ in zip(left, Ts)),)


_inv_unit_lower.defvjp(_inv_fwd, _inv_bwd)


@jax.custom_vjp
def _inv_known(Ls, Ts):
    return Ts


_inv_known.defvjp(lambda Ls, Ts: (Ts, Ts),
                  lambda Ts, dTs: (_inv_bwd(Ts, dTs)[0], tuple(jnp.zeros_like(t) for t in Ts)))


def _mm_01(a, b, dims):
    hi = b.astype(bf16)
    r1 = b - hi.astype(f32)
    mid = r1.astype(bf16)
    lo = (r1 - mid.astype(f32)).astype(bf16)
    a16 = a.astype(bf16)
    dot = lambda part: lax.dot_general(a16, part, (dims, ((), ())), preferred_element_type=f32)
    return dot(hi) + dot(mid) + dot(lo)


def _eye(n):
    return lax.broadcasted_iota(jnp.int32, (n, n), 0) == lax.broadcasted_iota(jnp.int32, (n, n), 1)


def _lower(n):
    return lax.broadcasted_iota(jnp.int32, (n, n), 0) >= lax.broadcasted_iota(jnp.int32, (n, n), 1)


@jax.custom_vjp
def _transpose(x):
    return _mm_01(_eye(x.shape[1]), x, NT)


_transpose.defvjp(lambda x: (_transpose(x), None), lambda _, ct: (_transpose(ct),))


@jax.custom_vjp
def _cumsum_rows(x):
    return _mm_01(_lower(x.shape[0]), x, NN)


_cumsum_rows.defvjp(lambda x: (_cumsum_rows(x), None), lambda _, ct: (_mm_01(_lower(ct.shape[0]), ct, TN),))


def _dn_chunk(q, k, v, gb, S, tinv_known=None):
    C = min(DN_CHUNK, q.shape[0])
    n_ch = q.shape[0] // C
    units = [(c, h) for c in range(n_ch) for h in range(DN_HEADS)]
    per_unit = lambda xs: dict(zip(units, xs))
    ii = lax.broadcasted_iota(jnp.int32, (C, C), 0)
    jj = lax.broadcasted_iota(jnp.int32, (C, C), 1)
    causal, strict = ii >= jj, ii > jj
    gbc = [gb[C * c:C * (c + 1)] for c in range(n_ch)]
    gc_all = [_cumsum_rows(g) for g in gbc]
    gc_t = [_transpose(g) for g in gc_all]
    part = lambda x, c, h: x[C * c:C * (c + 1), DN_DK * h:DN_DK * (h + 1)]
    qs, ks, vs = (per_unit([part(x, c, h) for c, h in units]) for x in (q, k, v))
    beta = per_unit([gbc[c][:, h:h + 1] for c, h in units])
    gcol = per_unit([gc_all[c][:, DN_HEADS + h:DN_HEADS + h + 1] for c, h in units])
    grow = per_unit([gc_t[c][DN_HEADS + h:DN_HEADS + h + 1, :] for c, h in units])
    decay = per_unit([jnp.where(causal, jnp.exp(jnp.where(causal, gcol[u] - grow[u], 0.0)), 0.0) for u in units])
    kb = per_unit([ks[u] * beta[u] for u in units])
    kk = per_unit([_mm(kb[u], ks[u], NT) for u in units])
    qk = per_unit([_mm(qs[u], ks[u], NT) for u in units])
    Ls = tuple(jnp.where(strict, kk[u] * decay[u], 0.0) for u in units)
    tinv = _inv_unit_lower(Ls) if tinv_known is None else _inv_known(Ls, tuple(tinv_known))
    ti = per_unit(tinv)
    eg = per_unit([jnp.exp(gcol[u]) for u in units])
    un = per_unit([_mm(ti[u], vs[u] * beta[u], NN) for u in units])
    w = per_unit([_mm(ti[u], kb[u] * eg[u], NN) for u in units])
    gl = per_unit([gcol[u][C - 1:C, :] for u in units])
    a_intra = per_unit([qk[u] * decay[u] for u in units])
    q_dec = per_unit([qs[u] * eg[u] for u in units])
    k_dec = per_unit([ks[u] * jnp.exp(gl[u] - gcol[u]) for u in units])
    H, state, outs = range(DN_HEADS), list(S), []
    for c in range(n_ch):
        ws = [_mm(w[c, h], state[h], NN) for h in H]
        qS = [_mm(q_dec[c, h], state[h], NN) for h in H]
        v_new = [un[c, h] - ws[h] for h in H]
        av = [_mm(a_intra[c, h], v_new[h], NN) for h in H]
        kv = [_mm(k_dec[c, h], v_new[h], TN) for h in H]
        outs.append(jnp.concatenate([qS[h] + av[h] for h in H], axis=-1))
        state = [state[h] * jnp.exp(gl[c, h]) + kv[h] for h in H]
    return (outs[0] if n_ch == 1 else jnp.concatenate(outs, axis=0)), tuple(state), tuple(tinv)


_DN_W = DN_HEADS * DN_DK


def _dn_inputs(first, xq, xk, xv, hq, hk, hv, ba, cw, avec, dvec):
    conv = lambda scale, x, h, j: _make_fn_conv(scale)(first, [x], [h], [cw[:, _DN_W * j:_DN_W * (j + 1)]])[0]
    return (conv(DN_DK ** -0.5, xq, hq, 0), conv(1.0, xk, hk, 1), conv(None, xv, hv, 2),
            _fn_gates(first, [ba], [], [avec, dvec])[0])


def _delta_specs(C, row_of):
    cols = (CB_Q, CB_K, CB_V)
    specs = [pl.BlockSpec((C, _DN_W), lambda i, cb=cb: (row_of(i), cb)) for cb in cols]
    specs += [pl.BlockSpec((CONV_HALO, _DN_W), lambda i, cb=cb: (jnp.maximum(row_of(i) * (C // CONV_HALO) - 1, 0), cb))
              for cb in cols]
    specs.append(pl.BlockSpec((C, LANES), lambda i: (row_of(i), CB_BA)))
    return specs


def _whole(a):
    return pl.BlockSpec(a.shape, lambda i, nd=a.ndim: (0,) * nd)


def _carrying(body, n_in, n_out, n_scratch, n_steps, ex):
    if ex is None:
        return body

    def wrapped(*refs):
        ins, rest = refs[:n_in], refs[n_in:]
        ex_in, rest = rest[:ex.n], rest[ex.n:]
        outs, rest = rest[:n_out], rest[n_out:]
        ex_out, rest = rest[:ex.n], rest[ex.n:]
        scratch, sems = rest[:n_scratch], rest[n_scratch:]
        start, finish = ex.bind(ex_in, ex_out, *sems)
        step = pl.program_id(0)
        pl.when(step == 0)(start)
        body(*ins, *outs, *scratch)
        pl.when(step == n_steps - 1)(finish)

    return wrapped


def _carried_call(name, body, grid, in_specs, out_specs, out_shape, scratch, operands, ex, ex_arrs):
    n_out = len(out_shape)
    if ex is not None:
        in_specs, out_specs = in_specs + ex.specs, out_specs + ex.specs
        out_shape, scratch, operands = out_shape + ex.out_shape, scratch + ex.scratch, tuple(operands) + tuple(ex_arrs)
    res = pl.pallas_call(
        _carrying(body, len(in_specs) - (ex.n if ex else 0), n_out, len(scratch) - (3 if ex else 0), grid[0], ex),
        name=name, grid=grid, in_specs=in_specs, out_specs=out_specs, out_shape=out_shape, scratch_shapes=scratch,
        compiler_params=pltpu.CompilerParams(dimension_semantics=("arbitrary",)),
    )(*operands)
    return list(res[:n_out]), list(res[n_out:])


def _delta_steps(T):
    rows = min(DN_STEP * DN_CHUNK, T)
    return rows, T // rows, (rows // min(DN_CHUNK, T)) * DN_HEADS


def _delta_fwd(proj, cw, avec, dvec, comm=None):
    T = proj.shape[0]
    C, n, n_units = _delta_steps(T)
    ex = None if comm is None else _Exchange(*comm)

    def body(xq, xk, xv, hq, hk, hv, ba, cw_ref, a_ref, d_ref, o_ref, hist_ref, tinv_ref,
             q_ref, k_ref, v_ref, gb_ref, s_ref):
        first = pl.program_id(0) == 0

        @pl.when(first)
        def _():
            s_ref[...] = jnp.zeros_like(s_ref)

        S = tuple(s_ref[h] for h in range(DN_HEADS))
        for h in range(DN_HEADS):
            hist_ref[0, h] = S[h]
        q, k, v, gb = _dn_inputs(first, xq[...], xk[...], xv[...], hq[...], hk[...], hv[...], ba[...],
                                 cw_ref[...], a_ref[...], d_ref[...])
        q_ref[...], k_ref[...], v_ref[...], gb_ref[...] = q, k, v, gb
        o, s_new, tinv = _dn_chunk(q, k, v, gb, S)
        o_ref[...] = o
        for h in range(DN_HEADS):
            s_ref[h] = s_new[h]
        for u in range(n_units):
            tinv_ref[0, u] = tinv[u]

    row = pl.BlockSpec((C, _DN_W), lambda i: (i, 0))
    c_inv = min(DN_CHUNK, T)
    return _carried_call(
        "delta_fwd", body, (n,),
        _delta_specs(C, lambda i: i) + [_whole(cw), _whole(avec), _whole(dvec)],
        [row, pl.BlockSpec((1, DN_HEADS, DN_DK, DN_DK), lambda i: (i, 0, 0, 0)),
         pl.BlockSpec((1, n_units, c_inv, c_inv), lambda i: (i, 0, 0, 0)),
         row, row, row, pl.BlockSpec((C, LANES), lambda i: (i, 0))],
        [jax.ShapeDtypeStruct((T, _DN_W), f32), jax.ShapeDtypeStruct((n, DN_HEADS, DN_DK, DN_DK), f32),
         jax.ShapeDtypeStruct((n, n_units, c_inv, c_inv), f32)]
        + [jax.ShapeDtypeStruct((T, _DN_W), f32)] * 3 + [jax.ShapeDtypeStruct((T, LANES), f32)],
        [pltpu.VMEM((DN_HEADS, DN_DK, DN_DK), f32)],
        (proj, proj, proj, proj, proj, proj, proj, cw, avec, dvec), ex, comm[0] if comm else ())


def _delta_bwd(qn, kn, vv, gb, hist, tinv, do, comm=None):
    T = qn.shape[0]
    C, n, n_units = _delta_steps(T)
    c_inv = min(DN_CHUNK, T)
    ex = None if comm is None else _Exchange(*comm)

    def body(q_ref, k_ref, v_ref, gb_ref, hist_ref, tinv_ref, do_ref, dq_ref, dk_ref, dv_ref, dgb_ref, ds_ref):
        @pl.when(pl.program_id(0) == 0)
        def _():
            ds_ref[...] = jnp.zeros_like(ds_ref)

        S = tuple(hist_ref[0, h] for h in range(DN_HEADS))
        known = tuple(tinv_ref[0, u] for u in range(n_units))
        chunk = lambda q, k, v, g, s: _dn_chunk(q, k, v, g, s, tinv_known=known)[:2]
        _, vjp = jax.vjp(chunk, q_ref[...], k_ref[...], v_ref[...], gb_ref[...], S)
        dS = tuple(ds_ref[h] for h in range(DN_HEADS))
        dq, dk, dv, dgb, dS_in = vjp((do_ref[...], dS))
        dq_ref[...] = dq
        dk_ref[...] = dk
        dv_ref[...] = dv
        dgb_ref[...] = dgb
        for h in range(DN_HEADS):
            ds_ref[h] = dS_in[h]

    row = pl.BlockSpec((C, _DN_W), lambda i: (n - 1 - i, 0))
    small = pl.BlockSpec((C, LANES), lambda i: (n - 1 - i, 0))
    return _carried_call(
        "delta_bwd", body, (n,),
        [row, row, row, small, pl.BlockSpec((1, DN_HEADS, DN_DK, DN_DK), lambda i: (n - 1 - i, 0, 0, 0)),
         pl.BlockSpec((1, n_units, c_inv, c_inv), lambda i: (n - 1 - i, 0, 0, 0)), row],
        [row, row, row, small],
        [jax.ShapeDtypeStruct((T, _DN_W), f32)] * 3 + [jax.ShapeDtypeStruct((T, LANES), f32)],
        [pltpu.VMEM((DN_HEADS, DN_DK, DN_DK), f32)],
        (qn, kn, vv, gb, hist, tinv, do), ex, comm[0] if comm else ())


TM_ROW = 512
TM_CONV = 256
W1 = D_MODEL


def _first_only(fn):
    return lambda *a: fn(*a)[:1]


def _swa_args(proj, cst):
    ins = [(proj, W1, CB_SWQ), (proj, LANES, CB_SWK), (proj, LANES, CB_SWV), (cst["cos"], LANES, 0), (cst["sin"], LANES, 0)]
    return ins, (1, 2, 3, 4)


def _layer_fwd(x, p, cst, comm=None, late=None):
    T = x.shape[0]
    r = {"x": x}
    (h,) = _tile_fwd("prenorm", _first_only(_fn_prenorm), T, TM_ROW, [(x, W1, 0)], [p["g1"]], [(W1, bf16)])
    proj = _matmul("proj", h, p["w_in"], "nn", f32, tm=2048, tn=1536)
    (o, hist, tinv, qn, kn, vv, gbt), comm_out = _delta_fwd(proj, p["conv_w"], p["avec"], p["dvec"], comm)
    if late is not None:
        p = {**p, **late(comm_out)}
    (dn_out,) = _tile_fwd("dnpost", _fn_dnpost, T, TM_ROW, [(o, W1, 0), (proj, W1, CB_Z)], [p["ng"]], [(W1, bf16)])
    sw_ins, sw_halo = _swa_args(proj, cst)
    sw_par = [p["sinks"], cst["sel_a0"], cst["sel_b0"], cst["sel_a1"], cst["sel_b1"]]
    (sw_out,) = _tile_fwd("swa", _fn_swa, T, SW_BLOCK, sw_ins, sw_par, [(W1, bf16)], halo_ids=sw_halo, HR=SW_BLOCK)
    y_a = _matmul("up_dn", dn_out, p["w_up_dn"], "nn", f32, tm=2048, tn=1024)
    y_b = _matmul("up_sw", sw_out, p["w_up_sw"], "nn", f32, tm=2048, tn=1024)
    (gated,) = _tile_fwd("merge", _fn_merge, T, TM_ROW,
                         [(proj, W1, CB_GA), (proj, W1, CB_GB), (y_a, W1, 0), (y_b, W1, 0)], [], [(W1, bf16)])
    mix = _matmul("w_o", gated, p["w_o"], "nn", f32, tm=2048, tn=1024)
    x1, h2 = _tile_fwd("postmix", _fn_postmix, T, TM_ROW, [(x, W1, 0), (mix, W1, 0)], [p["g2"], p["g3"]],
                       [(W1, f32), (W1, bf16)])
    ffh, act = _matmul("ff1", h2, p["w_ff1"], "nn", [f32, bf16], tm=2048, tn=1024,
                       epilogue=lambda acc, ex: [acc, jnp.square(jnp.maximum(acc, 0.0))])
    ff = _matmul("ff2", act, p["w_ff2"], "nn", f32, tm=2048, tn=1024)
    (x2,) = _tile_fwd("postmlp", _fn_postmlp, T, TM_ROW, [(x1, W1, 0), (ff, W1, 0)], [p["g4"]], [(W1, f32)])
    r.update(h=h, proj=proj, qn=qn, kn=kn, vv=vv, gbt=gbt, o=o, hist=hist, tinv=tinv, dn_out=dn_out, sw_out=sw_out,
             y_a=y_a, y_b=y_b, gated=gated, mix=mix, h2=h2, ffh=ffh, act=act, ff=ff)
    return x2, r, p, comm_out


def _pieces(n, full):
    return _shard_major(n, full[None])[:, 0].astype(bf16)


def _layer_bwd(dx2, r, p, cst, carry=None):
    T = dx2.shape[0]
    x, proj = r["x"], r["proj"]
    g = {}
    dff, g["g4"] = _tile_bwd("postmlp_b", _fn_rms_only, T, TM_ROW, [(r["ff"], W1, 0)], [p["g4"]], [(dx2, W1, 0)],
                             [(0, bf16)], [0])
    (dffh,) = _matmul("ff2_dx", dff, p["w_ff2"], "nt", [bf16], tm=2048, tn=1024, extra=[r["ffh"]],
                      epilogue=lambda acc, ex: [acc * (2.0 * jnp.maximum(ex[0], 0.0))])
    g["w_ff2"] = _matmul("ff2_dw", r["act"], dff, "tn", bf16, tm=1024, tn=1024, tk=2048)
    dh2 = _matmul("ff1_dx", dffh, p["w_ff1"], "nt", f32, tm=2048, tn=1024)
    g["w_ff1"] = _matmul("ff1_dw", r["h2"], dffh, "tn", bf16, tm=1024, tn=1024, tk=2048)
    dx1, dmix, g["g2"], g["g3"] = _tile_bwd("postmix_b", _fn_postmix, T, TM_ROW, [(x, W1, 0), (r["mix"], W1, 0)],
                                            [p["g2"], p["g3"]], [(dx2, W1, 0), (dh2, W1, 0)], [(0, f32), (1, bf16)], [0, 1])
    dgated = _matmul("w_o_dx", dmix, p["w_o"], "nt", f32, tm=2048, tn=1024)
    g["w_o"] = _matmul("w_o_dw", r["gated"], dmix, "tn", bf16, tm=1024, tn=1024, tk=2048)
    dga, dgb, dya, dyb = _tile_bwd("merge_b", _fn_merge, T, TM_ROW,
                                   [(proj, W1, CB_GA), (proj, W1, CB_GB), (r["y_a"], W1, 0), (r["y_b"], W1, 0)], [],
                                   [(dgated, W1, 0)], [(0, bf16), (1, bf16), (2, bf16), (3, bf16)], [])
    d_dn = _matmul("up_dn_dx", dya, p["w_up_dn"], "nt", f32, tm=2048, tn=1024)
    g["w_up_dn"] = _matmul("up_dn_dw", r["dn_out"], dya, "tn", bf16, tm=1024, tn=1024, tk=2048)
    d_sw = _matmul("up_sw_dx", dyb, p["w_up_sw"], "nt", f32, tm=2048, tn=1024)
    g["w_up_sw"] = _matmul("up_sw_dw", r["sw_out"], dyb, "tn", bf16, tm=1024, tn=1024, tk=2048)
    do, dz, g["ng"] = _tile_bwd("dnpost_b", _fn_dnpost, T, TM_ROW, [(r["o"], W1, 0), (proj, W1, CB_Z)], [p["ng"]],
                                [(d_dn, W1, 0)], [(0, f32), (1, bf16)], [0])
    comm = None
    if carry is not None:
        send = list(carry) + [_pieces(n, g[n]) for n in _LATE]
        comm = (send, ["scatter"] * len(send))
    (dqn, dkn, dvv, dgbt), comm_out = _delta_bwd(r["qn"], r["kn"], r["vv"], r["gbt"], r["hist"], r["tinv"], do, comm)
    conv_b = lambda nm, cb, scale, ct: _tile_bwd(nm, _make_fn_conv(scale), T, TM_CONV, [(proj, W1, cb)],
                                                 [p["conv_w"][:, W1 * cb:W1 * (cb + 1)]], [(ct, W1, 0)], [(0, bf16)], [0],
                                                 halo_ids=(0,))
    dq_in, dcw_q = conv_b("conv_q_b", CB_Q, DN_DK ** -0.5, dqn)
    dk_in, dcw_k = conv_b("conv_k_b", CB_K, 1.0, dkn)
    dv_in, dcw_v = conv_b("conv_v_b", CB_V, None, dvv)
    g["conv_w"] = jnp.concatenate([dcw_q, dcw_k, dcw_v], axis=-1)
    dba, g["avec"], g["dvec"] = _tile_bwd("gates_b", _fn_gates, T, TM_ROW, [(proj, LANES, CB_BA)], [p["avec"], p["dvec"]],
                                          [(dgbt, LANES, 0)], [(0, bf16)], [0, 1])
    sw_ins, sw_halo = _swa_args(proj, cst)
    sw_par = [p["sinks"], cst["sel_a0"], cst["sel_b0"], cst["sel_a1"], cst["sel_b1"]]
    dswq, dswk, dswv, g["sinks"] = _tile_bwd("swa_b", _fn_swa, T, SW_BLOCK, sw_ins, sw_par, [(d_sw, W1, 0)],
                                             [(0, bf16), (1, bf16), (2, bf16)], [0], halo_ids=sw_halo, HR=SW_BLOCK)
    dproj = jnp.concatenate([dq_in, dk_in, dv_in, dz, dswq, dga, dgb, dswk, dswv, dba, jnp.zeros((T, LANES), bf16)], axis=-1)
    dh = _matmul("proj_dx", dproj, p["w_in"], "nt", f32, tm=2048, tn=1024, tk=1536)
    g["w_in"] = _matmul("proj_dw", r["h"], dproj, "tn", bf16, tm=1024, tn=1536, tk=2048)
    dx, g["g1"] = _tile_bwd("prenorm_b", _fn_prenorm, T, TM_ROW, [(x, W1, 0)], [p["g1"]], [(dh, W1, 0), (dx1, W1, 0)],
                            [(0, f32)], [0])
    return dx, g, comm_out


_OFF_BA, _OFF_SWQ, _OFF_SWK, _OFF_GA, _D_IN = 4096, 4112, 5136, 5392, 7440


def _proj_cols(w):
    pad = lambda n: jnp.zeros(w.shape[:-1] + (n,), w.dtype)
    return jnp.concatenate([w[..., :_OFF_BA], w[..., _OFF_SWQ:_OFF_SWK], w[..., _OFF_GA:_D_IN],
                            w[..., _OFF_SWK:_OFF_GA], w[..., _OFF_BA:_OFF_SWQ], pad(PROJ_W - _D_IN)], axis=-1)


def _proj_cols_inv(w):
    n_ba = _OFF_SWQ - _OFF_BA
    return jnp.concatenate([w[..., :4096], w[..., 7424:7424 + n_ba], w[..., 4096:5120], w[..., 7168:7424],
                            w[..., 5120:7168]], axis=-1)


def _lane_pad(v, at):
    return jnp.pad(v.astype(f32), (at, LANES - at - v.shape[0])).reshape(1, LANES)


_EARLY = ("w_in", "dn_conv_w")
_LATE = ("w_up_dn", "w_up_sw", "w_o", "w_ff1", "w_ff2")


def _early_params(w):
    row = lambda v: v.reshape(1, -1).astype(f32)
    return dict(
        g1=row(w["pre_mix_g"]), g2=row(w["post_mix_g"]), g3=row(w["pre_mlp_g"]), g4=row(w["post_mlp_g"]),
        w_in=_proj_cols(w["w_in"]).astype(bf16), conv_w=w["dn_conv_w"].astype(f32),
        avec=_lane_pad(w["dn_a_log"], DN_HEADS), dvec=_lane_pad(w["dn_dt_bias"], DN_HEADS),
        ng=row(w["dn_norm_g"]), sinks=_lane_pad(w["sw_sinks"], 0))


def _late_params(w):
    return {n: w[n].astype(bf16) for n in _LATE}


def _layer_params(w):
    return {**_early_params(w), **_late_params(w)}


def _layer_grads_ref_layout(g):
    return dict(
        pre_mix_g=g["g1"][0], post_mix_g=g["g2"][0], pre_mlp_g=g["g3"][0], post_mlp_g=g["g4"][0],
        w_in=_proj_cols_inv(g["w_in"]), dn_conv_w=g["conv_w"],
        dn_a_log=g["avec"][0, DN_HEADS:2 * DN_HEADS], dn_dt_bias=g["dvec"][0, DN_HEADS:2 * DN_HEADS],
        dn_norm_g=g["ng"][0], sw_sinks=g["sinks"][0, :SW_Q_HEADS],
        w_up_dn=g["w_up_dn"], w_up_sw=g["w_up_sw"], w_o=g["w_o"], w_ff1=g["w_ff1"], w_ff2=g["w_ff2"])


def _consts(positions):
    T = positions.shape[0]
    half = ROT_DIM // 2
    inv_freq = ROPE_THETA ** (-jnp.arange(half, dtype=f32) * (2.0 / ROT_DIM))
    ang = positions.astype(f32)[:, None] * inv_freq
    cos8, sin8 = jnp.cos(ang), jnp.sin(ang)
    rest = SW_HEAD_DIM - ROT_DIM
    c64 = jnp.concatenate([cos8, cos8, jnp.ones((T, rest), f32)], axis=-1)
    s64 = jnp.concatenate([sin8, sin8, jnp.zeros((T, rest), f32)], axis=-1)
    sel = np.zeros((2, 2, LANES, LANES), np.float32)
    for hk in range(SW_KV_HEADS):
        for d in range(SW_HEAD_DIM):
            sel[hk, 0, SW_HEAD_DIM * hk + d, d] = 1.0
            sel[hk, 1, SW_HEAD_DIM * hk + d, SW_HEAD_DIM + d] = 1.0
    return dict(cos=jnp.concatenate([c64, c64], axis=-1), sin=jnp.concatenate([s64, s64], axis=-1),
                sel_a0=jnp.asarray(sel[0, 0]), sel_b0=jnp.asarray(sel[0, 1]),
                sel_a1=jnp.asarray(sel[1, 0]), sel_b1=jnp.asarray(sel[1, 1]))


def _loss(y, tgt):
    T, W = y.shape
    TM = min(TM_ROW, T)
    n = T // TM

    def body(y_ref, t_ref, dy_ref, acc_ref):
        @pl.when(pl.program_id(0) == 0)
        def _():
            acc_ref[...] = jnp.zeros_like(acc_ref)

        d = y_ref[...] - t_ref[...]
        dy_ref[...] = d * (1.0 / W)
        acc_ref[...] += jnp.sum(d * d, axis=0, keepdims=True)

    row = pl.BlockSpec((TM, W), lambda i: (i, 0))
    return pl.pallas_call(
        body, name="loss", grid=(n,), in_specs=[row, row],
        out_specs=[row, pl.BlockSpec((1, W), lambda i: (0, 0))],
        out_shape=[jax.ShapeDtypeStruct((T, W), f32), jax.ShapeDtypeStruct((1, W), f32)],
        compiler_params=pltpu.CompilerParams(dimension_semantics=("arbitrary",)),
    )(y, tgt)


N_SEM = N_DEV - 1


class _Exchange:
    def __init__(self, arrs, modes):
        self.modes, self.n = list(modes), len(arrs)
        self.out_shape = [jax.ShapeDtypeStruct((N_DEV,) + (a.shape[1:] if md == "scatter" else a.shape), a.dtype)
                          for a, md in zip(arrs, modes)]
        self.scratch = [pltpu.SemaphoreType.DMA((self.n, N_SEM)), pltpu.SemaphoreType.DMA((self.n, N_SEM)),
                        pltpu.SemaphoreType.DMA((self.n,))]
        self.specs = [pl.BlockSpec(memory_space=pltpu.HBM)] * self.n

    def bind(self, ins, outs, send_sems, recv_sems, loc_sems):
        x, y, c = lax.axis_index("x"), lax.axis_index("y"), lax.axis_index("c")
        me, sib = 4 * x + 2 * y + c, (x, y, 1 - c)
        flips = [(1 - x, y), (x, 1 - y), (1 - x, 1 - y)]

        def rcopy(a, k, src, dst, dev):
            return pltpu.make_async_remote_copy(src_ref=src, dst_ref=dst, send_sem=send_sems.at[a, k],
                                                recv_sem=recv_sems.at[a, k], device_id=dev,
                                                device_id_type=pl.DeviceIdType.MESH)

        sends, recvs, local, passes = [], [], [], []
        for a, md in enumerate(self.modes):
            src_all, out = ins[a], outs[a]
            mine = (lambda dev: src_all.at[dev]) if md == "scatter" else (lambda dev: src_all)
            local.append(pltpu.make_async_copy(mine(me), out.at[me], loc_sems.at[a]))
            if md in ("gather", "scatter"):
                for k in range(1, N_DEV):
                    px = 1 - x if (k >> 2) & 1 else x
                    py = 1 - y if (k >> 1) & 1 else y
                    pc = 1 - c if k & 1 else c
                    peer = 4 * px + 2 * py + pc
                    sends.append(rcopy(a, k - 1, mine(peer), out.at[me], (px, py, pc)))
                    recvs.append(rcopy(a, k - 1, mine(me), out.at[peer], (px, py, pc)))
            elif md == "gather2":
                sends.append(rcopy(a, 0, src_all, out.at[me], sib))
                recvs.append(rcopy(a, 0, src_all, out.at[4 * x + 2 * y + 1 - c], sib))
                for j, (px, py) in enumerate(flips):
                    sends.append(rcopy(a, 1 + j, src_all, out.at[me], (px, py, c)))
                    theirs = out.at[4 * px + 2 * py + c]
                    arrive = rcopy(a, 1 + j, src_all, theirs, (px, py, c))
                    passes.append((arrive, rcopy(a, 4 + j, theirs, theirs, sib)))
                    recvs.append(rcopy(a, 4 + j, src_all, out.at[4 * px + 2 * py + 1 - c], sib))

        def start():
            for cp in local + sends:
                cp.start()

        def finish():
            for arrive, onward in passes:
                arrive.wait_recv()
                onward.start()
            for cp in recvs:
                cp.wait_recv()
            for cp in sends:
                cp.wait_send()
            for _, onward in passes:
                onward.wait_send()
            for cp in local:
                cp.wait()

        return start, finish


def _exchange(name, arrs, modes):
    ex = _Exchange(arrs, modes)
    n = ex.n

    def body(*refs):
        start, finish = ex.bind(refs[:n], refs[n:2 * n], *refs[2 * n:])
        start()
        finish()

    res = pl.pallas_call(body, name=name, in_specs=ex.specs, out_specs=ex.specs, out_shape=ex.out_shape,
                         scratch_shapes=ex.scratch)(*arrs)
    return list(res)


def _adamw(name, land, w, m, v, tr):
    L_, R_, C_ = w.shape
    n_slots = land.shape[0]
    tr = min(tr, R_)
    assert R_ % tr == 0, (name, R_, tr)
    c1 = 1.0 - ADAM_B1 ** ADAM_STEP
    c2 = 1.0 - ADAM_B2 ** ADAM_STEP

    def body(l_ref, w_ref, m_ref, v_ref, g_ref, d_ref, mo_ref, vo_ref):
        g = l_ref[0].astype(f32)
        for s in range(1, n_slots):
            g = g + l_ref[s].astype(f32)
        m_new = ADAM_B1 * m_ref[...] + (1.0 - ADAM_B1) * g
        v_new = ADAM_B2 * v_ref[...] + (1.0 - ADAM_B2) * jnp.square(g)
        m_hat = m_new / c1
        v_hat = v_new / c2
        g_ref[...] = g
        d_ref[...] = -ADAM_LR * (m_hat / (jnp.sqrt(v_hat) + ADAM_EPS) + ADAM_WD * w_ref[...])
        mo_ref[...] = m_new
        vo_ref[...] = v_new

    row = pl.BlockSpec((1, tr, C_), lambda l, i: (l, i, 0))
    return pl.pallas_call(
        body, name=name, grid=(L_, R_ // tr),
        in_specs=[pl.BlockSpec((n_slots, 1, tr, C_), lambda l, i: (0, l, i, 0)), row, row, row],
        out_specs=[row] * 4, out_shape=[jax.ShapeDtypeStruct((L_, R_, C_), f32)] * 4,
        compiler_params=pltpu.CompilerParams(dimension_semantics=("arbitrary", "arbitrary")),
    )(land, w, m, v)


_BIG = ("w_in", "dn_conv_w", "w_up_dn", "w_up_sw", "w_o", "w_ff1", "w_ff2")
_COL_SHARDED = ("w_in", "dn_conv_w", "w_ff1")
_SMALL_ROWS = ("pre_mix_g", "post_mix_g", "pre_mlp_g", "post_mlp_g")
_SMALL_MISC = ("dn_a_log", "dn_dt_bias", "dn_norm_g", "sw_sinks")
_WEIGHTS = ("pre_mix_g", "w_in", "dn_conv_w", "dn_a_log", "dn_dt_bias", "dn_norm_g", "sw_sinks", "w_up_dn", "w_up_sw",
            "w_o", "post_mix_g", "pre_mlp_g", "w_ff1", "w_ff2", "post_mlp_g")
_SMALL_PACK_ROWS = 24


def _unshard(name, g):
    if name in _COL_SHARDED:
        g = jnp.moveaxis(g, 0, -2)
        return g.reshape(g.shape[:-2] + (g.shape[-2] * g.shape[-1],))
    g = jnp.moveaxis(g, 0, 1)
    return g.reshape((g.shape[0], g.shape[1] * g.shape[2]) + g.shape[3:])


def _shard_major(name, full):
    if name in _COL_SHARDED:
        s = full.reshape(full.shape[:-1] + (N_DEV, full.shape[-1] // N_DEV))
        return jnp.moveaxis(s, -2, 0)
    s = full.reshape((full.shape[0], N_DEV, full.shape[1] // N_DEV) + full.shape[2:])
    return jnp.moveaxis(s, 1, 0)


def _pack_small(d):
    rows = jnp.concatenate([d[n] for n in _SMALL_ROWS], axis=0)
    misc = jnp.concatenate([d[n].reshape(-1) for n in _SMALL_MISC])
    misc = jnp.pad(misc, (0, W1 - misc.shape[0])).reshape(1, W1)
    out = jnp.concatenate([rows, misc], axis=0)
    return jnp.pad(out, ((0, _SMALL_PACK_ROWS - out.shape[0]), (0, 0)))


def _unpack_small(a, like):
    out, L = {}, like[_SMALL_ROWS[0]].shape[0]
    for i, n in enumerate(_SMALL_ROWS):
        out[n] = a[L * i:L * (i + 1)]
    at, row = 0, a[L * len(_SMALL_ROWS)]
    for n in _SMALL_MISC:
        size = like[n].size
        out[n] = row[at:at + size].reshape(like[n].shape)
        at += size
    return out


def kernel(x, positions, pre_mix_g, w_in, dn_conv_w, dn_a_log, dn_dt_bias, dn_norm_g, sw_sinks, w_up_dn, w_up_sw, w_o, post_mix_g, pre_mlp_g, w_ff1, w_ff2, post_mlp_g, loss_target, m_pre_mix_g, m_w_in, m_dn_conv_w, m_dn_a_log, m_dn_dt_bias, m_dn_norm_g, m_sw_sinks, m_w_up_dn, m_w_up_sw, m_w_o, m_post_mix_g, m_pre_mlp_g, m_w_ff1, m_w_ff2, m_post_mlp_g, v_pre_mix_g, v_w_in, v_dn_conv_w, v_dn_a_log, v_dn_dt_bias, v_dn_norm_g, v_sw_sinks, v_w_up_dn, v_w_up_sw, v_w_o, v_post_mix_g, v_pre_mlp_g, v_w_ff1, v_w_ff2, v_post_mlp_g):
    w = dict(pre_mix_g=pre_mix_g, w_in=w_in, dn_conv_w=dn_conv_w, dn_a_log=dn_a_log, dn_dt_bias=dn_dt_bias,
             dn_norm_g=dn_norm_g, sw_sinks=sw_sinks, w_up_dn=w_up_dn, w_up_sw=w_up_sw, w_o=w_o, post_mix_g=post_mix_g,
             pre_mlp_g=pre_mlp_g, w_ff1=w_ff1, w_ff2=w_ff2, post_mlp_g=post_mlp_g)
    m = dict(pre_mix_g=m_pre_mix_g, w_in=m_w_in, dn_conv_w=m_dn_conv_w, dn_a_log=m_dn_a_log, dn_dt_bias=m_dn_dt_bias,
             dn_norm_g=m_dn_norm_g, sw_sinks=m_sw_sinks, w_up_dn=m_w_up_dn, w_up_sw=m_w_up_sw, w_o=m_w_o,
             post_mix_g=m_post_mix_g, pre_mlp_g=m_pre_mlp_g, w_ff1=m_w_ff1, w_ff2=m_w_ff2, post_mlp_g=m_post_mlp_g)
    v = dict(pre_mix_g=v_pre_mix_g, w_in=v_w_in, dn_conv_w=v_dn_conv_w, dn_a_log=v_dn_a_log, dn_dt_bias=v_dn_dt_bias,
             dn_norm_g=v_dn_norm_g, sw_sinks=v_sw_sinks, w_up_dn=v_w_up_dn, w_up_sw=v_w_up_sw, w_o=v_w_o,
             post_mix_g=v_post_mix_g, pre_mlp_g=v_pre_mlp_g, w_ff1=v_w_ff1, w_ff2=v_w_ff2, post_mlp_g=v_post_mlp_g)
    n_layers = pre_mix_g.shape[0]
    xs, pos, tgt = x[0], positions[0], loss_target[0]

    nb = len(_BIG)
    cst = _consts(pos)

    def payload(l, names):
        return [w[n][l] if n == "dn_conv_w" else w[n][l].astype(bf16) for n in names]

    def full(names, gathered):
        return {n: _unshard(n, g[:, None])[0] for n, g in zip(names, gathered)}

    early = _exchange("allgather_first", payload(0, _EARLY), ["gather2"] * len(_EARLY))
    h, res, layers = xs, [], []
    for l in range(n_layers):
        wl = {**full(_EARLY, early), **{n: w[n][l] for n in _WEIGHTS if n not in _BIG}}
        send = payload(l, _LATE) + (payload(l + 1, _EARLY) if l + 1 < n_layers else [])
        h, r, p, got = _layer_fwd(h, _early_params(wl), cst, (send, ["gather2"] * len(send)),
                                  late=lambda got: _late_params(full(_LATE, got[:len(_LATE)])))
        early = got[len(_LATE):]
        res.append(r)
        layers.append(p)
    dy, sq = _loss(h, tgt)
    loss = lax.psum(0.5 / D_MODEL * jnp.sum(sq), ("x", "y", "c"))

    grads, landed, carry = [None] * n_layers, {n: [None] * n_layers for n in _BIG}, []
    for l in reversed(range(n_layers)):
        dy, g, arrived = _layer_bwd(dy, res[l], layers[l], cst, carry)
        for n, a in zip(_EARLY, arrived[:len(carry)]):
            landed[n][l + 1] = a
        for n, a in zip(_LATE, arrived[len(carry):]):
            landed[n][l] = a
        grads[l] = _layer_grads_ref_layout(g)
        carry = [_pieces(n, grads[l][n]) for n in _EARLY]
    grad_x = dy[None]
    small_grads = _pack_small({n: jnp.stack([grads[l][n] for l in range(n_layers)]) for n in _SMALL_ROWS + _SMALL_MISC})
    last = _exchange("exchange_last", carry + [small_grads], ["scatter"] * len(carry) + ["gather"])
    for n, a in zip(_EARLY, last):
        landed[n][0] = a

    out_g, out_d, out_m, out_v = {}, {}, {}, {}
    for n in _BIG:
        land = jnp.stack(landed[n], axis=1)
        out_g[n], out_d[n], out_m[n], out_v[n] = _adamw("adamw_" + n, land, w[n], m[n], v[n], tr=256)
    small = _adamw("adamw_small", last[-1][:, None], _pack_small(w)[None], _pack_small(m)[None], _pack_small(v)[None],
                   tr=_SMALL_PACK_ROWS)
    for dst, a in zip((out_g, out_d, out_m, out_v), small):
        dst.update(_unpack_small(a[0], w))
    return (loss, grad_x, *[out_g[n] for n in _WEIGHTS], *[out_d[n] for n in _WEIGHTS],
            *[out_m[n] for n in _WEIGHTS], *[out_v[n] for n in _WEIGHTS])
```

```python
import math

import numpy as np
import jax
import jax.numpy as jnp
from jax import lax
from jax.experimental import pallas as pl
from jax.experimental.pallas import tpu as pltpu

f32 = jnp.float32
bf16 = jnp.bfloat16

N_DEV = 8
D_MODEL = 1024
DN_HEADS = 8
DN_DK = 128
DN_CHUNK = 128
DN_STEP = 2
DN_CONV = 4
SW_Q_HEADS = 16
SW_KV_HEADS = 2
SW_HEAD_DIM = 64
SW_BLOCK = 128
ROPE_THETA = 500000.0
ROT_DIM = SW_HEAD_DIM // 4
D_FF = 4 * D_MODEL
EPS = 1e-6
LANES = 128
CONV_HALO = 8
NEG_BIG = -1e30

ADAM_LR = 0.001
ADAM_B1 = 0.9
ADAM_B2 = 0.999
ADAM_EPS = 1e-08
ADAM_WD = 0.01
ADAM_STEP = 10

PROJ_W = 7680
CB_Q, CB_K, CB_V, CB_Z, CB_SWQ, CB_GA, CB_GB = 0, 1, 2, 3, 4, 5, 6
CB_SWK, CB_SWV, CB_BA = 56, 57, 58

NN = ((1,), (0,))
NT = ((1,), (1,))
TN = ((0,), (0,))


def _mm(a, b, dims):
    return lax.dot_general(a.astype(bf16), b.astype(bf16), (dims, ((), ())), preferred_element_type=f32)


def _matmul(name, a, b, form, out_dtype, tm=512, tn=512, tk=1024, extra=(), epilogue=None):
    if form == "nn":
        (M, K), (_, N) = a.shape, b.shape
    elif form == "nt":
        (M, K), (N, _) = a.shape, b.shape
    else:
        (K, M), (_, N) = a.shape, b.shape
    tm, tn, tk = min(tm, M), min(tn, N), min(tk, K)
    assert M % tm == 0 and N % tn == 0 and K % tk == 0, (name, M, N, K, tm, tn, tk)
    nk = K // tk
    dims = {"nn": NN, "nt": NT, "tn": TN}[form]
    out_dtypes = [out_dtype] if epilogue is None else list(out_dtype)
    ne, no = len(extra), len(out_dtypes)

    def body(a_ref, b_ref, *rest):
        e_refs, o_refs, acc_ref = rest[:ne], rest[ne:ne + no], rest[ne + no]

        def finish(acc):
            vals = [acc] if epilogue is None else epilogue(acc, [e[...] for e in e_refs])
            for o, val in zip(o_refs, vals):
                o[...] = val.astype(o.dtype)

        part = lax.dot_general(a_ref[...], b_ref[...], (dims, ((), ())), preferred_element_type=f32)
        if nk == 1:
            finish(part)
        else:
            k = pl.program_id(2)

            @pl.when(k == 0)
            def _():
                acc_ref[...] = part

            @pl.when(k > 0)
            def _():
                acc_ref[...] += part

            @pl.when(k == nk - 1)
            def _():
                finish(acc_ref[...])

    if form == "tn":
        a_spec = pl.BlockSpec((tk, tm), lambda i, j, k: (k, i))
    else:
        a_spec = pl.BlockSpec((tm, tk), lambda i, j, k: (i, k))
    if form == "nt":
        b_spec = pl.BlockSpec((tn, tk), lambda i, j, k: (j, k))
    else:
        b_spec = pl.BlockSpec((tk, tn), lambda i, j, k: (k, j))
    tile = pl.BlockSpec((tm, tn), lambda i, j, k: (i, j))
    res = pl.pallas_call(
        body, name=name,
        grid=(M // tm, N // tn, nk),
        in_specs=[a_spec, b_spec] + [tile] * ne,
        out_specs=[tile] * no,
        out_shape=[jax.ShapeDtypeStruct((M, N), dt) for dt in out_dtypes],
        scratch_shapes=[pltpu.VMEM((tm, tn) if nk > 1 else (8, 128), f32)],
        compiler_params=pltpu.CompilerParams(dimension_semantics=("parallel", "parallel", "arbitrary")),
    )(a, b, *extra)
    return res[0] if epilogue is None else list(res)


def _tile_specs(ins, halo_ids, params, TM, HR, row_of):
    specs = [pl.BlockSpec((TM, w), lambda i, cb=cb: (row_of(i), cb)) for (_, w, cb) in ins]
    for h in halo_ids:
        _, w, cb = ins[h]
        specs.append(pl.BlockSpec((HR, w), lambda i, cb=cb: (jnp.maximum(row_of(i) * (TM // HR) - 1, 0), cb)))
    for p in params:
        specs.append(pl.BlockSpec(p.shape, lambda i, nd=p.ndim: (0,) * nd))
    return specs


def _tile_fwd(name, fn, T, TM, ins, params, outs, halo_ids=(), HR=CONV_HALO):
    TM = min(TM, T)
    n = T // TM
    ni, nh, npar = len(ins), len(halo_ids), len(params)

    def body(*refs):
        in_v = [r[...] for r in refs[:ni]]
        halo_v = [r[...] for r in refs[ni:ni + nh]]
        par_v = [r[...] for r in refs[ni + nh:ni + nh + npar]]
        o_refs = refs[ni + nh + npar:]
        first = pl.program_id(0) == 0
        vals = fn(first, in_v, halo_v, par_v)
        for o, val in zip(o_refs, vals):
            o[...] = val.astype(o.dtype)

    res = pl.pallas_call(
        body, name=name, grid=(n,),
        in_specs=_tile_specs(ins, halo_ids, params, TM, HR, lambda i: i),
        out_specs=[pl.BlockSpec((TM, w), lambda i: (i, 0)) for (w, _) in outs],
        out_shape=[jax.ShapeDtypeStruct((T, w), dt) for (w, dt) in outs],
        compiler_params=pltpu.CompilerParams(dimension_semantics=("arbitrary",)),
    )(*[a for (a, _, _) in ins], *[ins[h][0] for h in halo_ids], *params)
    return list(res)


def _tile_bwd(name, fn, T, TM, ins, params, cts, din, dpar, halo_ids=(), HR=CONV_HALO):
    TM = min(TM, T)
    n = T // TM
    ni, nh, npar, nc = len(ins), len(halo_ids), len(params), len(cts)
    din_ids = [j for (j, _) in din]
    dh_ids = [h for h in halo_ids if h in din_ids]
    nd, ndp, ndh = len(din), len(dpar), len(dh_ids)

    def body(*refs):
        in_v = [r[...] for r in refs[:ni]]
        halo_v = [r[...] for r in refs[ni:ni + nh]]
        par_v = [r[...] for r in refs[ni + nh:ni + nh + npar]]
        ct_v = [r[...].astype(f32) for r in refs[ni + nh + npar:ni + nh + npar + nc]]
        o_refs = refs[ni + nh + npar + nc:ni + nh + npar + nc + nd + ndp]
        carry_refs = refs[ni + nh + npar + nc + nd + ndp:]
        i = pl.program_id(0)
        first = i == n - 1

        def g(d_in, d_halo, d_par):
            full_in = list(in_v)
            for j, val in zip(din_ids, d_in):
                full_in[j] = val
            full_halo = list(halo_v)
            for h, val in zip(dh_ids, d_halo):
                full_halo[list(halo_ids).index(h)] = val
            full_par = list(par_v)
            for j, val in zip(dpar, d_par):
                full_par[j] = val
            return tuple(fn(first, full_in, full_halo, full_par))

        prim = ([in_v[j].astype(f32) for j in din_ids],
                [halo_v[list(halo_ids).index(h)].astype(f32) for h in dh_ids],
                [par_v[j] for j in dpar])
        _, vjp = jax.vjp(g, *prim)
        g_in, g_halo, g_par = vjp(tuple(ct_v))

        @pl.when(i == 0)
        def _():
            for c in carry_refs:
                c[...] = jnp.zeros_like(c)
            for o in o_refs[nd:]:
                o[...] = jnp.zeros_like(o)

        for slot, (j, _) in enumerate(din):
            val = g_in[slot]
            if j in dh_ids:
                c = carry_refs[dh_ids.index(j)]
                val = jnp.concatenate([val[:TM - HR], val[TM - HR:] + c[...]], axis=0) if TM > HR else val + c[...]
                c[...] = g_halo[dh_ids.index(j)]
            o_refs[slot][...] = val.astype(o_refs[slot].dtype)
        for slot in range(ndp):
            o_refs[nd + slot][...] += g_par[slot]

    rev = lambda i: n - 1 - i
    in_specs = _tile_specs(ins, halo_ids, params, TM, HR, rev)
    ct_specs = [pl.BlockSpec((TM, w), lambda i, cb=cb: (rev(i), cb)) for (_, w, cb) in cts]
    out_specs = [pl.BlockSpec((TM, ins[j][1]), lambda i: (rev(i), 0)) for j in din_ids]
    out_specs += [pl.BlockSpec(params[j].shape, lambda i, nd_=params[j].ndim: (0,) * nd_) for j in dpar]
    out_shape = [jax.ShapeDtypeStruct((T, ins[j][1]), dt) for (j, dt) in din]
    out_shape += [jax.ShapeDtypeStruct(params[j].shape, f32) for j in dpar]
    res = pl.pallas_call(
        body, name=name, grid=(n,),
        in_specs=in_specs + ct_specs,
        out_specs=out_specs,
        out_shape=out_shape,
        scratch_shapes=[pltpu.VMEM((HR, ins[h][1]), f32) for h in dh_ids],
        compiler_params=pltpu.CompilerParams(dimension_semantics=("arbitrary",)),
    )(*[a for (a, _, _) in ins], *[ins[h][0] for h in halo_ids], *params, *[a for (a, _, _) in cts])
    return list(res)


def _rms(x, g):
    return x * lax.rsqrt(jnp.mean(x * x, axis=-1, keepdims=True) + EPS) * g


def _fn_prenorm(first, ins, halos, params):
    (x,), (g,) = ins, params
    x = x.astype(f32)
    return [_rms(x, g), x]


def _fn_postmix(first, ins, halos, params):
    (x, mix), (g2, g3) = ins, params
    x1 = x + _rms(mix, g2)
    return [x1, _rms(x1, g3)]


def _fn_postmlp(first, ins, halos, params):
    (x1, ff), (g4,) = ins, params
    return [x1 + _rms(ff, g4)]


def _fn_rms_only(first, ins, halos, params):
    (ff,), (g4,) = ins, params
    return [_rms(ff, g4)]


def _fn_merge(first, ins, halos, params):
    ga, gb, ya, yb = ins
    return [jax.nn.sigmoid(ga) * ya + jax.nn.sigmoid(gb) * yb]


def _roll_rows(x, shift):
    return pltpu.roll(x, shift, 0)


_row_roll = jax.custom_vjp(_roll_rows, nondiff_argnums=(1,))
_row_roll.defvjp(lambda x, shift: (_roll_rows(x, shift), None),
                 lambda shift, _, ct: (_roll_rows(ct, ct.shape[0] - shift),))


def _make_fn_conv(norm_scale):
    def fn(first, ins, halos, params):
        (x,), (xp,), (w,) = ins, halos, params
        xp = jnp.where(first, 0.0, xp)
        outs = []
        for h in range(DN_HEADS):
            sl = slice(DN_DK * h, DN_DK * (h + 1))
            xe, wh = jnp.concatenate([xp[:, sl], x[:, sl]], axis=0), w[:, sl]
            y = xe[CONV_HALO:] * wh[DN_CONV - 1:DN_CONV]
            for j in range(DN_CONV - 1):
                y = y + _row_roll(xe, DN_CONV - 1 - j)[CONV_HALO:] * wh[j:j + 1]
            y = jax.nn.silu(y)
            if norm_scale is not None:
                y = y * lax.rsqrt(jnp.sum(y * y, axis=-1, keepdims=True) + EPS) * norm_scale
            outs.append(y)
        return [jnp.concatenate(outs, axis=-1)]
    return fn


def _fn_gates(first, ins, halos, params):
    (ba,), (avec, dvec) = ins, params
    lane = lax.broadcasted_iota(jnp.int32, ba.shape, 1)
    beta = jax.nn.sigmoid(ba)
    g = -jnp.exp(avec) * jax.nn.softplus(ba + dvec)
    return [jnp.where(lane < DN_HEADS, beta, jnp.where(lane < 2 * DN_HEADS, g, 0.0))]


def _fn_dnpost(first, ins, halos, params):
    (o, z), (ng,) = ins, params
    outs = []
    for h in range(DN_HEADS):
        sl = slice(DN_DK * h, DN_DK * (h + 1))
        outs.append(_rms(o[:, sl], ng) * jax.nn.silu(z[:, sl]))
    return [jnp.concatenate(outs, axis=-1)]


def _roll_lanes(x, shift):
    return pltpu.roll(x, shift, 1)


_lane_roll = jax.custom_vjp(_roll_lanes, nondiff_argnums=(1,))
_lane_roll.defvjp(lambda x, shift: (_roll_lanes(x, shift), None),
                  lambda shift, _, ct: (_roll_lanes(ct, LANES - shift),))


def _kcopy(g):
    return (g // (SW_Q_HEADS // SW_KV_HEADS)) * 2 + g % 2


def _attn_probs(qp, kx, sink, mask):
    heads = range(len(sink))
    s = [jnp.where(mask, _mm(qp[g // 2], kx[_kcopy(g)], NT), NEG_BIG) for g in heads]
    m = [jnp.maximum(jnp.max(s[g], axis=-1, keepdims=True), sink[g]) for g in heads]
    p = [jnp.exp(s[g] - m[g]) for g in heads]
    ps = [jnp.exp(sink[g] - m[g]) for g in heads]
    inv = [1.0 / (jnp.sum(p[g], axis=-1, keepdims=True) + ps[g]) for g in heads]
    return [p[g] * inv[g] for g in heads], [ps[g] * inv[g] for g in heads]


@jax.custom_vjp
def _attn_group(qp, kx, vx, sink, mask):
    probs, _ = _attn_probs(qp, kx, sink, mask)
    return tuple(_mm(probs[g], vx[_kcopy(g)], NN) for g in range(len(sink)))


def _attn_group_fwd(qp, kx, vx, sink, mask):
    o = _attn_group(qp, kx, vx, sink, mask)
    return o, (qp, kx, vx, sink, mask, o)


def _attn_group_bwd(res, do):
    qp, kx, vx, sink, mask, o = res
    heads = range(len(sink))
    probs, p_sink = _attn_probs(qp, kx, sink, mask)
    d_probs = [_mm(do[g], vx[_kcopy(g)], NT) for g in heads]
    dot = [jnp.sum(do[g] * o[g], axis=-1, keepdims=True) for g in heads]
    ds = [probs[g] * (d_probs[g] - dot[g]) for g in heads]
    d_qp = tuple(_mm(ds[2 * j], kx[_kcopy(2 * j)], NN) + _mm(ds[2 * j + 1], kx[_kcopy(2 * j + 1)], NN)
                 for j in range(len(qp)))
    rows = lambda xs: jnp.concatenate(xs, axis=0)
    d_kx = tuple(_mm(rows([ds[g] for g in heads if _kcopy(g) == c]), rows([qp[g // 2] for g in heads if _kcopy(g) == c]), TN)
                 for c in range(len(kx)))
    d_vx = tuple(_mm(rows([probs[g] for g in heads if _kcopy(g) == c]), rows([do[g] for g in heads if _kcopy(g) == c]), TN)
                 for c in range(len(kx)))
    d_sink = tuple(-jnp.sum(p_sink[g] * dot[g], axis=0, keepdims=True) for g in heads)
    return d_qp, d_kx, d_vx, d_sink, None


_attn_group.defvjp(_attn_group_fwd, _attn_group_bwd)


def _fn_swa(first, ins, halos, params):
    q, k, v, cos, sin = ins
    kp, vp, cosp, sinp = halos
    sinks, sel_a0, sel_b0, sel_a1, sel_b1 = params
    B = q.shape[0]
    half = ROT_DIM // 2
    in_head = jnp.bitwise_and(lax.broadcasted_iota(jnp.int32, (1, LANES), 1), SW_HEAD_DIM - 1)

    def rope(x, c, s):
        return (x * c + _lane_roll(x, LANES - half) * jnp.where(in_head < half, -s, 0.0)
                + _lane_roll(x, half) * jnp.where(in_head >= half, s, 0.0))

    kcat = jnp.concatenate([rope(kp, cosp, sinp), rope(k, cos, sin)], axis=0)
    vcat = jnp.concatenate([vp, v], axis=0)
    r = lax.broadcasted_iota(jnp.int32, (B, 2 * B), 0)
    c = lax.broadcasted_iota(jnp.int32, (B, 2 * B), 1)
    mask = (c > r) & (c <= r + B) & ((c >= B) | jnp.logical_not(first))
    sels = (sel_a0, sel_b0, sel_a1, sel_b1)
    kx = tuple(_mm(kcat, s, NN) for s in sels)
    vx = tuple(_mm(vcat, s, NN) for s in sels)
    qp = tuple(rope(q[:, LANES * j:LANES * (j + 1)], cos, sin) * (SW_HEAD_DIM ** -0.5) for j in range(SW_Q_HEADS // 2))
    sink = tuple(sinks[:, g:g + 1] for g in range(SW_Q_HEADS))
    o = _attn_group(qp, kx, vx, sink, mask)
    return [jnp.concatenate([o[2 * j] + o[2 * j + 1] for j in range(SW_Q_HEADS // 2)], axis=-1)]


@jax.custom_vjp
def _inv_unit_lower(Ls):
    C = Ls[0].shape[0]
    ii = lax.broadcasted_iota(jnp.int32, (C, C), 0)
    jj = lax.broadcasted_iota(jnp.int32, (C, C), 1)

    def off_mask(level):
        same_pair = jnp.right_shift(ii, level + 1) == jnp.right_shift(jj, level + 1)
        lower_left = (jnp.bitwise_and(jnp.right_shift(ii, level), 1) == 1) & (jnp.bitwise_and(jnp.right_shift(jj, level), 1) == 0)
        return same_pair & lower_left

    eye = (ii == jj).astype(f32)
    m0 = off_mask(0)
    Ts = [eye - jnp.where(m0, L, 0.0) for L in Ls]
    for level in range(1, int(math.log2(C))):
        mk = off_mask(level)
        left = [_mm(T_, jnp.where(mk, L, 0.0), NN) for T_, L in zip(Ts, Ls)]
        Ts = [T_ - _mm(a, T_, NN) for a, T_ in zip(left, Ts)]
    return tuple(Ts)


def _inv_fwd(Ls):
    Ts = _inv_unit_lower(Ls)
    return Ts, Ts


def _inv_bwd(Ts, dTs):
    left = [_mm(T_, dT, TN) for T_, dT in zip(Ts, dTs)]
    return (tuple(-_mm(a, T_, NT) for a, T_ in zip(left, Ts)),)


_inv_unit_lower.defvjp(_inv_fwd, _inv_bwd)


@jax.custom_vjp
def _inv_known(Ls, Ts):
    return Ts


_inv_known.defvjp(lambda Ls, Ts: (Ts, Ts),
                  lambda Ts, dTs: (_inv_bwd(Ts, dTs)[0], tuple(jnp.zeros_like(t) for t in Ts)))


def _mm_01(a, b, dims):
    hi = b.astype(bf16)
    r1 = b - hi.astype(f32)
    mid = r1.astype(bf16)
    lo = (r1 - mid.astype(f32)).astype(bf16)
    a16 = a.astype(bf16)
    dot = lambda part: lax.dot_general(a16, part, (dims, ((), ())), preferred_element_type=f32)
    return dot(hi) + dot(mid) + dot(lo)


def _eye(n):
    return lax.broadcasted_iota(jnp.int32, (n, n), 0) == lax.broadcasted_iota(jnp.int32, (n, n), 1)


def _lower(n):
    return lax.broadcasted_iota(jnp.int32, (n, n), 0) >= lax.broadcasted_iota(jnp.int32, (n, n), 1)


@jax.custom_vjp
def _transpose(x):
    return _mm_01(_eye(x.shape[1]), x, NT)


_transpose.defvjp(lambda x: (_transpose(x), None), lambda _, ct: (_transpose(ct),))


@jax.custom_vjp
def _cumsum_rows(x):
    return _mm_01(_lower(x.shape[0]), x, NN)


_cumsum_rows.defvjp(lambda x: (_cumsum_rows(x), None), lambda _, ct: (_mm_01(_lower(ct.shape[0]), ct, TN),))


def _dn_chunk(q, k, v, gb, S, tinv_known=None):
    C = min(DN_CHUNK, q.shape[0])
    n_ch = q.shape[0] // C
    units = [(c, h) for c in range(n_ch) for h in range(DN_HEADS)]
    per_unit = lambda xs: dict(zip(units, xs))
    ii = lax.broadcasted_iota(jnp.int32, (C, C), 0)
    jj = lax.broadcasted_iota(jnp.int32, (C, C), 1)
    causal, strict = ii >= jj, ii > jj
    gbc = [gb[C * c:C * (c + 1)] for c in range(n_ch)]
    gc_all = [_cumsum_rows(g) for g in gbc]
    gc_t = [_transpose(g) for g in gc_all]
    part = lambda x, c, h: x[C * c:C * (c + 1), DN_DK * h:DN_DK * (h + 1)]
    qs, ks, vs = (per_unit([part(x, c, h) for c, h in units]) for x in (q, k, v))
    beta = per_unit([gbc[c][:, h:h + 1] for c, h in units])
    gcol = per_unit([gc_all[c][:, DN_HEADS + h:DN_HEADS + h + 1] for c, h in units])
    grow = per_unit([gc_t[c][DN_HEADS + h:DN_HEADS + h + 1, :] for c, h in units])
    decay = per_unit([jnp.where(causal, jnp.exp(jnp.where(causal, gcol[u] - grow[u], 0.0)), 0.0) for u in units])
    kb = per_unit([ks[u] * beta[u] for u in units])
    kk = per_unit([_mm(kb[u], ks[u], NT) for u in units])
    qk = per_unit([_mm(qs[u], ks[u], NT) for u in units])
    Ls = tuple(jnp.where(strict, kk[u] * decay[u], 0.0) for u in units)
    tinv = _inv_unit_lower(Ls) if tinv_known is None else _inv_known(Ls, tuple(tinv_known))
    ti = per_unit(tinv)
    eg = per_unit([jnp.exp(gcol[u]) for u in units])
    un = per_unit([_mm(ti[u], vs[u] * beta[u], NN) for u in units])
    w = per_unit([_mm(ti[u], kb[u] * eg[u], NN) for u in units])
    gl = per_unit([gcol[u][C - 1:C, :] for u in units])
    a_intra = per_unit([qk[u] * decay[u] for u in units])
    q_dec = per_unit([qs[u] * eg[u] for u in units])
    k_dec = per_unit([ks[u] * jnp.exp(gl[u] - gcol[u]) for u in units])
    H, state, outs = range(DN_HEADS), list(S), []
    for c in range(n_ch):
        ws = [_mm(w[c, h], state[h], NN) for h in H]
        qS = [_mm(q_dec[c, h], state[h], NN) for h in H]
        v_new = [un[c, h] - ws[h] for h in H]
        av = [_mm(a_intra[c, h], v_new[h], NN) for h in H]
        kv = [_mm(k_dec[c, h], v_new[h], TN) for h in H]
        outs.append(jnp.concatenate([qS[h] + av[h] for h in H], axis=-1))
        state = [state[h] * jnp.exp(gl[c, h]) + kv[h] for h in H]
    return (outs[0] if n_ch == 1 else jnp.concatenate(outs, axis=0)), tuple(state), tuple(tinv)


_DN_W = DN_HEADS * DN_DK


def _dn_inputs(first, xq, xk, xv, hq, hk, hv, ba, cw, avec, dvec):
    conv = lambda scale, x, h, j: _make_fn_conv(scale)(first, [x], [h], [cw[:, _DN_W * j:_DN_W * (j + 1)]])[0]
    return (conv(DN_DK ** -0.5, xq, hq, 0), conv(1.0, xk, hk, 1), conv(None, xv, hv, 2),
            _fn_gates(first, [ba], [], [avec, dvec])[0])


def _delta_specs(C, row_of):
    cols = (CB_Q, CB_K, CB_V)
    specs = [pl.BlockSpec((C, _DN_W), lambda i, cb=cb: (row_of(i), cb)) for cb in cols]
    specs += [pl.BlockSpec((CONV_HALO, _DN_W), lambda i, cb=cb: (jnp.maximum(row_of(i) * (C // CONV_HALO) - 1, 0), cb))
              for cb in cols]
    specs.append(pl.BlockSpec((C, LANES), lambda i: (row_of(i), CB_BA)))
    return specs


def _whole(a):
    return pl.BlockSpec(a.shape, lambda i, nd=a.ndim: (0,) * nd)


def _carrying(body, n_in, n_out, n_scratch, n_steps, ex):
    if ex is None:
        return body

    def wrapped(*refs):
        ins, rest = refs[:n_in], refs[n_in:]
        ex_in, rest = rest[:ex.n], rest[ex.n:]
        outs, rest = rest[:n_out], rest[n_out:]
        ex_out, rest = rest[:ex.n], rest[ex.n:]
        scratch, sems = rest[:n_scratch], rest[n_scratch:]
        start, finish = ex.bind(ex_in, ex_out, *sems)
        step = pl.program_id(0)
        pl.when(step == 0)(start)
        body(*ins, *outs, *scratch)
        pl.when(step == n_steps - 1)(finish)

    return wrapped


def _carried_call(name, body, grid, in_specs, out_specs, out_shape, scratch, operands, ex, ex_arrs):
    n_out = len(out_shape)
    if ex is not None:
        in_specs, out_specs = in_specs + ex.specs, out_specs + ex.specs
        out_shape, scratch, operands = out_shape + ex.out_shape, scratch + ex.scratch, tuple(operands) + tuple(ex_arrs)
    res = pl.pallas_call(
        _carrying(body, len(in_specs) - (ex.n if ex else 0), n_out, len(scratch) - (3 if ex else 0), grid[0], ex),
        name=name, grid=grid, in_specs=in_specs, out_specs=out_specs, out_shape=out_shape, scratch_shapes=scratch,
        compiler_params=pltpu.CompilerParams(dimension_semantics=("arbitrary",)),
    )(*operands)
    return list(res[:n_out]), list(res[n_out:])


def _delta_steps(T):
    rows = min(DN_STEP * DN_CHUNK, T)
    return rows, T // rows, (rows // min(DN_CHUNK, T)) * DN_HEADS


def _delta_fwd(proj, cw, avec, dvec, comm=None):
    T = proj.shape[0]
    C, n, n_units = _delta_steps(T)
    ex = None if comm is None else _Exchange(*comm)

    def body(xq, xk, xv, hq, hk, hv, ba, cw_ref, a_ref, d_ref, o_ref, hist_ref, tinv_ref,
             q_ref, k_ref, v_ref, gb_ref, s_ref):
        first = pl.program_id(0) == 0

        @pl.when(first)
        def _():
            s_ref[...] = jnp.zeros_like(s_ref)

        S = tuple(s_ref[h] for h in range(DN_HEADS))
        for h in range(DN_HEADS):
            hist_ref[0, h] = S[h]
        q, k, v, gb = _dn_inputs(first, xq[...], xk[...], xv[...], hq[...], hk[...], hv[...], ba[...],
                                 cw_ref[...], a_ref[...], d_ref[...])
        q_ref[...], k_ref[...], v_ref[...], gb_ref[...] = q, k, v, gb
        o, s_new, tinv = _dn_chunk(q, k, v, gb, S)
        o_ref[...] = o
        for h in range(DN_HEADS):
            s_ref[h] = s_new[h]
        for u in range(n_units):
            tinv_ref[0, u] = tinv[u]

    row = pl.BlockSpec((C, _DN_W), lambda i: (i, 0))
    c_inv = min(DN_CHUNK, T)
    return _carried_call(
        "delta_fwd", body, (n,),
        _delta_specs(C, lambda i: i) + [_whole(cw), _whole(avec), _whole(dvec)],
        [row, pl.BlockSpec((1, DN_HEADS, DN_DK, DN_DK), lambda i: (i, 0, 0, 0)),
         pl.BlockSpec((1, n_units, c_inv, c_inv), lambda i: (i, 0, 0, 0)),
         row, row, row, pl.BlockSpec((C, LANES), lambda i: (i, 0))],
        [jax.ShapeDtypeStruct((T, _DN_W), f32), jax.ShapeDtypeStruct((n, DN_HEADS, DN_DK, DN_DK), f32),
         jax.ShapeDtypeStruct((n, n_units, c_inv, c_inv), f32)]
        + [jax.ShapeDtypeStruct((T, _DN_W), f32)] * 3 + [jax.ShapeDtypeStruct((T, LANES), f32)],
        [pltpu.VMEM((DN_HEADS, DN_DK, DN_DK), f32)],
        (proj, proj, proj, proj, proj, proj, proj, cw, avec, dvec), ex, comm[0] if comm else ())


def _delta_bwd(qn, kn, vv, gb, hist, tinv, do, comm=None):
    T = qn.shape[0]
    C, n, n_units = _delta_steps(T)
    c_inv = min(DN_CHUNK, T)
    ex = None if comm is None else _Exchange(*comm)

    def body(q_ref, k_ref, v_ref, gb_ref, hist_ref, tinv_ref, do_ref, dq_ref, dk_ref, dv_ref, dgb_ref, ds_ref):
        @pl.when(pl.program_id(0) == 0)
        def _():
            ds_ref[...] = jnp.zeros_like(ds_ref)

        S = tuple(hist_ref[0, h] for h in range(DN_HEADS))
        known = tuple(tinv_ref[0, u] for u in range(n_units))
        chunk = lambda q, k, v, g, s: _dn_chunk(q, k, v, g, s, tinv_known=known)[:2]
        _, vjp = jax.vjp(chunk, q_ref[...], k_ref[...], v_ref[...], gb_ref[...], S)
        dS = tuple(ds_ref[h] for h in range(DN_HEADS))
        dq, dk, dv, dgb, dS_in = vjp((do_ref[...], dS))
        dq_ref[...] = dq
        dk_ref[...] = dk
        dv_ref[...] = dv
        dgb_ref[...] = dgb
        for h in range(DN_HEADS):
            ds_ref[h] = dS_in[h]

    row = pl.BlockSpec((C, _DN_W), lambda i: (n - 1 - i, 0))
    small = pl.BlockSpec((C, LANES), lambda i: (n - 1 - i, 0))
    return _carried_call(
        "delta_bwd", body, (n,),
        [row, row, row, small, pl.BlockSpec((1, DN_HEADS, DN_DK, DN_DK), lambda i: (n - 1 - i, 0, 0, 0)),
         pl.BlockSpec((1, n_units, c_inv, c_inv), lambda i: (n - 1 - i, 0, 0, 0)), row],
        [row, row, row, small],
        [jax.ShapeDtypeStruct((T, _DN_W), f32)] * 3 + [jax.ShapeDtypeStruct((T, LANES), f32)],
        [pltpu.VMEM((DN_HEADS, DN_DK, DN_DK), f32)],
        (qn, kn, vv, gb, hist, tinv, do), ex, comm[0] if comm else ())


MM_TOKENS = 2048
MM_FEAT = 1024
MM_PROJ = PROJ_W // 5
TM_ROW = 512
TM_CONV = 256
W1 = D_MODEL


def _first_only(fn):
    return lambda *a: fn(*a)[:1]


def _swa_args(proj, cst):
    ins = [(proj, W1, CB_SWQ), (proj, LANES, CB_SWK), (proj, LANES, CB_SWV), (cst["cos"], LANES, 0), (cst["sin"], LANES, 0)]
    return ins, (1, 2, 3, 4)


def _layer_fwd(x, p, cst, comm=None, late=None):
    T = x.shape[0]
    r = {"x": x}
    (h,) = _tile_fwd("prenorm", _first_only(_fn_prenorm), T, TM_ROW, [(x, W1, 0)], [p["g1"]], [(W1, bf16)])
    proj = _matmul("proj", h, p["w_in"], "nn", f32, tm=MM_TOKENS, tn=MM_PROJ)
    (o, hist, tinv, qn, kn, vv, gbt), comm_out = _delta_fwd(proj, p["conv_w"], p["avec"], p["dvec"], comm)
    if late is not None:
        p = {**p, **late(comm_out)}
    (dn_out,) = _tile_fwd("dnpost", _fn_dnpost, T, TM_ROW, [(o, W1, 0), (proj, W1, CB_Z)], [p["ng"]], [(W1, bf16)])
    sw_ins, sw_halo = _swa_args(proj, cst)
    sw_par = [p["sinks"], cst["sel_a0"], cst["sel_b0"], cst["sel_a1"], cst["sel_b1"]]
    (sw_out,) = _tile_fwd("swa", _fn_swa, T, SW_BLOCK, sw_ins, sw_par, [(W1, bf16)], halo_ids=sw_halo, HR=SW_BLOCK)
    y_a = _matmul("up_dn", dn_out, p["w_up_dn"], "nn", bf16, tm=MM_TOKENS, tn=MM_FEAT)
    y_b = _matmul("up_sw", sw_out, p["w_up_sw"], "nn", bf16, tm=MM_TOKENS, tn=MM_FEAT)
    (gated,) = _tile_fwd("merge", _fn_merge, T, TM_ROW,
                         [(proj, W1, CB_GA), (proj, W1, CB_GB), (y_a, W1, 0), (y_b, W1, 0)], [], [(W1, bf16)])
    mix = _matmul("w_o", gated, p["w_o"], "nn", f32, tm=MM_TOKENS, tn=MM_FEAT)
    x1, h2 = _tile_fwd("postmix", _fn_postmix, T, TM_ROW, [(x, W1, 0), (mix, W1, 0)], [p["g2"], p["g3"]],
                       [(W1, f32), (W1, bf16)])
    ffh, act = _matmul("ff1", h2, p["w_ff1"], "nn", [f32, bf16], tm=MM_TOKENS, tn=MM_FEAT,
                       epilogue=lambda acc, ex: [acc, jnp.square(jnp.maximum(acc, 0.0))])
    ff = _matmul("ff2", act, p["w_ff2"], "nn", f32, tm=MM_TOKENS, tn=MM_FEAT)
    (x2,) = _tile_fwd("postmlp", _fn_postmlp, T, TM_ROW, [(x1, W1, 0), (ff, W1, 0)], [p["g4"]], [(W1, f32)])
    r.update(h=h, proj=proj, qn=qn, kn=kn, vv=vv, gbt=gbt, o=o, hist=hist, tinv=tinv, dn_out=dn_out, sw_out=sw_out,
             y_a=y_a, y_b=y_b, gated=gated, mix=mix, h2=h2, ffh=ffh, act=act, ff=ff)
    return x2, r, p, comm_out


def _pieces(n, full):
    return _shard_major(n, full[None])[:, 0].astype(bf16)


def _layer_bwd(dx2, r, p, cst, carry=None):
    T = dx2.shape[0]
    x, proj = r["x"], r["proj"]
    g = {}
    dff, g["g4"] = _tile_bwd("postmlp_b", _fn_rms_only, T, TM_ROW, [(r["ff"], W1, 0)], [p["g4"]], [(dx2, W1, 0)],
                             [(0, bf16)], [0])
    (dffh,) = _matmul("ff2_dx", dff, p["w_ff2"], "nt", [bf16], tm=MM_TOKENS, tn=MM_FEAT, extra=[r["ffh"]],
                      epilogue=lambda acc, ex: [acc * (2.0 * jnp.maximum(ex[0], 0.0))])
    g["w_ff2"] = _matmul("ff2_dw", r["act"], dff, "tn", bf16, tm=MM_FEAT, tn=MM_FEAT, tk=MM_TOKENS)
    dh2 = _matmul("ff1_dx", dffh, p["w_ff1"], "nt", bf16, tm=MM_TOKENS, tn=MM_FEAT)
    g["w_ff1"] = _matmul("ff1_dw", r["h2"], dffh, "tn", bf16, tm=MM_FEAT, tn=MM_FEAT, tk=MM_TOKENS)
    dx1, dmix, g["g2"], g["g3"] = _tile_bwd("postmix_b", _fn_postmix, T, TM_ROW, [(x, W1, 0), (r["mix"], W1, 0)],
                                            [p["g2"], p["g3"]], [(dx2, W1, 0), (dh2, W1, 0)], [(0, f32), (1, bf16)], [0, 1])
    dgated = _matmul("w_o_dx", dmix, p["w_o"], "nt", bf16, tm=MM_TOKENS, tn=MM_FEAT)
    g["w_o"] = _matmul("w_o_dw", r["gated"], dmix, "tn", bf16, tm=MM_FEAT, tn=MM_FEAT, tk=MM_TOKENS)
    dga, dgb, dya, dyb = _tile_bwd("merge_b", _fn_merge, T, TM_ROW,
                                   [(proj, W1, CB_GA), (proj, W1, CB_GB), (r["y_a"], W1, 0), (r["y_b"], W1, 0)], [],
                                   [(dgated, W1, 0)], [(0, bf16), (1, bf16), (2, bf16), (3, bf16)], [])
    d_dn = _matmul("up_dn_dx", dya, p["w_up_dn"], "nt", bf16, tm=MM_TOKENS, tn=MM_FEAT)
    g["w_up_dn"] = _matmul("up_dn_dw", r["dn_out"], dya, "tn", bf16, tm=MM_FEAT, tn=MM_FEAT, tk=MM_TOKENS)
    d_sw = _matmul("up_sw_dx", dyb, p["w_up_sw"], "nt", bf16, tm=MM_TOKENS, tn=MM_FEAT)
    g["w_up_sw"] = _matmul("up_sw_dw", r["sw_out"], dyb, "tn", bf16, tm=MM_FEAT, tn=MM_FEAT, tk=MM_TOKENS)
    do, dz, g["ng"] = _tile_bwd("dnpost_b", _fn_dnpost, T, TM_ROW, [(r["o"], W1, 0), (proj, W1, CB_Z)], [p["ng"]],
                                [(d_dn, W1, 0)], [(0, f32), (1, bf16)], [0])
    comm = None
    if carry is not None:
        send = list(carry) + [_pieces(n, g[n]) for n in _LATE]
        comm = (send, ["scatter"] * len(send))
    (dqn, dkn, dvv, dgbt), comm_out = _delta_bwd(r["qn"], r["kn"], r["vv"], r["gbt"], r["hist"], r["tinv"], do, comm)
    conv_b = lambda nm, cb, scale, ct: _tile_bwd(nm, _make_fn_conv(scale), T, TM_CONV, [(proj, W1, cb)],
                                                 [p["conv_w"][:, W1 * cb:W1 * (cb + 1)]], [(ct, W1, 0)], [(0, bf16)], [0],
                                                 halo_ids=(0,))
    dq_in, dcw_q = conv_b("conv_q_b", CB_Q, DN_DK ** -0.5, dqn)
    dk_in, dcw_k = conv_b("conv_k_b", CB_K, 1.0, dkn)
    dv_in, dcw_v = conv_b("conv_v_b", CB_V, None, dvv)
    g["conv_w"] = jnp.concatenate([dcw_q, dcw_k, dcw_v], axis=-1)
    dba, g["avec"], g["dvec"] = _tile_bwd("gates_b", _fn_gates, T, TM_ROW, [(proj, LANES, CB_BA)], [p["avec"], p["dvec"]],
                                          [(dgbt, LANES, 0)], [(0, bf16)], [0, 1])
    sw_ins, sw_halo = _swa_args(proj, cst)
    sw_par = [p["sinks"], cst["sel_a0"], cst["sel_b0"], cst["sel_a1"], cst["sel_b1"]]
    dswq, dswk, dswv, g["sinks"] = _tile_bwd("swa_b", _fn_swa, T, SW_BLOCK, sw_ins, sw_par, [(d_sw, W1, 0)],
                                             [(0, bf16), (1, bf16), (2, bf16)], [0], halo_ids=sw_halo, HR=SW_BLOCK)
    dproj = jnp.concatenate([dq_in, dk_in, dv_in, dz, dswq, dga, dgb, dswk, dswv, dba, jnp.zeros((T, LANES), bf16)], axis=-1)
    dh = _matmul("proj_dx", dproj, p["w_in"], "nt", f32, tm=MM_TOKENS, tn=MM_FEAT, tk=MM_PROJ)
    g["w_in"] = _matmul("proj_dw", r["h"], dproj, "tn", bf16, tm=MM_FEAT, tn=MM_PROJ, tk=MM_TOKENS)
    dx, g["g1"] = _tile_bwd("prenorm_b", _fn_prenorm, T, TM_ROW, [(x, W1, 0)], [p["g1"]], [(dh, W1, 0), (dx1, W1, 0)],
                            [(0, f32)], [0])
    return dx, g, comm_out


_OFF_BA, _OFF_SWQ, _OFF_SWK, _OFF_GA, _D_IN = 4096, 4112, 5136, 5392, 7440


def _proj_cols(w):
    pad = lambda n: jnp.zeros(w.shape[:-1] + (n,), w.dtype)
    return jnp.concatenate([w[..., :_OFF_BA], w[..., _OFF_SWQ:_OFF_SWK], w[..., _OFF_GA:_D_IN],
                            w[..., _OFF_SWK:_OFF_GA], w[..., _OFF_BA:_OFF_SWQ], pad(PROJ_W - _D_IN)], axis=-1)


def _proj_cols_inv(w):
    n_ba = _OFF_SWQ - _OFF_BA
    at = lambda cb, width=W1: cb * width
    return jnp.concatenate([w[..., :at(CB_SWQ)], w[..., at(CB_BA, LANES):at(CB_BA, LANES) + n_ba],
                            w[..., at(CB_SWQ):at(CB_GA)], w[..., at(CB_SWK, LANES):at(CB_BA, LANES)],
                            w[..., at(CB_GA):at(CB_SWK, LANES)]], axis=-1)


def _lane_pad(v, at):
    return jnp.pad(v.astype(f32), (at, LANES - at - v.shape[0])).reshape(1, LANES)


_EARLY = ("w_in", "dn_conv_w")
_LATE = ("w_up_dn", "w_up_sw", "w_o", "w_ff1", "w_ff2")


def _early_params(w):
    row = lambda v: v.reshape(1, -1).astype(f32)
    return dict(
        g1=row(w["pre_mix_g"]), g2=row(w["post_mix_g"]), g3=row(w["pre_mlp_g"]), g4=row(w["post_mlp_g"]),
        w_in=_proj_cols(w["w_in"]).astype(bf16), conv_w=w["dn_conv_w"].astype(f32),
        avec=_lane_pad(w["dn_a_log"], DN_HEADS), dvec=_lane_pad(w["dn_dt_bias"], DN_HEADS),
        ng=row(w["dn_norm_g"]), sinks=_lane_pad(w["sw_sinks"], 0))


def _late_params(w):
    return {n: w[n].astype(bf16) for n in _LATE}


def _layer_params(w):
    return {**_early_params(w), **_late_params(w)}


def _layer_grads_ref_layout(g):
    return dict(
        pre_mix_g=g["g1"][0], post_mix_g=g["g2"][0], pre_mlp_g=g["g3"][0], post_mlp_g=g["g4"][0],
        w_in=_proj_cols_inv(g["w_in"]), dn_conv_w=g["conv_w"],
        dn_a_log=g["avec"][0, DN_HEADS:2 * DN_HEADS], dn_dt_bias=g["dvec"][0, DN_HEADS:2 * DN_HEADS],
        dn_norm_g=g["ng"][0], sw_sinks=g["sinks"][0, :SW_Q_HEADS],
        w_up_dn=g["w_up_dn"], w_up_sw=g["w_up_sw"], w_o=g["w_o"], w_ff1=g["w_ff1"], w_ff2=g["w_ff2"])


def _consts(positions):
    T = positions.shape[0]
    half = ROT_DIM // 2
    inv_freq = ROPE_THETA ** (-jnp.arange(half, dtype=f32) * (2.0 / ROT_DIM))
    ang = positions.astype(f32)[:, None] * inv_freq
    cos8, sin8 = jnp.cos(ang), jnp.sin(ang)
    rest = SW_HEAD_DIM - ROT_DIM
    c64 = jnp.concatenate([cos8, cos8, jnp.ones((T, rest), f32)], axis=-1)
    s64 = jnp.concatenate([sin8, sin8, jnp.zeros((T, rest), f32)], axis=-1)
    sel = np.zeros((2, 2, LANES, LANES), np.float32)
    for hk in range(SW_KV_HEADS):
        for d in range(SW_HEAD_DIM):
            sel[hk, 0, SW_HEAD_DIM * hk + d, d] = 1.0
            sel[hk, 1, SW_HEAD_DIM * hk + d, SW_HEAD_DIM + d] = 1.0
    return dict(cos=jnp.concatenate([c64, c64], axis=-1), sin=jnp.concatenate([s64, s64], axis=-1),
                sel_a0=jnp.asarray(sel[0, 0]), sel_b0=jnp.asarray(sel[0, 1]),
                sel_a1=jnp.asarray(sel[1, 0]), sel_b1=jnp.asarray(sel[1, 1]))


def _loss(y, tgt):
    T, W = y.shape
    TM = min(TM_ROW, T)
    n = T // TM

    def body(y_ref, t_ref, dy_ref, acc_ref):
        @pl.when(pl.program_id(0) == 0)
        def _():
            acc_ref[...] = jnp.zeros_like(acc_ref)

        d = y_ref[...] - t_ref[...]
        dy_ref[...] = d * (1.0 / W)
        acc_ref[...] += jnp.sum(d * d, axis=0, keepdims=True)

    row = pl.BlockSpec((TM, W), lambda i: (i, 0))
    return pl.pallas_call(
        body, name="loss", grid=(n,), in_specs=[row, row],
        out_specs=[row, pl.BlockSpec((1, W), lambda i: (0, 0))],
        out_shape=[jax.ShapeDtypeStruct((T, W), f32), jax.ShapeDtypeStruct((1, W), f32)],
        compiler_params=pltpu.CompilerParams(dimension_semantics=("arbitrary",)),
    )(y, tgt)


N_SEM = N_DEV - 1


class _Exchange:
    def __init__(self, arrs, modes):
        self.modes, self.n = list(modes), len(arrs)
        self.out_shape = [jax.ShapeDtypeStruct((N_DEV,) + (a.shape[1:] if md == "scatter" else a.shape), a.dtype)
                          for a, md in zip(arrs, modes)]
        self.scratch = [pltpu.SemaphoreType.DMA((self.n, N_SEM)), pltpu.SemaphoreType.DMA((self.n, N_SEM)),
                        pltpu.SemaphoreType.DMA((self.n,))]
        self.specs = [pl.BlockSpec(memory_space=pltpu.HBM)] * self.n

    def bind(self, ins, outs, send_sems, recv_sems, loc_sems):
        x, y, c = lax.axis_index("x"), lax.axis_index("y"), lax.axis_index("c")
        me, sib = 4 * x + 2 * y + c, (x, y, 1 - c)
        flips = [(1 - x, y), (x, 1 - y), (1 - x, 1 - y)]

        def rcopy(a, k, src, dst, dev):
            return pltpu.make_async_remote_copy(src_ref=src, dst_ref=dst, send_sem=send_sems.at[a, k],
                                                recv_sem=recv_sems.at[a, k], device_id=dev,
                                                device_id_type=pl.DeviceIdType.MESH)

        sends, recvs, local, passes = [], [], [], []
        for a, md in enumerate(self.modes):
            src_all, out = ins[a], outs[a]
            mine = (lambda dev: src_all.at[dev]) if md == "scatter" else (lambda dev: src_all)
            local.append(pltpu.make_async_copy(mine(me), out.at[me], loc_sems.at[a]))
            if md in ("gather", "scatter"):
                for k in range(1, N_DEV):
                    px = 1 - x if (k >> 2) & 1 else x
                    py = 1 - y if (k >> 1) & 1 else y
                    pc = 1 - c if k & 1 else c
                    peer = 4 * px + 2 * py + pc
                    sends.append(rcopy(a, k - 1, mine(peer), out.at[me], (px, py, pc)))
                    recvs.append(rcopy(a, k - 1, mine(me), out.at[peer], (px, py, pc)))
            elif md == "gather2":
                sends.append(rcopy(a, 0, src_all, out.at[me], sib))
                recvs.append(rcopy(a, 0, src_all, out.at[4 * x + 2 * y + 1 - c], sib))
                for j, (px, py) in enumerate(flips):
                    sends.append(rcopy(a, 1 + j, src_all, out.at[me], (px, py, c)))
                    theirs = out.at[4 * px + 2 * py + c]
                    arrive = rcopy(a, 1 + j, src_all, theirs, (px, py, c))
                    passes.append((arrive, rcopy(a, 4 + j, theirs, theirs, sib)))
                    recvs.append(rcopy(a, 4 + j, src_all, out.at[4 * px + 2 * py + 1 - c], sib))

        def start():
            for cp in local + sends:
                cp.start()

        def finish():
            for arrive, onward in passes:
                arrive.wait_recv()
                onward.start()
            for cp in recvs:
                cp.wait_recv()
            for cp in sends:
                cp.wait_send()
            for _, onward in passes:
                onward.wait_send()
            for cp in local:
                cp.wait()

        return start, finish


def _exchange(name, arrs, modes):
    ex = _Exchange(arrs, modes)
    n = ex.n

    def body(*refs):
        start, finish = ex.bind(refs[:n], refs[n:2 * n], *refs[2 * n:])
        start()
        finish()

    res = pl.pallas_call(body, name=name, in_specs=ex.specs, out_specs=ex.specs, out_shape=ex.out_shape,
                         scratch_shapes=ex.scratch)(*arrs)
    return list(res)


def _adamw(name, land, w, m, v, tr):
    L_, R_, C_ = w.shape
    n_slots = land.shape[0]
    tr = min(tr, R_)
    assert R_ % tr == 0, (name, R_, tr)
    c1 = 1.0 - ADAM_B1 ** ADAM_STEP
    c2 = 1.0 - ADAM_B2 ** ADAM_STEP

    def body(l_ref, w_ref, m_ref, v_ref, g_ref, d_ref, mo_ref, vo_ref):
        g = l_ref[0].astype(f32)
        for s in range(1, n_slots):
            g = g + l_ref[s].astype(f32)
        m_new = ADAM_B1 * m_ref[...] + (1.0 - ADAM_B1) * g
        v_new = ADAM_B2 * v_ref[...] + (1.0 - ADAM_B2) * jnp.square(g)
        m_hat = m_new / c1
        v_hat = v_new / c2
        g_ref[...] = g
        d_ref[...] = -ADAM_LR * (m_hat / (jnp.sqrt(v_hat) + ADAM_EPS) + ADAM_WD * w_ref[...])
        mo_ref[...] = m_new
        vo_ref[...] = v_new

    row = pl.BlockSpec((1, tr, C_), lambda l, i: (l, i, 0))
    return pl.pallas_call(
        body, name=name, grid=(L_, R_ // tr),
        in_specs=[pl.BlockSpec((n_slots, 1, tr, C_), lambda l, i: (0, l, i, 0)), row, row, row],
        out_specs=[row] * 4, out_shape=[jax.ShapeDtypeStruct((L_, R_, C_), f32)] * 4,
        compiler_params=pltpu.CompilerParams(dimension_semantics=("arbitrary", "arbitrary")),
    )(land, w, m, v)


_BIG = ("w_in", "dn_conv_w", "w_up_dn", "w_up_sw", "w_o", "w_ff1", "w_ff2")
_COL_SHARDED = ("w_in", "dn_conv_w", "w_ff1")
_SMALL_ROWS = ("pre_mix_g", "post_mix_g", "pre_mlp_g", "post_mlp_g")
_SMALL_MISC = ("dn_a_log", "dn_dt_bias", "dn_norm_g", "sw_sinks")
_WEIGHTS = ("pre_mix_g", "w_in", "dn_conv_w", "dn_a_log", "dn_dt_bias", "dn_norm_g", "sw_sinks", "w_up_dn", "w_up_sw",
            "w_o", "post_mix_g", "pre_mlp_g", "w_ff1", "w_ff2", "post_mlp_g")
_SMALL_PACK_ROWS = 24


def _unshard(name, g):
    if name in _COL_SHARDED:
        g = jnp.moveaxis(g, 0, -2)
        return g.reshape(g.shape[:-2] + (g.shape[-2] * g.shape[-1],))
    g = jnp.moveaxis(g, 0, 1)
    return g.reshape((g.shape[0], g.shape[1] * g.shape[2]) + g.shape[3:])


def _shard_major(name, full):
    if name in _COL_SHARDED:
        s = full.reshape(full.shape[:-1] + (N_DEV, full.shape[-1] // N_DEV))
        return jnp.moveaxis(s, -2, 0)
    s = full.reshape((full.shape[0], N_DEV, full.shape[1] // N_DEV) + full.shape[2:])
    return jnp.moveaxis(s, 1, 0)


def _pack_small(d):
    rows = jnp.concatenate([d[n] for n in _SMALL_ROWS], axis=0)
    misc = jnp.concatenate([d[n].reshape(-1) for n in _SMALL_MISC])
    misc = jnp.pad(misc, (0, W1 - misc.shape[0])).reshape(1, W1)
    out = jnp.concatenate([rows, misc], axis=0)
    return jnp.pad(out, ((0, _SMALL_PACK_ROWS - out.shape[0]), (0, 0)))


def _unpack_small(a, like):
    out, L = {}, like[_SMALL_ROWS[0]].shape[0]
    for i, n in enumerate(_SMALL_ROWS):
        out[n] = a[L * i:L * (i + 1)]
    at, row = 0, a[L * len(_SMALL_ROWS)]
    for n in _SMALL_MISC:
        size = like[n].size
        out[n] = row[at:at + size].reshape(like[n].shape)
        at += size
    return out


def kernel(x, positions, pre_mix_g, w_in, dn_conv_w, dn_a_log, dn_dt_bias, dn_norm_g, sw_sinks, w_up_dn, w_up_sw, w_o, post_mix_g, pre_mlp_g, w_ff1, w_ff2, post_mlp_g, loss_target, m_pre_mix_g, m_w_in, m_dn_conv_w, m_dn_a_log, m_dn_dt_bias, m_dn_norm_g, m_sw_sinks, m_w_up_dn, m_w_up_sw, m_w_o, m_post_mix_g, m_pre_mlp_g, m_w_ff1, m_w_ff2, m_post_mlp_g, v_pre_mix_g, v_w_in, v_dn_conv_w, v_dn_a_log, v_dn_dt_bias, v_dn_norm_g, v_sw_sinks, v_w_up_dn, v_w_up_sw, v_w_o, v_post_mix_g, v_pre_mlp_g, v_w_ff1, v_w_ff2, v_post_mlp_g):
    w = dict(pre_mix_g=pre_mix_g, w_in=w_in, dn_conv_w=dn_conv_w, dn_a_log=dn_a_log, dn_dt_bias=dn_dt_bias,
             dn_norm_g=dn_norm_g, sw_sinks=sw_sinks, w_up_dn=w_up_dn, w_up_sw=w_up_sw, w_o=w_o, post_mix_g=post_mix_g,
             pre_mlp_g=pre_mlp_g, w_ff1=w_ff1, w_ff2=w_ff2, post_mlp_g=post_mlp_g)
    m = dict(pre_mix_g=m_pre_mix_g, w_in=m_w_in, dn_conv_w=m_dn_conv_w, dn_a_log=m_dn_a_log, dn_dt_bias=m_dn_dt_bias,
             dn_norm_g=m_dn_norm_g, sw_sinks=m_sw_sinks, w_up_dn=m_w_up_dn, w_up_sw=m_w_up_sw, w_o=m_w_o,
             post_mix_g=m_post_mix_g, pre_mlp_g=m_pre_mlp_g, w_ff1=m_w_ff1, w_ff2=m_w_ff2, post_mlp_g=m_post_mlp_g)
    v = dict(pre_mix_g=v_pre_mix_g, w_in=v_w_in, dn_conv_w=v_dn_conv_w, dn_a_log=v_dn_a_log, dn_dt_bias=v_dn_dt_bias,
             dn_norm_g=v_dn_norm_g, sw_sinks=v_sw_sinks, w_up_dn=v_w_up_dn, w_up_sw=v_w_up_sw, w_o=v_w_o,
             post_mix_g=v_post_mix_g, pre_mlp_g=v_pre_mlp_g, w_ff1=v_w_ff1, w_ff2=v_w_ff2, post_mlp_g=v_post_mlp_g)
    n_layers = pre_mix_g.shape[0]
    xs, pos, tgt = x[0], positions[0], loss_target[0]

    cst = _consts(pos)

    def payload(l, names):
        return [w[n][l] if n == "dn_conv_w" else w[n][l].astype(bf16) for n in names]

    def full(names, gathered):
        return {n: _unshard(n, g[:, None])[0] for n, g in zip(names, gathered)}

    early = _exchange("allgather_first", payload(0, _EARLY), ["gather2"] * len(_EARLY))
    h, res, layers = xs, [], []
    for l in range(n_layers):
        wl = {**full(_EARLY, early), **{n: w[n][l] for n in _WEIGHTS if n not in _BIG}}
        send = payload(l, _LATE) + (payload(l + 1, _EARLY) if l + 1 < n_layers else [])
        h, r, p, got = _layer_fwd(h, _early_params(wl), cst, (send, ["gather2"] * len(send)),
                                  late=lambda got: _late_params(full(_LATE, got[:len(_LATE)])))
        early = got[len(_LATE):]
        res.append(r)
        layers.append(p)
    dy, sq = _loss(h, tgt)
    loss = lax.psum(0.5 / D_MODEL * jnp.sum(sq), ("x", "y", "c"))

    grads, landed, carry = [None] * n_layers, {n: [None] * n_layers for n in _BIG}, []
    for l in reversed(range(n_layers)):
        dy, g, arrived = _layer_bwd(dy, res[l], layers[l], cst, carry)
        for n, a in zip(_EARLY, arrived[:len(carry)]):
            landed[n][l + 1] = a
        for n, a in zip(_LATE, arrived[len(carry):]):
            landed[n][l] = a
        grads[l] = _layer_grads_ref_layout(g)
        carry = [_pieces(n, grads[l][n]) for n in _EARLY]
    grad_x = dy[None]
    small_grads = _pack_small({n: jnp.stack([grads[l][n] for l in range(n_layers)]) for n in _SMALL_ROWS + _SMALL_MISC})
    last = _exchange("exchange_last", carry + [small_grads], ["scatter"] * len(carry) + ["gather"])
    for n, a in zip(_EARLY, last):
        landed[n][0] = a

    out_g, out_d, out_m, out_v = {}, {}, {}, {}
    for n in _BIG:
        land = jnp.stack(landed[n], axis=1)
        out_g[n], out_d[n], out_m[n], out_v[n] = _adamw("adamw_" + n, land, w[n], m[n], v[n], tr=256)
    small = _adamw("adamw_small", last[-1][:, None], _pack_small(w)[None], _pack_small(m)[None], _pack_small(v)[None],
                   tr=_SMALL_PACK_ROWS)
    for dst, a in zip((out_g, out_d, out_m, out_v), small):
        dst.update(_unpack_small(a[0], w))
    return (loss, grad_x, *[out_g[n] for n in _WEIGHTS], *[out_d[n] for n in _WEIGHTS],
            *[out_m[n] for n in _WEIGHTS], *[out_v[n] for n in _WEIGHTS])
```
